```python
import math
import jax
import jax.numpy as jnp
from jax import lax
import numpy as np

D_MODEL = 1024
BATCH = 4
SEQ = 8192
DEPTH = 4

MEM_LEN = 256
GRID_W = 64
CHUNK = 128
Q_BLOCK = 128
EPS = 1e-6
GM_GROUPS = 4
GM_GROUP_DIM = 128
GM_WIDTH = GM_GROUPS * GM_GROUP_DIM
HEAD_DIM = 64
N_Q_HEADS = 8
N_KV_HEADS = 2
Q_PER_KV = N_Q_HEADS // N_KV_HEADS
ATT_WIDTH = N_Q_HEADS * HEAD_DIM
KV_WIDTH = N_KV_HEADS * HEAD_DIM
ROPE_THETA = 10000.0
MIX_WIDTH = GM_WIDTH + ATT_WIDTH
IN_WIDTH = 2 * GM_WIDTH + ATT_WIDTH + 2 * KV_WIDTH
SPLITS = (GM_WIDTH, 2 * GM_WIDTH, 2 * GM_WIDTH + ATT_WIDTH, 2 * GM_WIDTH + ATT_WIDTH + KV_WIDTH)
X_HEADS = 4
X_HEAD_DIM = D_MODEL // X_HEADS
N_EXPERTS = 16
EC_FACTOR = 2
EXPERT_FF = 1024

kernel_name = 'hybrid_gmlp_gqa_ec_encoder'


def rms_norm(x, g):
    xf = x.astype(jnp.float32)
    y = xf * lax.rsqrt(jnp.mean(xf * xf, axis=-1, keepdims=True) + EPS)
    return (y * g.astype(jnp.float32)).astype(x.dtype)


def axial_rope_tables(seq_len):
    rows = seq_len // GRID_W
    row_id = jnp.repeat(jnp.arange(rows, dtype=jnp.float32), GRID_W)
    col_id = jnp.tile(jnp.arange(GRID_W, dtype=jnp.float32), rows)
    n_pairs = HEAD_DIM // 4
    freqs = jnp.exp(-math.log(ROPE_THETA) * jnp.arange(n_pairs, dtype=jnp.float32) / n_pairs)
    ang = jnp.concatenate([row_id[:, None] * freqs[None, :], col_id[:, None] * freqs[None, :]], axis=-1)
    return jnp.cos(ang), jnp.sin(ang)


def apply_rope(x, cos, sin):
    b, s, h, d = x.shape
    xf = x.astype(jnp.float32).reshape(b, s, h, d // 2, 2)
    x0, x1 = xf[..., 0], xf[..., 1]
    c = cos[None, :, None, :]
    sn = sin[None, :, None, :]
    out = jnp.stack([x0 * c - x1 * sn, x0 * sn + x1 * c], axis=-1)
    return out.reshape(b, s, h, d).astype(x.dtype)


def gmlp_group(u, v, v_norm_g, w_s, b_s):
    b, s, _ = u.shape
    v = rms_norm(v, v_norm_g).reshape(b, s // CHUNK, CHUNK, GM_GROUPS, GM_GROUP_DIM)
    mixed = jnp.einsum('gij,bnjgc->bnigc', w_s, v) + b_s.T[None, None, :, :, None]
    return u * mixed.reshape(b, s, GM_WIDTH)


def gqa_group(q, k, v, q_norm_g, k_norm_g, cos, sin):
    b, s, _ = q.shape
    q = q.reshape(b, s, N_Q_HEADS, HEAD_DIM)
    k = k.reshape(b, s, N_KV_HEADS, HEAD_DIM)
    v = v.reshape(b, s, N_KV_HEADS, HEAD_DIM)
    q = apply_rope(rms_norm(q, q_norm_g), cos, sin) * (HEAD_DIM ** -0.5)
    k = apply_rope(rms_norm(k, k_norm_g), cos, sin)
    n_blk = s // Q_BLOCK
    q_blocks = q.reshape(b, n_blk, Q_BLOCK, N_KV_HEADS, Q_PER_KV, HEAD_DIM).transpose(1, 0, 3, 4, 2, 5)

    def attend(qb):
        sc = jnp.einsum('bkgqd,bskd->bkgqs', qb, k, preferred_element_type=jnp.float32)
        p = jax.nn.softmax(sc, axis=-1).astype(v.dtype)
        return jnp.einsum('bkgqs,bskd->bkgqd', p, v)

    o = lax.map(attend, q_blocks)
    return o.transpose(1, 0, 4, 2, 3, 5).reshape(b, s, ATT_WIDTH)


def memory_cross_attention(h, mem_n, w_q, w_kv, w_o):
    b, s, _ = h.shape
    m = mem_n.shape[1]
    q = (h @ w_q).reshape(b, s, X_HEADS, X_HEAD_DIM) * (X_HEAD_DIM ** -0.5)
    kv = (mem_n @ w_kv).reshape(b, m, 2, X_HEADS, X_HEAD_DIM)
    k, v = kv[:, :, 0], kv[:, :, 1]
    sc = jnp.einsum('bshd,bmhd->bhsm', q, k, preferred_element_type=jnp.float32)
    p = jax.nn.softmax(sc, axis=-1).astype(v.dtype)
    o = jnp.einsum('bhsm,bmhd->bshd', p, v).reshape(b, s, D_MODEL)
    return o @ w_o


def expert_choice_moe(h, w_router, w_gate, w_up, w_down):
    b, s, _ = h.shape
    cap = EC_FACTOR * s // N_EXPERTS
    logits = jnp.einsum('bsd,de->bse', h, w_router, preferred_element_type=jnp.float32)
    aff = jax.nn.softmax(logits, axis=-1)
    gates, idx = lax.top_k(aff.transpose(0, 2, 1), cap)
    b_ix = jnp.arange(b)[:, None, None]
    xs = h[b_ix, idx]
    a = jnp.einsum('becd,edf->becf', xs, w_gate)
    u = jnp.einsum('becd,edf->becf', xs, w_up)
    y = jnp.einsum('becf,efd->becd', jax.nn.silu(a) * u, w_down)
    y = y * gates[..., None].astype(y.dtype)
    return jnp.zeros_like(h).at[b_ix, idx].add(y)


def setup_inputs(seed: int = 0) -> dict:
    key = jax.random.key(seed)
    ks = jax.random.split(key, 24)

    def nrm(k, shape, scale):
        return jax.random.normal(k, shape, jnp.float32) * scale

    def gain(k, shape):
        return 1.0 + 0.05 * jax.random.normal(k, shape, jnp.float32)

    L, D = DEPTH, D_MODEL
    return {
        'x': nrm(ks[0], (BATCH, SEQ, D), 1.0),
        'mem': nrm(ks[1], (BATCH, MEM_LEN, D), 1.0),
        'mix_norm_g': gain(ks[2], (L, D)),
        'w_in': nrm(ks[3], (L, D, IN_WIDTH), D ** -0.5),
        'gm_v_norm_g': gain(ks[4], (L, GM_WIDTH)),
        'gm_w_s': nrm(ks[5], (L, GM_GROUPS, CHUNK, CHUNK), CHUNK ** -0.5),
        'gm_b_s': gain(ks[6], (L, GM_GROUPS, CHUNK)),
        'q_norm_g': gain(ks[7], (L, HEAD_DIM)),
        'k_norm_g': gain(ks[8], (L, HEAD_DIM)),
        'branch_norm_g': gain(ks[9], (L, 2, GM_WIDTH)),
        'w_out': nrm(ks[10], (L, MIX_WIDTH, D), MIX_WIDTH ** -0.5),
        'xattn_norm_g': gain(ks[11], (L, D)),
        'mem_norm_g': gain(ks[12], (D,)),
        'xattn_w_q': nrm(ks[13], (L, D, D), D ** -0.5),
        'xattn_w_kv': nrm(ks[14], (L, D, 2 * D), D ** -0.5),
        'xattn_w_o': nrm(ks[15], (L, D, D), D ** -0.5),
        'ffn_norm_g': gain(ks[16], (L, D)),
        'w_router': nrm(ks[17], (L, D, N_EXPERTS), D ** -0.5),
        'w_gate': nrm(ks[18], (L, N_EXPERTS, D, EXPERT_FF), D ** -0.5),
        'w_up': nrm(ks[19], (L, N_EXPERTS, D, EXPERT_FF), D ** -0.5),
        'w_down': nrm(ks[20], (L, N_EXPERTS, EXPERT_FF, D), EXPERT_FF ** -0.5),
        'final_norm_g': gain(ks[21], (D,)),
    }


def reference(x, mem, mix_norm_g, w_in, gm_v_norm_g, gm_w_s, gm_b_s, q_norm_g, k_norm_g,
              branch_norm_g, w_out, xattn_norm_g, mem_norm_g, xattn_w_q, xattn_w_kv, xattn_w_o,
              ffn_norm_g, w_router, w_gate, w_up, w_down, final_norm_g):
    cos, sin = axial_rope_tables(x.shape[1])
    mem_n = rms_norm(mem, mem_norm_g)
    for l in range(DEPTH):
        h = rms_norm(x, mix_norm_g[l])
        proj = h @ w_in[l]
        u, v, q, k, vv = jnp.split(proj, SPLITS, axis=-1)
        gm = gmlp_group(jax.nn.gelu(u), jax.nn.gelu(v), gm_v_norm_g[l], gm_w_s[l], gm_b_s[l])
        at = gqa_group(q, k, vv, q_norm_g[l], k_norm_g[l], cos, sin)
        merged = jnp.concatenate([rms_norm(gm, branch_norm_g[l, 0]), rms_norm(at, branch_norm_g[l, 1])], axis=-1)
        x = x + merged @ w_out[l]
        x = x + memory_cross_attention(rms_norm(x, xattn_norm_g[l]), mem_n,
                                       xattn_w_q[l], xattn_w_kv[l], xattn_w_o[l])
        x = x + expert_choice_moe(rms_norm(x, ffn_norm_g[l]), w_router[l], w_gate[l], w_up[l], w_down[l])
    return rms_norm(x, final_norm_g)
```

```python
import functools
import math

import jax
import jax.numpy as jnp
from jax import lax
from jax.experimental import pallas as pl
from jax.experimental.pallas import tpu as pltpu

F32 = jnp.float32
BF16 = jnp.bfloat16

EPS = 1e-6
CHUNK = 128
GM_GROUPS = 4
GM_WIDTH = 512
HEAD_DIM = 64
N_Q_HEADS = 8
N_KV_HEADS = 2
Q_PER_KV = N_Q_HEADS // N_KV_HEADS
ATT_WIDTH = N_Q_HEADS * HEAD_DIM
KV_WIDTH = N_KV_HEADS * HEAD_DIM
ROPE_THETA = 10000.0
GRID_W = 64
X_HEADS = 4
N_EXPERTS = 16
EC_FACTOR = 2

LANES = 128
VMEM_LIMIT = 56 * 1024 * 1024

TOK_BLOCK = 256
LOG2E = 1.4426950408889634


def _cparams(sem, vmem=VMEM_LIMIT, **kw):
    return pltpu.CompilerParams(dimension_semantics=sem, vmem_limit_bytes=vmem, **kw)


def _rms_rows(x, g):
    ms = jnp.mean(x * x, axis=-1, keepdims=True)
    return x * lax.rsqrt(ms + EPS) * g


def _gelu_tanh(x):
    c = math.sqrt(2.0 / math.pi)
    return 0.5 * x * (1.0 + jnp.tanh(c * (x + 0.044715 * (x * x * x))))


def _norm_rope_T(xT, gcol, cosT, sinT, n_heads, scale):
    half = HEAD_DIM // 2
    outs = []
    for h in range(n_heads):
        blk = xT[h * HEAD_DIM:(h + 1) * HEAD_DIM, :]
        ms = jnp.mean(blk * blk, axis=0, keepdims=True)
        n = blk * lax.rsqrt(ms + EPS) * gcol
        e = n[:half, :]
        o = n[half:, :]
        re = (e * cosT - o * sinT) * scale
        ro = (e * sinT + o * cosT) * scale
        outs.append(jnp.concatenate([re, ro], axis=0))
    return outs


def _mixer_in_kernel(x_ref, g_ref, w_ref, vg_ref, ws_ref, bs_ref, qg_ref, kg_ref, bg0_ref,
                     cos_ref, sin_ref, gm_ref, qT_ref, k_ref, vT_ref, gm_scr):
    tm = x_ref.shape[0]
    x = x_ref[...]
    h = _rms_rows(x, g_ref[...]).astype(BF16)
    proj = jnp.dot(h, w_ref[...], preferred_element_type=F32)

    u = _gelu_tanh(proj[:, :GM_WIDTH])
    v = _gelu_tanh(proj[:, GM_WIDTH:2 * GM_WIDTH])
    vn = _rms_rows(v, vg_ref[...]).astype(BF16)
    for c in range(tm // CHUNK):
        rs = slice(c * CHUNK, (c + 1) * CHUNK)
        for g in range(GM_GROUPS):
            cs = slice(g * LANES, (g + 1) * LANES)
            mixed = jnp.dot(ws_ref[g], vn[rs, cs], preferred_element_type=F32) + bs_ref[g]
            gm_scr[rs, cs] = u[rs, cs] * mixed
    gm_ref[...] = _rms_rows(gm_scr[...], bg0_ref[...]).astype(BF16)

    o_q = 2 * GM_WIDTH
    o_k = o_q + ATT_WIDTH
    o_v = o_k + KV_WIDTH
    cosT = cos_ref[...]
    sinT = sin_ref[...]

    qT = proj[:, o_q:o_k].T
    q_heads = _norm_rope_T(qT, qg_ref[...], cosT, sinT, N_Q_HEADS, (HEAD_DIM ** -0.5) * LOG2E)
    for hh in range(N_Q_HEADS):
        qT_ref[hh * HEAD_DIM:(hh + 1) * HEAD_DIM, :] = q_heads[hh].astype(BF16)

    kT = proj[:, o_k:o_v].T
    k_heads = _norm_rope_T(kT, kg_ref[...], cosT, sinT, N_KV_HEADS, 1.0)
    k_ref[...] = jnp.concatenate(k_heads, axis=0).T.astype(BF16)

    vT_ref[...] = proj[:, o_v:].T.astype(BF16)


def _mixer_in(x2d, g, w_in, vg, ws, bs, qg, kg, bg0, cosT, sinT, *, seq, tm):
    T, D = x2d.shape
    B = T // seq
    npb = seq // tm
    in_w = w_in.shape[1]
    full = lambda shape: pl.BlockSpec(shape, lambda i: (0,) * len(shape))
    return pl.pallas_call(
        _mixer_in_kernel,
        grid=(T // tm,),
        in_specs=[
            pl.BlockSpec((tm, D), lambda i: (i, 0)),
            full((1, D)),
            full((D, in_w)),
            full((1, GM_WIDTH)),
            full((GM_GROUPS, CHUNK, CHUNK)),
            full((GM_GROUPS, CHUNK, LANES)),
            full((HEAD_DIM, tm)),
            full((HEAD_DIM, tm)),
            full((1, GM_WIDTH)),
            pl.BlockSpec((HEAD_DIM // 2, tm), lambda i: (0, i % npb)),
            pl.BlockSpec((HEAD_DIM // 2, tm), lambda i: (0, i % npb)),
        ],
        out_specs=[
            pl.BlockSpec((tm, GM_WIDTH), lambda i: (i, 0)),
            pl.BlockSpec((None, ATT_WIDTH, tm), lambda i: (i // npb, 0, i % npb)),
            pl.BlockSpec((tm, KV_WIDTH), lambda i: (i, 0)),
            pl.BlockSpec((None, KV_WIDTH, tm), lambda i: (i // npb, 0, i % npb)),
        ],
        out_shape=[
            jax.ShapeDtypeStruct((T, GM_WIDTH), BF16),
            jax.ShapeDtypeStruct((B, ATT_WIDTH, seq), BF16),
            jax.ShapeDtypeStruct((T, KV_WIDTH), BF16),
            jax.ShapeDtypeStruct((B, KV_WIDTH, seq), BF16),
        ],
        scratch_shapes=[pltpu.VMEM((tm, GM_WIDTH), F32)],
        compiler_params=_cparams(("parallel",)),
        name="mixer_in",
    )(x2d, g, w_in, vg, ws, bs, qg, kg, bg0, cosT, sinT)


def _attn_kernel(qT_ref, k_ref, vT_ref, o_ref, vext_scr, m_scr, acc_scr, *, tk):
    kh = pl.program_id(1)
    i = pl.program_id(2)
    seq = k_ref.shape[0]
    tq = qT_ref.shape[1]
    M = Q_PER_KV * tq

    @pl.when(i == 0)
    def _():
        vext_scr[0:HEAD_DIM, :] = vT_ref[...]
        row = lax.broadcasted_iota(jnp.int32, (HEAD_DIM, seq), 0)
        vext_scr[HEAD_DIM:, :] = jnp.where(row == 0, 1.0, 0.0).astype(BF16)

    q4 = jnp.concatenate([qT_ref[g * HEAD_DIM:(g + 1) * HEAD_DIM, :] for g in range(Q_PER_KV)], axis=1)
    z = jnp.zeros_like(q4)
    qx = jnp.where(kh == 0, jnp.concatenate([q4, z], axis=0), jnp.concatenate([z, q4], axis=0))

    m_scr[...] = jnp.full(m_scr.shape, -jnp.inf, F32)
    acc_scr[...] = jnp.zeros(acc_scr.shape, F32)

    def body(c, carry):
        ks = pl.multiple_of(c * tk, tk)
        kc = k_ref[pl.ds(ks, tk), :]
        s = jnp.dot(kc, qx, preferred_element_type=F32)
        m_old = m_scr[...]
        m_new = jnp.maximum(m_old, jnp.max(s, axis=0, keepdims=True))
        p = jnp.exp2(s - m_new).astype(BF16)
        alpha = jnp.exp2(m_old - m_new)
        pv = jnp.dot(vext_scr[:, pl.ds(ks, tk)], p, preferred_element_type=F32)
        acc_scr[...] = acc_scr[...] * alpha + pv
        m_scr[...] = m_new
        return carry

    lax.fori_loop(0, seq // tk, body, 0)

    acc = acc_scr[...]
    inv = 1.0 / acc[HEAD_DIM:HEAD_DIM + 1, :]
    o = acc[0:HEAD_DIM, :] * inv
    for g in range(Q_PER_KV):
        o_ref[g * HEAD_DIM:(g + 1) * HEAD_DIM, :] = o[:, g * tq:(g + 1) * tq].astype(BF16)


def _attention(qT, k, vT, *, tq, tk):
    B, _, seq = qT.shape
    gw = Q_PER_KV * HEAD_DIM
    return pl.pallas_call(
        functools.partial(_attn_kernel, tk=tk),
        grid=(B, N_KV_HEADS, seq // tq),
        in_specs=[
            pl.BlockSpec((None, gw, tq), lambda b, kh, i: (b, kh, i)),
            pl.BlockSpec((None, seq, KV_WIDTH), lambda b, kh, i: (b, 0, 0)),
            pl.BlockSpec((None, HEAD_DIM, seq), lambda b, kh, i: (b, kh, 0)),
        ],
        out_specs=pl.BlockSpec((None, gw, tq), lambda b, kh, i: (b, kh, i)),
        out_shape=jax.ShapeDtypeStruct((B, ATT_WIDTH, seq), BF16),
        scratch_shapes=[
            pltpu.VMEM((2 * HEAD_DIM, seq), BF16),
            pltpu.VMEM((1, Q_PER_KV * tq), F32),
            pltpu.VMEM((2 * HEAD_DIM, Q_PER_KV * tq), F32),
        ],
        compiler_params=_cparams(("parallel", "parallel", "arbitrary")),
        name="attention",
    )(qT, k, vT)


def _mem_kv_kernel(mem_ref, g_ref, w_ref, kT_ref, v_ref):
    D = mem_ref.shape[1]
    mn = _rms_rows(mem_ref[...], g_ref[...]).astype(BF16)
    kv = jnp.dot(mn, w_ref[...], preferred_element_type=F32)
    kT_ref[...] = kv[:, :D].T.astype(BF16)
    v_ref[...] = kv[:, D:].astype(BF16)


def _mem_kv(mem, g, w_kv):
    B, M, D = mem.shape
    L = w_kv.shape[0]
    return pl.pallas_call(
        _mem_kv_kernel,
        grid=(L, B),
        in_specs=[
            pl.BlockSpec((None, M, D), lambda l, b: (b, 0, 0)),
            pl.BlockSpec((1, D), lambda l, b: (0, 0)),
            pl.BlockSpec((None, D, 2 * D), lambda l, b: (l, 0, 0)),
        ],
        out_specs=[
            pl.BlockSpec((None, None, D, M), lambda l, b: (l, b, 0, 0)),
            pl.BlockSpec((None, None, M, D), lambda l, b: (l, b, 0, 0)),
        ],
        out_shape=[
            jax.ShapeDtypeStruct((L, B, D, M), BF16),
            jax.ShapeDtypeStruct((L, B, M, D), BF16),
        ],
        compiler_params=_cparams(("parallel", "parallel")),
        name="mem_kv",
    )(mem, g, w_kv)


def _post_kernel(x_ref, gm_ref, atT_ref, bg1_ref, wout_ref, xg_ref, wq_ref, mkT_ref, mv_ref, wo_ref,
                 fg_ref, wrh_ref, wrl_ref, x2_ref, h3_ref, affT_ref):
    D = x_ref.shape[1]
    xhd = D // X_HEADS

    at = atT_ref[...].astype(F32)
    ms = jnp.mean(at * at, axis=0, keepdims=True)
    atn = (at * lax.rsqrt(ms + EPS) * bg1_ref[...]).astype(BF16)
    y = jnp.dot(gm_ref[...], wout_ref[0:GM_WIDTH, :], preferred_element_type=F32)
    y = y + lax.dot_general(atn, wout_ref[GM_WIDTH:, :], (((0,), (0,)), ((), ())),
                            preferred_element_type=F32)
    x1 = x_ref[...] + y

    h2 = _rms_rows(x1, xg_ref[...]).astype(BF16)
    q2 = (jnp.dot(h2, wq_ref[...], preferred_element_type=F32) * (xhd ** -0.5)).astype(BF16)
    outs = []
    for hh in range(X_HEADS):
        cs = slice(hh * xhd, (hh + 1) * xhd)
        s = jnp.dot(q2[:, cs], mkT_ref[cs, :], preferred_element_type=F32)
        s = s - jnp.max(s, axis=-1, keepdims=True)
        p = jnp.exp(s)
        p = (p / jnp.sum(p, axis=-1, keepdims=True)).astype(BF16)
        outs.append(jnp.dot(p, mv_ref[:, cs], preferred_element_type=F32).astype(BF16))
    o2 = jnp.concatenate(outs, axis=1)
    x2 = x1 + jnp.dot(o2, wo_ref[...], preferred_element_type=F32)
    x2_ref[...] = x2

    h3 = _rms_rows(x2, fg_ref[...])
    hi = h3.astype(BF16)
    h3_ref[...] = hi
    lo = (h3 - hi.astype(F32)).astype(BF16)
    nt = (((1,), (1,)), ((), ()))
    lg = (lax.dot_general(wrh_ref[...], hi, nt, preferred_element_type=F32)
          + lax.dot_general(wrh_ref[...], lo, nt, preferred_element_type=F32)
          + lax.dot_general(wrl_ref[...], hi, nt, preferred_element_type=F32))
    lg = lg - jnp.max(lg, axis=0, keepdims=True)
    ex = jnp.exp(lg)
    affT_ref[...] = ex / jnp.sum(ex, axis=0, keepdims=True)


def _post(x2d, gm, atT, bg1, w_out, xg, w_q, mkT, mv, w_o, fg, wr_hi, wr_lo, *, layer, seq, tm):
    T, D = x2d.shape
    npb = seq // tm
    M = mv.shape[2]
    E = wr_hi.shape[0]
    full = lambda shape: pl.BlockSpec(shape, lambda i: (0,) * len(shape))
    return pl.pallas_call(
        _post_kernel,
        grid=(T // tm,),
        in_specs=[
            pl.BlockSpec((tm, D), lambda i: (i, 0)),
            pl.BlockSpec((tm, GM_WIDTH), lambda i: (i, 0)),
            pl.BlockSpec((None, ATT_WIDTH, tm), lambda i: (i // npb, 0, i % npb)),
            full((ATT_WIDTH, tm)),
            full((GM_WIDTH + ATT_WIDTH, D)),
            full((1, D)),
            full((D, D)),
            pl.BlockSpec((None, None, D, M), lambda i: (layer, i // npb, 0, 0)),
            pl.BlockSpec((None, None, M, D), lambda i: (layer, i // npb, 0, 0)),
            full((D, D)),
            full((1, D)),
            full((E, D)),
            full((E, D)),
        ],
        out_specs=[
            pl.BlockSpec((tm, D), lambda i: (i, 0)),
            pl.BlockSpec((tm, D), lambda i: (i, 0)),
            pl.BlockSpec((E, tm), lambda i: (0, i)),
        ],
        out_shape=[
            jax.ShapeDtypeStruct((T, D), F32),
            jax.ShapeDtypeStruct((T, D), BF16),
            jax.ShapeDtypeStruct((E, T), F32),
        ],
        compiler_params=_cparams(("parallel",)),
        name="post",
    )(x2d, gm, atT, bg1, w_out, xg, w_q, mkT, mv, w_o, fg, wr_hi, wr_lo)


def _topk_kernel(aff_ref, slot_ref, off_ref, *, cap):
    E, R, L = aff_ref.shape
    a = aff_ref[...]
    bits = lax.bitcast_convert_type(a, jnp.int32)

    def count_ge(t):
        c = jnp.where(bits >= t, 1.0, 0.0)
        return jnp.sum(jnp.sum(c, axis=2, keepdims=True), axis=1, keepdims=True)

    def bis(_, carry):
        lo, hi = carry
        mid = lo + ((hi - lo) >> 1)
        ok = count_ge(mid) >= cap
        return jnp.where(ok, mid, lo), jnp.where(ok, hi, mid)

    lo0 = jnp.zeros((E, 1, 1), jnp.int32)
    hi0 = jnp.full((E, 1, 1), 0x3F800001, jnp.int32)
    thr, _ = lax.fori_loop(0, 31, bis, (lo0, hi0))

    kk = lax.broadcasted_iota(jnp.int32, (L, L), 0)
    nn = lax.broadcasted_iota(jnp.int32, (L, L), 1)
    upper = jnp.where(kk <= nn, 1.0, 0.0).astype(BF16)
    ones = jnp.ones((L, L), BF16)
    rr = lax.broadcasted_iota(jnp.int32, (E * R, E * R), 0)
    cc = lax.broadcasted_iota(jnp.int32, (E * R, E * R), 1)
    lower = jnp.where((rr // R == cc // R) & (cc < rr), 1.0, 0.0).astype(BF16)

    def prefix(xf):
        x2 = xf.reshape(E * R, L)
        xb = x2.astype(BF16)
        incl = jnp.dot(xb, upper, preferred_element_type=F32)
        tot = jnp.dot(xb, ones, preferred_element_type=F32).astype(BF16)
        rowoff = jnp.dot(lower, tot, preferred_element_type=F32)
        return (incl - x2 + rowoff).reshape(E, R, L), rowoff.reshape(E, R, L)

    gt = jnp.where(bits > thr, 1.0, 0.0)
    eq = jnp.where(bits == thr, 1.0, 0.0)
    n_gt = jnp.sum(jnp.sum(gt, axis=2, keepdims=True), axis=1, keepdims=True)
    need = cap - n_gt
    eq_rank, _ = prefix(eq)
    sel = gt + eq * jnp.where(eq_rank < need, 1.0, 0.0)
    pos, rowoff = prefix(sel)
    slot_ref[...] = jnp.where(sel > 0.0, pos, -1.0).astype(jnp.int32)
    off_ref[...] = rowoff.astype(jnp.int32)


def _topk(aff4, *, cap):
    E, B, R, L = aff4.shape
    return pl.pallas_call(
        functools.partial(_topk_kernel, cap=cap),
        grid=(B,),
        in_specs=[pl.BlockSpec((E, None, R, L), lambda b: (0, b, 0, 0))],
        out_specs=[
            pl.BlockSpec((None, E, R, L), lambda b: (b, 0, 0, 0)),
            pl.BlockSpec((None, E, R, L), lambda b: (b, 0, 0, 0)),
        ],
        out_shape=[
            jax.ShapeDtypeStruct((B, E, R, L), jnp.int32),
            jax.ShapeDtypeStruct((B, E, R, L), jnp.int32),
        ],
        compiler_params=_cparams(("parallel",)),
        name="topk",
    )(aff4)


def _window_plan(cnt_ref, idx, win, cap):
    s0 = cnt_ref[idx]
    s1 = cnt_ref[idx + 1]
    a0 = (s0 // 16) * 16
    nw = jnp.where(s1 > s0, (s1 - a0 + win - 1) // win, 0)
    return a0, nw


def _gather_kernel(cnt_ref, slot_ref, h_ref, xs_ref, acc_scr, *, win):
    b = pl.program_id(0)
    e = pl.program_id(1)
    ne = pl.num_programs(1)
    cap, D = xs_ref.shape
    nblk = slot_ref.shape[0]
    acc_scr[...] = jnp.zeros(acc_scr.shape, F32)
    r_iota = lax.broadcasted_iota(jnp.int32, (win, TOK_BLOCK), 0)

    def blk(j, carry):
        a0, nw = _window_plan(cnt_ref, (b * ne + e) * (nblk + 1) + j, win, cap)
        srow = slot_ref[j]
        hb = h_ref[pl.ds(pl.multiple_of(j * TOK_BLOCK, TOK_BLOCK), TOK_BLOCK), :]

        def wbody(w, c2):
            lo = a0 + w * win
            sa = pl.multiple_of(jnp.minimum(lo, cap - win), 16)
            rel = jnp.where(srow >= lo, srow - sa, -1)
            oh = jnp.where(rel == r_iota, 1.0, 0.0).astype(BF16)
            acc_scr[pl.ds(sa, win), :] += jnp.dot(oh, hb, preferred_element_type=F32)
            return c2

        lax.fori_loop(0, nw, wbody, 0)
        return carry

    lax.fori_loop(0, nblk, blk, 0)
    xs_ref[...] = acc_scr[...].astype(BF16)


def _gather(cnt, slots5, h3d, *, cap, win):
    B, seq, D = h3d.shape
    E = slots5.shape[1]
    nblk = seq // TOK_BLOCK
    gs = pltpu.PrefetchScalarGridSpec(
        num_scalar_prefetch=1,
        grid=(B, E),
        in_specs=[
            pl.BlockSpec((None, None, nblk, 1, TOK_BLOCK), lambda b, e, c: (b, e, 0, 0, 0)),
            pl.BlockSpec((None, seq, D), lambda b, e, c: (b, 0, 0), pipeline_mode=pl.Buffered(1)),
        ],
        out_specs=pl.BlockSpec((None, None, cap, D), lambda b, e, c: (b, e, 0, 0)),
        scratch_shapes=[pltpu.VMEM((cap, D), F32)],
    )
    return pl.pallas_call(
        functools.partial(_gather_kernel, win=win),
        grid_spec=gs,
        out_shape=jax.ShapeDtypeStruct((B, E, cap, D), BF16),
        compiler_params=_cparams(("parallel", "arbitrary")),
        name="gather",
    )(cnt, slots5, h3d)


def _ffn_kernel(xs_ref, wg_ref, wu_ref, wd_ref, y_ref):
    xs = xs_ref[...]
    a = jnp.dot(xs, wg_ref[...], preferred_element_type=F32)
    u = jnp.dot(xs, wu_ref[...], preferred_element_type=F32)
    hmid = (a * jax.nn.sigmoid(a) * u).astype(BF16)
    y_ref[...] = jnp.dot(hmid, wd_ref[...], preferred_element_type=F32).astype(BF16)


def _ffn(xs, w_gate, w_up, w_down, *, layer):
    B, E, cap, D = xs.shape
    Fd = w_gate.shape[-1]
    return pl.pallas_call(
        _ffn_kernel,
        grid=(E, B),
        in_specs=[
            pl.BlockSpec((None, None, cap, D), lambda e, b: (b, e, 0, 0)),
            pl.BlockSpec((None, None, D, Fd), lambda e, b: (layer, e, 0, 0)),
            pl.BlockSpec((None, None, D, Fd), lambda e, b: (layer, e, 0, 0)),
            pl.BlockSpec((None, None, Fd, D), lambda e, b: (layer, e, 0, 0)),
        ],
        out_specs=pl.BlockSpec((None, None, cap, D), lambda e, b: (b, e, 0, 0)),
        out_shape=jax.ShapeDtypeStruct((B, E, cap, D), BF16),
        compiler_params=_cparams(("parallel", "parallel")),
        name="ffn",
    )(xs, w_gate, w_up, w_down)


def _scatter_kernel(cnt_ref, slot_ref, gate_ref, y_ref, x_ref, fg_ref, o_ref, *, win, final_norm):
    b = pl.program_id(0)
    sq = pl.program_id(1)
    e = pl.program_id(2)
    ne = pl.num_programs(2)
    cap = y_ref.shape[0]
    nloc = slot_ref.shape[0]
    nblk = nloc * pl.num_programs(1)
    r_iota = lax.broadcasted_iota(jnp.int32, (win, TOK_BLOCK), 0)

    @pl.when(e == 0)
    def _():
        o_ref[...] = x_ref[...]

    def blk(jj, carry):
        j = sq * nloc + jj
        a0, nw = _window_plan(cnt_ref, (b * ne + e) * (nblk + 1) + j, win, cap)
        srow = slot_ref[jj]
        grow = gate_ref[jj]
        rows = pl.ds(pl.multiple_of(jj * TOK_BLOCK, TOK_BLOCK), TOK_BLOCK)

        def wbody(w, c2):
            lo = a0 + w * win
            sa = pl.multiple_of(jnp.minimum(lo, cap - win), 16)
            rel = jnp.where(srow >= lo, srow - sa, -1)
            gm = jnp.where(rel == r_iota, grow, 0.0).astype(BF16)
            yw = y_ref[pl.ds(sa, win), :]
            o_ref[rows, :] += lax.dot_general(gm, yw, (((0,), (0,)), ((), ())),
                                              preferred_element_type=F32)
            return c2

        lax.fori_loop(0, nw, wbody, 0)
        return carry

    lax.fori_loop(0, nloc, blk, 0)

    if final_norm:
        @pl.when(e == ne - 1)
        def _():
            o_ref[...] = _rms_rows(o_ref[...], fg_ref[...])


def _scatter(cnt, slots5, gates5, y, x3d, fg, *, win, sq_rows, final_norm):
    B, seq, D = x3d.shape
    E, cap = y.shape[1], y.shape[2]
    nloc = sq_rows // TOK_BLOCK
    gs = pltpu.PrefetchScalarGridSpec(
        num_scalar_prefetch=1,
        grid=(B, seq // sq_rows, E),
        in_specs=[
            pl.BlockSpec((None, None, nloc, 1, TOK_BLOCK), lambda b, s, e, c: (b, e, s, 0, 0)),
            pl.BlockSpec((None, None, nloc, 1, TOK_BLOCK), lambda b, s, e, c: (e, b, s, 0, 0)),
            pl.BlockSpec((None, None, cap, D), lambda b, s, e, c: (b, e, 0, 0)),
            pl.BlockSpec((None, sq_rows, D), lambda b, s, e, c: (b, s, 0), pipeline_mode=pl.Buffered(1)),
            pl.BlockSpec((1, D), lambda b, s, e, c: (0, 0)),
        ],
        out_specs=pl.BlockSpec((None, sq_rows, D), lambda b, s, e, c: (b, s, 0)),
    )
    return pl.pallas_call(
        functools.partial(_scatter_kernel, win=win, final_norm=final_norm),
        grid_spec=gs,
        out_shape=jax.ShapeDtypeStruct((B, seq, D), F32),
        compiler_params=_cparams(("parallel", "parallel", "arbitrary")),
        name="scatter",
    )(cnt, slots5, gates5, y, x3d, fg)


def _rope_tables_T(seq):
    rows = seq // GRID_W
    row_id = jnp.repeat(jnp.arange(rows, dtype=F32), GRID_W)
    col_id = jnp.tile(jnp.arange(GRID_W, dtype=F32), rows)
    n_pairs = HEAD_DIM // 4
    freqs = jnp.exp(-math.log(ROPE_THETA) * jnp.arange(n_pairs, dtype=F32) / n_pairs)
    ang = jnp.concatenate([freqs[:, None] * row_id[None, :], freqs[:, None] * col_id[None, :]], axis=0)
    return jnp.cos(ang), jnp.sin(ang)


def _head_perm(n_heads):
    base = jnp.concatenate([jnp.arange(0, HEAD_DIM, 2), jnp.arange(1, HEAD_DIM, 2)])
    return (jnp.arange(n_heads)[:, None] * HEAD_DIM + base[None, :]).reshape(-1)


def kernel(x, mem, mix_norm_g, w_in, gm_v_norm_g, gm_w_s, gm_b_s, q_norm_g, k_norm_g, branch_norm_g, w_out,
           xattn_norm_g, mem_norm_g, xattn_w_q, xattn_w_kv, xattn_w_o, ffn_norm_g, w_router, w_gate, w_up,
           w_down, final_norm_g):
    B, seq, D = x.shape
    L = w_in.shape[0]
    E = w_router.shape[-1]
    T = B * seq
    cap = EC_FACTOR * seq // E
    tm = min(512, seq)
    tq = min(256, seq)
    tk = min(512, seq)
    win = min(128, cap)
    sq_rows = min(2048, seq)
    nblk = seq // TOK_BLOCK

    cosT, sinT = _rope_tables_T(seq)
    o_q = 2 * GM_WIDTH
    o_k = o_q + ATT_WIDTH
    o_v = o_k + KV_WIDTH
    cols = jnp.concatenate([jnp.arange(o_q), o_q + _head_perm(N_Q_HEADS), o_k + _head_perm(N_KV_HEADS),
                            jnp.arange(o_v, o_v + KV_WIDTH)])
    hp = _head_perm(1)

    mkT_all, mv_all = _mem_kv(mem, mem_norm_g.reshape(1, D), xattn_w_kv.astype(BF16))
    w_gate_b = w_gate.astype(BF16)
    w_up_b = w_up.astype(BF16)
    w_down_b = w_down.astype(BF16)

    x2d = x.reshape(T, D)
    out = None
    for l in range(L):
        w_in_l = w_in[l][:, cols].astype(BF16)
        bs = jnp.broadcast_to(gm_b_s[l][:, :, None], (GM_GROUPS, CHUNK, LANES))
        qg = jnp.broadcast_to(q_norm_g[l][hp][:, None], (HEAD_DIM, tm))
        kg = jnp.broadcast_to(k_norm_g[l][hp][:, None], (HEAD_DIM, tm))
        gm, qT, k, vT = _mixer_in(
            x2d, mix_norm_g[l].reshape(1, D), w_in_l, gm_v_norm_g[l].reshape(1, GM_WIDTH),
            gm_w_s[l].astype(BF16), bs, qg, kg, branch_norm_g[l, 0].reshape(1, GM_WIDTH), cosT, sinT,
            seq=seq, tm=tm)
        atT = _attention(qT, k.reshape(B, seq, KV_WIDTH), vT, tq=tq, tk=tk)

        bg1 = jnp.broadcast_to(branch_norm_g[l, 1][:, None], (ATT_WIDTH, tm))
        wr = w_router[l].T
        wr_hi = wr.astype(BF16)
        wr_lo = (wr - wr_hi.astype(F32)).astype(BF16)
        x2, h3, affT = _post(
            x2d, gm, atT, bg1, w_out[l].astype(BF16), xattn_norm_g[l].reshape(1, D),
            xattn_w_q[l].astype(BF16), mkT_all, mv_all, xattn_w_o[l].astype(BF16),
            ffn_norm_g[l].reshape(1, D), wr_hi, wr_lo, layer=l, seq=seq, tm=tm)

        slots, offs = _topk(affT.reshape(E, B, seq // LANES, LANES), cap=cap)
        cnt = offs[:, :, ::TOK_BLOCK // LANES, 0]
        cnt = jnp.concatenate([cnt, jnp.full((B, E, 1), cap, jnp.int32)], axis=-1).reshape(-1)
        slots5 = slots.reshape(B, E, nblk, 1, TOK_BLOCK)
        gates5 = affT.reshape(E, B, nblk, 1, TOK_BLOCK)

        xs = _gather(cnt, slots5, h3.reshape(B, seq, D), cap=cap, win=win)
        y = _ffn(xs, w_gate_b, w_up_b, w_down_b, layer=l)
        out = _scatter(cnt, slots5, gates5, y, x2.reshape(B, seq, D), final_norm_g.reshape(1, D),
                       win=win, sq_rows=sq_rows, final_norm=(l == L - 1))
        x2d = out.reshape(T, D)
    return out
```

```python
import functools
import math

import jax
import jax.numpy as jnp
from jax import lax
from jax.experimental import pallas as pl
from jax.experimental.pallas import tpu as pltpu

F32 = jnp.float32
BF16 = jnp.bfloat16

EPS = 1e-6
CHUNK = 128
GM_GROUPS = 4
GM_WIDTH = 512
HEAD_DIM = 64
N_Q_HEADS = 8
N_KV_HEADS = 2
Q_PER_KV = N_Q_HEADS // N_KV_HEADS
ATT_WIDTH = N_Q_HEADS * HEAD_DIM
KV_WIDTH = N_KV_HEADS * HEAD_DIM
ROPE_THETA = 10000.0
GRID_W = 64
X_HEADS = 4
N_EXPERTS = 16
EC_FACTOR = 2

LANES = 128
VMEM_LIMIT = 56 * 1024 * 1024

TOK_BLOCK = 256
LOG2E = 1.4426950408889634


def _cparams(sem, vmem=VMEM_LIMIT, **kw):
    return pltpu.CompilerParams(dimension_semantics=sem, vmem_limit_bytes=vmem, **kw)


def _rms_rows(x, g):
    ms = jnp.mean(x * x, axis=-1, keepdims=True)
    return x * lax.rsqrt(ms + EPS) * g


def _gelu_tanh(x):
    c = math.sqrt(2.0 / math.pi)
    return 0.5 * x * (1.0 + jnp.tanh(c * (x + 0.044715 * (x * x * x))))


def _norm_rope_T(xT, gcol, cosT, sinT, n_heads, scale):
    half = HEAD_DIM // 2
    outs = []
    for h in range(n_heads):
        blk = xT[h * HEAD_DIM:(h + 1) * HEAD_DIM, :]
        ms = jnp.mean(blk * blk, axis=0, keepdims=True)
        n = blk * lax.rsqrt(ms + EPS) * gcol
        e = n[:half, :]
        o = n[half:, :]
        re = (e * cosT - o * sinT) * scale
        ro = (e * sinT + o * cosT) * scale
        outs.append(jnp.concatenate([re, ro], axis=0))
    return outs


def _mixer_in_kernel(x_ref, g_ref, w_ref, vg_ref, ws_ref, bs_ref, qg_ref, kg_ref, bg0_ref,
                     cos_ref, sin_ref, gm_ref, qT_ref, k_ref, vT_ref, gm_scr):
    tm = x_ref.shape[0]
    x = x_ref[...]
    h = _rms_rows(x, g_ref[...]).astype(BF16)
    proj = jnp.dot(h, w_ref[...], preferred_element_type=F32)

    u = _gelu_tanh(proj[:, :GM_WIDTH])
    v = _gelu_tanh(proj[:, GM_WIDTH:2 * GM_WIDTH])
    vn = _rms_rows(v, vg_ref[...]).astype(BF16)
    for c in range(tm // CHUNK):
        rs = slice(c * CHUNK, (c + 1) * CHUNK)
        for g in range(GM_GROUPS):
            cs = slice(g * LANES, (g + 1) * LANES)
            mixed = jnp.dot(ws_ref[g], vn[rs, cs], preferred_element_type=F32) + bs_ref[g]
            gm_scr[rs, cs] = u[rs, cs] * mixed
    gm_ref[...] = _rms_rows(gm_scr[...], bg0_ref[...]).astype(BF16)

    o_q = 2 * GM_WIDTH
    o_k = o_q + ATT_WIDTH
    o_v = o_k + KV_WIDTH
    cosT = cos_ref[...]
    sinT = sin_ref[...]

    qT = proj[:, o_q:o_k].T
    q_heads = _norm_rope_T(qT, qg_ref[...], cosT, sinT, N_Q_HEADS, (HEAD_DIM ** -0.5) * LOG2E)
    for hh in range(N_Q_HEADS):
        qT_ref[hh * HEAD_DIM:(hh + 1) * HEAD_DIM, :] = q_heads[hh].astype(BF16)

    kT = proj[:, o_k:o_v].T
    k_heads = _norm_rope_T(kT, kg_ref[...], cosT, sinT, N_KV_HEADS, 1.0)
    k_ref[...] = jnp.concatenate(k_heads, axis=0).T.astype(BF16)

    vT_ref[...] = proj[:, o_v:].T.astype(BF16)


def _mixer_in(x2d, g, w_in, vg, ws, bs, qg, kg, bg0, cosT, sinT, *, seq, tm):
    T, D = x2d.shape
    B = T // seq
    npb = seq // tm
    in_w = w_in.shape[1]
    full = lambda shape: pl.BlockSpec(shape, lambda i: (0,) * len(shape))
    return pl.pallas_call(
        _mixer_in_kernel,
        grid=(T // tm,),
        in_specs=[
            pl.BlockSpec((tm, D), lambda i: (i, 0)),
            full((1, D)),
            full((D, in_w)),
            full((1, GM_WIDTH)),
            full((GM_GROUPS, CHUNK, CHUNK)),
            full((GM_GROUPS, CHUNK, LANES)),
            full((HEAD_DIM, tm)),
            full((HEAD_DIM, tm)),
            full((1, GM_WIDTH)),
            pl.BlockSpec((HEAD_DIM // 2, tm), lambda i: (0, i % npb)),
            pl.BlockSpec((HEAD_DIM // 2, tm), lambda i: (0, i % npb)),
        ],
        out_specs=[
            pl.BlockSpec((tm, GM_WIDTH), lambda i: (i, 0)),
            pl.BlockSpec((None, ATT_WIDTH, tm), lambda i: (i // npb, 0, i % npb)),
            pl.BlockSpec((tm, KV_WIDTH), lambda i: (i, 0)),
            pl.BlockSpec((None, KV_WIDTH, tm), lambda i: (i // npb, 0, i % npb)),
        ],
        out_shape=[
            jax.ShapeDtypeStruct((T, GM_WIDTH), BF16),
            jax.ShapeDtypeStruct((B, ATT_WIDTH, seq), BF16),
            jax.ShapeDtypeStruct((T, KV_WIDTH), BF16),
            jax.ShapeDtypeStruct((B, KV_WIDTH, seq), BF16),
        ],
        scratch_shapes=[pltpu.VMEM((tm, GM_WIDTH), F32)],
        compiler_params=_cparams(("parallel",)),
        name="mixer_in",
    )(x2d, g, w_in, vg, ws, bs, qg, kg, bg0, cosT, sinT)


def _attn_kernel(qT_ref, k_ref, vT_ref, o_ref, vext_scr, qx_scr, m_scr, acc_scr, s_scr, cm_scr, *, tk, qb):
    kh = pl.program_id(1)
    i = pl.program_id(2)
    seq = k_ref.shape[0]
    tq = qT_ref.shape[1]
    M = Q_PER_KV * tq

    @pl.when(i == 0)
    def _():
        vext_scr[0:HEAD_DIM, :] = vT_ref[...]
        row = lax.broadcasted_iota(jnp.int32, (HEAD_DIM, seq), 0)
        vext_scr[HEAD_DIM:, :] = jnp.where(row == 0, 1.0, 0.0).astype(BF16)

    q4 = jnp.concatenate([qT_ref[g * HEAD_DIM:(g + 1) * HEAD_DIM, :] for g in range(Q_PER_KV)], axis=1)
    z = jnp.zeros_like(q4)
    qx_scr[...] = jnp.where(kh == 0, jnp.concatenate([q4, z], axis=0), jnp.concatenate([z, q4], axis=0))

    m_scr[...] = jnp.full(m_scr.shape, -jnp.inf, F32)
    acc_scr[...] = jnp.zeros(acc_scr.shape, F32)

    nchunk = seq // tk

    def step(c, cur, do_qk, do_pv):
        nxt = 1 - cur
        if do_qk:
            kn = k_ref[pl.ds(pl.multiple_of((c + 1) * tk, tk), tk), :]
        if do_pv:
            ve = vext_scr[:, pl.ds(pl.multiple_of(c * tk, tk), tk)]
        for j in range(M // qb):
            cs = slice(j * qb, (j + 1) * qb)
            if do_qk:
                s = jnp.dot(kn, qx_scr[:, cs], preferred_element_type=F32)
                s_scr[nxt, :, cs] = s
                cm_scr[nxt, :, cs] = jnp.max(s, axis=0, keepdims=True)
            if do_pv:
                m_old = m_scr[:, cs]
                m_new = jnp.maximum(m_old, cm_scr[cur, :, cs])
                p = jnp.exp2(s_scr[cur, :, cs] - m_new).astype(BF16)
                alpha = jnp.exp2(m_old - m_new)
                pv = jnp.dot(ve, p, preferred_element_type=F32)
                acc_scr[:, cs] = acc_scr[:, cs] * alpha + pv
                m_scr[:, cs] = m_new

    step(-1, 1, True, False)

    def body(i, carry):
        step(2 * i, 0, True, True)
        step(2 * i + 1, 1, True, True)
        return carry

    lax.fori_loop(0, nchunk // 2 - 1, body, 0)
    step(nchunk - 2, 0, True, True)
    step(nchunk - 1, 1, False, True)

    acc = acc_scr[...]
    inv = 1.0 / acc[HEAD_DIM:HEAD_DIM + 1, :]
    o = acc[0:HEAD_DIM, :] * inv
    for g in range(Q_PER_KV):
        o_ref[g * HEAD_DIM:(g + 1) * HEAD_DIM, :] = o[:, g * tq:(g + 1) * tq].astype(BF16)


def _attention(qT, k, vT, *, tq, tk, qb):
    B, _, seq = qT.shape
    gw = Q_PER_KV * HEAD_DIM
    return pl.pallas_call(
        functools.partial(_attn_kernel, tk=tk, qb=qb),
        grid=(B, N_KV_HEADS, seq // tq),
        in_specs=[
            pl.BlockSpec((None, gw, tq), lambda b, kh, i: (b, kh, i)),
            pl.BlockSpec((None, seq, KV_WIDTH), lambda b, kh, i: (b, 0, 0)),
            pl.BlockSpec((None, HEAD_DIM, seq), lambda b, kh, i: (b, kh, 0)),
        ],
        out_specs=pl.BlockSpec((None, gw, tq), lambda b, kh, i: (b, kh, i)),
        out_shape=jax.ShapeDtypeStruct((B, ATT_WIDTH, seq), BF16),
        scratch_shapes=[
            pltpu.VMEM((2 * HEAD_DIM, seq), BF16),
            pltpu.VMEM((2 * HEAD_DIM, Q_PER_KV * tq), BF16),
            pltpu.VMEM((1, Q_PER_KV * tq), F32),
            pltpu.VMEM((2 * HEAD_DIM, Q_PER_KV * tq), F32),
            pltpu.VMEM((2, tk, Q_PER_KV * tq), F32),
            pltpu.VMEM((2, 1, Q_PER_KV * tq), F32),
        ],
        compiler_params=_cparams(("parallel", "parallel", "arbitrary")),
        name="attention",
    )(qT, k, vT)


def _mem_kv_kernel(mem_ref, g_ref, w_ref, kT_ref, v_ref):
    D = mem_ref.shape[1]
    mn = _rms_rows(mem_ref[...], g_ref[...]).astype(BF16)
    kv = jnp.dot(mn, w_ref[...], preferred_element_type=F32)
    kT_ref[...] = kv[:, :D].T.astype(BF16)
    v_ref[...] = kv[:, D:].astype(BF16)


def _mem_kv(mem, g, w_kv):
    B, M, D = mem.shape
    L = w_kv.shape[0]
    return pl.pallas_call(
        _mem_kv_kernel,
        grid=(L, B),
        in_specs=[
            pl.BlockSpec((None, M, D), lambda l, b: (b, 0, 0)),
            pl.BlockSpec((1, D), lambda l, b: (0, 0)),
            pl.BlockSpec((None, D, 2 * D), lambda l, b: (l, 0, 0)),
        ],
        out_specs=[
            pl.BlockSpec((None, None, D, M), lambda l, b: (l, b, 0, 0)),
            pl.BlockSpec((None, None, M, D), lambda l, b: (l, b, 0, 0)),
        ],
        out_shape=[
            jax.ShapeDtypeStruct((L, B, D, M), BF16),
            jax.ShapeDtypeStruct((L, B, M, D), BF16),
        ],
        compiler_params=_cparams(("parallel", "parallel")),
        name="mem_kv",
    )(mem, g, w_kv)


def _post_kernel(x_ref, gm_ref, atT_ref, bg1_ref, wout_ref, xg_ref, wq_ref, mkT_ref, mv_ref, wo_ref,
                 fg_ref, wrh_ref, wrl_ref, x2_ref, h3_ref, affT_ref):
    D = x_ref.shape[1]
    xhd = D // X_HEADS

    at = atT_ref[...].astype(F32)
    ms = jnp.mean(at * at, axis=0, keepdims=True)
    atn = (at * lax.rsqrt(ms + EPS) * bg1_ref[...]).astype(BF16)
    y = jnp.dot(gm_ref[...], wout_ref[0:GM_WIDTH, :], preferred_element_type=F32)
    y = y + lax.dot_general(atn, wout_ref[GM_WIDTH:, :], (((0,), (0,)), ((), ())),
                            preferred_element_type=F32)
    x1 = x_ref[...] + y

    h2 = _rms_rows(x1, xg_ref[...]).astype(BF16)
    q2 = (jnp.dot(h2, wq_ref[...], preferred_element_type=F32) * (xhd ** -0.5)).astype(BF16)
    outs = []
    for hh in range(X_HEADS):
        cs = slice(hh * xhd, (hh + 1) * xhd)
        s = jnp.dot(q2[:, cs], mkT_ref[cs, :], preferred_element_type=F32)
        s = s - jnp.max(s, axis=-1, keepdims=True)
        p = jnp.exp(s)
        p = (p / jnp.sum(p, axis=-1, keepdims=True)).astype(BF16)
        outs.append(jnp.dot(p, mv_ref[:, cs], preferred_element_type=F32).astype(BF16))
    o2 = jnp.concatenate(outs, axis=1)
    x2 = x1 + jnp.dot(o2, wo_ref[...], preferred_element_type=F32)
    x2_ref[...] = x2

    h3 = _rms_rows(x2, fg_ref[...])
    hi = h3.astype(BF16)
    h3_ref[...] = hi
    lo = (h3 - hi.astype(F32)).astype(BF16)
    nt = (((1,), (1,)), ((), ()))
    lg = (lax.dot_general(wrh_ref[...], hi, nt, preferred_element_type=F32)
          + lax.dot_general(wrh_ref[...], lo, nt, preferred_element_type=F32)
          + lax.dot_general(wrl_ref[...], hi, nt, preferred_element_type=F32))
    lg = lg - jnp.max(lg, axis=0, keepdims=True)
    ex = jnp.exp(lg)
    affT_ref[...] = ex / jnp.sum(ex, axis=0, keepdims=True)


def _post(x2d, gm, atT, bg1, w_out, xg, w_q, mkT, mv, w_o, fg, wr_hi, wr_lo, *, layer, seq, tm):
    T, D = x2d.shape
    npb = seq // tm
    M = mv.shape[2]
    E = wr_hi.shape[0]
    full = lambda shape: pl.BlockSpec(shape, lambda i: (0,) * len(shape))
    return pl.pallas_call(
        _post_kernel,
        grid=(T // tm,),
        in_specs=[
            pl.BlockSpec((tm, D), lambda i: (i, 0)),
            pl.BlockSpec((tm, GM_WIDTH), lambda i: (i, 0)),
            pl.BlockSpec((None, ATT_WIDTH, tm), lambda i: (i // npb, 0, i % npb)),
            full((ATT_WIDTH, tm)),
            full((GM_WIDTH + ATT_WIDTH, D)),
            full((1, D)),
            full((D, D)),
            pl.BlockSpec((None, None, D, M), lambda i: (layer, i // npb, 0, 0)),
            pl.BlockSpec((None, None, M, D), lambda i: (layer, i // npb, 0, 0)),
            full((D, D)),
            full((1, D)),
            full((E, D)),
            full((E, D)),
        ],
        out_specs=[
            pl.BlockSpec((tm, D), lambda i: (i, 0)),
            pl.BlockSpec((tm, D), lambda i: (i, 0)),
            pl.BlockSpec((E, tm), lambda i: (0, i)),
        ],
        out_shape=[
            jax.ShapeDtypeStruct((T, D), F32),
            jax.ShapeDtypeStruct((T, D), BF16),
            jax.ShapeDtypeStruct((E, T), F32),
        ],
        compiler_params=_cparams(("parallel",)),
        name="post",
    )(x2d, gm, atT, bg1, w_out, xg, w_q, mkT, mv, w_o, fg, wr_hi, wr_lo)


def _topk_kernel(aff_ref, slot_ref, off_ref, *, cap):
    E, R, L = aff_ref.shape
    a = aff_ref[...]
    bits = lax.bitcast_convert_type(a, jnp.int32)

    def count_ge(t):
        c = jnp.where(bits >= t, 1.0, 0.0)
        return jnp.sum(jnp.sum(c, axis=2, keepdims=True), axis=1, keepdims=True)

    def bis(_, carry):
        lo, hi = carry
        mid = lo + ((hi - lo) >> 1)
        ok = count_ge(mid) >= cap
        return jnp.where(ok, mid, lo), jnp.where(ok, hi, mid)

    lo0 = jnp.zeros((E, 1, 1), jnp.int32)
    hi0 = jnp.full((E, 1, 1), 0x3F800001, jnp.int32)
    thr, _ = lax.fori_loop(0, 31, bis, (lo0, hi0))

    kk = lax.broadcasted_iota(jnp.int32, (L, L), 0)
    nn = lax.broadcasted_iota(jnp.int32, (L, L), 1)
    upper = jnp.where(kk <= nn, 1.0, 0.0).astype(BF16)
    ones = jnp.ones((L, L), BF16)
    rr = lax.broadcasted_iota(jnp.int32, (E * R, E * R), 0)
    cc = lax.broadcasted_iota(jnp.int32, (E * R, E * R), 1)
    lower = jnp.where((rr // R == cc // R) & (cc < rr), 1.0, 0.0).astype(BF16)

    def prefix(xf):
        x2 = xf.reshape(E * R, L)
        xb = x2.astype(BF16)
        incl = jnp.dot(xb, upper, preferred_element_type=F32)
        tot = jnp.dot(xb, ones, preferred_element_type=F32).astype(BF16)
        rowoff = jnp.dot(lower, tot, preferred_element_type=F32)
        return (incl - x2 + rowoff).reshape(E, R, L), rowoff.reshape(E, R, L)

    gt = jnp.where(bits > thr, 1.0, 0.0)
    eq = jnp.where(bits == thr, 1.0, 0.0)
    n_gt = jnp.sum(jnp.sum(gt, axis=2, keepdims=True), axis=1, keepdims=True)
    need = cap - n_gt
    eq_rank, _ = prefix(eq)
    sel = gt + eq * jnp.where(eq_rank < need, 1.0, 0.0)
    pos, rowoff = prefix(sel)
    slot_ref[...] = jnp.where(sel > 0.0, pos, -1.0).astype(jnp.int32)
    off_ref[...] = rowoff.astype(jnp.int32)


def _topk(aff4, *, cap):
    E, B, R, L = aff4.shape
    return pl.pallas_call(
        functools.partial(_topk_kernel, cap=cap),
        grid=(B,),
        in_specs=[pl.BlockSpec((E, None, R, L), lambda b: (0, b, 0, 0))],
        out_specs=[
            pl.BlockSpec((None, E, R, L), lambda b: (b, 0, 0, 0)),
            pl.BlockSpec((None, E, R, L), lambda b: (b, 0, 0, 0)),
        ],
        out_shape=[
            jax.ShapeDtypeStruct((B, E, R, L), jnp.int32),
            jax.ShapeDtypeStruct((B, E, R, L), jnp.int32),
        ],
        compiler_params=_cparams(("parallel",)),
        name="topk",
    )(aff4)


def _window_plan(cnt_ref, idx, win, cap):
    s0 = cnt_ref[idx]
    s1 = cnt_ref[idx + 1]
    a0 = (s0 // 16) * 16
    nw = jnp.where(s1 > s0, (s1 - a0 + win - 1) // win, 0)
    return a0, nw


def _gather_kernel(cnt_ref, slot_ref, h_ref, xs_ref, acc_scr, *, win):
    b = pl.program_id(0)
    e = pl.program_id(1)
    ne = pl.num_programs(1)
    cap, D = xs_ref.shape
    nblk = slot_ref.shape[0]
    acc_scr[...] = jnp.zeros(acc_scr.shape, F32)
    r_iota = lax.broadcasted_iota(jnp.int32, (win, TOK_BLOCK), 0)

    def blk(j, carry):
        a0, nw = _window_plan(cnt_ref, (b * ne + e) * (nblk + 1) + j, win, cap)
        srow = slot_ref[j]
        hb = h_ref[pl.ds(pl.multiple_of(j * TOK_BLOCK, TOK_BLOCK), TOK_BLOCK), :]

        def wbody(w, c2):
            lo = a0 + w * win
            sa = pl.multiple_of(jnp.minimum(lo, cap - win), 16)
            rel = jnp.where(srow >= lo, srow - sa, -1)
            oh = jnp.where(rel == r_iota, 1.0, 0.0).astype(BF16)
            acc_scr[pl.ds(sa, win), :] += jnp.dot(oh, hb, preferred_element_type=F32)
            return c2

        lax.fori_loop(0, nw, wbody, 0)
        return carry

    lax.fori_loop(0, nblk, blk, 0)
    xs_ref[...] = acc_scr[...].astype(BF16)


def _gather(cnt, slots5, h3d, *, cap, win):
    B, seq, D = h3d.shape
    E = slots5.shape[1]
    nblk = seq // TOK_BLOCK
    gs = pltpu.PrefetchScalarGridSpec(
        num_scalar_prefetch=1,
        grid=(B, E),
        in_specs=[
            pl.BlockSpec((None, None, nblk, 1, TOK_BLOCK), lambda b, e, c: (b, e, 0, 0, 0)),
            pl.BlockSpec((None, seq, D), lambda b, e, c: (b, 0, 0), pipeline_mode=pl.Buffered(1)),
        ],
        out_specs=pl.BlockSpec((None, None, cap, D), lambda b, e, c: (b, e, 0, 0)),
        scratch_shapes=[pltpu.VMEM((cap, D), F32)],
    )
    return pl.pallas_call(
        functools.partial(_gather_kernel, win=win),
        grid_spec=gs,
        out_shape=jax.ShapeDtypeStruct((B, E, cap, D), BF16),
        compiler_params=_cparams(("parallel", "arbitrary")),
        name="gather",
    )(cnt, slots5, h3d)


def _ffn_kernel(xs_ref, wg_ref, wu_ref, wd_ref, y_ref):
    xs = xs_ref[...]
    a = jnp.dot(xs, wg_ref[...], preferred_element_type=F32)
    u = jnp.dot(xs, wu_ref[...], preferred_element_type=F32)
    hmid = (a * jax.nn.sigmoid(a) * u).astype(BF16)
    y_ref[...] = jnp.dot(hmid, wd_ref[...], preferred_element_type=F32).astype(BF16)


def _ffn(xs, w_gate, w_up, w_down, *, layer):
    B, E, cap, D = xs.shape
    Fd = w_gate.shape[-1]
    return pl.pallas_call(
        _ffn_kernel,
        grid=(E, B),
        in_specs=[
            pl.BlockSpec((None, None, cap, D), lambda e, b: (b, e, 0, 0)),
            pl.BlockSpec((None, None, D, Fd), lambda e, b: (layer, e, 0, 0)),
            pl.BlockSpec((None, None, D, Fd), lambda e, b: (layer, e, 0, 0)),
            pl.BlockSpec((None, None, Fd, D), lambda e, b: (layer, e, 0, 0)),
        ],
        out_specs=pl.BlockSpec((None, None, cap, D), lambda e, b: (b, e, 0, 0)),
        out_shape=jax.ShapeDtypeStruct((B, E, cap, D), BF16),
        compiler_params=_cparams(("parallel", "parallel")),
        name="ffn",
    )(xs, w_gate, w_up, w_down)


def _scatter_kernel(cnt_ref, slot_ref, gate_ref, y_ref, x_ref, fg_ref, o_ref, *, win, final_norm):
    b = pl.program_id(0)
    sq = pl.program_id(1)
    e = pl.program_id(2)
    ne = pl.num_programs(2)
    cap = y_ref.shape[0]
    nloc = slot_ref.shape[0]
    nblk = nloc * pl.num_programs(1)
    r_iota = lax.broadcasted_iota(jnp.int32, (win, TOK_BLOCK), 0)

    @pl.when(e == 0)
    def _():
        o_ref[...] = x_ref[...]

    def blk(jj, carry):
        j = sq * nloc + jj
        a0, nw = _window_plan(cnt_ref, (b * ne + e) * (nblk + 1) + j, win, cap)
        srow = slot_ref[jj]
        grow = gate_ref[jj]
        rows = pl.ds(pl.multiple_of(jj * TOK_BLOCK, TOK_BLOCK), TOK_BLOCK)

        def wbody(w, c2):
            lo = a0 + w * win
            sa = pl.multiple_of(jnp.minimum(lo, cap - win), 16)
            rel = jnp.where(srow >= lo, srow - sa, -1)
            gm = jnp.where(rel == r_iota, grow, 0.0).astype(BF16)
            yw = y_ref[pl.ds(sa, win), :]
            o_ref[rows, :] += lax.dot_general(gm, yw, (((0,), (0,)), ((), ())),
                                              preferred_element_type=F32)
            return c2

        lax.fori_loop(0, nw, wbody, 0)
        return carry

    lax.fori_loop(0, nloc, blk, 0)

    if final_norm:
        @pl.when(e == ne - 1)
        def _():
            o_ref[...] = _rms_rows(o_ref[...], fg_ref[...])


def _scatter(cnt, slots5, gates5, y, x3d, fg, *, win, sq_rows, final_norm):
    B, seq, D = x3d.shape
    E, cap = y.shape[1], y.shape[2]
    nloc = sq_rows // TOK_BLOCK
    gs = pltpu.PrefetchScalarGridSpec(
        num_scalar_prefetch=1,
        grid=(B, seq // sq_rows, E),
        in_specs=[
            pl.BlockSpec((None, None, nloc, 1, TOK_BLOCK), lambda b, s, e, c: (b, e, s, 0, 0)),
            pl.BlockSpec((None, None, nloc, 1, TOK_BLOCK), lambda b, s, e, c: (e, b, s, 0, 0)),
            pl.BlockSpec((None, None, cap, D), lambda b, s, e, c: (b, e, 0, 0)),
            pl.BlockSpec((None, sq_rows, D), lambda b, s, e, c: (b, s, 0), pipeline_mode=pl.Buffered(1)),
            pl.BlockSpec((1, D), lambda b, s, e, c: (0, 0)),
        ],
        out_specs=pl.BlockSpec((None, sq_rows, D), lambda b, s, e, c: (b, s, 0)),
    )
    return pl.pallas_call(
        functools.partial(_scatter_kernel, win=win, final_norm=final_norm),
        grid_spec=gs,
        out_shape=jax.ShapeDtypeStruct((B, seq, D), F32),
        compiler_params=_cparams(("parallel", "parallel", "arbitrary")),
        name="scatter",
    )(cnt, slots5, gates5, y, x3d, fg)


def _rope_tables_T(seq):
    rows = seq // GRID_W
    row_id = jnp.repeat(jnp.arange(rows, dtype=F32), GRID_W)
    col_id = jnp.tile(jnp.arange(GRID_W, dtype=F32), rows)
    n_pairs = HEAD_DIM // 4
    freqs = jnp.exp(-math.log(ROPE_THETA) * jnp.arange(n_pairs, dtype=F32) / n_pairs)
    ang = jnp.concatenate([freqs[:, None] * row_id[None, :], freqs[:, None] * col_id[None, :]], axis=0)
    return jnp.cos(ang), jnp.sin(ang)


def _head_perm(n_heads):
    base = jnp.concatenate([jnp.arange(0, HEAD_DIM, 2), jnp.arange(1, HEAD_DIM, 2)])
    return (jnp.arange(n_heads)[:, None] * HEAD_DIM + base[None, :]).reshape(-1)


def kernel(x, mem, mix_norm_g, w_in, gm_v_norm_g, gm_w_s, gm_b_s, q_norm_g, k_norm_g, branch_norm_g, w_out,
           xattn_norm_g, mem_norm_g, xattn_w_q, xattn_w_kv, xattn_w_o, ffn_norm_g, w_router, w_gate, w_up,
           w_down, final_norm_g):
    B, seq, D = x.shape
    L = w_in.shape[0]
    E = w_router.shape[-1]
    T = B * seq
    cap = EC_FACTOR * seq // E
    tm = min(512, seq)
    tq = min(256, seq)
    tk = min(512, seq)
    win = min(128, cap)
    sq_rows = min(2048, seq)
    nblk = seq // TOK_BLOCK

    cosT, sinT = _rope_tables_T(seq)
    o_q = 2 * GM_WIDTH
    o_k = o_q + ATT_WIDTH
    o_v = o_k + KV_WIDTH
    cols = jnp.concatenate([jnp.arange(o_q), o_q + _head_perm(N_Q_HEADS), o_k + _head_perm(N_KV_HEADS),
                            jnp.arange(o_v, o_v + KV_WIDTH)])
    hp = _head_perm(1)

    mkT_all, mv_all = _mem_kv(mem, mem_norm_g.reshape(1, D), xattn_w_kv.astype(BF16))
    w_gate_b = w_gate.astype(BF16)
    w_up_b = w_up.astype(BF16)
    w_down_b = w_down.astype(BF16)

    x2d = x.reshape(T, D)
    out = None
    for l in range(L):
        w_in_l = w_in[l][:, cols].astype(BF16)
        bs = jnp.broadcast_to(gm_b_s[l][:, :, None], (GM_GROUPS, CHUNK, LANES))
        qg = jnp.broadcast_to(q_norm_g[l][hp][:, None], (HEAD_DIM, tm))
        kg = jnp.broadcast_to(k_norm_g[l][hp][:, None], (HEAD_DIM, tm))
        gm, qT, k, vT = _mixer_in(
            x2d, mix_norm_g[l].reshape(1, D), w_in_l, gm_v_norm_g[l].reshape(1, GM_WIDTH),
            gm_w_s[l].astype(BF16), bs, qg, kg, branch_norm_g[l, 0].reshape(1, GM_WIDTH), cosT, sinT,
            seq=seq, tm=tm)
        atT = _attention(qT, k.reshape(B, seq, KV_WIDTH), vT, tq=tq, tk=tk, qb=min(256, tq))

        bg1 = jnp.broadcast_to(branch_norm_g[l, 1][:, None], (ATT_WIDTH, tm))
        wr = w_router[l].T
        wr_hi = wr.astype(BF16)
        wr_lo = (wr - wr_hi.astype(F32)).astype(BF16)
        x2, h3, affT = _post(
            x2d, gm, atT, bg1, w_out[l].astype(BF16), xattn_norm_g[l].reshape(1, D),
            xattn_w_q[l].astype(BF16), mkT_all, mv_all, xattn_w_o[l].astype(BF16),
            ffn_norm_g[l].reshape(1, D), wr_hi, wr_lo, layer=l, seq=seq, tm=tm)

        slots, offs = _topk(affT.reshape(E, B, seq // LANES, LANES), cap=cap)
        cnt = offs[:, :, ::TOK_BLOCK // LANES, 0]
        cnt = jnp.concatenate([cnt, jnp.full((B, E, 1), cap, jnp.int32)], axis=-1).reshape(-1)
        slots5 = slots.reshape(B, E, nblk, 1, TOK_BLOCK)
        gates5 = affT.reshape(E, B, nblk, 1, TOK_BLOCK)

        xs = _gather(cnt, slots5, h3.reshape(B, seq, D), cap=cap, win=win)
        y = _ffn(xs, w_gate_b, w_up_b, w_down_b, layer=l)
        out = _scatter(cnt, slots5, gates5, y, x2.reshape(B, seq, D), final_norm_g.reshape(1, D),
                       win=win, sq_rows=sq_rows, final_norm=(l == L - 1))
        x2d = out.reshape(T, D)
    return out
```

```python
import functools
import math

import jax
import jax.numpy as jnp
from jax import lax
from jax.experimental import pallas as pl
from jax.experimental.pallas import tpu as pltpu

F32 = jnp.float32
BF16 = jnp.bfloat16

EPS = 1e-6
CHUNK = 128
GM_GROUPS = 4
GM_WIDTH = 512
HEAD_DIM = 64
N_Q_HEADS = 8
N_KV_HEADS = 2
Q_PER_KV = N_Q_HEADS // N_KV_HEADS
ATT_WIDTH = N_Q_HEADS * HEAD_DIM
KV_WIDTH = N_KV_HEADS * HEAD_DIM
ROPE_THETA = 10000.0
GRID_W = 64
X_HEADS = 4
N_EXPERTS = 16
EC_FACTOR = 2

LANES = 128
VMEM_LIMIT = 56 * 1024 * 1024

TOK_BLOCK = 256
LOG2E = 1.4426950408889634


def _cparams(sem, vmem=VMEM_LIMIT, **kw):
    return pltpu.CompilerParams(dimension_semantics=sem, vmem_limit_bytes=vmem, **kw)


def _rms_rows(x, g):
    ms = jnp.mean(x * x, axis=-1, keepdims=True)
    return x * lax.rsqrt(ms + EPS) * g


def _gelu_tanh(x):
    c = math.sqrt(2.0 / math.pi)
    return 0.5 * x * (1.0 + jnp.tanh(c * (x + 0.044715 * (x * x * x))))


def _norm_rope_T(xT, gcol, cosT, sinT, n_heads, scale):
    half = HEAD_DIM // 2
    outs = []
    for h in range(n_heads):
        blk = xT[h * HEAD_DIM:(h + 1) * HEAD_DIM, :]
        ms = jnp.mean(blk * blk, axis=0, keepdims=True)
        n = blk * lax.rsqrt(ms + EPS) * gcol
        e = n[:half, :]
        o = n[half:, :]
        re = (e * cosT - o * sinT) * scale
        ro = (e * sinT + o * cosT) * scale
        outs.append(jnp.concatenate([re, ro], axis=0))
    return outs


def _mixer_in_kernel(x_ref, g_ref, w_ref, vg_ref, ws_ref, bs_ref, qg_ref, kg_ref, bg0_ref,
                     cos_ref, sin_ref, gm_ref, qT_ref, k_ref, vT_ref, gm_scr):
    tm = x_ref.shape[0]
    x = x_ref[...]
    h = _rms_rows(x, g_ref[...]).astype(BF16)
    proj = jnp.dot(h, w_ref[...], preferred_element_type=F32)

    u = _gelu_tanh(proj[:, :GM_WIDTH])
    v = _gelu_tanh(proj[:, GM_WIDTH:2 * GM_WIDTH])
    vn = _rms_rows(v, vg_ref[...]).astype(BF16)
    for c in range(tm // CHUNK):
        rs = slice(c * CHUNK, (c + 1) * CHUNK)
        for g in range(GM_GROUPS):
            cs = slice(g * LANES, (g + 1) * LANES)
            mixed = jnp.dot(ws_ref[g], vn[rs, cs], preferred_element_type=F32) + bs_ref[g]
            gm_scr[rs, cs] = u[rs, cs] * mixed
    gm_ref[...] = _rms_rows(gm_scr[...], bg0_ref[...]).astype(BF16)

    o_q = 2 * GM_WIDTH
    o_k = o_q + ATT_WIDTH
    o_v = o_k + KV_WIDTH
    cosT = cos_ref[...]
    sinT = sin_ref[...]

    qT = proj[:, o_q:o_k].T
    q_heads = _norm_rope_T(qT, qg_ref[...], cosT, sinT, N_Q_HEADS, (HEAD_DIM ** -0.5) * LOG2E)
    for hh in range(N_Q_HEADS):
        qT_ref[hh * HEAD_DIM:(hh + 1) * HEAD_DIM, :] = q_heads[hh].astype(BF16)

    kT = proj[:, o_k:o_v].T
    k_heads = _norm_rope_T(kT, kg_ref[...], cosT, sinT, N_KV_HEADS, 1.0)
    k_ref[...] = jnp.concatenate(k_heads, axis=0).T.astype(BF16)

    vT_ref[...] = proj[:, o_v:].T.astype(BF16)


def _mixer_in(x2d, g, w_in, vg, ws, bs, qg, kg, bg0, cosT, sinT, *, seq, tm):
    T, D = x2d.shape
    B = T // seq
    npb = seq // tm
    in_w = w_in.shape[1]
    full = lambda shape: pl.BlockSpec(shape, lambda i: (0,) * len(shape))
    return pl.pallas_call(
        _mixer_in_kernel,
        grid=(T // tm,),
        in_specs=[
            pl.BlockSpec((tm, D), lambda i: (i, 0)),
            full((1, D)),
            full((D, in_w)),
            full((1, GM_WIDTH)),
            full((GM_GROUPS, CHUNK, CHUNK)),
            full((GM_GROUPS, CHUNK, LANES)),
            full((HEAD_DIM, tm)),
            full((HEAD_DIM, tm)),
            full((1, GM_WIDTH)),
            pl.BlockSpec((HEAD_DIM // 2, tm), lambda i: (0, i % npb)),
            pl.BlockSpec((HEAD_DIM // 2, tm), lambda i: (0, i % npb)),
        ],
        out_specs=[
            pl.BlockSpec((tm, GM_WIDTH), lambda i: (i, 0)),
            pl.BlockSpec((None, ATT_WIDTH, tm), lambda i: (i // npb, 0, i % npb)),
            pl.BlockSpec((tm, KV_WIDTH), lambda i: (i, 0)),
            pl.BlockSpec((None, KV_WIDTH, tm), lambda i: (i // npb, 0, i % npb)),
        ],
        out_shape=[
            jax.ShapeDtypeStruct((T, GM_WIDTH), BF16),
            jax.ShapeDtypeStruct((B, ATT_WIDTH, seq), BF16),
            jax.ShapeDtypeStruct((T, KV_WIDTH), BF16),
            jax.ShapeDtypeStruct((B, KV_WIDTH, seq), BF16),
        ],
        scratch_shapes=[pltpu.VMEM((tm, GM_WIDTH), F32)],
        compiler_params=_cparams(("parallel",)),
        name="mixer_in",
    )(x2d, g, w_in, vg, ws, bs, qg, kg, bg0, cosT, sinT)


def _attn_kernel(qT_ref, k_ref, vT_ref, o_ref, vext_scr, qx_scr, m_scr, acc_scr, s_scr, cm_scr, *, tk, qb):
    kh = pl.program_id(1)
    i = pl.program_id(2)
    seq = k_ref.shape[0]
    tq = qT_ref.shape[1]
    M = Q_PER_KV * tq

    @pl.when(i == 0)
    def _():
        vext_scr[0:HEAD_DIM, :] = vT_ref[...]
        row = lax.broadcasted_iota(jnp.int32, (HEAD_DIM, seq), 0)
        vext_scr[HEAD_DIM:, :] = jnp.where(row == 0, 1.0, 0.0).astype(BF16)

    q4 = jnp.concatenate([qT_ref[g * HEAD_DIM:(g + 1) * HEAD_DIM, :] for g in range(Q_PER_KV)], axis=1)
    z = jnp.zeros_like(q4)
    qx_scr[...] = jnp.where(kh == 0, jnp.concatenate([q4, z], axis=0), jnp.concatenate([z, q4], axis=0))

    m_scr[...] = jnp.full(m_scr.shape, -jnp.inf, F32)
    acc_scr[...] = jnp.zeros(acc_scr.shape, F32)

    nchunk = seq // tk

    def step(c, cur, do_qk, do_pv):
        nxt = 1 - cur
        if do_qk:
            kn = k_ref[pl.ds(pl.multiple_of((c + 1) * tk, tk), tk), :]
        if do_pv:
            ve = vext_scr[:, pl.ds(pl.multiple_of(c * tk, tk), tk)]
        for j in range(M // qb):
            cs = slice(j * qb, (j + 1) * qb)
            if do_qk:
                s = jnp.dot(kn, qx_scr[:, cs], preferred_element_type=F32)
                s_scr[nxt, :, cs] = s
                cm_scr[nxt, :, cs] = jnp.max(s, axis=0, keepdims=True)
            if do_pv:
                m_old = m_scr[:, cs]
                m_new = jnp.maximum(m_old, cm_scr[cur, :, cs])
                p = jnp.exp2(s_scr[cur, :, cs] - m_new).astype(BF16)
                alpha = jnp.exp2(m_old - m_new)
                pv = jnp.dot(ve, p, preferred_element_type=F32)
                acc_scr[:, cs] = acc_scr[:, cs] * alpha + pv
                m_scr[:, cs] = m_new

    step(-1, 1, True, False)

    def body(i, carry):
        step(2 * i, 0, True, True)
        step(2 * i + 1, 1, True, True)
        return carry

    lax.fori_loop(0, nchunk // 2 - 1, body, 0)
    step(nchunk - 2, 0, True, True)
    step(nchunk - 1, 1, False, True)

    acc = acc_scr[...]
    inv = 1.0 / acc[HEAD_DIM:HEAD_DIM + 1, :]
    o = acc[0:HEAD_DIM, :] * inv
    for g in range(Q_PER_KV):
        o_ref[g * HEAD_DIM:(g + 1) * HEAD_DIM, :] = o[:, g * tq:(g + 1) * tq].astype(BF16)


def _attention(qT, k, vT, *, tq, tk, qb):
    B, _, seq = qT.shape
    gw = Q_PER_KV * HEAD_DIM
    return pl.pallas_call(
        functools.partial(_attn_kernel, tk=tk, qb=qb),
        grid=(B, N_KV_HEADS, seq // tq),
        in_specs=[
            pl.BlockSpec((None, gw, tq), lambda b, kh, i: (b, kh, i)),
            pl.BlockSpec((None, seq, KV_WIDTH), lambda b, kh, i: (b, 0, 0)),
            pl.BlockSpec((None, HEAD_DIM, seq), lambda b, kh, i: (b, kh, 0)),
        ],
        out_specs=pl.BlockSpec((None, gw, tq), lambda b, kh, i: (b, kh, i)),
        out_shape=jax.ShapeDtypeStruct((B, ATT_WIDTH, seq), BF16),
        scratch_shapes=[
            pltpu.VMEM((2 * HEAD_DIM, seq), BF16),
            pltpu.VMEM((2 * HEAD_DIM, Q_PER_KV * tq), BF16),
            pltpu.VMEM((1, Q_PER_KV * tq), F32),
            pltpu.VMEM((2 * HEAD_DIM, Q_PER_KV * tq), F32),
            pltpu.VMEM((2, tk, Q_PER_KV * tq), F32),
            pltpu.VMEM((2, 1, Q_PER_KV * tq), F32),
        ],
        compiler_params=_cparams(("parallel", "parallel", "arbitrary")),
        name="attention",
    )(qT, k, vT)


def _mem_kv_kernel(mem_ref, g_ref, w_ref, kT_ref, v_ref):
    D = mem_ref.shape[1]
    mn = _rms_rows(mem_ref[...], g_ref[...]).astype(BF16)
    kv = jnp.dot(mn, w_ref[...], preferred_element_type=F32)
    kT_ref[...] = kv[:, :D].T.astype(BF16)
    v_ref[...] = kv[:, D:].astype(BF16)


def _mem_kv(mem, g, w_kv):
    B, M, D = mem.shape
    L = w_kv.shape[0]
    return pl.pallas_call(
        _mem_kv_kernel,
        grid=(L, B),
        in_specs=[
            pl.BlockSpec((None, M, D), lambda l, b: (b, 0, 0)),
            pl.BlockSpec((1, D), lambda l, b: (0, 0)),
            pl.BlockSpec((None, D, 2 * D), lambda l, b: (l, 0, 0)),
        ],
        out_specs=[
            pl.BlockSpec((None, None, D, M), lambda l, b: (l, b, 0, 0)),
            pl.BlockSpec((None, None, M, D), lambda l, b: (l, b, 0, 0)),
        ],
        out_shape=[
            jax.ShapeDtypeStruct((L, B, D, M), BF16),
            jax.ShapeDtypeStruct((L, B, M, D), BF16),
        ],
        compiler_params=_cparams(("parallel", "parallel")),
        name="mem_kv",
    )(mem, g, w_kv)


def _post_kernel(x_ref, gm_ref, atT_ref, bg1_ref, wout_ref, xg_ref, wq_ref, mkT_ref, mv_ref, wo_ref,
                 fg_ref, wrh_ref, wrl_ref, x2_ref, h3_ref, affT_ref):
    D = x_ref.shape[1]
    xhd = D // X_HEADS

    at = atT_ref[...].astype(F32)
    ms = jnp.mean(at * at, axis=0, keepdims=True)
    atn = (at * lax.rsqrt(ms + EPS) * bg1_ref[...]).astype(BF16)
    y = jnp.dot(gm_ref[...], wout_ref[0:GM_WIDTH, :], preferred_element_type=F32)
    y = y + lax.dot_general(atn, wout_ref[GM_WIDTH:, :], (((0,), (0,)), ((), ())),
                            preferred_element_type=F32)
    x1 = x_ref[...] + y

    h2 = _rms_rows(x1, xg_ref[...]).astype(BF16)
    q2 = (jnp.dot(h2, wq_ref[...], preferred_element_type=F32) * (xhd ** -0.5)).astype(BF16)
    outs = []
    for hh in range(X_HEADS):
        cs = slice(hh * xhd, (hh + 1) * xhd)
        s = jnp.dot(q2[:, cs], mkT_ref[cs, :], preferred_element_type=F32)
        s = s - jnp.max(s, axis=-1, keepdims=True)
        p = jnp.exp(s)
        p = (p / jnp.sum(p, axis=-1, keepdims=True)).astype(BF16)
        outs.append(jnp.dot(p, mv_ref[:, cs], preferred_element_type=F32).astype(BF16))
    o2 = jnp.concatenate(outs, axis=1)
    x2 = x1 + jnp.dot(o2, wo_ref[...], preferred_element_type=F32)
    x2_ref[...] = x2

    h3 = _rms_rows(x2, fg_ref[...])
    hi = h3.astype(BF16)
    h3_ref[...] = hi
    lo = (h3 - hi.astype(F32)).astype(BF16)
    nt = (((1,), (1,)), ((), ()))
    lg = (lax.dot_general(wrh_ref[...], hi, nt, preferred_element_type=F32)
          + lax.dot_general(wrh_ref[...], lo, nt, preferred_element_type=F32)
          + lax.dot_general(wrl_ref[...], hi, nt, preferred_element_type=F32))
    lg = lg - jnp.max(lg, axis=0, keepdims=True)
    ex = jnp.exp(lg)
    affT_ref[...] = ex / jnp.sum(ex, axis=0, keepdims=True)


def _post(x2d, gm, atT, bg1, w_out, xg, w_q, mkT, mv, w_o, fg, wr_hi, wr_lo, *, layer, seq, tm):
    T, D = x2d.shape
    npb = seq // tm
    M = mv.shape[2]
    E = wr_hi.shape[0]
    full = lambda shape: pl.BlockSpec(shape, lambda i: (0,) * len(shape))
    return pl.pallas_call(
        _post_kernel,
        grid=(T // tm,),
        in_specs=[
            pl.BlockSpec((tm, D), lambda i: (i, 0)),
            pl.BlockSpec((tm, GM_WIDTH), lambda i: (i, 0)),
            pl.BlockSpec((None, ATT_WIDTH, tm), lambda i: (i // npb, 0, i % npb)),
            full((ATT_WIDTH, tm)),
            full((GM_WIDTH + ATT_WIDTH, D)),
            full((1, D)),
            full((D, D)),
            pl.BlockSpec((None, None, D, M), lambda i: (layer, i // npb, 0, 0)),
            pl.BlockSpec((None, None, M, D), lambda i: (layer, i // npb, 0, 0)),
            full((D, D)),
            full((1, D)),
            full((E, D)),
            full((E, D)),
        ],
        out_specs=[
            pl.BlockSpec((tm, D), lambda i: (i, 0)),
            pl.BlockSpec((tm, D), lambda i: (i, 0)),
            pl.BlockSpec((E, tm), lambda i: (0, i)),
        ],
        out_shape=[
            jax.ShapeDtypeStruct((T, D), F32),
            jax.ShapeDtypeStruct((T, D), BF16),
            jax.ShapeDtypeStruct((E, T), F32),
        ],
        compiler_params=_cparams(("parallel",)),
        name="post",
    )(x2d, gm, atT, bg1, w_out, xg, w_q, mkT, mv, w_o, fg, wr_hi, wr_lo)


def _topk_kernel(aff_ref, slot_ref, off_ref, *, cap):
    E, R, L = aff_ref.shape
    a = aff_ref[...]
    bits = lax.bitcast_convert_type(a, jnp.int32)

    def count_ge(t):
        c = jnp.where(bits >= t, 1.0, 0.0)
        return jnp.sum(jnp.sum(c, axis=2, keepdims=True), axis=1, keepdims=True)

    def bis(_, carry):
        lo, hi = carry
        mid = lo + ((hi - lo) >> 1)
        ok = count_ge(mid) >= cap
        return jnp.where(ok, mid, lo), jnp.where(ok, hi, mid)

    lo0 = jnp.zeros((E, 1, 1), jnp.int32)
    hi0 = jnp.full((E, 1, 1), 0x3F800001, jnp.int32)
    thr, _ = lax.fori_loop(0, 31, bis, (lo0, hi0))

    kk = lax.broadcasted_iota(jnp.int32, (L, L), 0)
    nn = lax.broadcasted_iota(jnp.int32, (L, L), 1)
    upper = jnp.where(kk <= nn, 1.0, 0.0).astype(BF16)
    ones = jnp.ones((L, L), BF16)
    rr = lax.broadcasted_iota(jnp.int32, (E * R, E * R), 0)
    cc = lax.broadcasted_iota(jnp.int32, (E * R, E * R), 1)
    lower = jnp.where((rr // R == cc // R) & (cc < rr), 1.0, 0.0).astype(BF16)

    def prefix(xf):
        x2 = xf.reshape(E * R, L)
        xb = x2.astype(BF16)
        incl = jnp.dot(xb, upper, preferred_element_type=F32)
        tot = jnp.dot(xb, ones, preferred_element_type=F32).astype(BF16)
        rowoff = jnp.dot(lower, tot, preferred_element_type=F32)
        return (incl - x2 + rowoff).reshape(E, R, L), rowoff.reshape(E, R, L)

    gt = jnp.where(bits > thr, 1.0, 0.0)
    eq = jnp.where(bits == thr, 1.0, 0.0)
    n_gt = jnp.sum(jnp.sum(gt, axis=2, keepdims=True), axis=1, keepdims=True)
    need = cap - n_gt
    eq_rank, _ = prefix(eq)
    sel = gt + eq * jnp.where(eq_rank < need, 1.0, 0.0)
    pos, rowoff = prefix(sel)
    slot_ref[...] = jnp.where(sel > 0.0, pos, -1.0).astype(jnp.int32)
    off_ref[...] = rowoff.astype(jnp.int32)


def _topk(aff4, *, cap):
    E, B, R, L = aff4.shape
    return pl.pallas_call(
        functools.partial(_topk_kernel, cap=cap),
        grid=(B,),
        in_specs=[pl.BlockSpec((E, None, R, L), lambda b: (0, b, 0, 0))],
        out_specs=[
            pl.BlockSpec((None, E, R, L), lambda b: (b, 0, 0, 0)),
            pl.BlockSpec((None, E, R, L), lambda b: (b, 0, 0, 0)),
        ],
        out_shape=[
            jax.ShapeDtypeStruct((B, E, R, L), jnp.int32),
            jax.ShapeDtypeStruct((B, E, R, L), jnp.int32),
        ],
        compiler_params=_cparams(("parallel",)),
        name="topk",
    )(aff4)


def _last_block_below(cnt_ref, base, nblk, bound, strict):
    pos = jnp.int32(0)
    step = nblk // 2
    while step >= 1:
        cand = pos + step
        v = cnt_ref[base + cand]
        ok = (v < bound) if strict else (v <= bound)
        pos = jnp.where(ok, cand, pos)
        step //= 2
    return pos


def _gather_kernel(cnt_ref, slot_ref, h_ref, xs_ref, *, win, unroll):
    b = pl.program_id(0)
    e = pl.program_id(1)
    ne = pl.num_programs(1)
    cap, D = xs_ref.shape
    nblk = slot_ref.shape[0]
    base = (b * ne + e) * (nblk + 1)
    r_iota = lax.broadcasted_iota(jnp.int32, (win, TOK_BLOCK), 0)

    def contrib(j, lo):
        jc = jnp.minimum(j, nblk - 1)
        srow = slot_ref[jc]
        rel = jnp.where(j < nblk, srow - lo, -1)
        oh = jnp.where(rel == r_iota, 1.0, 0.0).astype(BF16)
        hb = h_ref[pl.ds(pl.multiple_of(jc * TOK_BLOCK, TOK_BLOCK), TOK_BLOCK), :]
        return jnp.dot(oh, hb, preferred_element_type=F32)

    for r in range(cap // win):
        lo = r * win
        j_lo = _last_block_below(cnt_ref, base, nblk, lo, False)
        j_hi = _last_block_below(cnt_ref, base, nblk, lo + win, True)
        acc = contrib(j_lo, lo)
        for u in range(1, unroll):
            acc = acc + contrib(j_lo + u, lo)
        xs_ref[lo:lo + win, :] = acc.astype(BF16)

        @pl.when(j_hi >= j_lo + unroll)
        def _():
            def more(j, carry):
                xs_ref[lo:lo + win, :] = (xs_ref[lo:lo + win, :].astype(F32) + contrib(j, lo)).astype(BF16)
                return carry
            lax.fori_loop(j_lo + unroll, j_hi + 1, more, 0)


def _gather(cnt, slots5, h3d, *, cap, win):
    B, seq, D = h3d.shape
    E = slots5.shape[1]
    nblk = seq // TOK_BLOCK
    assert nblk & (nblk - 1) == 0 and cap % win == 0
    gs = pltpu.PrefetchScalarGridSpec(
        num_scalar_prefetch=1,
        grid=(B, E),
        in_specs=[
            pl.BlockSpec((None, None, nblk, 1, TOK_BLOCK), lambda b, e, c: (b, e, 0, 0, 0)),
            pl.BlockSpec((None, seq, D), lambda b, e, c: (b, 0, 0), pipeline_mode=pl.Buffered(1)),
        ],
        out_specs=pl.BlockSpec((None, None, cap, D), lambda b, e, c: (b, e, 0, 0)),
    )
    return pl.pallas_call(
        functools.partial(_gather_kernel, win=win, unroll=min(6, nblk)),
        grid_spec=gs,
        out_shape=jax.ShapeDtypeStruct((B, E, cap, D), BF16),
        compiler_params=_cparams(("parallel", "arbitrary")),
        name="gather",
    )(cnt, slots5, h3d)


def _ffn_kernel(xs_ref, wg_ref, wu_ref, wd_ref, y_ref):
    xs = xs_ref[...]
    a = jnp.dot(xs, wg_ref[...], preferred_element_type=F32)
    u = jnp.dot(xs, wu_ref[...], preferred_element_type=F32)
    hmid = (a * jax.nn.sigmoid(a) * u).astype(BF16)
    y_ref[...] = jnp.dot(hmid, wd_ref[...], preferred_element_type=F32).astype(BF16)


def _ffn(xs, w_gate, w_up, w_down, *, layer):
    B, E, cap, D = xs.shape
    Fd = w_gate.shape[-1]
    return pl.pallas_call(
        _ffn_kernel,
        grid=(E, B),
        in_specs=[
            pl.BlockSpec((None, None, cap, D), lambda e, b: (b, e, 0, 0)),
            pl.BlockSpec((None, None, D, Fd), lambda e, b: (layer, e, 0, 0)),
            pl.BlockSpec((None, None, D, Fd), lambda e, b: (layer, e, 0, 0)),
            pl.BlockSpec((None, None, Fd, D), lambda e, b: (layer, e, 0, 0)),
        ],
        out_specs=pl.BlockSpec((None, None, cap, D), lambda e, b: (b, e, 0, 0)),
        out_shape=jax.ShapeDtypeStruct((B, E, cap, D), BF16),
        compiler_params=_cparams(("parallel", "parallel")),
        name="ffn",
    )(xs, w_gate, w_up, w_down)


def _scatter_kernel(cnt_ref, slot_ref, gate_ref, y_ref, x_ref, fg_ref, o_ref, *, win, final_norm):
    b = pl.program_id(0)
    sb = pl.program_id(1)
    ne, cap, D = y_ref.shape
    nloc = slot_ref.shape[0] // TOK_BLOCK
    nblk = nloc * pl.num_programs(1)
    lane = lax.broadcasted_iota(jnp.int32, (TOK_BLOCK, win), 1)

    def gated_onehot(scol, gcol, lo, sa):
        rel = jnp.where(scol >= lo, scol - sa, -1)
        return jnp.where(rel == lane, gcol, 0.0).astype(BF16)

    for t in range(nloc):
        j = sb * nloc + t
        rows = slice(t * TOK_BLOCK, (t + 1) * TOK_BLOCK)
        sl = slot_ref[rows, :]
        gt = gate_ref[rows, :]
        acc = x_ref[rows, :]
        extra = jnp.int32(0)
        for e0 in range(0, ne, 2):
            gs, ys = [], []
            for e in (e0, e0 + 1):
                idx = (b * ne + e) * (nblk + 1) + j
                s0 = cnt_ref[idx]
                s1 = cnt_ref[idx + 1]
                a0 = (s0 // 16) * 16
                sa = pl.multiple_of(jnp.minimum(a0, cap - win), 16)
                extra = jnp.maximum(extra, s1 - (a0 + win))
                gs.append(gated_onehot(sl[:, e:e + 1], gt[:, e:e + 1], a0, sa))
                ys.append(y_ref[e, pl.ds(sa, win), :])
            acc = acc + jnp.dot(jnp.concatenate(gs, axis=1), jnp.concatenate(ys, axis=0),
                                preferred_element_type=F32)
        o_ref[rows, :] = acc

        @pl.when(extra > 0)
        def _():
            for e in range(ne):
                idx = (b * ne + e) * (nblk + 1) + j
                s0 = cnt_ref[idx]
                s1 = cnt_ref[idx + 1]
                a0 = (s0 // 16) * 16
                nw = (s1 - a0 + win - 1) // win

                def wbody(w, carry):
                    lo = a0 + w * win
                    sa = pl.multiple_of(jnp.minimum(lo, cap - win), 16)
                    g = gated_onehot(sl[:, e:e + 1], gt[:, e:e + 1], lo, sa)
                    o_ref[rows, :] += jnp.dot(g, y_ref[e, pl.ds(sa, win), :], preferred_element_type=F32)
                    return carry

                lax.fori_loop(1, nw, wbody, 0)

    if final_norm:
        o_ref[...] = _rms_rows(o_ref[...], fg_ref[...])


def _scatter(cnt, slotsT, gatesT, y, x3d, fg, *, win, sb_rows, final_norm):
    B, seq, D = x3d.shape
    E, cap = y.shape[1], y.shape[2]
    gs = pltpu.PrefetchScalarGridSpec(
        num_scalar_prefetch=1,
        grid=(B, seq // sb_rows),
        in_specs=[
            pl.BlockSpec((None, sb_rows, E), lambda b, s, c: (b, s, 0)),
            pl.BlockSpec((None, sb_rows, E), lambda b, s, c: (b, s, 0)),
            pl.BlockSpec((None, E, cap, D), lambda b, s, c: (b, 0, 0, 0), pipeline_mode=pl.Buffered(1)),
            pl.BlockSpec((None, sb_rows, D), lambda b, s, c: (b, s, 0)),
            pl.BlockSpec((1, D), lambda b, s, c: (0, 0)),
        ],
        out_specs=pl.BlockSpec((None, sb_rows, D), lambda b, s, c: (b, s, 0)),
    )
    return pl.pallas_call(
        functools.partial(_scatter_kernel, win=win, final_norm=final_norm),
        grid_spec=gs,
        out_shape=jax.ShapeDtypeStruct((B, seq, D), F32),
        compiler_params=_cparams(("parallel", "arbitrary")),
        name="scatter",
    )(cnt, slotsT, gatesT, y, x3d, fg)


def _rope_tables_T(seq):
    rows = seq // GRID_W
    row_id = jnp.repeat(jnp.arange(rows, dtype=F32), GRID_W)
    col_id = jnp.tile(jnp.arange(GRID_W, dtype=F32), rows)
    n_pairs = HEAD_DIM // 4
    freqs = jnp.exp(-math.log(ROPE_THETA) * jnp.arange(n_pairs, dtype=F32) / n_pairs)
    ang = jnp.concatenate([freqs[:, None] * row_id[None, :], freqs[:, None] * col_id[None, :]], axis=0)
    return jnp.cos(ang), jnp.sin(ang)


def _head_perm(n_heads):
    base = jnp.concatenate([jnp.arange(0, HEAD_DIM, 2), jnp.arange(1, HEAD_DIM, 2)])
    return (jnp.arange(n_heads)[:, None] * HEAD_DIM + base[None, :]).reshape(-1)


def kernel(x, mem, mix_norm_g, w_in, gm_v_norm_g, gm_w_s, gm_b_s, q_norm_g, k_norm_g, branch_norm_g, w_out,
           xattn_norm_g, mem_norm_g, xattn_w_q, xattn_w_kv, xattn_w_o, ffn_norm_g, w_router, w_gate, w_up,
           w_down, final_norm_g):
    B, seq, D = x.shape
    L = w_in.shape[0]
    E = w_router.shape[-1]
    T = B * seq
    cap = EC_FACTOR * seq // E
    tm = min(512, seq)
    tq = min(256, seq)
    tk = min(512, seq)
    win = min(128, cap)
    sb_rows = min(512, seq)
    nblk = seq // TOK_BLOCK

    cosT, sinT = _rope_tables_T(seq)
    o_q = 2 * GM_WIDTH
    o_k = o_q + ATT_WIDTH
    o_v = o_k + KV_WIDTH
    cols = jnp.concatenate([jnp.arange(o_q), o_q + _head_perm(N_Q_HEADS), o_k + _head_perm(N_KV_HEADS),
                            jnp.arange(o_v, o_v + KV_WIDTH)])
    hp = _head_perm(1)

    mkT_all, mv_all = _mem_kv(mem, mem_norm_g.reshape(1, D), xattn_w_kv.astype(BF16))
    w_gate_b = w_gate.astype(BF16)
    w_up_b = w_up.astype(BF16)
    w_down_b = w_down.astype(BF16)

    x2d = x.reshape(T, D)
    out = None
    for l in range(L):
        w_in_l = w_in[l][:, cols].astype(BF16)
        bs = jnp.broadcast_to(gm_b_s[l][:, :, None], (GM_GROUPS, CHUNK, LANES))
        qg = jnp.broadcast_to(q_norm_g[l][hp][:, None], (HEAD_DIM, tm))
        kg = jnp.broadcast_to(k_norm_g[l][hp][:, None], (HEAD_DIM, tm))
        gm, qT, k, vT = _mixer_in(
            x2d, mix_norm_g[l].reshape(1, D), w_in_l, gm_v_norm_g[l].reshape(1, GM_WIDTH),
            gm_w_s[l].astype(BF16), bs, qg, kg, branch_norm_g[l, 0].reshape(1, GM_WIDTH), cosT, sinT,
            seq=seq, tm=tm)
        atT = _attention(qT, k.reshape(B, seq, KV_WIDTH), vT, tq=tq, tk=tk, qb=min(256, tq))

        bg1 = jnp.broadcast_to(branch_norm_g[l, 1][:, None], (ATT_WIDTH, tm))
        wr = w_router[l].T
        wr_hi = wr.astype(BF16)
        wr_lo = (wr - wr_hi.astype(F32)).astype(BF16)
        x2, h3, affT = _post(
            x2d, gm, atT, bg1, w_out[l].astype(BF16), xattn_norm_g[l].reshape(1, D),
            xattn_w_q[l].astype(BF16), mkT_all, mv_all, xattn_w_o[l].astype(BF16),
            ffn_norm_g[l].reshape(1, D), wr_hi, wr_lo, layer=l, seq=seq, tm=tm)

        slots, offs = _topk(affT.reshape(E, B, seq // LANES, LANES), cap=cap)
        cnt = offs[:, :, ::TOK_BLOCK // LANES, 0]
        cnt = jnp.concatenate([cnt, jnp.full((B, E, 1), cap, jnp.int32)], axis=-1).reshape(-1)
        slots5 = slots.reshape(B, E, nblk, 1, TOK_BLOCK)
        slotsT = slots.reshape(B, E, seq).transpose(0, 2, 1)
        gatesT = affT.reshape(E, B, seq).transpose(1, 2, 0)

        xs = _gather(cnt, slots5, h3.reshape(B, seq, D), cap=cap, win=win)
        y = _ffn(xs, w_gate_b, w_up_b, w_down_b, layer=l)
        out = _scatter(cnt, slotsT, gatesT, y, x2.reshape(B, seq, D), final_norm_g.reshape(1, D),
                       win=win, sb_rows=sb_rows, final_norm=(l == L - 1))
        x2d = out.reshape(T, D)
    return out
```

```python
import functools
import math

import jax
import jax.numpy as jnp
from jax import lax
from jax.experimental import pallas as pl
from jax.experimental.pallas import tpu as pltpu

F32 = jnp.float32
BF16 = jnp.bfloat16

EPS = 1e-6
CHUNK = 128
GM_GROUPS = 4
GM_WIDTH = 512
HEAD_DIM = 64
N_Q_HEADS = 8
N_KV_HEADS = 2
Q_PER_KV = N_Q_HEADS // N_KV_HEADS
ATT_WIDTH = N_Q_HEADS * HEAD_DIM
KV_WIDTH = N_KV_HEADS * HEAD_DIM
ROPE_THETA = 10000.0
GRID_W = 64
X_HEADS = 4
N_EXPERTS = 16
EC_FACTOR = 2

LANES = 128
VMEM_LIMIT = 56 * 1024 * 1024

TOK_BLOCK = 256
LOG2E = 1.4426950408889634


def _cparams(sem, vmem=VMEM_LIMIT, **kw):
    return pltpu.CompilerParams(dimension_semantics=sem, vmem_limit_bytes=vmem, **kw)


def _rms_rows(x, g):
    ms = jnp.mean(x * x, axis=-1, keepdims=True)
    return x * lax.rsqrt(ms + EPS) * g


def _gelu_tanh(x):
    c = math.sqrt(2.0 / math.pi)
    return 0.5 * x * (1.0 + jnp.tanh(c * (x + 0.044715 * (x * x * x))))


def _norm_rope_T(xT, gcol, cosT, sinT, n_heads, scale):
    half = HEAD_DIM // 2
    outs = []
    for h in range(n_heads):
        blk = xT[h * HEAD_DIM:(h + 1) * HEAD_DIM, :]
        ms = jnp.mean(blk * blk, axis=0, keepdims=True)
        n = blk * lax.rsqrt(ms + EPS) * gcol
        e = n[:half, :]
        o = n[half:, :]
        re = (e * cosT - o * sinT) * scale
        ro = (e * sinT + o * cosT) * scale
        outs.append(jnp.concatenate([re, ro], axis=0))
    return outs


def _mixer_in_kernel(x_ref, g_ref, w_ref, vg_ref, ws_ref, bs_ref, qg_ref, kg_ref, bg0_ref,
                     cos_ref, sin_ref, gm_ref, qT_ref, k_ref, vT_ref, gm_scr):
    tm = x_ref.shape[0]
    x = x_ref[...]
    h = _rms_rows(x, g_ref[...]).astype(BF16)
    proj = jnp.dot(h, w_ref[...], preferred_element_type=F32)

    u = _gelu_tanh(proj[:, :GM_WIDTH])
    v = _gelu_tanh(proj[:, GM_WIDTH:2 * GM_WIDTH])
    vn = _rms_rows(v, vg_ref[...]).astype(BF16)
    for c in range(tm // CHUNK):
        rs = slice(c * CHUNK, (c + 1) * CHUNK)
        for g in range(GM_GROUPS):
            cs = slice(g * LANES, (g + 1) * LANES)
            mixed = jnp.dot(ws_ref[g], vn[rs, cs], preferred_element_type=F32) + bs_ref[g]
            gm_scr[rs, cs] = u[rs, cs] * mixed
    gm_ref[...] = _rms_rows(gm_scr[...], bg0_ref[...]).astype(BF16)

    o_q = 2 * GM_WIDTH
    o_k = o_q + ATT_WIDTH
    o_v = o_k + KV_WIDTH
    cosT = cos_ref[...]
    sinT = sin_ref[...]

    qT = proj[:, o_q:o_k].T
    q_heads = _norm_rope_T(qT, qg_ref[...], cosT, sinT, N_Q_HEADS, (HEAD_DIM ** -0.5) * LOG2E)
    for hh in range(N_Q_HEADS):
        qT_ref[hh * HEAD_DIM:(hh + 1) * HEAD_DIM, :] = q_heads[hh].astype(BF16)

    kT = proj[:, o_k:o_v].T
    k_heads = _norm_rope_T(kT, kg_ref[...], cosT, sinT, N_KV_HEADS, 1.0)
    k_ref[...] = jnp.concatenate(k_heads, axis=0).T.astype(BF16)

    vT_ref[...] = proj[:, o_v:].T.astype(BF16)


def _mixer_in(x2d, g, w_in, vg, ws, bs, qg, kg, bg0, cosT, sinT, *, seq, tm):
    T, D = x2d.shape
    B = T // seq
    npb = seq // tm
    in_w = w_in.shape[1]
    full = lambda shape: pl.BlockSpec(shape, lambda i: (0,) * len(shape))
    return pl.pallas_call(
        _mixer_in_kernel,
        grid=(T // tm,),
        in_specs=[
            pl.BlockSpec((tm, D), lambda i: (i, 0)),
            full((1, D)),
            full((D, in_w)),
            full((1, GM_WIDTH)),
            full((GM_GROUPS, CHUNK, CHUNK)),
            full((GM_GROUPS, CHUNK, LANES)),
            full((HEAD_DIM, tm)),
            full((HEAD_DIM, tm)),
            full((1, GM_WIDTH)),
            pl.BlockSpec((HEAD_DIM // 2, tm), lambda i: (0, i % npb)),
            pl.BlockSpec((HEAD_DIM // 2, tm), lambda i: (0, i % npb)),
        ],
        out_specs=[
            pl.BlockSpec((tm, GM_WIDTH), lambda i: (i, 0)),
            pl.BlockSpec((None, ATT_WIDTH, tm), lambda i: (i // npb, 0, i % npb)),
            pl.BlockSpec((tm, KV_WIDTH), lambda i: (i, 0)),
            pl.BlockSpec((None, KV_WIDTH, tm), lambda i: (i // npb, 0, i % npb)),
        ],
        out_shape=[
            jax.ShapeDtypeStruct((T, GM_WIDTH), BF16),
            jax.ShapeDtypeStruct((B, ATT_WIDTH, seq), BF16),
            jax.ShapeDtypeStruct((T, KV_WIDTH), BF16),
            jax.ShapeDtypeStruct((B, KV_WIDTH, seq), BF16),
        ],
        scratch_shapes=[pltpu.VMEM((tm, GM_WIDTH), F32)],
        compiler_params=_cparams(("parallel",)),
        name="mixer_in",
    )(x2d, g, w_in, vg, ws, bs, qg, kg, bg0, cosT, sinT)


def _attn_kernel(qT_ref, k_ref, vT_ref, o_ref, vext_scr, qx_scr, m_scr, acc_scr, s_scr, cm_scr, *, tq, tk, qb):
    kh = pl.program_id(1)
    seq = k_ref.shape[0]
    M = Q_PER_KV * tq
    nq = seq // tq
    nchunk = seq // tk

    vext_scr[0:HEAD_DIM, :] = vT_ref[...]
    row = lax.broadcasted_iota(jnp.int32, (HEAD_DIM, seq), 0)
    vext_scr[HEAD_DIM:, :] = jnp.where(row == 0, 1.0, 0.0).astype(BF16)

    def load_q(i, slot):
        cols = pl.ds(pl.multiple_of(i * tq, tq), tq)
        q4 = jnp.concatenate([qT_ref[g * HEAD_DIM:(g + 1) * HEAD_DIM, cols] for g in range(Q_PER_KV)], axis=1)
        z = jnp.zeros_like(q4)
        qx_scr[slot] = jnp.where(kh == 0, jnp.concatenate([q4, z], axis=0), jnp.concatenate([z, q4], axis=0))

    def step(c, qslot, qk_chunk, do_pv):
        cur = c % 2
        nxt = 1 - cur
        if qk_chunk is not None:
            kn = k_ref[qk_chunk * tk:(qk_chunk + 1) * tk, :]
        if do_pv:
            ve = vext_scr[:, c * tk:(c + 1) * tk]
        for j in range(M // qb):
            cs = slice(j * qb, (j + 1) * qb)
            if qk_chunk is not None:
                s = jnp.dot(kn, qx_scr[qslot, :, cs], preferred_element_type=F32)
                s_scr[nxt, :, cs] = s
                cm_scr[nxt, :, cs] = jnp.max(s, axis=0, keepdims=True)
            if do_pv:
                m_old = m_scr[:, cs]
                m_new = jnp.maximum(m_old, cm_scr[cur, :, cs])
                p = jnp.exp2(s_scr[cur, :, cs] - m_new).astype(BF16)
                alpha = jnp.exp2(m_old - m_new)
                pv = jnp.dot(ve, p, preferred_element_type=F32)
                acc_scr[:, cs] = acc_scr[:, cs] * alpha + pv
                m_scr[:, cs] = m_new

    load_q(0, 0)
    step(-1, 0, 0, False)

    def qblock(i, carry):
        slot = i % 2
        m_scr[...] = jnp.full(m_scr.shape, -jnp.inf, F32)
        acc_scr[...] = jnp.zeros(acc_scr.shape, F32)
        for c in range(nchunk - 1):
            step(c, slot, c + 1, True)
        load_q(jnp.minimum(i + 1, nq - 1), 1 - slot)
        step(nchunk - 1, 1 - slot, 0, True)

        acc = acc_scr[...]
        inv = 1.0 / acc[HEAD_DIM:HEAD_DIM + 1, :]
        o = acc[0:HEAD_DIM, :] * inv
        cols = pl.ds(pl.multiple_of(i * tq, tq), tq)
        for g in range(Q_PER_KV):
            o_ref[g * HEAD_DIM:(g + 1) * HEAD_DIM, cols] = o[:, g * tq:(g + 1) * tq].astype(BF16)
        return carry

    lax.fori_loop(0, nq, qblock, 0)


def _attention(qT, k, vT, *, tq, tk, qb):
    B, _, seq = qT.shape
    gw = Q_PER_KV * HEAD_DIM
    assert (seq // tk) % 2 == 0
    return pl.pallas_call(
        functools.partial(_attn_kernel, tq=tq, tk=tk, qb=qb),
        grid=(B, N_KV_HEADS),
        in_specs=[
            pl.BlockSpec((None, gw, seq), lambda b, kh: (b, kh, 0)),
            pl.BlockSpec((None, seq, KV_WIDTH), lambda b, kh: (b, 0, 0)),
            pl.BlockSpec((None, HEAD_DIM, seq), lambda b, kh: (b, kh, 0)),
        ],
        out_specs=pl.BlockSpec((None, gw, seq), lambda b, kh: (b, kh, 0)),
        out_shape=jax.ShapeDtypeStruct((B, ATT_WIDTH, seq), BF16),
        scratch_shapes=[
            pltpu.VMEM((2 * HEAD_DIM, seq), BF16),
            pltpu.VMEM((2, 2 * HEAD_DIM, Q_PER_KV * tq), BF16),
            pltpu.VMEM((1, Q_PER_KV * tq), F32),
            pltpu.VMEM((2 * HEAD_DIM, Q_PER_KV * tq), F32),
            pltpu.VMEM((2, tk, Q_PER_KV * tq), F32),
            pltpu.VMEM((2, 1, Q_PER_KV * tq), F32),
        ],
        compiler_params=_cparams(("parallel", "parallel")),
        name="attention",
    )(qT, k, vT)


def _mem_kv_kernel(mem_ref, g_ref, w_ref, kT_ref, v_ref):
    D = mem_ref.shape[1]
    mn = _rms_rows(mem_ref[...], g_ref[...]).astype(BF16)
    kv = jnp.dot(mn, w_ref[...], preferred_element_type=F32)
    kT_ref[...] = kv[:, :D].T.astype(BF16)
    v_ref[...] = kv[:, D:].astype(BF16)


def _mem_kv(mem, g, w_kv):
    B, M, D = mem.shape
    L = w_kv.shape[0]
    return pl.pallas_call(
        _mem_kv_kernel,
        grid=(L, B),
        in_specs=[
            pl.BlockSpec((None, M, D), lambda l, b: (b, 0, 0)),
            pl.BlockSpec((1, D), lambda l, b: (0, 0)),
            pl.BlockSpec((None, D, 2 * D), lambda l, b: (l, 0, 0)),
        ],
        out_specs=[
            pl.BlockSpec((None, None, D, M), lambda l, b: (l, b, 0, 0)),
            pl.BlockSpec((None, None, M, D), lambda l, b: (l, b, 0, 0)),
        ],
        out_shape=[
            jax.ShapeDtypeStruct((L, B, D, M), BF16),
            jax.ShapeDtypeStruct((L, B, M, D), BF16),
        ],
        compiler_params=_cparams(("parallel", "parallel")),
        name="mem_kv",
    )(mem, g, w_kv)


def _post_kernel(x_ref, gm_ref, atT_ref, bg1_ref, wout_ref, xg_ref, wq_ref, mkT_ref, mv_ref, wo_ref,
                 fg_ref, wrh_ref, wrl_ref, x2_ref, h3_ref, affT_ref):
    D = x_ref.shape[1]
    xhd = D // X_HEADS

    at = atT_ref[...].astype(F32)
    ms = jnp.mean(at * at, axis=0, keepdims=True)
    atn = (at * lax.rsqrt(ms + EPS) * bg1_ref[...]).astype(BF16)
    y = jnp.dot(gm_ref[...], wout_ref[0:GM_WIDTH, :], preferred_element_type=F32)
    y = y + lax.dot_general(atn, wout_ref[GM_WIDTH:, :], (((0,), (0,)), ((), ())),
                            preferred_element_type=F32)
    x1 = x_ref[...] + y

    h2 = _rms_rows(x1, xg_ref[...]).astype(BF16)
    q2 = (jnp.dot(h2, wq_ref[...], preferred_element_type=F32) * (xhd ** -0.5)).astype(BF16)
    outs = []
    for hh in range(X_HEADS):
        cs = slice(hh * xhd, (hh + 1) * xhd)
        s = jnp.dot(q2[:, cs], mkT_ref[cs, :], preferred_element_type=F32)
        s = s - jnp.max(s, axis=-1, keepdims=True)
        p = jnp.exp(s)
        p = (p / jnp.sum(p, axis=-1, keepdims=True)).astype(BF16)
        outs.append(jnp.dot(p, mv_ref[:, cs], preferred_element_type=F32).astype(BF16))
    o2 = jnp.concatenate(outs, axis=1)
    x2 = x1 + jnp.dot(o2, wo_ref[...], preferred_element_type=F32)
    x2_ref[...] = x2

    h3 = _rms_rows(x2, fg_ref[...])
    hi = h3.astype(BF16)
    h3_ref[...] = hi
    lo = (h3 - hi.astype(F32)).astype(BF16)
    nt = (((1,), (1,)), ((), ()))
    lg = (lax.dot_general(wrh_ref[...], hi, nt, preferred_element_type=F32)
          + lax.dot_general(wrh_ref[...], lo, nt, preferred_element_type=F32)
          + lax.dot_general(wrl_ref[...], hi, nt, preferred_element_type=F32))
    lg = lg - jnp.max(lg, axis=0, keepdims=True)
    ex = jnp.exp(lg)
    affT_ref[...] = ex / jnp.sum(ex, axis=0, keepdims=True)


def _post(x2d, gm, atT, bg1, w_out, xg, w_q, mkT, mv, w_o, fg, wr_hi, wr_lo, *, layer, seq, tm):
    T, D = x2d.shape
    npb = seq // tm
    M = mv.shape[2]
    E = wr_hi.shape[0]
    full = lambda shape: pl.BlockSpec(shape, lambda i: (0,) * len(shape))
    return pl.pallas_call(
        _post_kernel,
        grid=(T // tm,),
        in_specs=[
            pl.BlockSpec((tm, D), lambda i: (i, 0)),
            pl.BlockSpec((tm, GM_WIDTH), lambda i: (i, 0)),
            pl.BlockSpec((None, ATT_WIDTH, tm), lambda i: (i // npb, 0, i % npb)),
            full((ATT_WIDTH, tm)),
            full((GM_WIDTH + ATT_WIDTH, D)),
            full((1, D)),
            full((D, D)),
            pl.BlockSpec((None, None, D, M), lambda i: (layer, i // npb, 0, 0)),
            pl.BlockSpec((None, None, M, D), lambda i: (layer, i // npb, 0, 0)),
            full((D, D)),
            full((1, D)),
            full((E, D)),
            full((E, D)),
        ],
        out_specs=[
            pl.BlockSpec((tm, D), lambda i: (i, 0)),
            pl.BlockSpec((tm, D), lambda i: (i, 0)),
            pl.BlockSpec((E, tm), lambda i: (0, i)),
        ],
        out_shape=[
            jax.ShapeDtypeStruct((T, D), F32),
            jax.ShapeDtypeStruct((T, D), BF16),
            jax.ShapeDtypeStruct((E, T), F32),
        ],
        compiler_params=_cparams(("parallel",)),
        name="post",
    )(x2d, gm, atT, bg1, w_out, xg, w_q, mkT, mv, w_o, fg, wr_hi, wr_lo)


def _topk_kernel(aff_ref, slot_ref, off_ref, *, cap):
    E, R, L = aff_ref.shape
    a = aff_ref[...]
    bits = lax.bitcast_convert_type(a, jnp.int32)

    def count_ge(t):
        c = jnp.where(bits >= t, 1.0, 0.0)
        return jnp.sum(jnp.sum(c, axis=2, keepdims=True), axis=1, keepdims=True)

    def bis(_, carry):
        lo, hi = carry
        mid = lo + ((hi - lo) >> 1)
        ok = count_ge(mid) >= cap
        return jnp.where(ok, mid, lo), jnp.where(ok, hi, mid)

    lo0 = jnp.zeros((E, 1, 1), jnp.int32)
    hi0 = jnp.full((E, 1, 1), 0x3F800001, jnp.int32)
    thr, _ = lax.fori_loop(0, 31, bis, (lo0, hi0))

    kk = lax.broadcasted_iota(jnp.int32, (L, L), 0)
    nn = lax.broadcasted_iota(jnp.int32, (L, L), 1)
    upper = jnp.where(kk <= nn, 1.0, 0.0).astype(BF16)
    ones = jnp.ones((L, L), BF16)
    rr = lax.broadcasted_iota(jnp.int32, (E * R, E * R), 0)
    cc = lax.broadcasted_iota(jnp.int32, (E * R, E * R), 1)
    lower = jnp.where((rr // R == cc // R) & (cc < rr), 1.0, 0.0).astype(BF16)

    def prefix(xf):
        x2 = xf.reshape(E * R, L)
        xb = x2.astype(BF16)
        incl = jnp.dot(xb, upper, preferred_element_type=F32)
        tot = jnp.dot(xb, ones, preferred_element_type=F32).astype(BF16)
        rowoff = jnp.dot(lower, tot, preferred_element_type=F32)
        return (incl - x2 + rowoff).reshape(E, R, L), rowoff.reshape(E, R, L)

    gt = jnp.where(bits > thr, 1.0, 0.0)
    eq = jnp.where(bits == thr, 1.0, 0.0)
    n_gt = jnp.sum(jnp.sum(gt, axis=2, keepdims=True), axis=1, keepdims=True)
    need = cap - n_gt
    eq_rank, _ = prefix(eq)
    sel = gt + eq * jnp.where(eq_rank < need, 1.0, 0.0)
    pos, rowoff = prefix(sel)
    slot_ref[...] = jnp.where(sel > 0.0, pos, -1.0).astype(jnp.int32)
    off_ref[...] = rowoff.astype(jnp.int32)


def _topk(aff4, *, cap):
    E, B, R, L = aff4.shape
    return pl.pallas_call(
        functools.partial(_topk_kernel, cap=cap),
        grid=(B,),
        in_specs=[pl.BlockSpec((E, None, R, L), lambda b: (0, b, 0, 0))],
        out_specs=[
            pl.BlockSpec((None, E, R, L), lambda b: (b, 0, 0, 0)),
            pl.BlockSpec((None, E, R, L), lambda b: (b, 0, 0, 0)),
        ],
        out_shape=[
            jax.ShapeDtypeStruct((B, E, R, L), jnp.int32),
            jax.ShapeDtypeStruct((B, E, R, L), jnp.int32),
        ],
        compiler_params=_cparams(("parallel",)),
        name="topk",
    )(aff4)


def _last_block_below(cnt_ref, base, nblk, bound, strict):
    pos = jnp.int32(0)
    step = nblk // 2
    while step >= 1:
        cand = pos + step
        v = cnt_ref[base + cand]
        ok = (v < bound) if strict else (v <= bound)
        pos = jnp.where(ok, cand, pos)
        step //= 2
    return pos


def _gather_kernel(cnt_ref, slot_ref, h_ref, xs_ref, *, win, unroll):
    b = pl.program_id(0)
    e = pl.program_id(1)
    ne = pl.num_programs(1)
    cap, D = xs_ref.shape
    nblk = slot_ref.shape[0]
    base = (b * ne + e) * (nblk + 1)
    r_iota = lax.broadcasted_iota(jnp.int32, (win, TOK_BLOCK), 0)

    def contrib(j, lo):
        jc = jnp.minimum(j, nblk - 1)
        srow = slot_ref[jc]
        rel = jnp.where(j < nblk, srow - lo, -1)
        oh = jnp.where(rel == r_iota, 1.0, 0.0).astype(BF16)
        hb = h_ref[pl.ds(pl.multiple_of(jc * TOK_BLOCK, TOK_BLOCK), TOK_BLOCK), :]
        return jnp.dot(oh, hb, preferred_element_type=F32)

    for r in range(cap // win):
        lo = r * win
        j_lo = _last_block_below(cnt_ref, base, nblk, lo, False)
        j_hi = _last_block_below(cnt_ref, base, nblk, lo + win, True)
        acc = contrib(j_lo, lo)
        for u in range(1, unroll):
            acc = acc + contrib(j_lo + u, lo)
        xs_ref[lo:lo + win, :] = acc.astype(BF16)

        @pl.when(j_hi >= j_lo + unroll)
        def _():
            def more(j, carry):
                xs_ref[lo:lo + win, :] = (xs_ref[lo:lo + win, :].astype(F32) + contrib(j, lo)).astype(BF16)
                return carry
            lax.fori_loop(j_lo + unroll, j_hi + 1, more, 0)


def _gather(cnt, slots5, h3d, *, cap, win):
    B, seq, D = h3d.shape
    E = slots5.shape[1]
    nblk = seq // TOK_BLOCK
    assert nblk & (nblk - 1) == 0 and cap % win == 0
    gs = pltpu.PrefetchScalarGridSpec(
        num_scalar_prefetch=1,
        grid=(B, E),
        in_specs=[
            pl.BlockSpec((None, None, nblk, 1, TOK_BLOCK), lambda b, e, c: (b, e, 0, 0, 0)),
            pl.BlockSpec((None, seq, D), lambda b, e, c: (b, 0, 0), pipeline_mode=pl.Buffered(1)),
        ],
        out_specs=pl.BlockSpec((None, None, cap, D), lambda b, e, c: (b, e, 0, 0)),
    )
    return pl.pallas_call(
        functools.partial(_gather_kernel, win=win, unroll=min(6, nblk)),
        grid_spec=gs,
        out_shape=jax.ShapeDtypeStruct((B, E, cap, D), BF16),
        compiler_params=_cparams(("parallel", "arbitrary")),
        name="gather",
    )(cnt, slots5, h3d)


def _ffn_kernel(xs_ref, wg_ref, wu_ref, wd_ref, y_ref):
    xs = xs_ref[...]
    a = jnp.dot(xs, wg_ref[...], preferred_element_type=F32)
    u = jnp.dot(xs, wu_ref[...], preferred_element_type=F32)
    hmid = (a * jax.nn.sigmoid(a) * u).astype(BF16)
    y_ref[...] = jnp.dot(hmid, wd_ref[...], preferred_element_type=F32).astype(BF16)


def _ffn(xs, w_gate, w_up, w_down, *, layer):
    B, E, cap, D = xs.shape
    Fd = w_gate.shape[-1]
    return pl.pallas_call(
        _ffn_kernel,
        grid=(E, B),
        in_specs=[
            pl.BlockSpec((None, None, cap, D), lambda e, b: (b, e, 0, 0)),
            pl.BlockSpec((None, None, D, Fd), lambda e, b: (layer, e, 0, 0)),
            pl.BlockSpec((None, None, D, Fd), lambda e, b: (layer, e, 0, 0)),
            pl.BlockSpec((None, None, Fd, D), lambda e, b: (layer, e, 0, 0)),
        ],
        out_specs=pl.BlockSpec((None, None, cap, D), lambda e, b: (b, e, 0, 0)),
        out_shape=jax.ShapeDtypeStruct((B, E, cap, D), BF16),
        compiler_params=_cparams(("parallel", "parallel")),
        name="ffn",
    )(xs, w_gate, w_up, w_down)


def _scatter_kernel(cnt_ref, slot_ref, gate_ref, y_ref, x_ref, fg_ref, o_ref, *, win, final_norm):
    b = pl.program_id(0)
    sb = pl.program_id(1)
    ne, cap, D = y_ref.shape
    nloc = slot_ref.shape[0] // TOK_BLOCK
    nblk = nloc * pl.num_programs(1)
    lane = lax.broadcasted_iota(jnp.int32, (TOK_BLOCK, win), 1)

    def gated_onehot(scol, gcol, lo, sa):
        rel = jnp.where(scol >= lo, scol - sa, -1)
        return jnp.where(rel == lane, gcol, 0.0).astype(BF16)

    for t in range(nloc):
        j = sb * nloc + t
        rows = slice(t * TOK_BLOCK, (t + 1) * TOK_BLOCK)
        sl = slot_ref[rows, :]
        gt = gate_ref[rows, :]
        acc = x_ref[rows, :]
        extra = jnp.int32(0)
        for e0 in range(0, ne, 2):
            gs, ys = [], []
            for e in (e0, e0 + 1):
                idx = (b * ne + e) * (nblk + 1) + j
                s0 = cnt_ref[idx]
                s1 = cnt_ref[idx + 1]
                a0 = (s0 // 16) * 16
                sa = pl.multiple_of(jnp.minimum(a0, cap - win), 16)
                extra = jnp.maximum(extra, s1 - (a0 + win))
                gs.append(gated_onehot(sl[:, e:e + 1], gt[:, e:e + 1], a0, sa))
                ys.append(y_ref[e, pl.ds(sa, win), :])
            acc = acc + jnp.dot(jnp.concatenate(gs, axis=1), jnp.concatenate(ys, axis=0),
                                preferred_element_type=F32)
        o_ref[rows, :] = acc

        @pl.when(extra > 0)
        def _():
            for e in range(ne):
                idx = (b * ne + e) * (nblk + 1) + j
                s0 = cnt_ref[idx]
                s1 = cnt_ref[idx + 1]
                a0 = (s0 // 16) * 16
                nw = (s1 - a0 + win - 1) // win

                def wbody(w, carry):
                    lo = a0 + w * win
                    sa = pl.multiple_of(jnp.minimum(lo, cap - win), 16)
                    g = gated_onehot(sl[:, e:e + 1], gt[:, e:e + 1], lo, sa)
                    o_ref[rows, :] += jnp.dot(g, y_ref[e, pl.ds(sa, win), :], preferred_element_type=F32)
                    return carry

                lax.fori_loop(1, nw, wbody, 0)

    if final_norm:
        o_ref[...] = _rms_rows(o_ref[...], fg_ref[...])


def _scatter(cnt, slotsT, gatesT, y, x3d, fg, *, win, sb_rows, final_norm):
    B, seq, D = x3d.shape
    E, cap = y.shape[1], y.shape[2]
    gs = pltpu.PrefetchScalarGridSpec(
        num_scalar_prefetch=1,
        grid=(B, seq // sb_rows),
        in_specs=[
            pl.BlockSpec((None, sb_rows, E), lambda b, s, c: (b, s, 0)),
            pl.BlockSpec((None, sb_rows, E), lambda b, s, c: (b, s, 0)),
            pl.BlockSpec((None, E, cap, D), lambda b, s, c: (b, 0, 0, 0), pipeline_mode=pl.Buffered(1)),
            pl.BlockSpec((None, sb_rows, D), lambda b, s, c: (b, s, 0)),
            pl.BlockSpec((1, D), lambda b, s, c: (0, 0)),
        ],
        out_specs=pl.BlockSpec((None, sb_rows, D), lambda b, s, c: (b, s, 0)),
    )
    return pl.pallas_call(
        functools.partial(_scatter_kernel, win=win, final_norm=final_norm),
        grid_spec=gs,
        out_shape=jax.ShapeDtypeStruct((B, seq, D), F32),
        compiler_params=_cparams(("parallel", "arbitrary")),
        name="scatter",
    )(cnt, slotsT, gatesT, y, x3d, fg)


def _rope_tables_T(seq):
    rows = seq // GRID_W
    row_id = jnp.repeat(jnp.arange(rows, dtype=F32), GRID_W)
    col_id = jnp.tile(jnp.arange(GRID_W, dtype=F32), rows)
    n_pairs = HEAD_DIM // 4
    freqs = jnp.exp(-math.log(ROPE_THETA) * jnp.arange(n_pairs, dtype=F32) / n_pairs)
    ang = jnp.concatenate([freqs[:, None] * row_id[None, :], freqs[:, None] * col_id[None, :]], axis=0)
    return jnp.cos(ang), jnp.sin(ang)


def _head_perm(n_heads):
    base = jnp.concatenate([jnp.arange(0, HEAD_DIM, 2), jnp.arange(1, HEAD_DIM, 2)])
    return (jnp.arange(n_heads)[:, None] * HEAD_DIM + base[None, :]).reshape(-1)


def kernel(x, mem, mix_norm_g, w_in, gm_v_norm_g, gm_w_s, gm_b_s, q_norm_g, k_norm_g, branch_norm_g, w_out,
           xattn_norm_g, mem_norm_g, xattn_w_q, xattn_w_kv, xattn_w_o, ffn_norm_g, w_router, w_gate, w_up,
           w_down, final_norm_g):
    B, seq, D = x.shape
    L = w_in.shape[0]
    E = w_router.shape[-1]
    T = B * seq
    cap = EC_FACTOR * seq // E
    tm = min(512, seq)
    tq = min(256, seq)
    tk = min(512, seq)
    win = min(128, cap)
    sb_rows = min(512, seq)
    nblk = seq // TOK_BLOCK

    cosT, sinT = _rope_tables_T(seq)
    o_q = 2 * GM_WIDTH
    o_k = o_q + ATT_WIDTH
    o_v = o_k + KV_WIDTH
    cols = jnp.concatenate([jnp.arange(o_q), o_q + _head_perm(N_Q_HEADS), o_k + _head_perm(N_KV_HEADS),
                            jnp.arange(o_v, o_v + KV_WIDTH)])
    hp = _head_perm(1)

    mkT_all, mv_all = _mem_kv(mem, mem_norm_g.reshape(1, D), xattn_w_kv.astype(BF16))
    w_gate_b = w_gate.astype(BF16)
    w_up_b = w_up.astype(BF16)
    w_down_b = w_down.astype(BF16)

    x2d = x.reshape(T, D)
    out = None
    for l in range(L):
        w_in_l = w_in[l][:, cols].astype(BF16)
        bs = jnp.broadcast_to(gm_b_s[l][:, :, None], (GM_GROUPS, CHUNK, LANES))
        qg = jnp.broadcast_to(q_norm_g[l][hp][:, None], (HEAD_DIM, tm))
        kg = jnp.broadcast_to(k_norm_g[l][hp][:, None], (HEAD_DIM, tm))
        gm, qT, k, vT = _mixer_in(
            x2d, mix_norm_g[l].reshape(1, D), w_in_l, gm_v_norm_g[l].reshape(1, GM_WIDTH),
            gm_w_s[l].astype(BF16), bs, qg, kg, branch_norm_g[l, 0].reshape(1, GM_WIDTH), cosT, sinT,
            seq=seq, tm=tm)
        atT = _attention(qT, k.reshape(B, seq, KV_WIDTH), vT, tq=tq, tk=tk, qb=min(256, tq))

        bg1 = jnp.broadcast_to(branch_norm_g[l, 1][:, None], (ATT_WIDTH, tm))
        wr = w_router[l].T
        wr_hi = wr.astype(BF16)
        wr_lo = (wr - wr_hi.astype(F32)).astype(BF16)
        x2, h3, affT = _post(
            x2d, gm, atT, bg1, w_out[l].astype(BF16), xattn_norm_g[l].reshape(1, D),
            xattn_w_q[l].astype(BF16), mkT_all, mv_all, xattn_w_o[l].astype(BF16),
            ffn_norm_g[l].reshape(1, D), wr_hi, wr_lo, layer=l, seq=seq, tm=tm)

        slots, offs = _topk(affT.reshape(E, B, seq // LANES, LANES), cap=cap)
        cnt = offs[:, :, ::TOK_BLOCK // LANES, 0]
        cnt = jnp.concatenate([cnt, jnp.full((B, E, 1), cap, jnp.int32)], axis=-1).reshape(-1)
        slots5 = slots.reshape(B, E, nblk, 1, TOK_BLOCK)
        slotsT = slots.reshape(B, E, seq).transpose(0, 2, 1)
        gatesT = affT.reshape(E, B, seq).transpose(1, 2, 0)

        xs = _gather(cnt, slots5, h3.reshape(B, seq, D), cap=cap, win=win)
        y = _ffn(xs, w_gate_b, w_up_b, w_down_b, layer=l)
        out = _scatter(cnt, slotsT, gatesT, y, x2.reshape(B, seq, D), final_norm_g.reshape(1, D),
                       win=win, sb_rows=sb_rows, final_norm=(l == L - 1))
        x2d = out.reshape(T, D)
    return out
```

```python
import functools
import math

import jax
import jax.numpy as jnp
from jax import lax
from jax.experimental import pallas as pl
from jax.experimental.pallas import tpu as pltpu

F32 = jnp.float32
BF16 = jnp.bfloat16

EPS = 1e-6
CHUNK = 128
GM_GROUPS = 4
GM_WIDTH = 512
HEAD_DIM = 64
N_Q_HEADS = 8
N_KV_HEADS = 2
Q_PER_KV = N_Q_HEADS // N_KV_HEADS
ATT_WIDTH = N_Q_HEADS * HEAD_DIM
KV_WIDTH = N_KV_HEADS * HEAD_DIM
ROPE_THETA = 10000.0
GRID_W = 64
X_HEADS = 4
N_EXPERTS = 16
EC_FACTOR = 2

LANES = 128
VMEM_LIMIT = 56 * 1024 * 1024

TOK_BLOCK = 256
LOG2E = 1.4426950408889634


def _cparams(sem, vmem=VMEM_LIMIT, **kw):
    return pltpu.CompilerParams(dimension_semantics=sem, vmem_limit_bytes=vmem, **kw)


def _rms_rows(x, g):
    ms = jnp.mean(x * x, axis=-1, keepdims=True)
    return x * lax.rsqrt(ms + EPS) * g


def _gelu_tanh(x):
    c = math.sqrt(2.0 / math.pi)
    return 0.5 * x * (1.0 + jnp.tanh(c * (x + 0.044715 * (x * x * x))))


def _norm_rope_T(xT, gcol, cosT, sinT, n_heads, scale):
    half = HEAD_DIM // 2
    outs = []
    for h in range(n_heads):
        blk = xT[h * HEAD_DIM:(h + 1) * HEAD_DIM, :]
        ms = jnp.mean(blk * blk, axis=0, keepdims=True)
        n = blk * lax.rsqrt(ms + EPS) * gcol
        e = n[:half, :]
        o = n[half:, :]
        re = (e * cosT - o * sinT) * scale
        ro = (e * sinT + o * cosT) * scale
        outs.append(jnp.concatenate([re, ro], axis=0))
    return outs


def _mixer_in_kernel(x_ref, g_ref, w_ref, vg_ref, ws_ref, bs_ref, qg_ref, kg_ref, bg0_ref,
                     cos_ref, sin_ref, gm_ref, qT_ref, k_ref, vT_ref, gm_scr):
    tm = x_ref.shape[0]
    x = x_ref[...]
    h = _rms_rows(x, g_ref[...]).astype(BF16)
    proj = jnp.dot(h, w_ref[...], preferred_element_type=F32)

    u = _gelu_tanh(proj[:, :GM_WIDTH])
    v = _gelu_tanh(proj[:, GM_WIDTH:2 * GM_WIDTH])
    vn = _rms_rows(v, vg_ref[...]).astype(BF16)
    for c in range(tm // CHUNK):
        rs = slice(c * CHUNK, (c + 1) * CHUNK)
        for g in range(GM_GROUPS):
            cs = slice(g * LANES, (g + 1) * LANES)
            mixed = jnp.dot(ws_ref[g], vn[rs, cs], preferred_element_type=F32) + bs_ref[g]
            gm_scr[rs, cs] = u[rs, cs] * mixed
    gm_ref[...] = _rms_rows(gm_scr[...], bg0_ref[...]).astype(BF16)

    o_q = 2 * GM_WIDTH
    o_k = o_q + ATT_WIDTH
    o_v = o_k + KV_WIDTH
    cosT = cos_ref[...]
    sinT = sin_ref[...]

    qT = proj[:, o_q:o_k].T
    q_heads = _norm_rope_T(qT, qg_ref[...], cosT, sinT, N_Q_HEADS, (HEAD_DIM ** -0.5) * LOG2E)
    for hh in range(N_Q_HEADS):
        qT_ref[hh * HEAD_DIM:(hh + 1) * HEAD_DIM, :] = q_heads[hh].astype(BF16)

    kT = proj[:, o_k:o_v].T
    k_heads = _norm_rope_T(kT, kg_ref[...], cosT, sinT, N_KV_HEADS, 1.0)
    k_ref[...] = jnp.concatenate(k_heads, axis=0).T.astype(BF16)

    vT_ref[...] = proj[:, o_v:].T.astype(BF16)


def _mixer_in(x2d, g, w_in, vg, ws, bs, qg, kg, bg0, cosT, sinT, *, seq, tm):
    T, D = x2d.shape
    B = T // seq
    npb = seq // tm
    in_w = w_in.shape[1]
    full = lambda shape: pl.BlockSpec(shape, lambda i: (0,) * len(shape))
    return pl.pallas_call(
        _mixer_in_kernel,
        grid=(T // tm,),
        in_specs=[
            pl.BlockSpec((tm, D), lambda i: (i, 0)),
            full((1, D)),
            full((D, in_w)),
            full((1, GM_WIDTH)),
            full((GM_GROUPS, CHUNK, CHUNK)),
            full((GM_GROUPS, CHUNK, LANES)),
            full((HEAD_DIM, tm)),
            full((HEAD_DIM, tm)),
            full((1, GM_WIDTH)),
            pl.BlockSpec((HEAD_DIM // 2, tm), lambda i: (0, i % npb)),
            pl.BlockSpec((HEAD_DIM // 2, tm), lambda i: (0, i % npb)),
        ],
        out_specs=[
            pl.BlockSpec((tm, GM_WIDTH), lambda i: (i, 0)),
            pl.BlockSpec((None, ATT_WIDTH, tm), lambda i: (i // npb, 0, i % npb)),
            pl.BlockSpec((tm, KV_WIDTH), lambda i: (i, 0)),
            pl.BlockSpec((None, KV_WIDTH, tm), lambda i: (i // npb, 0, i % npb)),
        ],
        out_shape=[
            jax.ShapeDtypeStruct((T, GM_WIDTH), BF16),
            jax.ShapeDtypeStruct((B, ATT_WIDTH, seq), BF16),
            jax.ShapeDtypeStruct((T, KV_WIDTH), BF16),
            jax.ShapeDtypeStruct((B, KV_WIDTH, seq), BF16),
        ],
        scratch_shapes=[pltpu.VMEM((tm, GM_WIDTH), F32)],
        compiler_params=_cparams(("parallel",)),
        name="mixer_in",
    )(x2d, g, w_in, vg, ws, bs, qg, kg, bg0, cosT, sinT)


def _attn_kernel(qT_ref, k_ref, vT_ref, o_ref, vext_scr, qx_scr, m_scr, acc_scr, s_scr, cm_scr, *, tq, tk, qb):
    kh = pl.program_id(1)
    seq = k_ref.shape[0]
    M = Q_PER_KV * tq
    nq = seq // tq
    nchunk = seq // tk

    vext_scr[0:HEAD_DIM, :] = vT_ref[...]
    row = lax.broadcasted_iota(jnp.int32, (HEAD_DIM, seq), 0)
    vext_scr[HEAD_DIM:, :] = jnp.where(row == 0, 1.0, 0.0).astype(BF16)

    def load_q(i, slot):
        cols = pl.ds(pl.multiple_of(i * tq, tq), tq)
        q4 = jnp.concatenate([qT_ref[g * HEAD_DIM:(g + 1) * HEAD_DIM, cols] for g in range(Q_PER_KV)], axis=1)
        z = jnp.zeros_like(q4)
        qx_scr[slot] = jnp.where(kh == 0, jnp.concatenate([q4, z], axis=0), jnp.concatenate([z, q4], axis=0))

    def step(c, qslot, qk_chunk, do_pv):
        cur = c % 2
        nxt = 1 - cur
        if qk_chunk is not None:
            kn = k_ref[qk_chunk * tk:(qk_chunk + 1) * tk, :]
        if do_pv:
            ve = vext_scr[:, c * tk:(c + 1) * tk]
        for j in range(M // qb):
            cs = slice(j * qb, (j + 1) * qb)
            if qk_chunk is not None:
                s = jnp.dot(kn, qx_scr[qslot, :, cs], preferred_element_type=F32)
                s_scr[nxt, :, cs] = s
                cm_scr[nxt, :, cs] = jnp.max(s, axis=0, keepdims=True)
            if do_pv:
                m_old = m_scr[:, cs]
                m_new = jnp.maximum(m_old, cm_scr[cur, :, cs])
                p = jnp.exp2(s_scr[cur, :, cs] - m_new).astype(BF16)
                alpha = jnp.exp2(m_old - m_new)
                pv = jnp.dot(ve, p, preferred_element_type=F32)
                acc_scr[:, cs] = acc_scr[:, cs] * alpha + pv
                m_scr[:, cs] = m_new

    load_q(0, 0)
    step(-1, 0, 0, False)

    def qblock(i, carry):
        slot = i % 2
        m_scr[...] = jnp.full(m_scr.shape, -jnp.inf, F32)
        acc_scr[...] = jnp.zeros(acc_scr.shape, F32)
        for c in range(nchunk - 1):
            step(c, slot, c + 1, True)
        load_q(jnp.minimum(i + 1, nq - 1), 1 - slot)
        step(nchunk - 1, 1 - slot, 0, True)

        acc = acc_scr[...]
        inv = 1.0 / acc[HEAD_DIM:HEAD_DIM + 1, :]
        o = acc[0:HEAD_DIM, :] * inv
        cols = pl.ds(pl.multiple_of(i * tq, tq), tq)
        for g in range(Q_PER_KV):
            o_ref[g * HEAD_DIM:(g + 1) * HEAD_DIM, cols] = o[:, g * tq:(g + 1) * tq].astype(BF16)
        return carry

    lax.fori_loop(0, nq, qblock, 0)


def _attention(qT, k, vT, *, tq, tk, qb):
    B, _, seq = qT.shape
    gw = Q_PER_KV * HEAD_DIM
    assert (seq // tk) % 2 == 0
    return pl.pallas_call(
        functools.partial(_attn_kernel, tq=tq, tk=tk, qb=qb),
        grid=(B, N_KV_HEADS),
        in_specs=[
            pl.BlockSpec((None, gw, seq), lambda b, kh: (b, kh, 0)),
            pl.BlockSpec((None, seq, KV_WIDTH), lambda b, kh: (b, 0, 0)),
            pl.BlockSpec((None, HEAD_DIM, seq), lambda b, kh: (b, kh, 0)),
        ],
        out_specs=pl.BlockSpec((None, gw, seq), lambda b, kh: (b, kh, 0)),
        out_shape=jax.ShapeDtypeStruct((B, ATT_WIDTH, seq), BF16),
        scratch_shapes=[
            pltpu.VMEM((2 * HEAD_DIM, seq), BF16),
            pltpu.VMEM((2, 2 * HEAD_DIM, Q_PER_KV * tq), BF16),
            pltpu.VMEM((1, Q_PER_KV * tq), F32),
            pltpu.VMEM((2 * HEAD_DIM, Q_PER_KV * tq), F32),
            pltpu.VMEM((2, tk, Q_PER_KV * tq), F32),
            pltpu.VMEM((2, 1, Q_PER_KV * tq), F32),
        ],
        compiler_params=_cparams(("parallel", "parallel")),
        name="attention",
    )(qT, k, vT)


def _mem_kv_kernel(mem_ref, g_ref, w_ref, kT_ref, v_ref):
    D = mem_ref.shape[1]
    mn = _rms_rows(mem_ref[...], g_ref[...]).astype(BF16)
    kv = jnp.dot(mn, w_ref[...], preferred_element_type=F32)
    kT_ref[...] = kv[:, :D].T.astype(BF16)
    v_ref[...] = kv[:, D:].astype(BF16)


def _mem_kv(mem, g, w_kv):
    B, M, D = mem.shape
    L = w_kv.shape[0]
    return pl.pallas_call(
        _mem_kv_kernel,
        grid=(L, B),
        in_specs=[
            pl.BlockSpec((None, M, D), lambda l, b: (b, 0, 0)),
            pl.BlockSpec((1, D), lambda l, b: (0, 0)),
            pl.BlockSpec((None, D, 2 * D), lambda l, b: (l, 0, 0)),
        ],
        out_specs=[
            pl.BlockSpec((None, None, D, M), lambda l, b: (l, b, 0, 0)),
            pl.BlockSpec((None, None, M, D), lambda l, b: (l, b, 0, 0)),
        ],
        out_shape=[
            jax.ShapeDtypeStruct((L, B, D, M), BF16),
            jax.ShapeDtypeStruct((L, B, M, D), BF16),
        ],
        compiler_params=_cparams(("parallel", "parallel")),
        name="mem_kv",
    )(mem, g, w_kv)


def _post_kernel(x_ref, gm_ref, atT_ref, bg1_ref, wout_ref, xg_ref, wq_ref, mkT_ref, mv_ref, wo_ref,
                 fg_ref, wrh_ref, wrl_ref, x2_ref, h3_ref, affT_ref):
    D = x_ref.shape[1]
    xhd = D // X_HEADS

    at = atT_ref[...].astype(F32)
    ms = jnp.mean(at * at, axis=0, keepdims=True)
    atn = (at * lax.rsqrt(ms + EPS) * bg1_ref[...]).astype(BF16)
    y = jnp.dot(gm_ref[...], wout_ref[0:GM_WIDTH, :], preferred_element_type=F32)
    y = y + lax.dot_general(atn, wout_ref[GM_WIDTH:, :], (((0,), (0,)), ((), ())),
                            preferred_element_type=F32)
    x1 = x_ref[...] + y

    h2 = _rms_rows(x1, xg_ref[...]).astype(BF16)
    q2 = (jnp.dot(h2, wq_ref[...], preferred_element_type=F32) * (xhd ** -0.5)).astype(BF16)
    outs = []
    for hh in range(X_HEADS):
        cs = slice(hh * xhd, (hh + 1) * xhd)
        s = jnp.dot(q2[:, cs], mkT_ref[cs, :], preferred_element_type=F32)
        s = s - jnp.max(s, axis=-1, keepdims=True)
        p = jnp.exp(s)
        p = (p / jnp.sum(p, axis=-1, keepdims=True)).astype(BF16)
        outs.append(jnp.dot(p, mv_ref[:, cs], preferred_element_type=F32).astype(BF16))
    o2 = jnp.concatenate(outs, axis=1)
    x2 = x1 + jnp.dot(o2, wo_ref[...], preferred_element_type=F32)
    x2_ref[...] = x2

    h3 = _rms_rows(x2, fg_ref[...])
    hi = h3.astype(BF16)
    h3_ref[...] = hi
    lo = (h3 - hi.astype(F32)).astype(BF16)
    nt = (((1,), (1,)), ((), ()))
    lg = (lax.dot_general(wrh_ref[...], hi, nt, preferred_element_type=F32)
          + lax.dot_general(wrh_ref[...], lo, nt, preferred_element_type=F32)
          + lax.dot_general(wrl_ref[...], hi, nt, preferred_element_type=F32))
    lg = lg - jnp.max(lg, axis=0, keepdims=True)
    ex = jnp.exp(lg)
    affT_ref[...] = ex / jnp.sum(ex, axis=0, keepdims=True)


def _post(x2d, gm, atT, bg1, w_out, xg, w_q, mkT, mv, w_o, fg, wr_hi, wr_lo, *, layer, seq, tm):
    T, D = x2d.shape
    npb = seq // tm
    M = mv.shape[2]
    E = wr_hi.shape[0]
    full = lambda shape: pl.BlockSpec(shape, lambda i: (0,) * len(shape))
    return pl.pallas_call(
        _post_kernel,
        grid=(T // tm,),
        in_specs=[
            pl.BlockSpec((tm, D), lambda i: (i, 0)),
            pl.BlockSpec((tm, GM_WIDTH), lambda i: (i, 0)),
            pl.BlockSpec((None, ATT_WIDTH, tm), lambda i: (i // npb, 0, i % npb)),
            full((ATT_WIDTH, tm)),
            full((GM_WIDTH + ATT_WIDTH, D)),
            full((1, D)),
            full((D, D)),
            pl.BlockSpec((None, None, D, M), lambda i: (layer, i // npb, 0, 0)),
            pl.BlockSpec((None, None, M, D), lambda i: (layer, i // npb, 0, 0)),
            full((D, D)),
            full((1, D)),
            full((E, D)),
            full((E, D)),
        ],
        out_specs=[
            pl.BlockSpec((tm, D), lambda i: (i, 0)),
            pl.BlockSpec((tm, D), lambda i: (i, 0)),
            pl.BlockSpec((E, tm), lambda i: (0, i)),
        ],
        out_shape=[
            jax.ShapeDtypeStruct((T, D), F32),
            jax.ShapeDtypeStruct((T, D), BF16),
            jax.ShapeDtypeStruct((E, T), F32),
        ],
        compiler_params=_cparams(("parallel",)),
        name="post",
    )(x2d, gm, atT, bg1, w_out, xg, w_q, mkT, mv, w_o, fg, wr_hi, wr_lo)


def _topk_kernel(aff_ref, slot_ref, off_ref, *, cap):
    E, R, L = aff_ref.shape
    a = aff_ref[...]
    bits = lax.bitcast_convert_type(a, jnp.int32)

    def count_ge(t):
        c = jnp.where(bits >= t, 1.0, 0.0)
        return jnp.sum(jnp.sum(c, axis=2, keepdims=True), axis=1, keepdims=True)

    def bis(_, carry):
        lo, hi = carry
        mid = lo + ((hi - lo) >> 1)
        ok = count_ge(mid) >= cap
        return jnp.where(ok, mid, lo), jnp.where(ok, hi, mid)

    lo0 = jnp.zeros((E, 1, 1), jnp.int32)
    hi0 = jnp.full((E, 1, 1), 0x3F800001, jnp.int32)
    thr, _ = lax.fori_loop(0, 31, bis, (lo0, hi0))

    kk = lax.broadcasted_iota(jnp.int32, (L, L), 0)
    nn = lax.broadcasted_iota(jnp.int32, (L, L), 1)
    upper = jnp.where(kk <= nn, 1.0, 0.0).astype(BF16)
    ones = jnp.ones((L, L), BF16)
    rr = lax.broadcasted_iota(jnp.int32, (E * R, E * R), 0)
    cc = lax.broadcasted_iota(jnp.int32, (E * R, E * R), 1)
    lower = jnp.where((rr // R == cc // R) & (cc < rr), 1.0, 0.0).astype(BF16)

    def prefix(xf):
        x2 = xf.reshape(E * R, L)
        xb = x2.astype(BF16)
        incl = jnp.dot(xb, upper, preferred_element_type=F32)
        tot = jnp.dot(xb, ones, preferred_element_type=F32).astype(BF16)
        rowoff = jnp.dot(lower, tot, preferred_element_type=F32)
        return (incl - x2 + rowoff).reshape(E, R, L), rowoff.reshape(E, R, L)

    gt = jnp.where(bits > thr, 1.0, 0.0)
    eq = jnp.where(bits == thr, 1.0, 0.0)
    n_gt = jnp.sum(jnp.sum(gt, axis=2, keepdims=True), axis=1, keepdims=True)
    need = cap - n_gt
    eq_rank, _ = prefix(eq)
    sel = gt + eq * jnp.where(eq_rank < need, 1.0, 0.0)
    pos, rowoff = prefix(sel)
    slot_ref[...] = jnp.where(sel > 0.0, pos, -1.0).astype(jnp.int32)
    off_ref[...] = rowoff.astype(jnp.int32)


def _topk(aff4, *, cap):
    E, B, R, L = aff4.shape
    return pl.pallas_call(
        functools.partial(_topk_kernel, cap=cap),
        grid=(B,),
        in_specs=[pl.BlockSpec((E, None, R, L), lambda b: (0, b, 0, 0))],
        out_specs=[
            pl.BlockSpec((None, E, R, L), lambda b: (b, 0, 0, 0)),
            pl.BlockSpec((None, E, R, L), lambda b: (b, 0, 0, 0)),
        ],
        out_shape=[
            jax.ShapeDtypeStruct((B, E, R, L), jnp.int32),
            jax.ShapeDtypeStruct((B, E, R, L), jnp.int32),
        ],
        compiler_params=_cparams(("parallel",)),
        name="topk",
    )(aff4)


def _last_block_below(cnt_ref, base, nblk, bound, strict):
    pos = jnp.int32(0)
    step = nblk // 2
    while step >= 1:
        cand = pos + step
        v = cnt_ref[base + cand]
        ok = (v < bound) if strict else (v <= bound)
        pos = jnp.where(ok, cand, pos)
        step //= 2
    return pos


def _gather_kernel(cnt_ref, slot_ref, h_ref, xs_ref, *, win, unroll):
    b = pl.program_id(0)
    e = pl.program_id(1)
    ne = pl.num_programs(1)
    cap, D = xs_ref.shape
    nblk = slot_ref.shape[0]
    base = (b * ne + e) * (nblk + 1)
    r_iota = lax.broadcasted_iota(jnp.int32, (win, TOK_BLOCK), 0)

    def contrib(j, lo):
        jc = jnp.minimum(j, nblk - 1)
        srow = slot_ref[jc]
        rel = jnp.where(j < nblk, srow - lo, -1)
        oh = jnp.where(rel == r_iota, 1.0, 0.0).astype(BF16)
        hb = h_ref[pl.ds(pl.multiple_of(jc * TOK_BLOCK, TOK_BLOCK), TOK_BLOCK), :]
        return jnp.dot(oh, hb, preferred_element_type=F32)

    spans = []
    left = jnp.int32(0)
    for r in range(cap // win):
        lo = r * win
        j_lo = _last_block_below(cnt_ref, base, nblk, lo, False)
        j_hi = _last_block_below(cnt_ref, base, nblk, lo + win, True)
        acc = contrib(j_lo, lo)
        for u in range(1, unroll):
            acc = acc + contrib(j_lo + u, lo)
        xs_ref[lo:lo + win, :] = acc.astype(BF16)
        spans.append((j_lo + unroll, j_hi + 1))
        left = jnp.maximum(left, j_hi + 1 - (j_lo + unroll))

    @pl.when(left > 0)
    def _():
        for r, (j0, j1) in enumerate(spans):
            lo = r * win

            def more(j, carry, lo=lo):
                xs_ref[lo:lo + win, :] = (xs_ref[lo:lo + win, :].astype(F32) + contrib(j, lo)).astype(BF16)
                return carry

            lax.fori_loop(j0, j1, more, 0)


def _gather(cnt, slots5, h3d, *, cap, win):
    B, seq, D = h3d.shape
    E = slots5.shape[1]
    nblk = seq // TOK_BLOCK
    assert nblk & (nblk - 1) == 0 and cap % win == 0
    gs = pltpu.PrefetchScalarGridSpec(
        num_scalar_prefetch=1,
        grid=(B, E),
        in_specs=[
            pl.BlockSpec((None, None, nblk, 1, TOK_BLOCK), lambda b, e, c: (b, e, 0, 0, 0)),
            pl.BlockSpec((None, seq, D), lambda b, e, c: (b, 0, 0), pipeline_mode=pl.Buffered(1)),
        ],
        out_specs=pl.BlockSpec((None, None, cap, D), lambda b, e, c: (b, e, 0, 0)),
    )
    return pl.pallas_call(
        functools.partial(_gather_kernel, win=win, unroll=min(6, nblk)),
        grid_spec=gs,
        out_shape=jax.ShapeDtypeStruct((B, E, cap, D), BF16),
        compiler_params=_cparams(("parallel", "arbitrary")),
        name="gather",
    )(cnt, slots5, h3d)


def _ffn_kernel(xs_ref, wg_ref, wu_ref, wd_ref, y_ref, wg_scr, wu_scr, wd_scr, *, f_chunk):
    @pl.when(pl.program_id(1) == 0)
    def _():
        wg_scr[...] = wg_ref[...].astype(BF16)
        wu_scr[...] = wu_ref[...].astype(BF16)
        wd_scr[...] = wd_ref[...].astype(BF16)

    xs = xs_ref[...]
    y = None
    for f0 in range(0, wg_scr.shape[1], f_chunk):
        fs = slice(f0, f0 + f_chunk)
        a = jnp.dot(xs, wg_scr[:, fs], preferred_element_type=F32)
        u = jnp.dot(xs, wu_scr[:, fs], preferred_element_type=F32)
        hmid = (a * jax.nn.sigmoid(a) * u).astype(BF16)
        part = jnp.dot(hmid, wd_scr[fs, :], preferred_element_type=F32)
        y = part if y is None else y + part
    y_ref[...] = y.astype(BF16)


def _ffn(xs, w_gate, w_up, w_down, *, layer):
    B, E, cap, D = xs.shape
    Fd = w_gate.shape[-1]
    return pl.pallas_call(
        functools.partial(_ffn_kernel, f_chunk=min(512, Fd)),
        grid=(E, B),
        in_specs=[
            pl.BlockSpec((None, None, cap, D), lambda e, b: (b, e, 0, 0)),
            pl.BlockSpec((None, None, D, Fd), lambda e, b: (layer, e, 0, 0)),
            pl.BlockSpec((None, None, D, Fd), lambda e, b: (layer, e, 0, 0)),
            pl.BlockSpec((None, None, Fd, D), lambda e, b: (layer, e, 0, 0)),
        ],
        out_specs=pl.BlockSpec((None, None, cap, D), lambda e, b: (b, e, 0, 0)),
        out_shape=jax.ShapeDtypeStruct((B, E, cap, D), BF16),
        scratch_shapes=[
            pltpu.VMEM((D, Fd), BF16),
            pltpu.VMEM((D, Fd), BF16),
            pltpu.VMEM((Fd, D), BF16),
        ],
        compiler_params=_cparams(("parallel", "arbitrary")),
        name="ffn",
    )(xs, w_gate, w_up, w_down)


def _scatter_kernel(cnt_ref, slot_ref, gate_ref, y_ref, x_ref, fg_ref, o_ref, *, win, final_norm):
    b = pl.program_id(0)
    sb = pl.program_id(1)
    ne, cap, D = y_ref.shape
    nloc = slot_ref.shape[0] // TOK_BLOCK
    nblk = nloc * pl.num_programs(1)
    lane = lax.broadcasted_iota(jnp.int32, (TOK_BLOCK, win), 1)

    def gated_onehot(scol, gcol, lo, sa):
        rel = jnp.where(scol >= lo, scol - sa, -1)
        return jnp.where(rel == lane, gcol, 0.0).astype(BF16)

    extra = jnp.int32(0)
    for t in range(nloc):
        j = sb * nloc + t
        rows = slice(t * TOK_BLOCK, (t + 1) * TOK_BLOCK)
        sl = slot_ref[rows, :]
        gt = gate_ref[rows, :]
        acc = x_ref[rows, :]
        for e0 in range(0, ne, 2):
            gs, ys = [], []
            for e in (e0, e0 + 1):
                idx = (b * ne + e) * (nblk + 1) + j
                s0 = cnt_ref[idx]
                s1 = cnt_ref[idx + 1]
                a0 = (s0 // 16) * 16
                sa = pl.multiple_of(jnp.minimum(a0, cap - win), 16)
                extra = jnp.maximum(extra, s1 - (a0 + win))
                gs.append(gated_onehot(sl[:, e:e + 1], gt[:, e:e + 1], a0, sa))
                ys.append(y_ref[e, pl.ds(sa, win), :])
            acc = acc + jnp.dot(jnp.concatenate(gs, axis=1), jnp.concatenate(ys, axis=0),
                                preferred_element_type=F32)
        o_ref[rows, :] = acc

    @pl.when(extra > 0)
    def _():
        for t in range(nloc):
            j = sb * nloc + t
            rows = slice(t * TOK_BLOCK, (t + 1) * TOK_BLOCK)
            sl = slot_ref[rows, :]
            gt = gate_ref[rows, :]
            for e in range(ne):
                idx = (b * ne + e) * (nblk + 1) + j
                s0 = cnt_ref[idx]
                s1 = cnt_ref[idx + 1]
                a0 = (s0 // 16) * 16
                nw = (s1 - a0 + win - 1) // win

                def wbody(w, carry):
                    lo = a0 + w * win
                    sa = pl.multiple_of(jnp.minimum(lo, cap - win), 16)
                    g = gated_onehot(sl[:, e:e + 1], gt[:, e:e + 1], lo, sa)
                    o_ref[rows, :] += jnp.dot(g, y_ref[e, pl.ds(sa, win), :], preferred_element_type=F32)
                    return carry

                lax.fori_loop(1, nw, wbody, 0)

    if final_norm:
        o_ref[...] = _rms_rows(o_ref[...], fg_ref[...])


def _scatter(cnt, slotsT, gatesT, y, x3d, fg, *, win, sb_rows, final_norm):
    B, seq, D = x3d.shape
    E, cap = y.shape[1], y.shape[2]
    gs = pltpu.PrefetchScalarGridSpec(
        num_scalar_prefetch=1,
        grid=(B, seq // sb_rows),
        in_specs=[
            pl.BlockSpec((None, sb_rows, E), lambda b, s, c: (b, s, 0)),
            pl.BlockSpec((None, sb_rows, E), lambda b, s, c: (b, s, 0)),
            pl.BlockSpec((None, E, cap, D), lambda b, s, c: (b, 0, 0, 0), pipeline_mode=pl.Buffered(1)),
            pl.BlockSpec((None, sb_rows, D), lambda b, s, c: (b, s, 0)),
            pl.BlockSpec((1, D), lambda b, s, c: (0, 0)),
        ],
        out_specs=pl.BlockSpec((None, sb_rows, D), lambda b, s, c: (b, s, 0)),
    )
    return pl.pallas_call(
        functools.partial(_scatter_kernel, win=win, final_norm=final_norm),
        grid_spec=gs,
        out_shape=jax.ShapeDtypeStruct((B, seq, D), F32),
        compiler_params=_cparams(("parallel", "arbitrary")),
        name="scatter",
    )(cnt, slotsT, gatesT, y, x3d, fg)


def _rope_tables_T(seq):
    rows = seq // GRID_W
    row_id = jnp.repeat(jnp.arange(rows, dtype=F32), GRID_W)
    col_id = jnp.tile(jnp.arange(GRID_W, dtype=F32), rows)
    n_pairs = HEAD_DIM // 4
    freqs = jnp.exp(-math.log(ROPE_THETA) * jnp.arange(n_pairs, dtype=F32) / n_pairs)
    ang = jnp.concatenate([freqs[:, None] * row_id[None, :], freqs[:, None] * col_id[None, :]], axis=0)
    return jnp.cos(ang), jnp.sin(ang)


def _head_perm(n_heads):
    base = jnp.concatenate([jnp.arange(0, HEAD_DIM, 2), jnp.arange(1, HEAD_DIM, 2)])
    return (jnp.arange(n_heads)[:, None] * HEAD_DIM + base[None, :]).reshape(-1)


def kernel(x, mem, mix_norm_g, w_in, gm_v_norm_g, gm_w_s, gm_b_s, q_norm_g, k_norm_g, branch_norm_g, w_out,
           xattn_norm_g, mem_norm_g, xattn_w_q, xattn_w_kv, xattn_w_o, ffn_norm_g, w_router, w_gate, w_up,
           w_down, final_norm_g):
    B, seq, D = x.shape
    L = w_in.shape[0]
    E = w_router.shape[-1]
    T = B * seq
    cap = EC_FACTOR * seq // E
    tm = min(512, seq)
    tq = min(256, seq)
    tk = min(512, seq)
    win = min(128, cap)
    sb_rows = min(512, seq)
    nblk = seq // TOK_BLOCK

    cosT, sinT = _rope_tables_T(seq)
    o_q = 2 * GM_WIDTH
    o_k = o_q + ATT_WIDTH
    o_v = o_k + KV_WIDTH
    cols = jnp.concatenate([jnp.arange(o_q), o_q + _head_perm(N_Q_HEADS), o_k + _head_perm(N_KV_HEADS),
                            jnp.arange(o_v, o_v + KV_WIDTH)])
    hp = _head_perm(1)

    mkT_all, mv_all = _mem_kv(mem, mem_norm_g.reshape(1, D), xattn_w_kv.astype(BF16))

    x2d = x.reshape(T, D)
    out = None
    for l in range(L):
        w_in_l = w_in[l][:, cols].astype(BF16)
        bs = jnp.broadcast_to(gm_b_s[l][:, :, None], (GM_GROUPS, CHUNK, LANES))
        qg = jnp.broadcast_to(q_norm_g[l][hp][:, None], (HEAD_DIM, tm))
        kg = jnp.broadcast_to(k_norm_g[l][hp][:, None], (HEAD_DIM, tm))
        gm, qT, k, vT = _mixer_in(
            x2d, mix_norm_g[l].reshape(1, D), w_in_l, gm_v_norm_g[l].reshape(1, GM_WIDTH),
            gm_w_s[l].astype(BF16), bs, qg, kg, branch_norm_g[l, 0].reshape(1, GM_WIDTH), cosT, sinT,
            seq=seq, tm=tm)
        atT = _attention(qT, k.reshape(B, seq, KV_WIDTH), vT, tq=tq, tk=tk, qb=min(256, tq))

        bg1 = jnp.broadcast_to(branch_norm_g[l, 1][:, None], (ATT_WIDTH, tm))
        wr = w_router[l].T
        wr_hi = wr.astype(BF16)
        wr_lo = (wr - wr_hi.astype(F32)).astype(BF16)
        x2, h3, affT = _post(
            x2d, gm, atT, bg1, w_out[l].astype(BF16), xattn_norm_g[l].reshape(1, D),
            xattn_w_q[l].astype(BF16), mkT_all, mv_all, xattn_w_o[l].astype(BF16),
            ffn_norm_g[l].reshape(1, D), wr_hi, wr_lo, layer=l, seq=seq, tm=tm)

        slots, offs = _topk(affT.reshape(E, B, seq // LANES, LANES), cap=cap)
        cnt = offs[:, :, ::TOK_BLOCK // LANES, 0]
        cnt = jnp.concatenate([cnt, jnp.full((B, E, 1), cap, jnp.int32)], axis=-1).reshape(-1)
        slots5 = slots.reshape(B, E, nblk, 1, TOK_BLOCK)
        slotsT = slots.reshape(B, E, seq).transpose(0, 2, 1)
        gatesT = affT.reshape(E, B, seq).transpose(1, 2, 0)

        xs = _gather(cnt, slots5, h3.reshape(B, seq, D), cap=cap, win=win)
        y = _ffn(xs, w_gate, w_up, w_down, layer=l)
        out = _scatter(cnt, slotsT, gatesT, y, x2.reshape(B, seq, D), final_norm_g.reshape(1, D),
                       win=win, sb_rows=sb_rows, final_norm=(l == L - 1))
        x2d = out.reshape(T, D)
    return out
```

```python
import functools
import math

import jax
import jax.numpy as jnp
from jax import lax
from jax.experimental import pallas as pl
from jax.experimental.pallas import tpu as pltpu

F32 = jnp.float32
BF16 = jnp.bfloat16

EPS = 1e-6
CHUNK = 128
GM_GROUPS = 4
GM_WIDTH = 512
HEAD_DIM = 64
N_Q_HEADS = 8
N_KV_HEADS = 2
Q_PER_KV = N_Q_HEADS // N_KV_HEADS
ATT_WIDTH = N_Q_HEADS * HEAD_DIM
KV_WIDTH = N_KV_HEADS * HEAD_DIM
ROPE_THETA = 10000.0
GRID_W = 64
X_HEADS = 4
N_EXPERTS = 16
EC_FACTOR = 2

LANES = 128
VMEM_LIMIT = 56 * 1024 * 1024

TOK_BLOCK = 256
LOG2E = 1.4426950408889634


def _cparams(sem, vmem=VMEM_LIMIT, **kw):
    return pltpu.CompilerParams(dimension_semantics=sem, vmem_limit_bytes=vmem, **kw)


def _rms_rows(x, g):
    ms = jnp.mean(x * x, axis=-1, keepdims=True)
    return x * lax.rsqrt(ms + EPS) * g


def _gelu_tanh(x):
    c = math.sqrt(2.0 / math.pi)
    return 0.5 * x * (1.0 + jnp.tanh(c * (x + 0.044715 * (x * x * x))))


def _norm_rope_T(xT, gcol, cosT, sinT, n_heads, scale):
    half = HEAD_DIM // 2
    outs = []
    for h in range(n_heads):
        blk = xT[h * HEAD_DIM:(h + 1) * HEAD_DIM, :]
        ms = jnp.mean(blk * blk, axis=0, keepdims=True)
        n = blk * lax.rsqrt(ms + EPS) * gcol
        e = n[:half, :]
        o = n[half:, :]
        re = (e * cosT - o * sinT) * scale
        ro = (e * sinT + o * cosT) * scale
        outs.append(jnp.concatenate([re, ro], axis=0))
    return outs


def _mixer_in_kernel(x_ref, g_ref, w_ref, vg_ref, ws_ref, bs_ref, qg_ref, kg_ref, bg0_ref,
                     cos_ref, sin_ref, gm_ref, qT_ref, k_ref, vT_ref, gm_scr, *, rows_per_group):
    tm = x_ref.shape[0]
    o_q = 2 * GM_WIDTH
    o_k = o_q + ATT_WIDTH
    o_v = o_k + KV_WIDTH
    groups = [slice(r, r + rows_per_group) for r in range(0, tm, rows_per_group)]

    projs = []
    for rs in groups:
        h = _rms_rows(x_ref[rs, :], g_ref[...]).astype(BF16)
        projs.append(jnp.dot(h, w_ref[...], preferred_element_type=F32))

    for rs, proj in zip(groups, projs):
        u = _gelu_tanh(proj[:, :GM_WIDTH])
        v = _gelu_tanh(proj[:, GM_WIDTH:2 * GM_WIDTH])
        vn = _rms_rows(v, vg_ref[...]).astype(BF16)
        for c in range(rows_per_group // CHUNK):
            cr = slice(c * CHUNK, (c + 1) * CHUNK)
            sr = slice(rs.start + c * CHUNK, rs.start + (c + 1) * CHUNK)
            for g in range(GM_GROUPS):
                cs = slice(g * LANES, (g + 1) * LANES)
                mixed = jnp.dot(ws_ref[g], vn[cr, cs], preferred_element_type=F32) + bs_ref[g]
                gm_scr[sr, cs] = u[cr, cs] * mixed
        gm_ref[rs, :] = _rms_rows(gm_scr[rs, :], bg0_ref[...]).astype(BF16)

        cosT = cos_ref[:, rs]
        sinT = sin_ref[:, rs]
        qT = proj[:, o_q:o_k].T
        q_heads = _norm_rope_T(qT, qg_ref[:, rs], cosT, sinT, N_Q_HEADS, (HEAD_DIM ** -0.5) * LOG2E)
        for hh in range(N_Q_HEADS):
            qT_ref[hh * HEAD_DIM:(hh + 1) * HEAD_DIM, rs] = q_heads[hh].astype(BF16)

        kT = proj[:, o_k:o_v].T
        k_heads = _norm_rope_T(kT, kg_ref[:, rs], cosT, sinT, N_KV_HEADS, 1.0)
        k_ref[rs, :] = jnp.concatenate(k_heads, axis=0).T.astype(BF16)

        vT_ref[:, rs] = proj[:, o_v:].T.astype(BF16)


def _mixer_in(x2d, g, w_in, vg, ws, bs, qg, kg, bg0, cosT, sinT, *, seq, tm):
    T, D = x2d.shape
    B = T // seq
    npb = seq // tm
    in_w = w_in.shape[1]
    full = lambda shape: pl.BlockSpec(shape, lambda i: (0,) * len(shape))
    return pl.pallas_call(
        functools.partial(_mixer_in_kernel, rows_per_group=min(256, tm)),
        grid=(T // tm,),
        in_specs=[
            pl.BlockSpec((tm, D), lambda i: (i, 0)),
            full((1, D)),
            full((D, in_w)),
            full((1, GM_WIDTH)),
            full((GM_GROUPS, CHUNK, CHUNK)),
            full((GM_GROUPS, CHUNK, LANES)),
            full((HEAD_DIM, tm)),
            full((HEAD_DIM, tm)),
            full((1, GM_WIDTH)),
            pl.BlockSpec((HEAD_DIM // 2, tm), lambda i: (0, i % npb)),
            pl.BlockSpec((HEAD_DIM // 2, tm), lambda i: (0, i % npb)),
        ],
        out_specs=[
            pl.BlockSpec((tm, GM_WIDTH), lambda i: (i, 0)),
            pl.BlockSpec((None, ATT_WIDTH, tm), lambda i: (i // npb, 0, i % npb)),
            pl.BlockSpec((tm, KV_WIDTH), lambda i: (i, 0)),
            pl.BlockSpec((None, KV_WIDTH, tm), lambda i: (i // npb, 0, i % npb)),
        ],
        out_shape=[
            jax.ShapeDtypeStruct((T, GM_WIDTH), BF16),
            jax.ShapeDtypeStruct((B, ATT_WIDTH, seq), BF16),
            jax.ShapeDtypeStruct((T, KV_WIDTH), BF16),
            jax.ShapeDtypeStruct((B, KV_WIDTH, seq), BF16),
        ],
        scratch_shapes=[pltpu.VMEM((tm, GM_WIDTH), F32)],
        compiler_params=_cparams(("parallel",)),
        name="mixer_in",
    )(x2d, g, w_in, vg, ws, bs, qg, kg, bg0, cosT, sinT)


def _attn_kernel(qT_ref, k_ref, vT_ref, o_ref, vext_scr, qx_scr, m_scr, acc_scr, s_scr, cm_scr, *, tq, tk, qb):
    kh = pl.program_id(1)
    seq = k_ref.shape[0]
    M = Q_PER_KV * tq
    nq = seq // tq
    nchunk = seq // tk

    vext_scr[0:HEAD_DIM, :] = vT_ref[...]
    row = lax.broadcasted_iota(jnp.int32, (HEAD_DIM, seq), 0)
    vext_scr[HEAD_DIM:, :] = jnp.where(row == 0, 1.0, 0.0).astype(BF16)

    def load_q(i, slot):
        cols = pl.ds(pl.multiple_of(i * tq, tq), tq)
        q4 = jnp.concatenate([qT_ref[g * HEAD_DIM:(g + 1) * HEAD_DIM, cols] for g in range(Q_PER_KV)], axis=1)
        z = jnp.zeros_like(q4)
        qx_scr[slot] = jnp.where(kh == 0, jnp.concatenate([q4, z], axis=0), jnp.concatenate([z, q4], axis=0))

    def step(c, qslot, qk_chunk, do_pv):
        cur = c % 2
        nxt = 1 - cur
        if qk_chunk is not None:
            kn = k_ref[qk_chunk * tk:(qk_chunk + 1) * tk, :]
        if do_pv:
            ve = vext_scr[:, c * tk:(c + 1) * tk]
        for j in range(M // qb):
            cs = slice(j * qb, (j + 1) * qb)
            if qk_chunk is not None:
                s = jnp.dot(kn, qx_scr[qslot, :, cs], preferred_element_type=F32)
                s_scr[nxt, :, cs] = s
                cm_scr[nxt, :, cs] = jnp.max(s, axis=0, keepdims=True)
            if do_pv:
                m_old = m_scr[:, cs]
                m_new = jnp.maximum(m_old, cm_scr[cur, :, cs])
                p = jnp.exp2(s_scr[cur, :, cs] - m_new).astype(BF16)
                alpha = jnp.exp2(m_old - m_new)
                pv = jnp.dot(ve, p, preferred_element_type=F32)
                acc_scr[:, cs] = acc_scr[:, cs] * alpha + pv
                m_scr[:, cs] = m_new

    load_q(0, 0)
    step(-1, 0, 0, False)

    def qblock(i, carry):
        slot = i % 2
        m_scr[...] = jnp.full(m_scr.shape, -jnp.inf, F32)
        acc_scr[...] = jnp.zeros(acc_scr.shape, F32)
        for c in range(nchunk - 1):
            step(c, slot, c + 1, True)
        load_q(jnp.minimum(i + 1, nq - 1), 1 - slot)
        step(nchunk - 1, 1 - slot, 0, True)

        acc = acc_scr[...]
        inv = 1.0 / acc[HEAD_DIM:HEAD_DIM + 1, :]
        o = acc[0:HEAD_DIM, :] * inv
        cols = pl.ds(pl.multiple_of(i * tq, tq), tq)
        for g in range(Q_PER_KV):
            o_ref[g * HEAD_DIM:(g + 1) * HEAD_DIM, cols] = o[:, g * tq:(g + 1) * tq].astype(BF16)
        return carry

    lax.fori_loop(0, nq, qblock, 0)


def _attention(qT, k, vT, *, tq, tk, qb):
    B, _, seq = qT.shape
    gw = Q_PER_KV * HEAD_DIM
    assert (seq // tk) % 2 == 0
    return pl.pallas_call(
        functools.partial(_attn_kernel, tq=tq, tk=tk, qb=qb),
        grid=(B, N_KV_HEADS),
        in_specs=[
            pl.BlockSpec((None, gw, seq), lambda b, kh: (b, kh, 0)),
            pl.BlockSpec((None, seq, KV_WIDTH), lambda b, kh: (b, 0, 0)),
            pl.BlockSpec((None, HEAD_DIM, seq), lambda b, kh: (b, kh, 0)),
        ],
        out_specs=pl.BlockSpec((None, gw, seq), lambda b, kh: (b, kh, 0)),
        out_shape=jax.ShapeDtypeStruct((B, ATT_WIDTH, seq), BF16),
        scratch_shapes=[
            pltpu.VMEM((2 * HEAD_DIM, seq), BF16),
            pltpu.VMEM((2, 2 * HEAD_DIM, Q_PER_KV * tq), BF16),
            pltpu.VMEM((1, Q_PER_KV * tq), F32),
            pltpu.VMEM((2 * HEAD_DIM, Q_PER_KV * tq), F32),
            pltpu.VMEM((2, tk, Q_PER_KV * tq), F32),
            pltpu.VMEM((2, 1, Q_PER_KV * tq), F32),
        ],
        compiler_params=_cparams(("parallel", "parallel")),
        name="attention",
    )(qT, k, vT)


def _mem_kv_kernel(mem_ref, g_ref, w_ref, kT_ref, v_ref):
    D = mem_ref.shape[1]
    mn = _rms_rows(mem_ref[...], g_ref[...]).astype(BF16)
    kv = jnp.dot(mn, w_ref[...], preferred_element_type=F32)
    kT_ref[...] = kv[:, :D].T.astype(BF16)
    v_ref[...] = kv[:, D:].astype(BF16)


def _mem_kv(mem, g, w_kv):
    B, M, D = mem.shape
    L = w_kv.shape[0]
    return pl.pallas_call(
        _mem_kv_kernel,
        grid=(L, B),
        in_specs=[
            pl.BlockSpec((None, M, D), lambda l, b: (b, 0, 0)),
            pl.BlockSpec((1, D), lambda l, b: (0, 0)),
            pl.BlockSpec((None, D, 2 * D), lambda l, b: (l, 0, 0)),
        ],
        out_specs=[
            pl.BlockSpec((None, None, D, M), lambda l, b: (l, b, 0, 0)),
            pl.BlockSpec((None, None, M, D), lambda l, b: (l, b, 0, 0)),
        ],
        out_shape=[
            jax.ShapeDtypeStruct((L, B, D, M), BF16),
            jax.ShapeDtypeStruct((L, B, M, D), BF16),
        ],
        compiler_params=_cparams(("parallel", "parallel")),
        name="mem_kv",
    )(mem, g, w_kv)


def _post_kernel(x_ref, gm_ref, atT_ref, bg1_ref, wout_ref, xg_ref, wq_ref, mkT_ref, mv_ref, wo_ref,
                 fg_ref, wr_ref, x2_ref, h3_ref, affT_ref, *, rows_per_group):
    tm, D = x_ref.shape
    xhd = D // X_HEADS
    groups = [slice(r, r + rows_per_group) for r in range(0, tm, rows_per_group)]

    x1 = []
    for rs in groups:
        at = atT_ref[:, rs].astype(F32)
        ms = jnp.mean(at * at, axis=0, keepdims=True)
        atn = (at * lax.rsqrt(ms + EPS) * bg1_ref[:, rs]).astype(BF16)
        y = jnp.dot(gm_ref[rs, :], wout_ref[0:GM_WIDTH, :], preferred_element_type=F32)
        y = y + lax.dot_general(atn, wout_ref[GM_WIDTH:, :], (((0,), (0,)), ((), ())),
                                preferred_element_type=F32)
        x1.append(x_ref[rs, :] + y)

    q2 = []
    for x1g in x1:
        h2 = _rms_rows(x1g, xg_ref[...]).astype(BF16)
        q2.append((jnp.dot(h2, wq_ref[...], preferred_element_type=F32) * (xhd ** -0.5)).astype(BF16))

    scores = {}
    for hh in range(X_HEADS):
        cs = slice(hh * xhd, (hh + 1) * xhd)
        for gi in range(len(groups)):
            scores[hh, gi] = jnp.dot(q2[gi][:, cs], mkT_ref[cs, :], preferred_element_type=F32)
    outs = [[] for _ in groups]
    for hh in range(X_HEADS):
        cs = slice(hh * xhd, (hh + 1) * xhd)
        for gi in range(len(groups)):
            s = scores[hh, gi]
            s = s - jnp.max(s, axis=-1, keepdims=True)
            p = jnp.exp(s)
            p = (p / jnp.sum(p, axis=-1, keepdims=True)).astype(BF16)
            outs[gi].append(jnp.dot(p, mv_ref[:, cs], preferred_element_type=F32).astype(BF16))

    x2 = []
    for gi, rs in enumerate(groups):
        o2 = jnp.concatenate(outs[gi], axis=1)
        x2g = x1[gi] + jnp.dot(o2, wo_ref[...], preferred_element_type=F32)
        x2_ref[rs, :] = x2g
        x2.append(x2g)

    ne = affT_ref.shape[0]
    for gi, rs in enumerate(groups):
        h3 = _rms_rows(x2[gi], fg_ref[...]).astype(BF16)
        h3_ref[rs, :] = h3
        lg = jnp.dot(h3, wr_ref[...], preferred_element_type=F32).T[0:ne, :]
        lg = lg - jnp.max(lg, axis=0, keepdims=True)
        ex = jnp.exp(lg)
        affT_ref[:, rs] = ex / jnp.sum(ex, axis=0, keepdims=True)


def _post(x2d, gm, atT, bg1, w_out, xg, w_q, mkT, mv, w_o, fg, wr_pad, *, n_experts, layer, seq, tm):
    T, D = x2d.shape
    npb = seq // tm
    M = mv.shape[2]
    E = n_experts
    full = lambda shape: pl.BlockSpec(shape, lambda i: (0,) * len(shape))
    return pl.pallas_call(
        functools.partial(_post_kernel, rows_per_group=min(256, tm)),
        grid=(T // tm,),
        in_specs=[
            pl.BlockSpec((tm, D), lambda i: (i, 0)),
            pl.BlockSpec((tm, GM_WIDTH), lambda i: (i, 0)),
            pl.BlockSpec((None, ATT_WIDTH, tm), lambda i: (i // npb, 0, i % npb)),
            full((ATT_WIDTH, tm)),
            full((GM_WIDTH + ATT_WIDTH, D)),
            full((1, D)),
            full((D, D)),
            pl.BlockSpec((None, None, D, M), lambda i: (layer, i // npb, 0, 0)),
            pl.BlockSpec((None, None, M, D), lambda i: (layer, i // npb, 0, 0)),
            full((D, D)),
            full((1, D)),
            full((D, LANES)),
        ],
        out_specs=[
            pl.BlockSpec((tm, D), lambda i: (i, 0)),
            pl.BlockSpec((tm, D), lambda i: (i, 0)),
            pl.BlockSpec((E, tm), lambda i: (0, i)),
        ],
        out_shape=[
            jax.ShapeDtypeStruct((T, D), F32),
            jax.ShapeDtypeStruct((T, D), BF16),
            jax.ShapeDtypeStruct((E, T), F32),
        ],
        compiler_params=_cparams(("parallel",)),
        name="post",
    )(x2d, gm, atT, bg1, w_out, xg, w_q, mkT, mv, w_o, fg, wr_pad)


def _topk_kernel(aff_ref, slot_ref, off_ref, *, cap):
    E, R, L = aff_ref.shape
    a = aff_ref[...]
    bits = lax.bitcast_convert_type(a, jnp.int32)

    def count_ge(t):
        c = jnp.where(bits >= t, 1.0, 0.0)
        return jnp.sum(jnp.sum(c, axis=2, keepdims=True), axis=1, keepdims=True)

    def bis(_, carry):
        lo, hi = carry
        mid = lo + ((hi - lo) >> 1)
        ok = count_ge(mid) >= cap
        return jnp.where(ok, mid, lo), jnp.where(ok, hi, mid)

    lo0 = jnp.zeros((E, 1, 1), jnp.int32)
    hi0 = jnp.full((E, 1, 1), 0x3F800001, jnp.int32)
    thr, _ = lax.fori_loop(0, 31, bis, (lo0, hi0))

    kk = lax.broadcasted_iota(jnp.int32, (L, L), 0)
    nn = lax.broadcasted_iota(jnp.int32, (L, L), 1)
    upper = jnp.where(kk <= nn, 1.0, 0.0).astype(BF16)
    ones = jnp.ones((L, L), BF16)
    rr = lax.broadcasted_iota(jnp.int32, (E * R, E * R), 0)
    cc = lax.broadcasted_iota(jnp.int32, (E * R, E * R), 1)
    lower = jnp.where((rr // R == cc // R) & (cc < rr), 1.0, 0.0).astype(BF16)

    def prefix(xf):
        x2 = xf.reshape(E * R, L)
        xb = x2.astype(BF16)
        incl = jnp.dot(xb, upper, preferred_element_type=F32)
        tot = jnp.dot(xb, ones, preferred_element_type=F32).astype(BF16)
        rowoff = jnp.dot(lower, tot, preferred_element_type=F32)
        return (incl - x2 + rowoff).reshape(E, R, L), rowoff.reshape(E, R, L)

    gt = jnp.where(bits > thr, 1.0, 0.0)
    eq = jnp.where(bits == thr, 1.0, 0.0)
    n_gt = jnp.sum(jnp.sum(gt, axis=2, keepdims=True), axis=1, keepdims=True)
    need = cap - n_gt
    eq_rank, _ = prefix(eq)
    sel = gt + eq * jnp.where(eq_rank < need, 1.0, 0.0)
    pos, rowoff = prefix(sel)
    slot_ref[...] = jnp.where(sel > 0.0, pos, -1.0).astype(jnp.int32)
    off_ref[...] = rowoff.astype(jnp.int32)


def _topk(aff4, *, cap):
    E, B, R, L = aff4.shape
    return pl.pallas_call(
        functools.partial(_topk_kernel, cap=cap),
        grid=(B,),
        in_specs=[pl.BlockSpec((E, None, R, L), lambda b: (0, b, 0, 0))],
        out_specs=[
            pl.BlockSpec((None, E, R, L), lambda b: (b, 0, 0, 0)),
            pl.BlockSpec((None, E, R, L), lambda b: (b, 0, 0, 0)),
        ],
        out_shape=[
            jax.ShapeDtypeStruct((B, E, R, L), jnp.int32),
            jax.ShapeDtypeStruct((B, E, R, L), jnp.int32),
        ],
        compiler_params=_cparams(("parallel",)),
        name="topk",
    )(aff4)


def _last_block_below(cnt_ref, base, nblk, bound, strict):
    pos = jnp.int32(0)
    step = nblk // 2
    while step >= 1:
        cand = pos + step
        v = cnt_ref[base + cand]
        ok = (v < bound) if strict else (v <= bound)
        pos = jnp.where(ok, cand, pos)
        step //= 2
    return pos


def _gather_kernel(cnt_ref, slot_ref, h_ref, xs_ref, *, win, unroll):
    b = pl.program_id(0)
    e = pl.program_id(1)
    ne = pl.num_programs(1)
    cap, D = xs_ref.shape
    nblk = slot_ref.shape[0]
    base = (b * ne + e) * (nblk + 1)
    r_iota = lax.broadcasted_iota(jnp.int32, (win, TOK_BLOCK), 0)

    def contrib(j, lo):
        jc = jnp.minimum(j, nblk - 1)
        srow = slot_ref[jc]
        rel = jnp.where(j < nblk, srow - lo, -1)
        oh = jnp.where(rel == r_iota, 1.0, 0.0).astype(BF16)
        hb = h_ref[pl.ds(pl.multiple_of(jc * TOK_BLOCK, TOK_BLOCK), TOK_BLOCK), :]
        return jnp.dot(oh, hb, preferred_element_type=F32)

    spans = []
    left = jnp.int32(0)
    for r in range(cap // win):
        lo = r * win
        j_lo = _last_block_below(cnt_ref, base, nblk, lo, False)
        j_hi = _last_block_below(cnt_ref, base, nblk, lo + win, True)
        acc = contrib(j_lo, lo)
        for u in range(1, unroll):
            acc = acc + contrib(j_lo + u, lo)
        xs_ref[lo:lo + win, :] = acc.astype(BF16)
        spans.append((j_lo + unroll, j_hi + 1))
        left = jnp.maximum(left, j_hi + 1 - (j_lo + unroll))

    @pl.when(left > 0)
    def _():
        for r, (j0, j1) in enumerate(spans):
            lo = r * win

            def more(j, carry, lo=lo):
                xs_ref[lo:lo + win, :] = (xs_ref[lo:lo + win, :].astype(F32) + contrib(j, lo)).astype(BF16)
                return carry

            lax.fori_loop(j0, j1, more, 0)


def _gather(cnt, slots5, h3d, *, cap, win):
    B, seq, D = h3d.shape
    E = slots5.shape[1]
    nblk = seq // TOK_BLOCK
    assert nblk & (nblk - 1) == 0 and cap % win == 0
    gs = pltpu.PrefetchScalarGridSpec(
        num_scalar_prefetch=1,
        grid=(B, E),
        in_specs=[
            pl.BlockSpec((None, None, nblk, 1, TOK_BLOCK), lambda b, e, c: (b, e, 0, 0, 0)),
            pl.BlockSpec((None, seq, D), lambda b, e, c: (b, 0, 0), pipeline_mode=pl.Buffered(1)),
        ],
        out_specs=pl.BlockSpec((None, None, cap, D), lambda b, e, c: (b, e, 0, 0)),
    )
    return pl.pallas_call(
        functools.partial(_gather_kernel, win=win, unroll=min(6, nblk)),
        grid_spec=gs,
        out_shape=jax.ShapeDtypeStruct((B, E, cap, D), BF16),
        compiler_params=_cparams(("parallel", "arbitrary")),
        name="gather",
    )(cnt, slots5, h3d)


def _ffn_kernel(xs_ref, wg_ref, wu_ref, wd_ref, y_ref, wg_scr, wu_scr, wd_scr, *, f_chunk):
    @pl.when(pl.program_id(1) == 0)
    def _():
        wg_scr[...] = wg_ref[...].astype(BF16)
        wu_scr[...] = wu_ref[...].astype(BF16)
        wd_scr[...] = wd_ref[...].astype(BF16)

    xs = xs_ref[...]
    y = None
    for f0 in range(0, wg_scr.shape[1], f_chunk):
        fs = slice(f0, f0 + f_chunk)
        a = jnp.dot(xs, wg_scr[:, fs], preferred_element_type=F32)
        u = jnp.dot(xs, wu_scr[:, fs], preferred_element_type=F32)
        hmid = (a * jax.nn.sigmoid(a) * u).astype(BF16)
        part = jnp.dot(hmid, wd_scr[fs, :], preferred_element_type=F32)
        y = part if y is None else y + part
    y_ref[...] = y.astype(BF16)


def _ffn(xs, w_gate, w_up, w_down, *, layer):
    B, E, cap, D = xs.shape
    Fd = w_gate.shape[-1]
    return pl.pallas_call(
        functools.partial(_ffn_kernel, f_chunk=min(512, Fd)),
        grid=(E, B),
        in_specs=[
            pl.BlockSpec((None, None, cap, D), lambda e, b: (b, e, 0, 0)),
            pl.BlockSpec((None, None, D, Fd), lambda e, b: (layer, e, 0, 0)),
            pl.BlockSpec((None, None, D, Fd), lambda e, b: (layer, e, 0, 0)),
            pl.BlockSpec((None, None, Fd, D), lambda e, b: (layer, e, 0, 0)),
        ],
        out_specs=pl.BlockSpec((None, None, cap, D), lambda e, b: (b, e, 0, 0)),
        out_shape=jax.ShapeDtypeStruct((B, E, cap, D), BF16),
        scratch_shapes=[
            pltpu.VMEM((D, Fd), BF16),
            pltpu.VMEM((D, Fd), BF16),
            pltpu.VMEM((Fd, D), BF16),
        ],
        compiler_params=_cparams(("parallel", "arbitrary")),
        name="ffn",
    )(xs, w_gate, w_up, w_down)


def _scatter_kernel(cnt_ref, slot_ref, gate_ref, y_ref, x_ref, fg_ref, o_ref, *, win, final_norm):
    b = pl.program_id(0)
    sb = pl.program_id(1)
    ne, cap, D = y_ref.shape
    nloc = slot_ref.shape[0] // TOK_BLOCK
    nblk = nloc * pl.num_programs(1)
    lane = lax.broadcasted_iota(jnp.int32, (TOK_BLOCK, win), 1)

    def gated_onehot(scol, gcol, lo, sa):
        rel = jnp.where(scol >= lo, scol - sa, -1)
        return jnp.where(rel == lane, gcol, 0.0).astype(BF16)

    extra = jnp.int32(0)
    for t in range(nloc):
        j = sb * nloc + t
        rows = slice(t * TOK_BLOCK, (t + 1) * TOK_BLOCK)
        sl = slot_ref[rows, :]
        gt = gate_ref[rows, :]
        acc = x_ref[rows, :]
        for e0 in range(0, ne, 2):
            gs, ys = [], []
            for e in (e0, e0 + 1):
                idx = (b * ne + e) * (nblk + 1) + j
                s0 = cnt_ref[idx]
                s1 = cnt_ref[idx + 1]
                a0 = (s0 // 16) * 16
                sa = pl.multiple_of(jnp.minimum(a0, cap - win), 16)
                extra = jnp.maximum(extra, s1 - (a0 + win))
                gs.append(gated_onehot(sl[:, e:e + 1], gt[:, e:e + 1], a0, sa))
                ys.append(y_ref[e, pl.ds(sa, win), :])
            acc = acc + jnp.dot(jnp.concatenate(gs, axis=1), jnp.concatenate(ys, axis=0),
                                preferred_element_type=F32)
        o_ref[rows, :] = acc

    @pl.when(extra > 0)
    def _():
        for t in range(nloc):
            j = sb * nloc + t
            rows = slice(t * TOK_BLOCK, (t + 1) * TOK_BLOCK)
            sl = slot_ref[rows, :]
            gt = gate_ref[rows, :]
            for e in range(ne):
                idx = (b * ne + e) * (nblk + 1) + j
                s0 = cnt_ref[idx]
                s1 = cnt_ref[idx + 1]
                a0 = (s0 // 16) * 16
                nw = (s1 - a0 + win - 1) // win

                def wbody(w, carry):
                    lo = a0 + w * win
                    sa = pl.multiple_of(jnp.minimum(lo, cap - win), 16)
                    g = gated_onehot(sl[:, e:e + 1], gt[:, e:e + 1], lo, sa)
                    o_ref[rows, :] += jnp.dot(g, y_ref[e, pl.ds(sa, win), :], preferred_element_type=F32)
                    return carry

                lax.fori_loop(1, nw, wbody, 0)

    if final_norm:
        o_ref[...] = _rms_rows(o_ref[...], fg_ref[...])


def _scatter(cnt, slotsT, gatesT, y, x3d, fg, *, win, sb_rows, final_norm):
    B, seq, D = x3d.shape
    E, cap = y.shape[1], y.shape[2]
    gs = pltpu.PrefetchScalarGridSpec(
        num_scalar_prefetch=1,
        grid=(B, seq // sb_rows),
        in_specs=[
            pl.BlockSpec((None, sb_rows, E), lambda b, s, c: (b, s, 0)),
            pl.BlockSpec((None, sb_rows, E), lambda b, s, c: (b, s, 0)),
            pl.BlockSpec((None, E, cap, D), lambda b, s, c: (b, 0, 0, 0), pipeline_mode=pl.Buffered(1)),
            pl.BlockSpec((None, sb_rows, D), lambda b, s, c: (b, s, 0)),
            pl.BlockSpec((1, D), lambda b, s, c: (0, 0)),
        ],
        out_specs=pl.BlockSpec((None, sb_rows, D), lambda b, s, c: (b, s, 0)),
    )
    return pl.pallas_call(
        functools.partial(_scatter_kernel, win=win, final_norm=final_norm),
        grid_spec=gs,
        out_shape=jax.ShapeDtypeStruct((B, seq, D), F32),
        compiler_params=_cparams(("parallel", "arbitrary")),
        name="scatter",
    )(cnt, slotsT, gatesT, y, x3d, fg)


def _rope_tables_T(seq):
    rows = seq // GRID_W
    row_id = jnp.repeat(jnp.arange(rows, dtype=F32), GRID_W)
    col_id = jnp.tile(jnp.arange(GRID_W, dtype=F32), rows)
    n_pairs = HEAD_DIM // 4
    freqs = jnp.exp(-math.log(ROPE_THETA) * jnp.arange(n_pairs, dtype=F32) / n_pairs)
    ang = jnp.concatenate([freqs[:, None] * row_id[None, :], freqs[:, None] * col_id[None, :]], axis=0)
    return jnp.cos(ang), jnp.sin(ang)


def _head_perm(n_heads):
    base = jnp.concatenate([jnp.arange(0, HEAD_DIM, 2), jnp.arange(1, HEAD_DIM, 2)])
    return (jnp.arange(n_heads)[:, None] * HEAD_DIM + base[None, :]).reshape(-1)


def kernel(x, mem, mix_norm_g, w_in, gm_v_norm_g, gm_w_s, gm_b_s, q_norm_g, k_norm_g, branch_norm_g, w_out,
           xattn_norm_g, mem_norm_g, xattn_w_q, xattn_w_kv, xattn_w_o, ffn_norm_g, w_router, w_gate, w_up,
           w_down, final_norm_g):
    B, seq, D = x.shape
    L = w_in.shape[0]
    E = w_router.shape[-1]
    T = B * seq
    cap = EC_FACTOR * seq // E
    tm = min(512, seq)
    tm_in = min(1024, seq)
    tq = min(256, seq)
    tk = min(512, seq)
    win = min(128, cap)
    sb_rows = min(512, seq)
    nblk = seq // TOK_BLOCK

    cosT, sinT = _rope_tables_T(seq)
    o_q = 2 * GM_WIDTH
    o_k = o_q + ATT_WIDTH
    o_v = o_k + KV_WIDTH
    cols = jnp.concatenate([jnp.arange(o_q), o_q + _head_perm(N_Q_HEADS), o_k + _head_perm(N_KV_HEADS),
                            jnp.arange(o_v, o_v + KV_WIDTH)])
    hp = _head_perm(1)

    mkT_all, mv_all = _mem_kv(mem, mem_norm_g.reshape(1, D), xattn_w_kv.astype(BF16))

    x2d = x.reshape(T, D)
    out = None
    for l in range(L):
        w_in_l = w_in[l][:, cols].astype(BF16)
        bs = jnp.broadcast_to(gm_b_s[l][:, :, None], (GM_GROUPS, CHUNK, LANES))
        qg = jnp.broadcast_to(q_norm_g[l][hp][:, None], (HEAD_DIM, tm_in))
        kg = jnp.broadcast_to(k_norm_g[l][hp][:, None], (HEAD_DIM, tm_in))
        gm, qT, k, vT = _mixer_in(
            x2d, mix_norm_g[l].reshape(1, D), w_in_l, gm_v_norm_g[l].reshape(1, GM_WIDTH),
            gm_w_s[l].astype(BF16), bs, qg, kg, branch_norm_g[l, 0].reshape(1, GM_WIDTH), cosT, sinT,
            seq=seq, tm=tm_in)
        atT = _attention(qT, k.reshape(B, seq, KV_WIDTH), vT, tq=tq, tk=tk, qb=min(256, tq))

        bg1 = jnp.broadcast_to(branch_norm_g[l, 1][:, None], (ATT_WIDTH, tm))
        wr_pad = jnp.pad(w_router[l].astype(BF16), ((0, 0), (0, LANES - E)))
        x2, h3, affT = _post(
            x2d, gm, atT, bg1, w_out[l].astype(BF16), xattn_norm_g[l].reshape(1, D),
            xattn_w_q[l].astype(BF16), mkT_all, mv_all, xattn_w_o[l].astype(BF16),
            ffn_norm_g[l].reshape(1, D), wr_pad, n_experts=E, layer=l, seq=seq, tm=tm)

        slots, offs = _topk(affT.reshape(E, B, seq // LANES, LANES), cap=cap)
        cnt = offs[:, :, ::TOK_BLOCK // LANES, 0]
        cnt = jnp.concatenate([cnt, jnp.full((B, E, 1), cap, jnp.int32)], axis=-1).reshape(-1)
        slots5 = slots.reshape(B, E, nblk, 1, TOK_BLOCK)
        slotsT = slots.reshape(B, E, seq).transpose(0, 2, 1)
        gatesT = affT.reshape(E, B, seq).transpose(1, 2, 0)

        xs = _gather(cnt, slots5, h3.reshape(B, seq, D), cap=cap, win=win)
        y = _ffn(xs, w_gate, w_up, w_down, layer=l)
        out = _scatter(cnt, slotsT, gatesT, y, x2.reshape(B, seq, D), final_norm_g.reshape(1, D),
                       win=win, sb_rows=sb_rows, final_norm=(l == L - 1))
        x2d = out.reshape(T, D)
    return out
```

```python
import functools
import math

import jax
import jax.numpy as jnp
from jax import lax
from jax.experimental import pallas as pl
from jax.experimental.pallas import tpu as pltpu

F32 = jnp.float32
BF16 = jnp.bfloat16

EPS = 1e-6
CHUNK = 128
GM_GROUPS = 4
GM_WIDTH = 512
HEAD_DIM = 64
N_Q_HEADS = 8
N_KV_HEADS = 2
Q_PER_KV = N_Q_HEADS // N_KV_HEADS
ATT_WIDTH = N_Q_HEADS * HEAD_DIM
KV_WIDTH = N_KV_HEADS * HEAD_DIM
ROPE_THETA = 10000.0
GRID_W = 64
X_HEADS = 4
N_EXPERTS = 16
EC_FACTOR = 2

LANES = 128
MXU_DEPTH = 256
VMEM_LIMIT = 56 * 1024 * 1024

TOK_BLOCK = 256
LOG2E = 1.4426950408889634


def _cparams(sem, vmem=VMEM_LIMIT, **kw):
    return pltpu.CompilerParams(dimension_semantics=sem, vmem_limit_bytes=vmem, **kw)


def _rms_rows(x, g):
    ms = jnp.mean(x * x, axis=-1, keepdims=True)
    return x * lax.rsqrt(ms + EPS) * g


def _gelu_tanh(x):
    c = math.sqrt(2.0 / math.pi)
    return 0.5 * x * (1.0 + jnp.tanh(c * (x + 0.044715 * (x * x * x))))


def _norm_rope_T(xT, gcol, cosT, sinT, n_heads, scale):
    half = HEAD_DIM // 2
    outs = []
    for h in range(n_heads):
        blk = xT[h * HEAD_DIM:(h + 1) * HEAD_DIM, :]
        ms = jnp.mean(blk * blk, axis=0, keepdims=True)
        n = blk * lax.rsqrt(ms + EPS) * gcol
        e = n[:half, :]
        o = n[half:, :]
        re = (e * cosT - o * sinT) * scale
        ro = (e * sinT + o * cosT) * scale
        outs.append(jnp.concatenate([re, ro], axis=0))
    return outs


def _mixer_in_kernel(x_ref, g_ref, w_ref, vg_ref, ws_ref, bs_ref, qg_ref, kg_ref, bg0_ref,
                     cos_ref, sin_ref, gm_ref, qT_ref, k_ref, vT_ref, gm_scr, *, rows_per_group):
    tm = x_ref.shape[0]
    o_q = 2 * GM_WIDTH
    o_k = o_q + ATT_WIDTH
    o_v = o_k + KV_WIDTH
    groups = [slice(r, r + rows_per_group) for r in range(0, tm, rows_per_group)]

    projs = []
    for rs in groups:
        h = _rms_rows(x_ref[rs, :], g_ref[...]).astype(BF16)
        projs.append(jnp.dot(h, w_ref[...], preferred_element_type=F32))

    for rs, proj in zip(groups, projs):
        u = _gelu_tanh(proj[:, :GM_WIDTH])
        v = _gelu_tanh(proj[:, GM_WIDTH:2 * GM_WIDTH])
        vn = _rms_rows(v, vg_ref[...]).astype(BF16)
        for c in range(rows_per_group // CHUNK):
            cr = slice(c * CHUNK, (c + 1) * CHUNK)
            sr = slice(rs.start + c * CHUNK, rs.start + (c + 1) * CHUNK)
            for g in range(GM_GROUPS):
                cs = slice(g * LANES, (g + 1) * LANES)
                mixed = jnp.dot(ws_ref[g], vn[cr, cs], preferred_element_type=F32) + bs_ref[g]
                gm_scr[sr, cs] = u[cr, cs] * mixed
        gm_ref[rs, :] = _rms_rows(gm_scr[rs, :], bg0_ref[...]).astype(BF16)

        cosT = cos_ref[:, rs]
        sinT = sin_ref[:, rs]
        qT = proj[:, o_q:o_k].T
        q_heads = _norm_rope_T(qT, qg_ref[:, rs], cosT, sinT, N_Q_HEADS, (HEAD_DIM ** -0.5) * LOG2E)
        for hh in range(N_Q_HEADS):
            qT_ref[hh * HEAD_DIM:(hh + 1) * HEAD_DIM, rs] = q_heads[hh].astype(BF16)

        kT = proj[:, o_k:o_v].T
        k_heads = _norm_rope_T(kT, kg_ref[:, rs], cosT, sinT, N_KV_HEADS, 1.0)
        k_ref[rs, :] = jnp.concatenate(k_heads, axis=0).T.astype(BF16)

        vT_ref[:, rs] = proj[:, o_v:].T.astype(BF16)


def _mixer_in(x2d, g, w_in, vg, ws, bs, qg, kg, bg0, cosT, sinT, *, seq, tm):
    T, D = x2d.shape
    B = T // seq
    npb = seq // tm
    in_w = w_in.shape[1]
    full = lambda shape: pl.BlockSpec(shape, lambda i: (0,) * len(shape))
    return pl.pallas_call(
        functools.partial(_mixer_in_kernel, rows_per_group=min(256, tm)),
        grid=(T // tm,),
        in_specs=[
            pl.BlockSpec((tm, D), lambda i: (i, 0)),
            full((1, D)),
            full((D, in_w)),
            full((1, GM_WIDTH)),
            full((GM_GROUPS, CHUNK, CHUNK)),
            full((GM_GROUPS, CHUNK, LANES)),
            full((HEAD_DIM, tm)),
            full((HEAD_DIM, tm)),
            full((1, GM_WIDTH)),
            pl.BlockSpec((HEAD_DIM // 2, tm), lambda i: (0, i % npb)),
            pl.BlockSpec((HEAD_DIM // 2, tm), lambda i: (0, i % npb)),
        ],
        out_specs=[
            pl.BlockSpec((tm, GM_WIDTH), lambda i: (i, 0)),
            pl.BlockSpec((None, ATT_WIDTH, tm), lambda i: (i // npb, 0, i % npb)),
            pl.BlockSpec((tm, KV_WIDTH), lambda i: (i, 0)),
            pl.BlockSpec((None, KV_WIDTH, tm), lambda i: (i // npb, 0, i % npb)),
        ],
        out_shape=[
            jax.ShapeDtypeStruct((T, GM_WIDTH), BF16),
            jax.ShapeDtypeStruct((B, ATT_WIDTH, seq), BF16),
            jax.ShapeDtypeStruct((T, KV_WIDTH), BF16),
            jax.ShapeDtypeStruct((B, KV_WIDTH, seq), BF16),
        ],
        scratch_shapes=[pltpu.VMEM((tm, GM_WIDTH), F32)],
        compiler_params=_cparams(("parallel",)),
        name="mixer_in",
    )(x2d, g, w_in, vg, ws, bs, qg, kg, bg0, cosT, sinT)


def _attn_kernel(qT_ref, k_ref, vT_ref, o_ref, vext_scr, qx_scr, m_scr, acc_scr, s_scr, cm_scr, *, tq, tk, qb):
    kh = pl.program_id(1)
    seq = k_ref.shape[0]
    M = Q_PER_KV * tq
    nq = seq // tq
    nchunk = seq // tk

    vext_scr[0:HEAD_DIM, :] = vT_ref[...]
    row = lax.broadcasted_iota(jnp.int32, (HEAD_DIM, seq), 0)
    vext_scr[HEAD_DIM:, :] = jnp.where(row == 0, 1.0, 0.0).astype(BF16)

    def load_q(i, slot):
        cols = pl.ds(pl.multiple_of(i * tq, tq), tq)
        q4 = jnp.concatenate([qT_ref[g * HEAD_DIM:(g + 1) * HEAD_DIM, cols] for g in range(Q_PER_KV)], axis=1)
        z = jnp.zeros_like(q4)
        qx_scr[slot] = jnp.where(kh == 0, jnp.concatenate([q4, z], axis=0), jnp.concatenate([z, q4], axis=0))

    def step(c, qslot, qk_chunk, do_pv):
        cur = c % 2
        nxt = 1 - cur
        if qk_chunk is not None:
            kn = k_ref[qk_chunk * tk:(qk_chunk + 1) * tk, :]
        if do_pv:
            ve = vext_scr[:, c * tk:(c + 1) * tk]
        for j in range(M // qb):
            cs = slice(j * qb, (j + 1) * qb)
            if qk_chunk is not None:
                s = jnp.dot(kn, qx_scr[qslot, :, cs], preferred_element_type=F32)
                s_scr[nxt, :, cs] = s
                cm_scr[nxt, :, cs] = jnp.max(s, axis=0, keepdims=True)
            if do_pv:
                m_old = m_scr[:, cs]
                m_new = jnp.maximum(m_old, cm_scr[cur, :, cs])
                p = jnp.exp2(s_scr[cur, :, cs] - m_new).astype(BF16)
                alpha = jnp.exp2(m_old - m_new)
                pv = jnp.dot(ve, p, preferred_element_type=F32)
                acc_scr[:, cs] = acc_scr[:, cs] * alpha + pv
                m_scr[:, cs] = m_new

    load_q(0, 0)
    step(-1, 0, 0, False)

    def qblock(i, carry):
        slot = i % 2
        m_scr[...] = jnp.full(m_scr.shape, -jnp.inf, F32)
        acc_scr[...] = jnp.zeros(acc_scr.shape, F32)
        for c in range(nchunk - 1):
            step(c, slot, c + 1, True)
        load_q(jnp.minimum(i + 1, nq - 1), 1 - slot)
        step(nchunk - 1, 1 - slot, 0, True)

        acc = acc_scr[...]
        inv = 1.0 / acc[HEAD_DIM:HEAD_DIM + 1, :]
        o = acc[0:HEAD_DIM, :] * inv
        cols = pl.ds(pl.multiple_of(i * tq, tq), tq)
        for g in range(Q_PER_KV):
            o_ref[g * HEAD_DIM:(g + 1) * HEAD_DIM, cols] = o[:, g * tq:(g + 1) * tq].astype(BF16)
        return carry

    lax.fori_loop(0, nq, qblock, 0)


def _attention(qT, k, vT, *, tq, tk, qb):
    B, _, seq = qT.shape
    gw = Q_PER_KV * HEAD_DIM
    assert (seq // tk) % 2 == 0
    return pl.pallas_call(
        functools.partial(_attn_kernel, tq=tq, tk=tk, qb=qb),
        grid=(B, N_KV_HEADS),
        in_specs=[
            pl.BlockSpec((None, gw, seq), lambda b, kh: (b, kh, 0)),
            pl.BlockSpec((None, seq, KV_WIDTH), lambda b, kh: (b, 0, 0)),
            pl.BlockSpec((None, HEAD_DIM, seq), lambda b, kh: (b, kh, 0)),
        ],
        out_specs=pl.BlockSpec((None, gw, seq), lambda b, kh: (b, kh, 0)),
        out_shape=jax.ShapeDtypeStruct((B, ATT_WIDTH, seq), BF16),
        scratch_shapes=[
            pltpu.VMEM((2 * HEAD_DIM, seq), BF16),
            pltpu.VMEM((2, 2 * HEAD_DIM, Q_PER_KV * tq), BF16),
            pltpu.VMEM((1, Q_PER_KV * tq), F32),
            pltpu.VMEM((2 * HEAD_DIM, Q_PER_KV * tq), F32),
            pltpu.VMEM((2, tk, Q_PER_KV * tq), F32),
            pltpu.VMEM((2, 1, Q_PER_KV * tq), F32),
        ],
        compiler_params=_cparams(("parallel", "parallel")),
        name="attention",
    )(qT, k, vT)


def _mem_kv_kernel(mem_ref, g_ref, w_ref, kT_ref, v_ref):
    D = mem_ref.shape[1]
    mn = _rms_rows(mem_ref[...], g_ref[...]).astype(BF16)
    kv = jnp.dot(mn, w_ref[...], preferred_element_type=F32)
    kT_ref[...] = kv[:, :D].T.astype(BF16)
    v_ref[...] = kv[:, D:].astype(BF16)


def _mem_kv(mem, g, w_kv):
    B, M, D = mem.shape
    L = w_kv.shape[0]
    return pl.pallas_call(
        _mem_kv_kernel,
        grid=(L, B),
        in_specs=[
            pl.BlockSpec((None, M, D), lambda l, b: (b, 0, 0)),
            pl.BlockSpec((1, D), lambda l, b: (0, 0)),
            pl.BlockSpec((None, D, 2 * D), lambda l, b: (l, 0, 0)),
        ],
        out_specs=[
            pl.BlockSpec((None, None, D, M), lambda l, b: (l, b, 0, 0)),
            pl.BlockSpec((None, None, M, D), lambda l, b: (l, b, 0, 0)),
        ],
        out_shape=[
            jax.ShapeDtypeStruct((L, B, D, M), BF16),
            jax.ShapeDtypeStruct((L, B, M, D), BF16),
        ],
        compiler_params=_cparams(("parallel", "parallel")),
        name="mem_kv",
    )(mem, g, w_kv)


def _post_kernel(x_ref, gm_ref, atT_ref, bg1_ref, wout_ref, xg_ref, wq_ref, mkT_ref, mv_ref, wo_ref,
                 fg_ref, wr_ref, x2_ref, h3_ref, affT_ref, *, rows_per_group):
    tm, D = x_ref.shape
    xhd = D // X_HEADS
    groups = [slice(r, r + rows_per_group) for r in range(0, tm, rows_per_group)]

    x1 = []
    for rs in groups:
        at = atT_ref[:, rs].astype(F32)
        ms = jnp.mean(at * at, axis=0, keepdims=True)
        atn = (at * lax.rsqrt(ms + EPS) * bg1_ref[:, rs]).astype(BF16)
        y = jnp.dot(gm_ref[rs, :], wout_ref[0:GM_WIDTH, :], preferred_element_type=F32)
        y = y + lax.dot_general(atn, wout_ref[GM_WIDTH:, :], (((0,), (0,)), ((), ())),
                                preferred_element_type=F32)
        x1.append(x_ref[rs, :] + y)

    q2 = []
    for x1g in x1:
        h2 = _rms_rows(x1g, xg_ref[...]).astype(BF16)
        q2.append((jnp.dot(h2, wq_ref[...], preferred_element_type=F32) * (xhd ** -0.5)).astype(BF16))

    scores = {}
    for hh in range(X_HEADS):
        cs = slice(hh * xhd, (hh + 1) * xhd)
        for gi in range(len(groups)):
            scores[hh, gi] = jnp.dot(q2[gi][:, cs], mkT_ref[cs, :], preferred_element_type=F32)
    outs = [[] for _ in groups]
    for hh in range(X_HEADS):
        cs = slice(hh * xhd, (hh + 1) * xhd)
        for gi in range(len(groups)):
            s = scores[hh, gi]
            s = s - jnp.max(s, axis=-1, keepdims=True)
            p = jnp.exp(s)
            p = (p / jnp.sum(p, axis=-1, keepdims=True)).astype(BF16)
            outs[gi].append(jnp.dot(p, mv_ref[:, cs], preferred_element_type=F32).astype(BF16))

    x2 = []
    for gi, rs in enumerate(groups):
        o2 = jnp.concatenate(outs[gi], axis=1)
        x2g = x1[gi] + jnp.dot(o2, wo_ref[...], preferred_element_type=F32)
        x2_ref[rs, :] = x2g
        x2.append(x2g)

    ne = affT_ref.shape[0]
    for gi, rs in enumerate(groups):
        h3 = _rms_rows(x2[gi], fg_ref[...]).astype(BF16)
        h3_ref[rs, :] = h3
        lg = jnp.dot(h3, wr_ref[...], preferred_element_type=F32).T[0:ne, :]
        lg = lg - jnp.max(lg, axis=0, keepdims=True)
        ex = jnp.exp(lg)
        affT_ref[:, rs] = ex / jnp.sum(ex, axis=0, keepdims=True)


def _post(x2d, gm, atT, bg1, w_out, xg, w_q, mkT, mv, w_o, fg, wr_pad, *, n_experts, layer, seq, tm):
    T, D = x2d.shape
    npb = seq // tm
    M = mv.shape[2]
    E = n_experts
    full = lambda shape: pl.BlockSpec(shape, lambda i: (0,) * len(shape))
    return pl.pallas_call(
        functools.partial(_post_kernel, rows_per_group=min(256, tm)),
        grid=(T // tm,),
        in_specs=[
            pl.BlockSpec((tm, D), lambda i: (i, 0)),
            pl.BlockSpec((tm, GM_WIDTH), lambda i: (i, 0)),
            pl.BlockSpec((None, ATT_WIDTH, tm), lambda i: (i // npb, 0, i % npb)),
            full((ATT_WIDTH, tm)),
            full((GM_WIDTH + ATT_WIDTH, D)),
            full((1, D)),
            full((D, D)),
            pl.BlockSpec((None, None, D, M), lambda i: (layer, i // npb, 0, 0)),
            pl.BlockSpec((None, None, M, D), lambda i: (layer, i // npb, 0, 0)),
            full((D, D)),
            full((1, D)),
            full((D, LANES)),
        ],
        out_specs=[
            pl.BlockSpec((tm, D), lambda i: (i, 0)),
            pl.BlockSpec((tm, D), lambda i: (i, 0)),
            pl.BlockSpec((E, tm), lambda i: (0, i)),
        ],
        out_shape=[
            jax.ShapeDtypeStruct((T, D), F32),
            jax.ShapeDtypeStruct((T, D), BF16),
            jax.ShapeDtypeStruct((E, T), F32),
        ],
        compiler_params=_cparams(("parallel",)),
        name="post",
    )(x2d, gm, atT, bg1, w_out, xg, w_q, mkT, mv, w_o, fg, wr_pad)


def _topk_kernel(aff_ref, slot_ref, off_ref, *, cap):
    E, R, L = aff_ref.shape
    a = aff_ref[...]
    bits = lax.bitcast_convert_type(a, jnp.int32)

    def count_ge(t):
        c = jnp.where(bits >= t, 1.0, 0.0)
        return jnp.sum(jnp.sum(c, axis=2, keepdims=True), axis=1, keepdims=True)

    def bis(_, carry):
        lo, hi = carry
        mid = lo + ((hi - lo) >> 1)
        ok = count_ge(mid) >= cap
        return jnp.where(ok, mid, lo), jnp.where(ok, hi, mid)

    lo0 = jnp.zeros((E, 1, 1), jnp.int32)
    hi0 = jnp.full((E, 1, 1), 0x3F800001, jnp.int32)
    thr, _ = lax.fori_loop(0, 31, bis, (lo0, hi0))

    kk = lax.broadcasted_iota(jnp.int32, (L, L), 0)
    nn = lax.broadcasted_iota(jnp.int32, (L, L), 1)
    upper = jnp.where(kk <= nn, 1.0, 0.0).astype(BF16)
    ones = jnp.ones((L, L), BF16)
    rr = lax.broadcasted_iota(jnp.int32, (E * R, E * R), 0)
    cc = lax.broadcasted_iota(jnp.int32, (E * R, E * R), 1)
    lower = jnp.where((rr // R == cc // R) & (cc < rr), 1.0, 0.0).astype(BF16)

    def prefix(xf):
        x2 = xf.reshape(E * R, L)
        xb = x2.astype(BF16)
        incl = jnp.dot(xb, upper, preferred_element_type=F32)
        tot = jnp.dot(xb, ones, preferred_element_type=F32).astype(BF16)
        rowoff = jnp.dot(lower, tot, preferred_element_type=F32)
        return (incl - x2 + rowoff).reshape(E, R, L), rowoff.reshape(E, R, L)

    gt = jnp.where(bits > thr, 1.0, 0.0)
    eq = jnp.where(bits == thr, 1.0, 0.0)
    n_gt = jnp.sum(jnp.sum(gt, axis=2, keepdims=True), axis=1, keepdims=True)
    need = cap - n_gt
    eq_rank, _ = prefix(eq)
    sel = gt + eq * jnp.where(eq_rank < need, 1.0, 0.0)
    pos, rowoff = prefix(sel)
    slot_ref[...] = jnp.where(sel > 0.0, pos, -1.0).astype(jnp.int32)
    off_ref[...] = rowoff.astype(jnp.int32)


def _topk(aff4, *, cap):
    E, B, R, L = aff4.shape
    return pl.pallas_call(
        functools.partial(_topk_kernel, cap=cap),
        grid=(B,),
        in_specs=[pl.BlockSpec((E, None, R, L), lambda b: (0, b, 0, 0))],
        out_specs=[
            pl.BlockSpec((None, E, R, L), lambda b: (b, 0, 0, 0)),
            pl.BlockSpec((None, E, R, L), lambda b: (b, 0, 0, 0)),
        ],
        out_shape=[
            jax.ShapeDtypeStruct((B, E, R, L), jnp.int32),
            jax.ShapeDtypeStruct((B, E, R, L), jnp.int32),
        ],
        compiler_params=_cparams(("parallel",)),
        name="topk",
    )(aff4)


def _last_block_below(cnt_ref, base, nblk, bound, strict):
    pos = jnp.int32(0)
    step = nblk // 2
    while step >= 1:
        cand = pos + step
        v = cnt_ref[base + cand]
        ok = (v < bound) if strict else (v <= bound)
        pos = jnp.where(ok, cand, pos)
        step //= 2
    return pos


def _gather_kernel(cnt_ref, slot_ref, gate_ref, h_ref, xs_ref, gc_ref, *, win, unroll):
    b = pl.program_id(0)
    e = pl.program_id(1)
    ne = pl.num_programs(1)
    cap, D = xs_ref.shape
    nblk = slot_ref.shape[0]
    base = (b * ne + e) * (nblk + 1)
    r_iota = lax.broadcasted_iota(jnp.int32, (win, TOK_BLOCK), 0)

    def contrib(j, lo):
        jc = jnp.minimum(j, nblk - 1)
        srow = slot_ref[jc]
        rel = jnp.where(j < nblk, srow - lo, -1)
        hit = rel == r_iota
        oh = jnp.where(hit, 1.0, 0.0).astype(BF16)
        hb = h_ref[pl.ds(pl.multiple_of(jc * TOK_BLOCK, TOK_BLOCK), TOK_BLOCK), :]
        gpart = jnp.sum(jnp.where(hit, gate_ref[jc], 0.0), axis=1, keepdims=True)
        return jnp.dot(oh, hb, preferred_element_type=F32), gpart

    spans = []
    left = jnp.int32(0)
    for r in range(cap // win):
        lo = r * win
        j_lo = _last_block_below(cnt_ref, base, nblk, lo, False)
        j_hi = _last_block_below(cnt_ref, base, nblk, lo + win, True)
        acc, gacc = contrib(j_lo, lo)
        for u in range(1, unroll):
            part, gpart = contrib(j_lo + u, lo)
            acc = acc + part
            gacc = gacc + gpart
        xs_ref[lo:lo + win, :] = acc.astype(BF16)
        gc_ref[lo:lo + win, :] = gacc
        spans.append((j_lo + unroll, j_hi + 1))
        left = jnp.maximum(left, j_hi + 1 - (j_lo + unroll))

    @pl.when(left > 0)
    def _():
        for r, (j0, j1) in enumerate(spans):
            lo = r * win

            def more(j, carry, lo=lo):
                part, gpart = contrib(j, lo)
                xs_ref[lo:lo + win, :] = (xs_ref[lo:lo + win, :].astype(F32) + part).astype(BF16)
                gc_ref[lo:lo + win, :] += gpart
                return carry

            lax.fori_loop(j0, j1, more, 0)


def _gather(cnt, slots5, gates5, h3d, *, cap, win):
    B, seq, D = h3d.shape
    E = slots5.shape[1]
    nblk = seq // TOK_BLOCK
    assert nblk & (nblk - 1) == 0 and cap % win == 0
    gs = pltpu.PrefetchScalarGridSpec(
        num_scalar_prefetch=1,
        grid=(B, E),
        in_specs=[
            pl.BlockSpec((None, None, nblk, 1, TOK_BLOCK), lambda b, e, c: (b, e, 0, 0, 0)),
            pl.BlockSpec((None, None, nblk, 1, TOK_BLOCK), lambda b, e, c: (e, b, 0, 0, 0)),
            pl.BlockSpec((None, seq, D), lambda b, e, c: (b, 0, 0), pipeline_mode=pl.Buffered(1)),
        ],
        out_specs=[
            pl.BlockSpec((None, None, cap, D), lambda b, e, c: (b, e, 0, 0)),
            pl.BlockSpec((None, None, cap, 1), lambda b, e, c: (b, e, 0, 0)),
        ],
    )
    return pl.pallas_call(
        functools.partial(_gather_kernel, win=win, unroll=min(6, nblk)),
        grid_spec=gs,
        out_shape=[
            jax.ShapeDtypeStruct((B, E, cap, D), BF16),
            jax.ShapeDtypeStruct((B, E, cap, 1), F32),
        ],
        compiler_params=_cparams(("parallel", "arbitrary")),
        name="gather",
    )(cnt, slots5, gates5, h3d)


def _ffn_kernel(xs_ref, gc_ref, wg_ref, wu_ref, wd_ref, y_ref, wg_scr, wu_scr, wd_scr, *, f_chunk):
    @pl.when(pl.program_id(1) == 0)
    def _():
        wg_scr[...] = wg_ref[...].astype(BF16)
        wu_scr[...] = wu_ref[...].astype(BF16)
        wd_scr[...] = wd_ref[...].astype(BF16)

    xs = xs_ref[...]
    y = None
    for f0 in range(0, wg_scr.shape[1], f_chunk):
        fs = slice(f0, f0 + f_chunk)
        a = jnp.dot(xs, wg_scr[:, fs], preferred_element_type=F32)
        u = jnp.dot(xs, wu_scr[:, fs], preferred_element_type=F32)
        hmid = (a * jax.nn.sigmoid(a) * u).astype(BF16)
        part = jnp.dot(hmid, wd_scr[fs, :], preferred_element_type=F32)
        y = part if y is None else y + part
    y_ref[...] = (y * gc_ref[...]).astype(BF16)


def _ffn(xs, gc, w_gate, w_up, w_down, *, layer):
    B, E, cap, D = xs.shape
    Fd = w_gate.shape[-1]
    return pl.pallas_call(
        functools.partial(_ffn_kernel, f_chunk=min(512, Fd)),
        grid=(E, B),
        in_specs=[
            pl.BlockSpec((None, None, cap, D), lambda e, b: (b, e, 0, 0)),
            pl.BlockSpec((None, None, cap, 1), lambda e, b: (b, e, 0, 0)),
            pl.BlockSpec((None, None, D, Fd), lambda e, b: (layer, e, 0, 0)),
            pl.BlockSpec((None, None, D, Fd), lambda e, b: (layer, e, 0, 0)),
            pl.BlockSpec((None, None, Fd, D), lambda e, b: (layer, e, 0, 0)),
        ],
        out_specs=pl.BlockSpec((None, None, cap, D), lambda e, b: (b, e, 0, 0)),
        out_shape=jax.ShapeDtypeStruct((B, E, cap, D), BF16),
        scratch_shapes=[
            pltpu.VMEM((D, Fd), BF16),
            pltpu.VMEM((D, Fd), BF16),
            pltpu.VMEM((Fd, D), BF16),
        ],
        compiler_params=_cparams(("parallel", "arbitrary")),
        name="ffn",
    )(xs, gc, w_gate, w_up, w_down)


def _scatter_kernel(cnt_ref, slot_ref, y_ref, x_ref, fg_ref, o_ref, *, win, final_norm):
    b = pl.program_id(0)
    sb = pl.program_id(1)
    ne, cap, D = y_ref.shape
    nloc = slot_ref.shape[0] // TOK_BLOCK
    nblk = nloc * pl.num_programs(1)
    lane = lax.broadcasted_iota(jnp.int32, (TOK_BLOCK, win), 1)
    lane2 = lax.broadcasted_iota(jnp.int32, (TOK_BLOCK, 2 * win), 1)
    second = lane2 >= win

    def onehot(scol, lo, sa):
        rel = jnp.where(scol >= lo, scol - sa, -1)
        return jnp.where(rel == lane, 1.0, 0.0).astype(BF16)

    def window(e, j):
        idx = (b * ne + e) * (nblk + 1) + j
        s0 = cnt_ref[idx]
        s1 = cnt_ref[idx + 1]
        a0 = (s0 // 16) * 16
        sa = pl.multiple_of(jnp.minimum(a0, cap - win), 16)
        return a0, sa, s1 - (a0 + win)

    extra = jnp.int32(0)
    per_dot = MXU_DEPTH // win
    for t in range(nloc):
        j = sb * nloc + t
        rows = slice(t * TOK_BLOCK, (t + 1) * TOK_BLOCK)
        sl = slot_ref[rows, :]
        acc = x_ref[rows, :]
        for e0 in range(0, ne, per_dot):
            gs, ys = [], []
            for ea in range(e0, e0 + per_dot, 2):
                eb = ea + 1
                a0a, saa, xa = window(ea, j)
                a0b, sab, xb = window(eb, j)
                extra = jnp.maximum(extra, jnp.maximum(xa, xb))
                slot = jnp.where(second, sl[:, eb:eb + 1], sl[:, ea:ea + 1])
                lo = jnp.where(second, a0b, a0a)
                off = jnp.where(second, sab - win, saa)
                rel = jnp.where(slot >= lo, slot - off, -1)
                gs.append(jnp.where(rel == lane2, 1.0, 0.0).astype(BF16))
                ys.append(y_ref[ea, pl.ds(saa, win), :])
                ys.append(y_ref[eb, pl.ds(sab, win), :])
            acc = acc + jnp.dot(jnp.concatenate(gs, axis=1), jnp.concatenate(ys, axis=0),
                                preferred_element_type=F32)
        o_ref[rows, :] = acc

    @pl.when(extra > 0)
    def _():
        for t in range(nloc):
            j = sb * nloc + t
            rows = slice(t * TOK_BLOCK, (t + 1) * TOK_BLOCK)
            sl = slot_ref[rows, :]
            for e in range(ne):
                idx = (b * ne + e) * (nblk + 1) + j
                s0 = cnt_ref[idx]
                s1 = cnt_ref[idx + 1]
                a0 = (s0 // 16) * 16
                nw = (s1 - a0 + win - 1) // win

                def wbody(w, carry):
                    lo = a0 + w * win
                    sa = pl.multiple_of(jnp.minimum(lo, cap - win), 16)
                    g = onehot(sl[:, e:e + 1], lo, sa)
                    o_ref[rows, :] += jnp.dot(g, y_ref[e, pl.ds(sa, win), :], preferred_element_type=F32)
                    return carry

                lax.fori_loop(1, nw, wbody, 0)

    if final_norm:
        o_ref[...] = _rms_rows(o_ref[...], fg_ref[...])


def _scatter(cnt, slotsT, y, x3d, fg, *, win, sb_rows, final_norm):
    B, seq, D = x3d.shape
    E, cap = y.shape[1], y.shape[2]
    assert MXU_DEPTH % win == 0 and E % max(2, MXU_DEPTH // win) == 0
    gs = pltpu.PrefetchScalarGridSpec(
        num_scalar_prefetch=1,
        grid=(B, seq // sb_rows),
        in_specs=[
            pl.BlockSpec((None, sb_rows, E), lambda b, s, c: (b, s, 0)),
            pl.BlockSpec((None, E, cap, D), lambda b, s, c: (b, 0, 0, 0), pipeline_mode=pl.Buffered(1)),
            pl.BlockSpec((None, sb_rows, D), lambda b, s, c: (b, s, 0)),
            pl.BlockSpec((1, D), lambda b, s, c: (0, 0)),
        ],
        out_specs=pl.BlockSpec((None, sb_rows, D), lambda b, s, c: (b, s, 0)),
    )
    return pl.pallas_call(
        functools.partial(_scatter_kernel, win=win, final_norm=final_norm),
        grid_spec=gs,
        out_shape=jax.ShapeDtypeStruct((B, seq, D), F32),
        compiler_params=_cparams(("parallel", "arbitrary")),
        name="scatter",
    )(cnt, slotsT, y, x3d, fg)


def _rope_tables_T(seq):
    rows = seq // GRID_W
    row_id = jnp.repeat(jnp.arange(rows, dtype=F32), GRID_W)
    col_id = jnp.tile(jnp.arange(GRID_W, dtype=F32), rows)
    n_pairs = HEAD_DIM // 4
    freqs = jnp.exp(-math.log(ROPE_THETA) * jnp.arange(n_pairs, dtype=F32) / n_pairs)
    ang = jnp.concatenate([freqs[:, None] * row_id[None, :], freqs[:, None] * col_id[None, :]], axis=0)
    return jnp.cos(ang), jnp.sin(ang)


def _head_perm(n_heads):
    base = jnp.concatenate([jnp.arange(0, HEAD_DIM, 2), jnp.arange(1, HEAD_DIM, 2)])
    return (jnp.arange(n_heads)[:, None] * HEAD_DIM + base[None, :]).reshape(-1)


def kernel(x, mem, mix_norm_g, w_in, gm_v_norm_g, gm_w_s, gm_b_s, q_norm_g, k_norm_g, branch_norm_g, w_out,
           xattn_norm_g, mem_norm_g, xattn_w_q, xattn_w_kv, xattn_w_o, ffn_norm_g, w_router, w_gate, w_up,
           w_down, final_norm_g):
    B, seq, D = x.shape
    L = w_in.shape[0]
    E = w_router.shape[-1]
    T = B * seq
    cap = EC_FACTOR * seq // E
    tm = min(512, seq)
    tm_in = min(1024, seq)
    tq = min(256, seq)
    tk = min(512, seq)
    win = min(128, cap)
    sb_rows = min(512, seq)
    nblk = seq // TOK_BLOCK

    cosT, sinT = _rope_tables_T(seq)
    o_q = 2 * GM_WIDTH
    o_k = o_q + ATT_WIDTH
    o_v = o_k + KV_WIDTH
    cols = jnp.concatenate([jnp.arange(o_q), o_q + _head_perm(N_Q_HEADS), o_k + _head_perm(N_KV_HEADS),
                            jnp.arange(o_v, o_v + KV_WIDTH)])
    hp = _head_perm(1)

    mkT_all, mv_all = _mem_kv(mem, mem_norm_g.reshape(1, D), xattn_w_kv.astype(BF16))

    x2d = x.reshape(T, D)
    out = None
    for l in range(L):
        w_in_l = w_in[l][:, cols].astype(BF16)
        bs = jnp.broadcast_to(gm_b_s[l][:, :, None], (GM_GROUPS, CHUNK, LANES))
        qg = jnp.broadcast_to(q_norm_g[l][hp][:, None], (HEAD_DIM, tm_in))
        kg = jnp.broadcast_to(k_norm_g[l][hp][:, None], (HEAD_DIM, tm_in))
        gm, qT, k, vT = _mixer_in(
            x2d, mix_norm_g[l].reshape(1, D), w_in_l, gm_v_norm_g[l].reshape(1, GM_WIDTH),
            gm_w_s[l].astype(BF16), bs, qg, kg, branch_norm_g[l, 0].reshape(1, GM_WIDTH), cosT, sinT,
            seq=seq, tm=tm_in)
        atT = _attention(qT, k.reshape(B, seq, KV_WIDTH), vT, tq=tq, tk=tk, qb=min(256, tq))

        bg1 = jnp.broadcast_to(branch_norm_g[l, 1][:, None], (ATT_WIDTH, tm))
        wr_pad = jnp.pad(w_router[l].astype(BF16), ((0, 0), (0, LANES - E)))
        x2, h3, affT = _post(
            x2d, gm, atT, bg1, w_out[l].astype(BF16), xattn_norm_g[l].reshape(1, D),
            xattn_w_q[l].astype(BF16), mkT_all, mv_all, xattn_w_o[l].astype(BF16),
            ffn_norm_g[l].reshape(1, D), wr_pad, n_experts=E, layer=l, seq=seq, tm=tm)

        slots, offs = _topk(affT.reshape(E, B, seq // LANES, LANES), cap=cap)
        cnt = offs[:, :, ::TOK_BLOCK // LANES, 0]
        cnt = jnp.concatenate([cnt, jnp.full((B, E, 1), cap, jnp.int32)], axis=-1).reshape(-1)
        slots5 = slots.reshape(B, E, nblk, 1, TOK_BLOCK)
        slotsT = slots.reshape(B, E, seq).transpose(0, 2, 1)
        gates5 = affT.reshape(E, B, nblk, 1, TOK_BLOCK)

        xs, gc = _gather(cnt, slots5, gates5, h3.reshape(B, seq, D), cap=cap, win=win)
        y = _ffn(xs, gc, w_gate, w_up, w_down, layer=l)
        out = _scatter(cnt, slotsT, y, x2.reshape(B, seq, D), final_norm_g.reshape(1, D),
                       win=min(64, cap), sb_rows=sb_rows, final_norm=(l == L - 1))
        x2d = out.reshape(T, D)
    return out
```

```python
import functools
import math

import jax
import jax.numpy as jnp
from jax import lax
from jax.experimental import pallas as pl
from jax.experimental.pallas import tpu as pltpu

F32 = jnp.float32
BF16 = jnp.bfloat16
F8 = jnp.float8_e4m3fn
F8_MAX = 448.0

EPS = 1e-6
CHUNK = 128
GM_GROUPS = 4
GM_WIDTH = 512
HEAD_DIM = 64
N_Q_HEADS = 8
N_KV_HEADS = 2
Q_PER_KV = N_Q_HEADS // N_KV_HEADS
ATT_WIDTH = N_Q_HEADS * HEAD_DIM
KV_WIDTH = N_KV_HEADS * HEAD_DIM
ROPE_THETA = 10000.0
GRID_W = 64
X_HEADS = 4
N_EXPERTS = 16
EC_FACTOR = 2

LANES = 128
MXU_DEPTH = 256
VMEM_LIMIT = 56 * 1024 * 1024

TOK_BLOCK = 256
LOG2E = 1.4426950408889634


def _cparams(sem, vmem=VMEM_LIMIT, **kw):
    return pltpu.CompilerParams(dimension_semantics=sem, vmem_limit_bytes=vmem, **kw)


def _rms_rows(x, g):
    ms = jnp.mean(x * x, axis=-1, keepdims=True)
    return x * lax.rsqrt(ms + EPS) * g


def _gelu_tanh(x):
    c = math.sqrt(2.0 / math.pi)
    return 0.5 * x * (1.0 + jnp.tanh(c * (x + 0.044715 * (x * x * x))))


def _norm_rope_T(xT, gcol, cosT, sinT, n_heads, scale):
    half = HEAD_DIM // 2
    outs = []
    for h in range(n_heads):
        blk = xT[h * HEAD_DIM:(h + 1) * HEAD_DIM, :]
        ms = jnp.mean(blk * blk, axis=0, keepdims=True)
        n = blk * lax.rsqrt(ms + EPS) * gcol
        e = n[:half, :]
        o = n[half:, :]
        re = (e * cosT - o * sinT) * scale
        ro = (e * sinT + o * cosT) * scale
        outs.append(jnp.concatenate([re, ro], axis=0))
    return outs


def _split_f8(x):
    x = jnp.clip(x, -F8_MAX, F8_MAX)
    hi = x.astype(F8).astype(F32)
    lo = (x - hi).astype(F8).astype(F32)
    return hi, lo


def _mixer_in_kernel(x_ref, g_ref, w_ref, vg_ref, ws_ref, bs_ref, qg_ref, kg_ref, bg0_ref,
                     cos_ref, sin_ref, gm_ref, qhT_ref, qlT_ref, k_ref, vT_ref, gm_scr, *, rows_per_group):
    tm = x_ref.shape[0]
    o_q = 2 * GM_WIDTH
    o_k = o_q + ATT_WIDTH
    o_v = o_k + KV_WIDTH
    groups = [slice(r, r + rows_per_group) for r in range(0, tm, rows_per_group)]

    projs = []
    for rs in groups:
        h = _rms_rows(x_ref[rs, :], g_ref[...]).astype(BF16)
        projs.append(jnp.dot(h, w_ref[...], preferred_element_type=F32))

    for rs, proj in zip(groups, projs):
        u = _gelu_tanh(proj[:, :GM_WIDTH])
        v = _gelu_tanh(proj[:, GM_WIDTH:2 * GM_WIDTH])
        vn = _rms_rows(v, vg_ref[...]).astype(BF16)
        for c in range(rows_per_group // CHUNK):
            cr = slice(c * CHUNK, (c + 1) * CHUNK)
            sr = slice(rs.start + c * CHUNK, rs.start + (c + 1) * CHUNK)
            for g in range(GM_GROUPS):
                cs = slice(g * LANES, (g + 1) * LANES)
                mixed = jnp.dot(ws_ref[g], vn[cr, cs], preferred_element_type=F32) + bs_ref[g]
                gm_scr[sr, cs] = u[cr, cs] * mixed
        gm_ref[rs, :] = _rms_rows(gm_scr[rs, :], bg0_ref[...]).astype(BF16)

        cosT = cos_ref[:, rs]
        sinT = sin_ref[:, rs]
        qT = proj[:, o_q:o_k].T
        q_heads = _norm_rope_T(qT, qg_ref[:, rs], cosT, sinT, N_Q_HEADS, (HEAD_DIM ** -0.5) * LOG2E)
        for hh in range(N_Q_HEADS):
            hi, lo = _split_f8(q_heads[hh])
            qhT_ref[hh * HEAD_DIM:(hh + 1) * HEAD_DIM, rs] = hi.astype(F8)
            qlT_ref[hh * HEAD_DIM:(hh + 1) * HEAD_DIM, rs] = lo.astype(F8)

        kT = proj[:, o_k:o_v].T
        k_heads = _norm_rope_T(kT, kg_ref[:, rs], cosT, sinT, N_KV_HEADS, 1.0)
        pieces = []
        for kh in k_heads:
            hi, lo = _split_f8(kh)
            pieces += [hi, lo, hi, jnp.zeros_like(hi)]
        k_ref[rs, :] = jnp.concatenate(pieces, axis=0).T.astype(F8)

        vT_ref[:, rs] = proj[:, o_v:].T.astype(BF16)


def _mixer_in(x2d, g, w_in, vg, ws, bs, qg, kg, bg0, cosT, sinT, *, seq, tm):
    T, D = x2d.shape
    B = T // seq
    npb = seq // tm
    in_w = w_in.shape[1]
    full = lambda shape: pl.BlockSpec(shape, lambda i: (0,) * len(shape))
    return pl.pallas_call(
        functools.partial(_mixer_in_kernel, rows_per_group=min(256, tm)),
        grid=(T // tm,),
        in_specs=[
            pl.BlockSpec((tm, D), lambda i: (i, 0)),
            full((1, D)),
            full((D, in_w)),
            full((1, GM_WIDTH)),
            full((GM_GROUPS, CHUNK, CHUNK)),
            full((GM_GROUPS, CHUNK, LANES)),
            full((HEAD_DIM, tm)),
            full((HEAD_DIM, tm)),
            full((1, GM_WIDTH)),
            pl.BlockSpec((HEAD_DIM // 2, tm), lambda i: (0, i % npb)),
            pl.BlockSpec((HEAD_DIM // 2, tm), lambda i: (0, i % npb)),
        ],
        out_specs=[
            pl.BlockSpec((tm, GM_WIDTH), lambda i: (i, 0)),
            pl.BlockSpec((None, ATT_WIDTH, tm), lambda i: (i // npb, 0, i % npb)),
            pl.BlockSpec((None, ATT_WIDTH, tm), lambda i: (i // npb, 0, i % npb)),
            pl.BlockSpec((tm, N_KV_HEADS * MXU_DEPTH), lambda i: (i, 0)),
            pl.BlockSpec((None, KV_WIDTH, tm), lambda i: (i // npb, 0, i % npb)),
        ],
        out_shape=[
            jax.ShapeDtypeStruct((T, GM_WIDTH), BF16),
            jax.ShapeDtypeStruct((B, ATT_WIDTH, seq), F8),
            jax.ShapeDtypeStruct((B, ATT_WIDTH, seq), F8),
            jax.ShapeDtypeStruct((T, N_KV_HEADS * MXU_DEPTH), F8),
            jax.ShapeDtypeStruct((B, KV_WIDTH, seq), BF16),
        ],
        scratch_shapes=[pltpu.VMEM((tm, GM_WIDTH), F32)],
        compiler_params=_cparams(("parallel",)),
        name="mixer_in",
    )(x2d, g, w_in, vg, ws, bs, qg, kg, bg0, cosT, sinT)


def _attn_kernel(qhT_ref, qlT_ref, k_ref, vT_ref, o_ref, vext_scr, qx_scr, m_scr, acc_scr, s_scr, cm_scr,
                 *, tq, tk, qb):
    seq = k_ref.shape[0]
    M = Q_PER_KV * tq
    nq = seq // tq
    nchunk = seq // tk

    vext_scr[0:HEAD_DIM, :] = vT_ref[...]
    row = lax.broadcasted_iota(jnp.int32, (HEAD_DIM, seq), 0)
    vext_scr[HEAD_DIM:, :] = jnp.where(row == 0, 1.0, 0.0).astype(BF16)

    def load_q(i, slot):
        cols = pl.ds(pl.multiple_of(i * tq, tq), tq)
        qh = jnp.concatenate([qhT_ref[g * HEAD_DIM:(g + 1) * HEAD_DIM, cols] for g in range(Q_PER_KV)], axis=1)
        ql = jnp.concatenate([qlT_ref[g * HEAD_DIM:(g + 1) * HEAD_DIM, cols] for g in range(Q_PER_KV)], axis=1)
        qx_scr[slot] = jnp.concatenate([qh, qh, ql, jnp.zeros_like(qh)], axis=0)

    def step(c, qslot, qk_chunk, do_pv):
        cur = c % 2
        nxt = 1 - cur
        if qk_chunk is not None:
            kn = k_ref[qk_chunk * tk:(qk_chunk + 1) * tk, :]
        if do_pv:
            ve = vext_scr[:, c * tk:(c + 1) * tk]
        for j in range(M // qb):
            cs = slice(j * qb, (j + 1) * qb)
            if qk_chunk is not None:
                s = jnp.dot(kn, qx_scr[qslot, :, cs], preferred_element_type=F32)
                s_scr[nxt, :, cs] = s
                cm_scr[nxt, :, cs] = jnp.max(s, axis=0, keepdims=True)
            if do_pv:
                m_old = m_scr[:, cs]
                m_new = jnp.maximum(m_old, cm_scr[cur, :, cs])
                p = jnp.exp2(s_scr[cur, :, cs] - m_new).astype(BF16)
                alpha = jnp.exp2(m_old - m_new)
                pv = jnp.dot(ve, p, preferred_element_type=F32)
                acc_scr[:, cs] = acc_scr[:, cs] * alpha + pv
                m_scr[:, cs] = m_new

    load_q(0, 0)
    step(-1, 0, 0, False)

    def qblock(i, carry):
        slot = i % 2
        m_scr[...] = jnp.full(m_scr.shape, -jnp.inf, F32)
        acc_scr[...] = jnp.zeros(acc_scr.shape, F32)
        for c in range(nchunk - 1):
            step(c, slot, c + 1, True)
        load_q(jnp.minimum(i + 1, nq - 1), 1 - slot)
        step(nchunk - 1, 1 - slot, 0, True)

        acc = acc_scr[...]
        inv = 1.0 / acc[HEAD_DIM:HEAD_DIM + 1, :]
        o = acc[0:HEAD_DIM, :] * inv
        cols = pl.ds(pl.multiple_of(i * tq, tq), tq)
        for g in range(Q_PER_KV):
            o_ref[g * HEAD_DIM:(g + 1) * HEAD_DIM, cols] = o[:, g * tq:(g + 1) * tq].astype(BF16)
        return carry

    lax.fori_loop(0, nq, qblock, 0)


def _attention(qhT, qlT, k, vT, *, tq, tk, qb):
    B, _, seq = qhT.shape
    gw = Q_PER_KV * HEAD_DIM
    assert (seq // tk) % 2 == 0
    return pl.pallas_call(
        functools.partial(_attn_kernel, tq=tq, tk=tk, qb=qb),
        grid=(B, N_KV_HEADS),
        in_specs=[
            pl.BlockSpec((None, gw, seq), lambda b, kh: (b, kh, 0)),
            pl.BlockSpec((None, gw, seq), lambda b, kh: (b, kh, 0)),
            pl.BlockSpec((None, seq, MXU_DEPTH), lambda b, kh: (b, 0, kh)),
            pl.BlockSpec((None, HEAD_DIM, seq), lambda b, kh: (b, kh, 0)),
        ],
        out_specs=pl.BlockSpec((None, gw, seq), lambda b, kh: (b, kh, 0)),
        out_shape=jax.ShapeDtypeStruct((B, ATT_WIDTH, seq), BF16),
        scratch_shapes=[
            pltpu.VMEM((2 * HEAD_DIM, seq), BF16),
            pltpu.VMEM((2, MXU_DEPTH, Q_PER_KV * tq), F8),
            pltpu.VMEM((1, Q_PER_KV * tq), F32),
            pltpu.VMEM((2 * HEAD_DIM, Q_PER_KV * tq), F32),
            pltpu.VMEM((2, tk, Q_PER_KV * tq), F32),
            pltpu.VMEM((2, 1, Q_PER_KV * tq), F32),
        ],
        compiler_params=_cparams(("parallel", "parallel")),
        name="attention",
    )(qhT, qlT, k, vT)


def _mem_kv_kernel(mem_ref, g_ref, w_ref, kT_ref, v_ref):
    D = mem_ref.shape[1]
    mn = _rms_rows(mem_ref[...], g_ref[...]).astype(BF16)
    kv = jnp.dot(mn, w_ref[...], preferred_element_type=F32)
    kT_ref[...] = kv[:, :D].T.astype(BF16)
    v_ref[...] = kv[:, D:].astype(BF16)


def _mem_kv(mem, g, w_kv):
    B, M, D = mem.shape
    L = w_kv.shape[0]
    return pl.pallas_call(
        _mem_kv_kernel,
        grid=(L, B),
        in_specs=[
            pl.BlockSpec((None, M, D), lambda l, b: (b, 0, 0)),
            pl.BlockSpec((1, D), lambda l, b: (0, 0)),
            pl.BlockSpec((None, D, 2 * D), lambda l, b: (l, 0, 0)),
        ],
        out_specs=[
            pl.BlockSpec((None, None, D, M), lambda l, b: (l, b, 0, 0)),
            pl.BlockSpec((None, None, M, D), lambda l, b: (l, b, 0, 0)),
        ],
        out_shape=[
            jax.ShapeDtypeStruct((L, B, D, M), BF16),
            jax.ShapeDtypeStruct((L, B, M, D), BF16),
        ],
        compiler_params=_cparams(("parallel", "parallel")),
        name="mem_kv",
    )(mem, g, w_kv)


def _post_kernel(x_ref, gm_ref, atT_ref, bg1_ref, wout_ref, xg_ref, wq_ref, mkT_ref, mv_ref, wo_ref,
                 fg_ref, wr_ref, x2_ref, h3_ref, affT_ref, *, rows_per_group):
    tm, D = x_ref.shape
    xhd = D // X_HEADS
    groups = [slice(r, r + rows_per_group) for r in range(0, tm, rows_per_group)]

    x1 = []
    for rs in groups:
        at = atT_ref[:, rs].astype(F32)
        ms = jnp.mean(at * at, axis=0, keepdims=True)
        atn = (at * lax.rsqrt(ms + EPS) * bg1_ref[:, rs]).astype(BF16)
        y = jnp.dot(gm_ref[rs, :], wout_ref[0:GM_WIDTH, :], preferred_element_type=F32)
        y = y + lax.dot_general(atn, wout_ref[GM_WIDTH:, :], (((0,), (0,)), ((), ())),
                                preferred_element_type=F32)
        x1.append(x_ref[rs, :] + y)

    q2 = []
    for x1g in x1:
        h2 = _rms_rows(x1g, xg_ref[...]).astype(BF16)
        q2.append((jnp.dot(h2, wq_ref[...], preferred_element_type=F32) * (xhd ** -0.5)).astype(BF16))

    scores = {}
    for hh in range(X_HEADS):
        cs = slice(hh * xhd, (hh + 1) * xhd)
        for gi in range(len(groups)):
            scores[hh, gi] = jnp.dot(q2[gi][:, cs], mkT_ref[cs, :], preferred_element_type=F32)
    outs = [[] for _ in groups]
    for hh in range(X_HEADS):
        cs = slice(hh * xhd, (hh + 1) * xhd)
        for gi in range(len(groups)):
            s = scores[hh, gi]
            s = s - jnp.max(s, axis=-1, keepdims=True)
            p = jnp.exp(s)
            p = (p / jnp.sum(p, axis=-1, keepdims=True)).astype(BF16)
            outs[gi].append(jnp.dot(p, mv_ref[:, cs], preferred_element_type=F32).astype(BF16))

    x2 = []
    for gi, rs in enumerate(groups):
        o2 = jnp.concatenate(outs[gi], axis=1)
        x2g = x1[gi] + jnp.dot(o2, wo_ref[...], preferred_element_type=F32)
        x2_ref[rs, :] = x2g
        x2.append(x2g)

    ne = affT_ref.shape[0]
    for gi, rs in enumerate(groups):
        h3 = _rms_rows(x2[gi], fg_ref[...]).astype(BF16)
        h3_ref[rs, :] = h3
        lg = jnp.dot(h3, wr_ref[...], preferred_element_type=F32).T[0:ne, :]
        lg = lg - jnp.max(lg, axis=0, keepdims=True)
        ex = jnp.exp(lg)
        affT_ref[:, rs] = ex / jnp.sum(ex, axis=0, keepdims=True)


def _post(x2d, gm, atT, bg1, w_out, xg, w_q, mkT, mv, w_o, fg, wr_pad, *, n_experts, layer, seq, tm):
    T, D = x2d.shape
    npb = seq // tm
    M = mv.shape[2]
    E = n_experts
    full = lambda shape: pl.BlockSpec(shape, lambda i: (0,) * len(shape))
    return pl.pallas_call(
        functools.partial(_post_kernel, rows_per_group=min(256, tm)),
        grid=(T // tm,),
        in_specs=[
            pl.BlockSpec((tm, D), lambda i: (i, 0)),
            pl.BlockSpec((tm, GM_WIDTH), lambda i: (i, 0)),
            pl.BlockSpec((None, ATT_WIDTH, tm), lambda i: (i // npb, 0, i % npb)),
            full((ATT_WIDTH, tm)),
            full((GM_WIDTH + ATT_WIDTH, D)),
            full((1, D)),
            full((D, D)),
            pl.BlockSpec((None, None, D, M), lambda i: (layer, i // npb, 0, 0)),
            pl.BlockSpec((None, None, M, D), lambda i: (layer, i // npb, 0, 0)),
            full((D, D)),
            full((1, D)),
            full((D, LANES)),
        ],
        out_specs=[
            pl.BlockSpec((tm, D), lambda i: (i, 0)),
            pl.BlockSpec((tm, D), lambda i: (i, 0)),
            pl.BlockSpec((E, tm), lambda i: (0, i)),
        ],
        out_shape=[
            jax.ShapeDtypeStruct((T, D), F32),
            jax.ShapeDtypeStruct((T, D), BF16),
            jax.ShapeDtypeStruct((E, T), F32),
        ],
        compiler_params=_cparams(("parallel",)),
        name="post",
    )(x2d, gm, atT, bg1, w_out, xg, w_q, mkT, mv, w_o, fg, wr_pad)


def _topk_kernel(aff_ref, slot_ref, off_ref, *, cap):
    E, R, L = aff_ref.shape
    a = aff_ref[...]
    bits = lax.bitcast_convert_type(a, jnp.int32)

    def count_ge(t):
        c = jnp.where(bits >= t, 1.0, 0.0)
        return jnp.sum(jnp.sum(c, axis=2, keepdims=True), axis=1, keepdims=True)

    def bis(_, carry):
        lo, hi = carry
        mid = lo + ((hi - lo) >> 1)
        ok = count_ge(mid) >= cap
        return jnp.where(ok, mid, lo), jnp.where(ok, hi, mid)

    lo0 = jnp.zeros((E, 1, 1), jnp.int32)
    hi0 = jnp.full((E, 1, 1), 0x3F800001, jnp.int32)
    thr, _ = lax.fori_loop(0, 31, bis, (lo0, hi0))

    kk = lax.broadcasted_iota(jnp.int32, (L, L), 0)
    nn = lax.broadcasted_iota(jnp.int32, (L, L), 1)
    upper = jnp.where(kk <= nn, 1.0, 0.0).astype(BF16)
    ones = jnp.ones((L, L), BF16)
    rr = lax.broadcasted_iota(jnp.int32, (E * R, E * R), 0)
    cc = lax.broadcasted_iota(jnp.int32, (E * R, E * R), 1)
    lower = jnp.where((rr // R == cc // R) & (cc < rr), 1.0, 0.0).astype(BF16)

    def prefix(xf):
        x2 = xf.reshape(E * R, L)
        xb = x2.astype(BF16)
        incl = jnp.dot(xb, upper, preferred_element_type=F32)
        tot = jnp.dot(xb, ones, preferred_element_type=F32).astype(BF16)
        rowoff = jnp.dot(lower, tot, preferred_element_type=F32)
        return (incl - x2 + rowoff).reshape(E, R, L), rowoff.reshape(E, R, L)

    gt = jnp.where(bits > thr, 1.0, 0.0)
    eq = jnp.where(bits == thr, 1.0, 0.0)
    n_gt = jnp.sum(jnp.sum(gt, axis=2, keepdims=True), axis=1, keepdims=True)
    need = cap - n_gt
    eq_rank, _ = prefix(eq)
    sel = gt + eq * jnp.where(eq_rank < need, 1.0, 0.0)
    pos, rowoff = prefix(sel)
    slot_ref[...] = jnp.where(sel > 0.0, pos, -1.0).astype(jnp.int32)
    off_ref[...] = rowoff.astype(jnp.int32)


def _topk(aff4, *, cap):
    E, B, R, L = aff4.shape
    return pl.pallas_call(
        functools.partial(_topk_kernel, cap=cap),
        grid=(B,),
        in_specs=[pl.BlockSpec((E, None, R, L), lambda b: (0, b, 0, 0))],
        out_specs=[
            pl.BlockSpec((None, E, R, L), lambda b: (b, 0, 0, 0)),
            pl.BlockSpec((None, E, R, L), lambda b: (b, 0, 0, 0)),
        ],
        out_shape=[
            jax.ShapeDtypeStruct((B, E, R, L), jnp.int32),
            jax.ShapeDtypeStruct((B, E, R, L), jnp.int32),
        ],
        compiler_params=_cparams(("parallel",)),
        name="topk",
    )(aff4)


def _last_block_below(cnt_ref, base, nblk, bound, strict):
    pos = jnp.int32(0)
    step = nblk // 2
    while step >= 1:
        cand = pos + step
        v = cnt_ref[base + cand]
        ok = (v < bound) if strict else (v <= bound)
        pos = jnp.where(ok, cand, pos)
        step //= 2
    return pos


def _gather_kernel(cnt_ref, slot_ref, gate_ref, h_ref, xs_ref, gc_ref, *, win, unroll):
    b = pl.program_id(0)
    e = pl.program_id(1)
    ne = pl.num_programs(1)
    cap, D = xs_ref.shape
    nblk = slot_ref.shape[0]
    base = (b * ne + e) * (nblk + 1)
    r_iota = lax.broadcasted_iota(jnp.int32, (win, TOK_BLOCK), 0)

    def contrib(j, lo):
        jc = jnp.minimum(j, nblk - 1)
        srow = slot_ref[jc]
        rel = jnp.where(j < nblk, srow - lo, -1)
        hit = rel == r_iota
        oh = jnp.where(hit, 1.0, 0.0).astype(BF16)
        hb = h_ref[pl.ds(pl.multiple_of(jc * TOK_BLOCK, TOK_BLOCK), TOK_BLOCK), :]
        gpart = jnp.sum(jnp.where(hit, gate_ref[jc], 0.0), axis=1, keepdims=True)
        return jnp.dot(oh, hb, preferred_element_type=F32), gpart

    spans = []
    left = jnp.int32(0)
    for r in range(cap // win):
        lo = r * win
        j_lo = _last_block_below(cnt_ref, base, nblk, lo, False)
        j_hi = _last_block_below(cnt_ref, base, nblk, lo + win, True)
        acc, gacc = contrib(j_lo, lo)
        for u in range(1, unroll):
            part, gpart = contrib(j_lo + u, lo)
            acc = acc + part
            gacc = gacc + gpart
        xs_ref[lo:lo + win, :] = acc.astype(BF16)
        gc_ref[lo:lo + win, :] = gacc
        spans.append((j_lo + unroll, j_hi + 1))
        left = jnp.maximum(left, j_hi + 1 - (j_lo + unroll))

    @pl.when(left > 0)
    def _():
        for r, (j0, j1) in enumerate(spans):
            lo = r * win

            def more(j, carry, lo=lo):
                part, gpart = contrib(j, lo)
                xs_ref[lo:lo + win, :] = (xs_ref[lo:lo + win, :].astype(F32) + part).astype(BF16)
                gc_ref[lo:lo + win, :] += gpart
                return carry

            lax.fori_loop(j0, j1, more, 0)


def _gather(cnt, slots5, gates5, h3d, *, cap, win):
    B, seq, D = h3d.shape
    E = slots5.shape[1]
    nblk = seq // TOK_BLOCK
    assert nblk & (nblk - 1) == 0 and cap % win == 0
    gs = pltpu.PrefetchScalarGridSpec(
        num_scalar_prefetch=1,
        grid=(B, E),
        in_specs=[
            pl.BlockSpec((None, None, nblk, 1, TOK_BLOCK), lambda b, e, c: (b, e, 0, 0, 0)),
            pl.BlockSpec((None, None, nblk, 1, TOK_BLOCK), lambda b, e, c: (e, b, 0, 0, 0)),
            pl.BlockSpec((None, seq, D), lambda b, e, c: (b, 0, 0), pipeline_mode=pl.Buffered(1)),
        ],
        out_specs=[
            pl.BlockSpec((None, None, cap, D), lambda b, e, c: (b, e, 0, 0)),
            pl.BlockSpec((None, None, cap, 1), lambda b, e, c: (b, e, 0, 0)),
        ],
    )
    return pl.pallas_call(
        functools.partial(_gather_kernel, win=win, unroll=min(6, nblk)),
        grid_spec=gs,
        out_shape=[
            jax.ShapeDtypeStruct((B, E, cap, D), BF16),
            jax.ShapeDtypeStruct((B, E, cap, 1), F32),
        ],
        compiler_params=_cparams(("parallel", "arbitrary")),
        name="gather",
    )(cnt, slots5, gates5, h3d)


def _ffn_kernel(xs_ref, gc_ref, wg_ref, wu_ref, wd_ref, y_ref, wg_scr, wu_scr, wd_scr, *, f_chunk):
    @pl.when(pl.program_id(1) == 0)
    def _():
        wg_scr[...] = wg_ref[...].astype(BF16)
        wu_scr[...] = wu_ref[...].astype(BF16)
        wd_scr[...] = wd_ref[...].astype(BF16)

    xs = xs_ref[...]
    y = None
    for f0 in range(0, wg_scr.shape[1], f_chunk):
        fs = slice(f0, f0 + f_chunk)
        a = jnp.dot(xs, wg_scr[:, fs], preferred_element_type=F32)
        u = jnp.dot(xs, wu_scr[:, fs], preferred_element_type=F32)
        hmid = (a * jax.nn.sigmoid(a) * u).astype(BF16)
        part = jnp.dot(hmid, wd_scr[fs, :], preferred_element_type=F32)
        y = part if y is None else y + part
    y_ref[...] = (y * gc_ref[...]).astype(BF16)


def _ffn(xs, gc, w_gate, w_up, w_down, *, layer):
    B, E, cap, D = xs.shape
    Fd = w_gate.shape[-1]
    return pl.pallas_call(
        functools.partial(_ffn_kernel, f_chunk=min(512, Fd)),
        grid=(E, B),
        in_specs=[
            pl.BlockSpec((None, None, cap, D), lambda e, b: (b, e, 0, 0)),
            pl.BlockSpec((None, None, cap, 1), lambda e, b: (b, e, 0, 0)),
            pl.BlockSpec((None, None, D, Fd), lambda e, b: (layer, e, 0, 0)),
            pl.BlockSpec((None, None, D, Fd), lambda e, b: (layer, e, 0, 0)),
            pl.BlockSpec((None, None, Fd, D), lambda e, b: (layer, e, 0, 0)),
        ],
        out_specs=pl.BlockSpec((None, None, cap, D), lambda e, b: (b, e, 0, 0)),
        out_shape=jax.ShapeDtypeStruct((B, E, cap, D), BF16),
        scratch_shapes=[
            pltpu.VMEM((D, Fd), BF16),
            pltpu.VMEM((D, Fd), BF16),
            pltpu.VMEM((Fd, D), BF16),
        ],
        compiler_params=_cparams(("parallel", "arbitrary")),
        name="ffn",
    )(xs, gc, w_gate, w_up, w_down)


def _scatter_kernel(cnt_ref, slot_ref, y_ref, x_ref, fg_ref, o_ref, *, win, final_norm):
    b = pl.program_id(0)
    sb = pl.program_id(1)
    ne, cap, D = y_ref.shape
    nloc = slot_ref.shape[0] // TOK_BLOCK
    nblk = nloc * pl.num_programs(1)
    lane = lax.broadcasted_iota(jnp.int32, (TOK_BLOCK, win), 1)
    lane2 = lax.broadcasted_iota(jnp.int32, (TOK_BLOCK, 2 * win), 1)
    second = lane2 >= win

    def onehot(scol, lo, sa):
        rel = jnp.where(scol >= lo, scol - sa, -1)
        return jnp.where(rel == lane, 1.0, 0.0).astype(BF16)

    def window(e, j):
        idx = (b * ne + e) * (nblk + 1) + j
        s0 = cnt_ref[idx]
        s1 = cnt_ref[idx + 1]
        a0 = (s0 // 16) * 16
        sa = pl.multiple_of(jnp.minimum(a0, cap - win), 16)
        return a0, sa, s1 - (a0 + win)

    extra = jnp.int32(0)
    per_dot = MXU_DEPTH // win
    for t in range(nloc):
        j = sb * nloc + t
        rows = slice(t * TOK_BLOCK, (t + 1) * TOK_BLOCK)
        sl = slot_ref[rows, :]
        acc = x_ref[rows, :]
        for e0 in range(0, ne, per_dot):
            gs, ys = [], []
            for ea in range(e0, e0 + per_dot, 2):
                eb = ea + 1
                a0a, saa, xa = window(ea, j)
                a0b, sab, xb = window(eb, j)
                extra = jnp.maximum(extra, jnp.maximum(xa, xb))
                slot = jnp.where(second, sl[:, eb:eb + 1], sl[:, ea:ea + 1])
                lo = jnp.where(second, a0b, a0a)
                off = jnp.where(second, sab - win, saa)
                rel = jnp.where(slot >= lo, slot - off, -1)
                gs.append(jnp.where(rel == lane2, 1.0, 0.0).astype(BF16))
                ys.append(y_ref[ea, pl.ds(saa, win), :])
                ys.append(y_ref[eb, pl.ds(sab, win), :])
            acc = acc + jnp.dot(jnp.concatenate(gs, axis=1), jnp.concatenate(ys, axis=0),
                                preferred_element_type=F32)
        o_ref[rows, :] = acc

    @pl.when(extra > 0)
    def _():
        for t in range(nloc):
            j = sb * nloc + t
            rows = slice(t * TOK_BLOCK, (t + 1) * TOK_BLOCK)
            sl = slot_ref[rows, :]
            for e in range(ne):
                idx = (b * ne + e) * (nblk + 1) + j
                s0 = cnt_ref[idx]
                s1 = cnt_ref[idx + 1]
                a0 = (s0 // 16) * 16
                nw = (s1 - a0 + win - 1) // win

                def wbody(w, carry):
                    lo = a0 + w * win
                    sa = pl.multiple_of(jnp.minimum(lo, cap - win), 16)
                    g = onehot(sl[:, e:e + 1], lo, sa)
                    o_ref[rows, :] += jnp.dot(g, y_ref[e, pl.ds(sa, win), :], preferred_element_type=F32)
                    return carry

                lax.fori_loop(1, nw, wbody, 0)

    if final_norm:
        o_ref[...] = _rms_rows(o_ref[...], fg_ref[...])


def _scatter(cnt, slotsT, y, x3d, fg, *, win, sb_rows, final_norm):
    B, seq, D = x3d.shape
    E, cap = y.shape[1], y.shape[2]
    assert MXU_DEPTH % win == 0 and E % max(2, MXU_DEPTH // win) == 0
    gs = pltpu.PrefetchScalarGridSpec(
        num_scalar_prefetch=1,
        grid=(B, seq // sb_rows),
        in_specs=[
            pl.BlockSpec((None, sb_rows, E), lambda b, s, c: (b, s, 0)),
            pl.BlockSpec((None, E, cap, D), lambda b, s, c: (b, 0, 0, 0), pipeline_mode=pl.Buffered(1)),
            pl.BlockSpec((None, sb_rows, D), lambda b, s, c: (b, s, 0)),
            pl.BlockSpec((1, D), lambda b, s, c: (0, 0)),
        ],
        out_specs=pl.BlockSpec((None, sb_rows, D), lambda b, s, c: (b, s, 0)),
    )
    return pl.pallas_call(
        functools.partial(_scatter_kernel, win=win, final_norm=final_norm),
        grid_spec=gs,
        out_shape=jax.ShapeDtypeStruct((B, seq, D), F32),
        compiler_params=_cparams(("parallel", "arbitrary")),
        name="scatter",
    )(cnt, slotsT, y, x3d, fg)


def _rope_tables_T(seq):
    rows = seq // GRID_W
    row_id = jnp.repeat(jnp.arange(rows, dtype=F32), GRID_W)
    col_id = jnp.tile(jnp.arange(GRID_W, dtype=F32), rows)
    n_pairs = HEAD_DIM // 4
    freqs = jnp.exp(-math.log(ROPE_THETA) * jnp.arange(n_pairs, dtype=F32) / n_pairs)
    ang = jnp.concatenate([freqs[:, None] * row_id[None, :], freqs[:, None] * col_id[None, :]], axis=0)
    return jnp.cos(ang), jnp.sin(ang)


def _head_perm(n_heads):
    base = jnp.concatenate([jnp.arange(0, HEAD_DIM, 2), jnp.arange(1, HEAD_DIM, 2)])
    return (jnp.arange(n_heads)[:, None] * HEAD_DIM + base[None, :]).reshape(-1)


def kernel(x, mem, mix_norm_g, w_in, gm_v_norm_g, gm_w_s, gm_b_s, q_norm_g, k_norm_g, branch_norm_g, w_out,
           xattn_norm_g, mem_norm_g, xattn_w_q, xattn_w_kv, xattn_w_o, ffn_norm_g, w_router, w_gate, w_up,
           w_down, final_norm_g):
    B, seq, D = x.shape
    L = w_in.shape[0]
    E = w_router.shape[-1]
    T = B * seq
    cap = EC_FACTOR * seq // E
    tm = min(512, seq)
    tm_in = min(1024, seq)
    tq = min(256, seq)
    tk = min(512, seq)
    win = min(128, cap)
    sb_rows = min(512, seq)
    nblk = seq // TOK_BLOCK

    cosT, sinT = _rope_tables_T(seq)
    o_q = 2 * GM_WIDTH
    o_k = o_q + ATT_WIDTH
    o_v = o_k + KV_WIDTH
    cols = jnp.concatenate([jnp.arange(o_q), o_q + _head_perm(N_Q_HEADS), o_k + _head_perm(N_KV_HEADS),
                            jnp.arange(o_v, o_v + KV_WIDTH)])
    hp = _head_perm(1)

    mkT_all, mv_all = _mem_kv(mem, mem_norm_g.reshape(1, D), xattn_w_kv.astype(BF16))

    x2d = x.reshape(T, D)
    out = None
    for l in range(L):
        w_in_l = w_in[l][:, cols].astype(BF16)
        bs = jnp.broadcast_to(gm_b_s[l][:, :, None], (GM_GROUPS, CHUNK, LANES))
        qg = jnp.broadcast_to(q_norm_g[l][hp][:, None], (HEAD_DIM, tm_in))
        kg = jnp.broadcast_to(k_norm_g[l][hp][:, None], (HEAD_DIM, tm_in))
        gm, qhT, qlT, k8, vT = _mixer_in(
            x2d, mix_norm_g[l].reshape(1, D), w_in_l, gm_v_norm_g[l].reshape(1, GM_WIDTH),
            gm_w_s[l].astype(BF16), bs, qg, kg, branch_norm_g[l, 0].reshape(1, GM_WIDTH), cosT, sinT,
            seq=seq, tm=tm_in)
        atT = _attention(qhT, qlT, k8.reshape(B, seq, N_KV_HEADS * MXU_DEPTH), vT,
                         tq=tq, tk=tk, qb=min(256, tq))

        bg1 = jnp.broadcast_to(branch_norm_g[l, 1][:, None], (ATT_WIDTH, tm))
        wr_pad = jnp.pad(w_router[l].astype(BF16), ((0, 0), (0, LANES - E)))
        x2, h3, affT = _post(
            x2d, gm, atT, bg1, w_out[l].astype(BF16), xattn_norm_g[l].reshape(1, D),
            xattn_w_q[l].astype(BF16), mkT_all, mv_all, xattn_w_o[l].astype(BF16),
            ffn_norm_g[l].reshape(1, D), wr_pad, n_experts=E, layer=l, seq=seq, tm=tm)

        slots, offs = _topk(affT.reshape(E, B, seq // LANES, LANES), cap=cap)
        cnt = offs[:, :, ::TOK_BLOCK // LANES, 0]
        cnt = jnp.concatenate([cnt, jnp.full((B, E, 1), cap, jnp.int32)], axis=-1).reshape(-1)
        slots5 = slots.reshape(B, E, nblk, 1, TOK_BLOCK)
        slotsT = slots.reshape(B, E, seq).transpose(0, 2, 1)
        gates5 = affT.reshape(E, B, nblk, 1, TOK_BLOCK)

        xs, gc = _gather(cnt, slots5, gates5, h3.reshape(B, seq, D), cap=cap, win=win)
        y = _ffn(xs, gc, w_gate, w_up, w_down, layer=l)
        out = _scatter(cnt, slotsT, y, x2.reshape(B, seq, D), final_norm_g.reshape(1, D),
                       win=min(64, cap), sb_rows=sb_rows, final_norm=(l == L - 1))
        x2d = out.reshape(T, D)
    return out
```

```python
import functools
import math

import jax
import jax.numpy as jnp
from jax import lax
from jax.experimental import pallas as pl
from jax.experimental.pallas import tpu as pltpu

F32 = jnp.float32
BF16 = jnp.bfloat16
F8 = jnp.float8_e4m3fn
F8_MAX = 448.0

EPS = 1e-6
CHUNK = 128
GM_GROUPS = 4
GM_WIDTH = 512
HEAD_DIM = 64
N_Q_HEADS = 8
N_KV_HEADS = 2
Q_PER_KV = N_Q_HEADS // N_KV_HEADS
ATT_WIDTH = N_Q_HEADS * HEAD_DIM
KV_WIDTH = N_KV_HEADS * HEAD_DIM
ROPE_THETA = 10000.0
GRID_W = 64
X_HEADS = 4
N_EXPERTS = 16
EC_FACTOR = 2

LANES = 128
MXU_DEPTH = 256
BF16_SUBLANES = 16
PV_ROWS = HEAD_DIM + BF16_SUBLANES
VMEM_LIMIT = 56 * 1024 * 1024

TOK_BLOCK = 256
LOG2E = 1.4426950408889634


def _cparams(sem, vmem=VMEM_LIMIT, **kw):
    return pltpu.CompilerParams(dimension_semantics=sem, vmem_limit_bytes=vmem, **kw)


def _rms_rows(x, g):
    ms = jnp.mean(x * x, axis=-1, keepdims=True)
    return x * lax.rsqrt(ms + EPS) * g


def _gelu_tanh(x):
    c = math.sqrt(2.0 / math.pi)
    return 0.5 * x * (1.0 + jnp.tanh(c * (x + 0.044715 * (x * x * x))))


def _norm_rope_T(xT, gcol, cosT, sinT, n_heads, scale):
    half = HEAD_DIM // 2
    outs = []
    for h in range(n_heads):
        blk = xT[h * HEAD_DIM:(h + 1) * HEAD_DIM, :]
        ms = jnp.mean(blk * blk, axis=0, keepdims=True)
        n = blk * lax.rsqrt(ms + EPS) * gcol
        e = n[:half, :]
        o = n[half:, :]
        re = (e * cosT - o * sinT) * scale
        ro = (e * sinT + o * cosT) * scale
        outs.append(jnp.concatenate([re, ro], axis=0))
    return outs


def _split_f8(x):
    x = jnp.clip(x, -F8_MAX, F8_MAX)
    hi = x.astype(F8).astype(F32)
    lo = (x - hi).astype(F8).astype(F32)
    return hi, lo


def _mixer_in_kernel(x_ref, g_ref, w_ref, vg_ref, ws_ref, bs_ref, qg_ref, kg_ref, bg0_ref,
                     cos_ref, sin_ref, gm_ref, qhT_ref, qlT_ref, k_ref, vT_ref, gm_scr, *, rows_per_group):
    tm = x_ref.shape[0]
    o_q = 2 * GM_WIDTH
    o_k = o_q + ATT_WIDTH
    o_v = o_k + KV_WIDTH
    groups = [slice(r, r + rows_per_group) for r in range(0, tm, rows_per_group)]

    projs = []
    for rs in groups:
        h = _rms_rows(x_ref[rs, :], g_ref[...]).astype(BF16)
        projs.append(jnp.dot(h, w_ref[...], preferred_element_type=F32))

    for rs, proj in zip(groups, projs):
        u = _gelu_tanh(proj[:, :GM_WIDTH])
        v = _gelu_tanh(proj[:, GM_WIDTH:2 * GM_WIDTH])
        vn = _rms_rows(v, vg_ref[...]).astype(BF16)
        for c in range(rows_per_group // CHUNK):
            cr = slice(c * CHUNK, (c + 1) * CHUNK)
            sr = slice(rs.start + c * CHUNK, rs.start + (c + 1) * CHUNK)
            for g in range(GM_GROUPS):
                cs = slice(g * LANES, (g + 1) * LANES)
                mixed = jnp.dot(ws_ref[g], vn[cr, cs], preferred_element_type=F32) + bs_ref[g]
                gm_scr[sr, cs] = u[cr, cs] * mixed
        gm_ref[rs, :] = _rms_rows(gm_scr[rs, :], bg0_ref[...]).astype(BF16)

        cosT = cos_ref[:, rs]
        sinT = sin_ref[:, rs]
        qT = proj[:, o_q:o_k].T
        q_heads = _norm_rope_T(qT, qg_ref[:, rs], cosT, sinT, N_Q_HEADS, (HEAD_DIM ** -0.5) * LOG2E)
        for hh in range(N_Q_HEADS):
            hi, lo = _split_f8(q_heads[hh])
            qhT_ref[hh * HEAD_DIM:(hh + 1) * HEAD_DIM, rs] = hi.astype(F8)
            qlT_ref[hh * HEAD_DIM:(hh + 1) * HEAD_DIM, rs] = lo.astype(F8)

        kT = proj[:, o_k:o_v].T
        k_heads = _norm_rope_T(kT, kg_ref[:, rs], cosT, sinT, N_KV_HEADS, 1.0)
        pieces = []
        for kh in k_heads:
            hi, lo = _split_f8(kh)
            pieces += [hi, lo, hi, jnp.zeros_like(hi)]
        k_ref[rs, :] = jnp.concatenate(pieces, axis=0).T.astype(F8)

        vT_ref[:, rs] = proj[:, o_v:].T.astype(BF16)


def _mixer_in(x2d, g, w_in, vg, ws, bs, qg, kg, bg0, cosT, sinT, *, seq, tm):
    T, D = x2d.shape
    B = T // seq
    npb = seq // tm
    in_w = w_in.shape[1]
    full = lambda shape: pl.BlockSpec(shape, lambda i: (0,) * len(shape))
    return pl.pallas_call(
        functools.partial(_mixer_in_kernel, rows_per_group=min(256, tm)),
        grid=(T // tm,),
        in_specs=[
            pl.BlockSpec((tm, D), lambda i: (i, 0)),
            full((1, D)),
            full((D, in_w)),
            full((1, GM_WIDTH)),
            full((GM_GROUPS, CHUNK, CHUNK)),
            full((GM_GROUPS, CHUNK, LANES)),
            full((HEAD_DIM, tm)),
            full((HEAD_DIM, tm)),
            full((1, GM_WIDTH)),
            pl.BlockSpec((HEAD_DIM // 2, tm), lambda i: (0, i % npb)),
            pl.BlockSpec((HEAD_DIM // 2, tm), lambda i: (0, i % npb)),
        ],
        out_specs=[
            pl.BlockSpec((tm, GM_WIDTH), lambda i: (i, 0)),
            pl.BlockSpec((None, ATT_WIDTH, tm), lambda i: (i // npb, 0, i % npb)),
            pl.BlockSpec((None, ATT_WIDTH, tm), lambda i: (i // npb, 0, i % npb)),
            pl.BlockSpec((tm, N_KV_HEADS * MXU_DEPTH), lambda i: (i, 0)),
            pl.BlockSpec((None, KV_WIDTH, tm), lambda i: (i // npb, 0, i % npb)),
        ],
        out_shape=[
            jax.ShapeDtypeStruct((T, GM_WIDTH), BF16),
            jax.ShapeDtypeStruct((B, ATT_WIDTH, seq), F8),
            jax.ShapeDtypeStruct((B, ATT_WIDTH, seq), F8),
            jax.ShapeDtypeStruct((T, N_KV_HEADS * MXU_DEPTH), F8),
            jax.ShapeDtypeStruct((B, KV_WIDTH, seq), BF16),
        ],
        scratch_shapes=[pltpu.VMEM((tm, GM_WIDTH), F32)],
        compiler_params=_cparams(("parallel",)),
        name="mixer_in",
    )(x2d, g, w_in, vg, ws, bs, qg, kg, bg0, cosT, sinT)


def _attn_kernel(qhT_ref, qlT_ref, k_ref, vT_ref, o_ref, vext_scr, qx_scr, m_scr, acc_scr, s_scr, cm_scr,
                 *, tq, tk, qb):
    seq = k_ref.shape[0]
    M = Q_PER_KV * tq
    nq = seq // tq
    nchunk = seq // tk

    vext_scr[0:HEAD_DIM, :] = vT_ref[...]
    row = lax.broadcasted_iota(jnp.int32, (PV_ROWS - HEAD_DIM, seq), 0)
    vext_scr[HEAD_DIM:, :] = jnp.where(row == 0, 1.0, 0.0).astype(BF16)

    def load_q(i, slot):
        cols = pl.ds(pl.multiple_of(i * tq, tq), tq)
        qh = jnp.concatenate([qhT_ref[g * HEAD_DIM:(g + 1) * HEAD_DIM, cols] for g in range(Q_PER_KV)], axis=1)
        ql = jnp.concatenate([qlT_ref[g * HEAD_DIM:(g + 1) * HEAD_DIM, cols] for g in range(Q_PER_KV)], axis=1)
        qx_scr[slot] = jnp.concatenate([qh, qh, ql, jnp.zeros_like(qh)], axis=0)

    def step(c, qslot, qk_chunk, do_pv):
        cur = c % 2
        nxt = 1 - cur
        if qk_chunk is not None:
            kn = k_ref[qk_chunk * tk:(qk_chunk + 1) * tk, :]
        if do_pv:
            ve = vext_scr[:, c * tk:(c + 1) * tk]
        for j in range(M // qb):
            cs = slice(j * qb, (j + 1) * qb)
            if qk_chunk is not None:
                s = jnp.dot(kn, qx_scr[qslot, :, cs], preferred_element_type=F32)
                s_scr[nxt, :, cs] = s
                cm_scr[nxt, :, cs] = jnp.max(s, axis=0, keepdims=True)
            if do_pv:
                m_old = m_scr[:, cs]
                m_new = jnp.maximum(m_old, cm_scr[cur, :, cs])
                p = jnp.exp2(s_scr[cur, :, cs] - m_new).astype(BF16)
                alpha = jnp.exp2(m_old - m_new)
                pv = jnp.dot(ve, p, preferred_element_type=F32)
                acc_scr[:, cs] = acc_scr[:, cs] * alpha + pv
                m_scr[:, cs] = m_new

    load_q(0, 0)
    step(-1, 0, 0, False)

    def qblock(i, carry):
        slot = i % 2
        m_scr[...] = jnp.full(m_scr.shape, -jnp.inf, F32)
        acc_scr[...] = jnp.zeros(acc_scr.shape, F32)
        for c in range(nchunk - 1):
            step(c, slot, c + 1, True)
        load_q(jnp.minimum(i + 1, nq - 1), 1 - slot)
        step(nchunk - 1, 1 - slot, 0, True)

        acc = acc_scr[...]
        inv = 1.0 / acc[HEAD_DIM:HEAD_DIM + 1, :]
        o = acc[0:HEAD_DIM, :] * inv
        cols = pl.ds(pl.multiple_of(i * tq, tq), tq)
        for g in range(Q_PER_KV):
            o_ref[g * HEAD_DIM:(g + 1) * HEAD_DIM, cols] = o[:, g * tq:(g + 1) * tq].astype(BF16)
        return carry

    lax.fori_loop(0, nq, qblock, 0)


def _attention(qhT, qlT, k, vT, *, tq, tk, qb):
    B, _, seq = qhT.shape
    gw = Q_PER_KV * HEAD_DIM
    assert (seq // tk) % 2 == 0
    return pl.pallas_call(
        functools.partial(_attn_kernel, tq=tq, tk=tk, qb=qb),
        grid=(B, N_KV_HEADS),
        in_specs=[
            pl.BlockSpec((None, gw, seq), lambda b, kh: (b, kh, 0)),
            pl.BlockSpec((None, gw, seq), lambda b, kh: (b, kh, 0)),
            pl.BlockSpec((None, seq, MXU_DEPTH), lambda b, kh: (b, 0, kh)),
            pl.BlockSpec((None, HEAD_DIM, seq), lambda b, kh: (b, kh, 0)),
        ],
        out_specs=pl.BlockSpec((None, gw, seq), lambda b, kh: (b, kh, 0)),
        out_shape=jax.ShapeDtypeStruct((B, ATT_WIDTH, seq), BF16),
        scratch_shapes=[
            pltpu.VMEM((PV_ROWS, seq), BF16),
            pltpu.VMEM((2, MXU_DEPTH, Q_PER_KV * tq), F8),
            pltpu.VMEM((1, Q_PER_KV * tq), F32),
            pltpu.VMEM((PV_ROWS, Q_PER_KV * tq), F32),
            pltpu.VMEM((2, tk, Q_PER_KV * tq), F32),
            pltpu.VMEM((2, 1, Q_PER_KV * tq), F32),
        ],
        compiler_params=_cparams(("parallel", "parallel")),
        name="attention",
    )(qhT, qlT, k, vT)


def _mem_kv_kernel(mem_ref, g_ref, w_ref, kT_ref, v_ref):
    D = mem_ref.shape[1]
    mn = _rms_rows(mem_ref[...], g_ref[...]).astype(BF16)
    kv = jnp.dot(mn, w_ref[...], preferred_element_type=F32)
    kT_ref[...] = kv[:, :D].T.astype(BF16)
    v_ref[...] = kv[:, D:].astype(BF16)


def _mem_kv(mem, g, w_kv):
    B, M, D = mem.shape
    L = w_kv.shape[0]
    return pl.pallas_call(
        _mem_kv_kernel,
        grid=(L, B),
        in_specs=[
            pl.BlockSpec((None, M, D), lambda l, b: (b, 0, 0)),
            pl.BlockSpec((1, D), lambda l, b: (0, 0)),
            pl.BlockSpec((None, D, 2 * D), lambda l, b: (l, 0, 0)),
        ],
        out_specs=[
            pl.BlockSpec((None, None, D, M), lambda l, b: (l, b, 0, 0)),
            pl.BlockSpec((None, None, M, D), lambda l, b: (l, b, 0, 0)),
        ],
        out_shape=[
            jax.ShapeDtypeStruct((L, B, D, M), BF16),
            jax.ShapeDtypeStruct((L, B, M, D), BF16),
        ],
        compiler_params=_cparams(("parallel", "parallel")),
        name="mem_kv",
    )(mem, g, w_kv)


def _post_kernel(x_ref, gm_ref, atT_ref, bg1_ref, wout_ref, xg_ref, wq_ref, mkT_ref, mv_ref, wo_ref,
                 fg_ref, wr_ref, x2_ref, h3_ref, affT_ref, *, rows_per_group):
    tm, D = x_ref.shape
    xhd = D // X_HEADS
    groups = [slice(r, r + rows_per_group) for r in range(0, tm, rows_per_group)]

    x1 = []
    for rs in groups:
        at = atT_ref[:, rs].astype(F32)
        ms = jnp.mean(at * at, axis=0, keepdims=True)
        atn = (at * lax.rsqrt(ms + EPS) * bg1_ref[:, rs]).astype(BF16)
        y = jnp.dot(gm_ref[rs, :], wout_ref[0:GM_WIDTH, :], preferred_element_type=F32)
        y = y + lax.dot_general(atn, wout_ref[GM_WIDTH:, :], (((0,), (0,)), ((), ())),
                                preferred_element_type=F32)
        x1.append(x_ref[rs, :] + y)

    q2 = []
    for x1g in x1:
        h2 = _rms_rows(x1g, xg_ref[...]).astype(BF16)
        q2.append((jnp.dot(h2, wq_ref[...], preferred_element_type=F32) * (xhd ** -0.5)).astype(BF16))

    scores = {}
    for hh in range(X_HEADS):
        cs = slice(hh * xhd, (hh + 1) * xhd)
        for gi in range(len(groups)):
            scores[hh, gi] = jnp.dot(q2[gi][:, cs], mkT_ref[cs, :], preferred_element_type=F32)
    outs = [[] for _ in groups]
    for hh in range(X_HEADS):
        cs = slice(hh * xhd, (hh + 1) * xhd)
        for gi in range(len(groups)):
            s = scores[hh, gi]
            s = s - jnp.max(s, axis=-1, keepdims=True)
            p = jnp.exp(s)
            p = (p / jnp.sum(p, axis=-1, keepdims=True)).astype(BF16)
            outs[gi].append(jnp.dot(p, mv_ref[:, cs], preferred_element_type=F32).astype(BF16))

    x2 = []
    for gi, rs in enumerate(groups):
        o2 = jnp.concatenate(outs[gi], axis=1)
        x2g = x1[gi] + jnp.dot(o2, wo_ref[...], preferred_element_type=F32)
        x2_ref[rs, :] = x2g
        x2.append(x2g)

    ne = affT_ref.shape[0]
    for gi, rs in enumerate(groups):
        h3 = _rms_rows(x2[gi], fg_ref[...]).astype(BF16)
        h3_ref[rs, :] = h3
        lg = jnp.dot(h3, wr_ref[...], preferred_element_type=F32).T[0:ne, :]
        lg = lg - jnp.max(lg, axis=0, keepdims=True)
        ex = jnp.exp(lg)
        affT_ref[:, rs] = ex / jnp.sum(ex, axis=0, keepdims=True)


def _post(x2d, gm, atT, bg1, w_out, xg, w_q, mkT, mv, w_o, fg, wr_pad, *, n_experts, layer, seq, tm):
    T, D = x2d.shape
    npb = seq // tm
    M = mv.shape[2]
    E = n_experts
    full = lambda shape: pl.BlockSpec(shape, lambda i: (0,) * len(shape))
    return pl.pallas_call(
        functools.partial(_post_kernel, rows_per_group=min(256, tm)),
        grid=(T // tm,),
        in_specs=[
            pl.BlockSpec((tm, D), lambda i: (i, 0)),
            pl.BlockSpec((tm, GM_WIDTH), lambda i: (i, 0)),
            pl.BlockSpec((None, ATT_WIDTH, tm), lambda i: (i // npb, 0, i % npb)),
            full((ATT_WIDTH, tm)),
            full((GM_WIDTH + ATT_WIDTH, D)),
            full((1, D)),
            full((D, D)),
            pl.BlockSpec((None, None, D, M), lambda i: (layer, i // npb, 0, 0)),
            pl.BlockSpec((None, None, M, D), lambda i: (layer, i // npb, 0, 0)),
            full((D, D)),
            full((1, D)),
            full((D, LANES)),
        ],
        out_specs=[
            pl.BlockSpec((tm, D), lambda i: (i, 0)),
            pl.BlockSpec((tm, D), lambda i: (i, 0)),
            pl.BlockSpec((E, tm), lambda i: (0, i)),
        ],
        out_shape=[
            jax.ShapeDtypeStruct((T, D), F32),
            jax.ShapeDtypeStruct((T, D), BF16),
            jax.ShapeDtypeStruct((E, T), F32),
        ],
        compiler_params=_cparams(("parallel",)),
        name="post",
    )(x2d, gm, atT, bg1, w_out, xg, w_q, mkT, mv, w_o, fg, wr_pad)


def _topk_kernel(aff_ref, slot_ref, off_ref, *, cap):
    E, R, L = aff_ref.shape
    a = aff_ref[...]
    bits = lax.bitcast_convert_type(a, jnp.int32)

    def count_ge(t):
        c = jnp.where(bits >= t, 1.0, 0.0)
        return jnp.sum(jnp.sum(c, axis=2, keepdims=True), axis=1, keepdims=True)

    def bis(_, carry):
        lo, hi = carry
        mid = lo + ((hi - lo) >> 1)
        ok = count_ge(mid) >= cap
        return jnp.where(ok, mid, lo), jnp.where(ok, hi, mid)

    lo0 = jnp.zeros((E, 1, 1), jnp.int32)
    hi0 = jnp.full((E, 1, 1), 0x3F800001, jnp.int32)
    thr, _ = lax.fori_loop(0, 31, bis, (lo0, hi0))

    kk = lax.broadcasted_iota(jnp.int32, (L, L), 0)
    nn = lax.broadcasted_iota(jnp.int32, (L, L), 1)
    upper = jnp.where(kk <= nn, 1.0, 0.0).astype(BF16)
    ones = jnp.ones((L, L), BF16)
    rr = lax.broadcasted_iota(jnp.int32, (E * R, E * R), 0)
    cc = lax.broadcasted_iota(jnp.int32, (E * R, E * R), 1)
    lower = jnp.where((rr // R == cc // R) & (cc < rr), 1.0, 0.0).astype(BF16)

    def prefix(xf):
        x2 = xf.reshape(E * R, L)
        xb = x2.astype(BF16)
        incl = jnp.dot(xb, upper, preferred_element_type=F32)
        tot = jnp.dot(xb, ones, preferred_element_type=F32).astype(BF16)
        rowoff = jnp.dot(lower, tot, preferred_element_type=F32)
        return (incl - x2 + rowoff).reshape(E, R, L), rowoff.reshape(E, R, L)

    gt = jnp.where(bits > thr, 1.0, 0.0)
    eq = jnp.where(bits == thr, 1.0, 0.0)
    n_gt = jnp.sum(jnp.sum(gt, axis=2, keepdims=True), axis=1, keepdims=True)
    need = cap - n_gt
    eq_rank, _ = prefix(eq)
    sel = gt + eq * jnp.where(eq_rank < need, 1.0, 0.0)
    pos, rowoff = prefix(sel)
    slot_ref[...] = jnp.where(sel > 0.0, pos, -1.0).astype(jnp.int32)
    off_ref[...] = rowoff.astype(jnp.int32)


def _topk(aff4, *, cap):
    E, B, R, L = aff4.shape
    return pl.pallas_call(
        functools.partial(_topk_kernel, cap=cap),
        grid=(B,),
        in_specs=[pl.BlockSpec((E, None, R, L), lambda b: (0, b, 0, 0))],
        out_specs=[
            pl.BlockSpec((None, E, R, L), lambda b: (b, 0, 0, 0)),
            pl.BlockSpec((None, E, R, L), lambda b: (b, 0, 0, 0)),
        ],
        out_shape=[
            jax.ShapeDtypeStruct((B, E, R, L), jnp.int32),
            jax.ShapeDtypeStruct((B, E, R, L), jnp.int32),
        ],
        compiler_params=_cparams(("parallel",)),
        name="topk",
    )(aff4)


def _last_block_below(cnt_ref, base, nblk, bound, strict):
    pos = jnp.int32(0)
    step = nblk // 2
    while step >= 1:
        cand = pos + step
        v = cnt_ref[base + cand]
        ok = (v < bound) if strict else (v <= bound)
        pos = jnp.where(ok, cand, pos)
        step //= 2
    return pos


def _gather_kernel(cnt_ref, slot_ref, gate_ref, h_ref, xs_ref, gc_ref, *, win, unroll):
    b = pl.program_id(0)
    e = pl.program_id(1)
    ne = pl.num_programs(1)
    cap, D = xs_ref.shape
    nblk = slot_ref.shape[0]
    base = (b * ne + e) * (nblk + 1)
    r_iota = lax.broadcasted_iota(jnp.int32, (win, TOK_BLOCK), 0)

    def contrib(j, lo):
        jc = jnp.minimum(j, nblk - 1)
        srow = slot_ref[jc]
        rel = jnp.where(j < nblk, srow - lo, -1)
        hit = rel == r_iota
        oh = jnp.where(hit, 1.0, 0.0).astype(BF16)
        hb = h_ref[pl.ds(pl.multiple_of(jc * TOK_BLOCK, TOK_BLOCK), TOK_BLOCK), :]
        gpart = jnp.sum(jnp.where(hit, gate_ref[jc], 0.0), axis=1, keepdims=True)
        return jnp.dot(oh, hb, preferred_element_type=F32), gpart

    spans = []
    left = jnp.int32(0)
    for r in range(cap // win):
        lo = r * win
        j_lo = _last_block_below(cnt_ref, base, nblk, lo, False)
        j_hi = _last_block_below(cnt_ref, base, nblk, lo + win, True)
        acc, gacc = contrib(j_lo, lo)
        for u in range(1, unroll):
            part, gpart = contrib(j_lo + u, lo)
            acc = acc + part
            gacc = gacc + gpart
        xs_ref[lo:lo + win, :] = acc.astype(BF16)
        gc_ref[lo:lo + win, :] = gacc
        spans.append((j_lo + unroll, j_hi + 1))
        left = jnp.maximum(left, j_hi + 1 - (j_lo + unroll))

    @pl.when(left > 0)
    def _():
        for r, (j0, j1) in enumerate(spans):
            lo = r * win

            def more(j, carry, lo=lo):
                part, gpart = contrib(j, lo)
                xs_ref[lo:lo + win, :] = (xs_ref[lo:lo + win, :].astype(F32) + part).astype(BF16)
                gc_ref[lo:lo + win, :] += gpart
                return carry

            lax.fori_loop(j0, j1, more, 0)


def _gather(cnt, slots5, gates5, h3d, *, cap, win):
    B, seq, D = h3d.shape
    E = slots5.shape[1]
    nblk = seq // TOK_BLOCK
    assert nblk & (nblk - 1) == 0 and cap % win == 0
    gs = pltpu.PrefetchScalarGridSpec(
        num_scalar_prefetch=1,
        grid=(B, E),
        in_specs=[
            pl.BlockSpec((None, None, nblk, 1, TOK_BLOCK), lambda b, e, c: (b, e, 0, 0, 0)),
            pl.BlockSpec((None, None, nblk, 1, TOK_BLOCK), lambda b, e, c: (e, b, 0, 0, 0)),
            pl.BlockSpec((None, seq, D), lambda b, e, c: (b, 0, 0), pipeline_mode=pl.Buffered(1)),
        ],
        out_specs=[
            pl.BlockSpec((None, None, cap, D), lambda b, e, c: (b, e, 0, 0)),
            pl.BlockSpec((None, None, cap, 1), lambda b, e, c: (b, e, 0, 0)),
        ],
    )
    return pl.pallas_call(
        functools.partial(_gather_kernel, win=win, unroll=min(6, nblk)),
        grid_spec=gs,
        out_shape=[
            jax.ShapeDtypeStruct((B, E, cap, D), BF16),
            jax.ShapeDtypeStruct((B, E, cap, 1), F32),
        ],
        compiler_params=_cparams(("parallel", "arbitrary")),
        name="gather",
    )(cnt, slots5, gates5, h3d)


def _ffn_kernel(xs_ref, gc_ref, wg_ref, wu_ref, wd_ref, y_ref, wg_scr, wu_scr, wd_scr, *, f_chunk):
    @pl.when(pl.program_id(1) == 0)
    def _():
        wg_scr[...] = wg_ref[...].astype(BF16)
        wu_scr[...] = wu_ref[...].astype(BF16)
        wd_scr[...] = wd_ref[...].astype(BF16)

    xs = xs_ref[...]
    y = None
    for f0 in range(0, wg_scr.shape[1], f_chunk):
        fs = slice(f0, f0 + f_chunk)
        a = jnp.dot(xs, wg_scr[:, fs], preferred_element_type=F32)
        u = jnp.dot(xs, wu_scr[:, fs], preferred_element_type=F32)
        hmid = (a * jax.nn.sigmoid(a) * u).astype(BF16)
        part = jnp.dot(hmid, wd_scr[fs, :], preferred_element_type=F32)
        y = part if y is None else y + part
    y_ref[...] = (y * gc_ref[...]).astype(BF16)


def _ffn(xs, gc, w_gate, w_up, w_down, *, layer):
    B, E, cap, D = xs.shape
    Fd = w_gate.shape[-1]
    return pl.pallas_call(
        functools.partial(_ffn_kernel, f_chunk=min(512, Fd)),
        grid=(E, B),
        in_specs=[
            pl.BlockSpec((None, None, cap, D), lambda e, b: (b, e, 0, 0)),
            pl.BlockSpec((None, None, cap, 1), lambda e, b: (b, e, 0, 0)),
            pl.BlockSpec((None, None, D, Fd), lambda e, b: (layer, e, 0, 0)),
            pl.BlockSpec((None, None, D, Fd), lambda e, b: (layer, e, 0, 0)),
            pl.BlockSpec((None, None, Fd, D), lambda e, b: (layer, e, 0, 0)),
        ],
        out_specs=pl.BlockSpec((None, None, cap, D), lambda e, b: (b, e, 0, 0)),
        out_shape=jax.ShapeDtypeStruct((B, E, cap, D), BF16),
        scratch_shapes=[
            pltpu.VMEM((D, Fd), BF16),
            pltpu.VMEM((D, Fd), BF16),
            pltpu.VMEM((Fd, D), BF16),
        ],
        compiler_params=_cparams(("parallel", "arbitrary")),
        name="ffn",
    )(xs, gc, w_gate, w_up, w_down)


def _scatter_kernel(cnt_ref, slot_ref, y_ref, x_ref, fg_ref, o_ref, *, win, final_norm):
    b = pl.program_id(0)
    sb = pl.program_id(1)
    ne, cap, D = y_ref.shape
    nloc = slot_ref.shape[0] // TOK_BLOCK
    nblk = nloc * pl.num_programs(1)
    lane = lax.broadcasted_iota(jnp.int32, (TOK_BLOCK, win), 1)
    lane2 = lax.broadcasted_iota(jnp.int32, (TOK_BLOCK, 2 * win), 1)
    second = lane2 >= win

    def onehot(scol, lo, sa):
        rel = jnp.where(scol >= lo, scol - sa, -1)
        return jnp.where(rel == lane, 1.0, 0.0).astype(BF16)

    def window(e, j):
        idx = (b * ne + e) * (nblk + 1) + j
        s0 = cnt_ref[idx]
        s1 = cnt_ref[idx + 1]
        a0 = (s0 // 16) * 16
        sa = pl.multiple_of(jnp.minimum(a0, cap - win), 16)
        return a0, sa, s1 - (a0 + win)

    extra = jnp.int32(0)
    per_dot = MXU_DEPTH // win
    for t in range(nloc):
        j = sb * nloc + t
        rows = slice(t * TOK_BLOCK, (t + 1) * TOK_BLOCK)
        sl = slot_ref[rows, :]
        acc = x_ref[rows, :]
        for e0 in range(0, ne, per_dot):
            gs, ys = [], []
            for ea in range(e0, e0 + per_dot, 2):
                eb = ea + 1
                a0a, saa, xa = window(ea, j)
                a0b, sab, xb = window(eb, j)
                extra = jnp.maximum(extra, jnp.maximum(xa, xb))
                slot = jnp.where(second, sl[:, eb:eb + 1], sl[:, ea:ea + 1])
                lo = jnp.where(second, a0b, a0a)
                off = jnp.where(second, sab - win, saa)
                rel = jnp.where(slot >= lo, slot - off, -1)
                gs.append(jnp.where(rel == lane2, 1.0, 0.0).astype(BF16))
                ys.append(y_ref[ea, pl.ds(saa, win), :])
                ys.append(y_ref[eb, pl.ds(sab, win), :])
            acc = acc + jnp.dot(jnp.concatenate(gs, axis=1), jnp.concatenate(ys, axis=0),
                                preferred_element_type=F32)
        o_ref[rows, :] = acc

    @pl.when(extra > 0)
    def _():
        for t in range(nloc):
            j = sb * nloc + t
            rows = slice(t * TOK_BLOCK, (t + 1) * TOK_BLOCK)
            sl = slot_ref[rows, :]
            for e in range(ne):
                idx = (b * ne + e) * (nblk + 1) + j
                s0 = cnt_ref[idx]
                s1 = cnt_ref[idx + 1]
                a0 = (s0 // 16) * 16
                nw = (s1 - a0 + win - 1) // win

                def wbody(w, carry):
                    lo = a0 + w * win
                    sa = pl.multiple_of(jnp.minimum(lo, cap - win), 16)
                    g = onehot(sl[:, e:e + 1], lo, sa)
                    o_ref[rows, :] += jnp.dot(g, y_ref[e, pl.ds(sa, win), :], preferred_element_type=F32)
                    return carry

                lax.fori_loop(1, nw, wbody, 0)

    if final_norm:
        o_ref[...] = _rms_rows(o_ref[...], fg_ref[...])


def _scatter(cnt, slotsT, y, x3d, fg, *, win, sb_rows, final_norm):
    B, seq, D = x3d.shape
    E, cap = y.shape[1], y.shape[2]
    assert MXU_DEPTH % win == 0 and E % max(2, MXU_DEPTH // win) == 0
    gs = pltpu.PrefetchScalarGridSpec(
        num_scalar_prefetch=1,
        grid=(B, seq // sb_rows),
        in_specs=[
            pl.BlockSpec((None, sb_rows, E), lambda b, s, c: (b, s, 0)),
            pl.BlockSpec((None, E, cap, D), lambda b, s, c: (b, 0, 0, 0), pipeline_mode=pl.Buffered(1)),
            pl.BlockSpec((None, sb_rows, D), lambda b, s, c: (b, s, 0)),
            pl.BlockSpec((1, D), lambda b, s, c: (0, 0)),
        ],
        out_specs=pl.BlockSpec((None, sb_rows, D), lambda b, s, c: (b, s, 0)),
    )
    return pl.pallas_call(
        functools.partial(_scatter_kernel, win=win, final_norm=final_norm),
        grid_spec=gs,
        out_shape=jax.ShapeDtypeStruct((B, seq, D), F32),
        compiler_params=_cparams(("parallel", "arbitrary")),
        name="scatter",
    )(cnt, slotsT, y, x3d, fg)


def _rope_tables_T(seq):
    rows = seq // GRID_W
    row_id = jnp.repeat(jnp.arange(rows, dtype=F32), GRID_W)
    col_id = jnp.tile(jnp.arange(GRID_W, dtype=F32), rows)
    n_pairs = HEAD_DIM // 4
    freqs = jnp.exp(-math.log(ROPE_THETA) * jnp.arange(n_pairs, dtype=F32) / n_pairs)
    ang = jnp.concatenate([freqs[:, None] * row_id[None, :], freqs[:, None] * col_id[None, :]], axis=0)
    return jnp.cos(ang), jnp.sin(ang)


def _head_perm(n_heads):
    base = jnp.concatenate([jnp.arange(0, HEAD_DIM, 2), jnp.arange(1, HEAD_DIM, 2)])
    return (jnp.arange(n_heads)[:, None] * HEAD_DIM + base[None, :]).reshape(-1)


def kernel(x, mem, mix_norm_g, w_in, gm_v_norm_g, gm_w_s, gm_b_s, q_norm_g, k_norm_g, branch_norm_g, w_out,
           xattn_norm_g, mem_norm_g, xattn_w_q, xattn_w_kv, xattn_w_o, ffn_norm_g, w_router, w_gate, w_up,
           w_down, final_norm_g):
    B, seq, D = x.shape
    L = w_in.shape[0]
    E = w_router.shape[-1]
    T = B * seq
    cap = EC_FACTOR * seq // E
    tm = min(1024, seq)
    tm_in = min(1024, seq)
    tq = min(256, seq)
    tk = min(512, seq)
    win = min(128, cap)
    sb_rows = min(512, seq)
    nblk = seq // TOK_BLOCK

    cosT, sinT = _rope_tables_T(seq)
    o_q = 2 * GM_WIDTH
    o_k = o_q + ATT_WIDTH
    o_v = o_k + KV_WIDTH
    cols = jnp.concatenate([jnp.arange(o_q), o_q + _head_perm(N_Q_HEADS), o_k + _head_perm(N_KV_HEADS),
                            jnp.arange(o_v, o_v + KV_WIDTH)])
    hp = _head_perm(1)

    mkT_all, mv_all = _mem_kv(mem, mem_norm_g.reshape(1, D), xattn_w_kv.astype(BF16))

    x2d = x.reshape(T, D)
    out = None
    for l in range(L):
        w_in_l = w_in[l][:, cols].astype(BF16)
        bs = jnp.broadcast_to(gm_b_s[l][:, :, None], (GM_GROUPS, CHUNK, LANES))
        qg = jnp.broadcast_to(q_norm_g[l][hp][:, None], (HEAD_DIM, tm_in))
        kg = jnp.broadcast_to(k_norm_g[l][hp][:, None], (HEAD_DIM, tm_in))
        gm, qhT, qlT, k8, vT = _mixer_in(
            x2d, mix_norm_g[l].reshape(1, D), w_in_l, gm_v_norm_g[l].reshape(1, GM_WIDTH),
            gm_w_s[l].astype(BF16), bs, qg, kg, branch_norm_g[l, 0].reshape(1, GM_WIDTH), cosT, sinT,
            seq=seq, tm=tm_in)
        atT = _attention(qhT, qlT, k8.reshape(B, seq, N_KV_HEADS * MXU_DEPTH), vT,
                         tq=tq, tk=tk, qb=min(256, tq))

        bg1 = jnp.broadcast_to(branch_norm_g[l, 1][:, None], (ATT_WIDTH, tm))
        wr_pad = jnp.pad(w_router[l].astype(BF16), ((0, 0), (0, LANES - E)))
        x2, h3, affT = _post(
            x2d, gm, atT, bg1, w_out[l].astype(BF16), xattn_norm_g[l].reshape(1, D),
            xattn_w_q[l].astype(BF16), mkT_all, mv_all, xattn_w_o[l].astype(BF16),
            ffn_norm_g[l].reshape(1, D), wr_pad, n_experts=E, layer=l, seq=seq, tm=tm)

        slots, offs = _topk(affT.reshape(E, B, seq // LANES, LANES), cap=cap)
        cnt = offs[:, :, ::TOK_BLOCK // LANES, 0]
        cnt = jnp.concatenate([cnt, jnp.full((B, E, 1), cap, jnp.int32)], axis=-1).reshape(-1)
        slots5 = slots.reshape(B, E, nblk, 1, TOK_BLOCK)
        slotsT = slots.reshape(B, E, seq).transpose(0, 2, 1)
        gates5 = affT.reshape(E, B, nblk, 1, TOK_BLOCK)

        xs, gc = _gather(cnt, slots5, gates5, h3.reshape(B, seq, D), cap=cap, win=win)
        y = _ffn(xs, gc, w_gate, w_up, w_down, layer=l)
        out = _scatter(cnt, slotsT, y, x2.reshape(B, seq, D), final_norm_g.reshape(1, D),
                       win=min(64, cap), sb_rows=sb_rows, final_norm=(l == L - 1))
        x2d = out.reshape(T, D)
    return out
```

```python
import functools
import math

import jax
import jax.numpy as jnp
from jax import lax
from jax.experimental import pallas as pl
from jax.experimental.pallas import tpu as pltpu

F32 = jnp.float32
BF16 = jnp.bfloat16
F8 = jnp.float8_e4m3fn
F8_MAX = 448.0

EPS = 1e-6
CHUNK = 128
GM_GROUPS = 4
GM_WIDTH = 512
HEAD_DIM = 64
N_Q_HEADS = 8
N_KV_HEADS = 2
Q_PER_KV = N_Q_HEADS // N_KV_HEADS
ATT_WIDTH = N_Q_HEADS * HEAD_DIM
KV_WIDTH = N_KV_HEADS * HEAD_DIM
ROPE_THETA = 10000.0
GRID_W = 64
X_HEADS = 4
N_EXPERTS = 16
EC_FACTOR = 2

LANES = 128
MXU_DEPTH = 256
PV_ROWS = 2 * HEAD_DIM
VMEM_LIMIT = 56 * 1024 * 1024

TOK_BLOCK = 256
LOG2E = 1.4426950408889634


def _cparams(sem, vmem=VMEM_LIMIT, **kw):
    return pltpu.CompilerParams(dimension_semantics=sem, vmem_limit_bytes=vmem, **kw)


def _rms_rows(x, g):
    ms = jnp.mean(x * x, axis=-1, keepdims=True)
    return x * lax.rsqrt(ms + EPS) * g


def _gelu_tanh(x):
    c = math.sqrt(2.0 / math.pi)
    return 0.5 * x * (1.0 + jnp.tanh(c * (x + 0.044715 * (x * x * x))))


def _norm_rope_T(xT, gcol, cosT, sinT, n_heads, scale):
    half = HEAD_DIM // 2
    outs = []
    for h in range(n_heads):
        blk = xT[h * HEAD_DIM:(h + 1) * HEAD_DIM, :]
        ms = jnp.mean(blk * blk, axis=0, keepdims=True)
        n = blk * lax.rsqrt(ms + EPS) * gcol
        e = n[:half, :]
        o = n[half:, :]
        re = (e * cosT - o * sinT) * scale
        ro = (e * sinT + o * cosT) * scale
        outs.append(jnp.concatenate([re, ro], axis=0))
    return outs


def _split_f8(x):
    x = jnp.clip(x, -F8_MAX, F8_MAX)
    hi = x.astype(F8).astype(F32)
    lo = (x - hi).astype(F8).astype(F32)
    return hi, lo


def _mixer_in_kernel(x_ref, g_ref, w_ref, vg_ref, ws_ref, bs_ref, qg_ref, kg_ref, bg0_ref,
                     cos_ref, sin_ref, gm_ref, qhT_ref, qlT_ref, k_ref, vT_ref, gm_scr, *, rows_per_group):
    tm = x_ref.shape[0]
    o_q = 2 * GM_WIDTH
    o_k = o_q + ATT_WIDTH
    o_v = o_k + KV_WIDTH
    groups = [slice(r, r + rows_per_group) for r in range(0, tm, rows_per_group)]

    projs = []
    for rs in groups:
        h = _rms_rows(x_ref[rs, :], g_ref[...]).astype(BF16)
        projs.append(jnp.dot(h, w_ref[...], preferred_element_type=F32))

    for rs, proj in zip(groups, projs):
        u = _gelu_tanh(proj[:, :GM_WIDTH])
        v = _gelu_tanh(proj[:, GM_WIDTH:2 * GM_WIDTH])
        vn = _rms_rows(v, vg_ref[...]).astype(BF16)
        for c in range(rows_per_group // CHUNK):
            cr = slice(c * CHUNK, (c + 1) * CHUNK)
            sr = slice(rs.start + c * CHUNK, rs.start + (c + 1) * CHUNK)
            for g in range(GM_GROUPS):
                cs = slice(g * LANES, (g + 1) * LANES)
                mixed = jnp.dot(ws_ref[g], vn[cr, cs], preferred_element_type=F32) + bs_ref[g]
                gm_scr[sr, cs] = u[cr, cs] * mixed
        gm_ref[rs, :] = _rms_rows(gm_scr[rs, :], bg0_ref[...]).astype(BF16)

        cosT = cos_ref[:, rs]
        sinT = sin_ref[:, rs]
        qT = proj[:, o_q:o_k].T
        q_heads = _norm_rope_T(qT, qg_ref[:, rs], cosT, sinT, N_Q_HEADS, (HEAD_DIM ** -0.5) * LOG2E)
        for hh in range(N_Q_HEADS):
            hi, lo = _split_f8(q_heads[hh])
            qhT_ref[hh * HEAD_DIM:(hh + 1) * HEAD_DIM, rs] = hi.astype(F8)
            qlT_ref[hh * HEAD_DIM:(hh + 1) * HEAD_DIM, rs] = lo.astype(F8)

        kT = proj[:, o_k:o_v].T
        k_heads = _norm_rope_T(kT, kg_ref[:, rs], cosT, sinT, N_KV_HEADS, 1.0)
        pieces = []
        for kh in k_heads:
            hi, lo = _split_f8(kh)
            pieces += [hi, lo, hi, jnp.zeros_like(hi)]
        k_ref[rs, :] = jnp.concatenate(pieces, axis=0).T.astype(F8)

        vT_ref[:, rs] = proj[:, o_v:].T.astype(BF16)


def _mixer_in(x2d, g, w_in, vg, ws, bs, qg, kg, bg0, cosT, sinT, *, seq, tm):
    T, D = x2d.shape
    B = T // seq
    npb = seq // tm
    in_w = w_in.shape[1]
    full = lambda shape: pl.BlockSpec(shape, lambda i: (0,) * len(shape))
    return pl.pallas_call(
        functools.partial(_mixer_in_kernel, rows_per_group=min(256, tm)),
        grid=(T // tm,),
        in_specs=[
            pl.BlockSpec((tm, D), lambda i: (i, 0)),
            full((1, D)),
            full((D, in_w)),
            full((1, GM_WIDTH)),
            full((GM_GROUPS, CHUNK, CHUNK)),
            full((GM_GROUPS, CHUNK, LANES)),
            full((HEAD_DIM, tm)),
            full((HEAD_DIM, tm)),
            full((1, GM_WIDTH)),
            pl.BlockSpec((HEAD_DIM // 2, tm), lambda i: (0, i % npb)),
            pl.BlockSpec((HEAD_DIM // 2, tm), lambda i: (0, i % npb)),
        ],
        out_specs=[
            pl.BlockSpec((tm, GM_WIDTH), lambda i: (i, 0)),
            pl.BlockSpec((None, ATT_WIDTH, tm), lambda i: (i // npb, 0, i % npb)),
            pl.BlockSpec((None, ATT_WIDTH, tm), lambda i: (i // npb, 0, i % npb)),
            pl.BlockSpec((tm, N_KV_HEADS * MXU_DEPTH), lambda i: (i, 0)),
            pl.BlockSpec((None, KV_WIDTH, tm), lambda i: (i // npb, 0, i % npb)),
        ],
        out_shape=[
            jax.ShapeDtypeStruct((T, GM_WIDTH), BF16),
            jax.ShapeDtypeStruct((B, ATT_WIDTH, seq), F8),
            jax.ShapeDtypeStruct((B, ATT_WIDTH, seq), F8),
            jax.ShapeDtypeStruct((T, N_KV_HEADS * MXU_DEPTH), F8),
            jax.ShapeDtypeStruct((B, KV_WIDTH, seq), BF16),
        ],
        scratch_shapes=[pltpu.VMEM((tm, GM_WIDTH), F32)],
        compiler_params=_cparams(("parallel",)),
        name="mixer_in",
    )(x2d, g, w_in, vg, ws, bs, qg, kg, bg0, cosT, sinT)


def _attn_kernel(qhT_ref, qlT_ref, k_ref, vT_ref, o_ref, vext_scr, qx_scr, m_scr, acc_scr, s_scr, cm_scr,
                 *, tq, tk, qb):
    seq = k_ref.shape[0]
    M = Q_PER_KV * tq
    nq = seq // tq
    nchunk = seq // tk

    vext_scr[0:HEAD_DIM, :] = vT_ref[...]
    row = lax.broadcasted_iota(jnp.int32, (PV_ROWS - HEAD_DIM, seq), 0)
    vext_scr[HEAD_DIM:, :] = jnp.where(row == 0, 1.0, 0.0).astype(BF16)

    def load_q(i, slot):
        cols = pl.ds(pl.multiple_of(i * tq, tq), tq)
        qh = jnp.concatenate([qhT_ref[g * HEAD_DIM:(g + 1) * HEAD_DIM, cols] for g in range(Q_PER_KV)], axis=1)
        ql = jnp.concatenate([qlT_ref[g * HEAD_DIM:(g + 1) * HEAD_DIM, cols] for g in range(Q_PER_KV)], axis=1)
        qx_scr[slot] = jnp.concatenate([qh, qh, ql, jnp.zeros_like(qh)], axis=0)

    def step(c, qslot, qk_chunk, do_pv):
        cur = c % 2
        nxt = 1 - cur
        if qk_chunk is not None:
            kn = k_ref[qk_chunk * tk:(qk_chunk + 1) * tk, :]
        if do_pv:
            ve = vext_scr[:, c * tk:(c + 1) * tk]
        for j in range(M // qb):
            cs = slice(j * qb, (j + 1) * qb)
            if qk_chunk is not None:
                s = jnp.dot(kn, qx_scr[qslot, :, cs], preferred_element_type=F32)
                s_scr[nxt, :, cs] = s
                cm_scr[nxt, :, cs] = jnp.max(s, axis=0, keepdims=True)
            if do_pv:
                m_old = m_scr[:, cs]
                m_new = jnp.maximum(m_old, cm_scr[cur, :, cs])
                p = jnp.exp2(s_scr[cur, :, cs] - m_new).astype(BF16)
                alpha = jnp.exp2(m_old - m_new)
                pv = jnp.dot(ve, p, preferred_element_type=F32)
                acc_scr[:, cs] = acc_scr[:, cs] * alpha + pv
                m_scr[:, cs] = m_new

    load_q(0, 0)
    step(-1, 0, 0, False)

    def qblock(i, carry):
        slot = i % 2
        m_scr[...] = jnp.full(m_scr.shape, -jnp.inf, F32)
        acc_scr[...] = jnp.zeros(acc_scr.shape, F32)
        for c in range(nchunk - 1):
            step(c, slot, c + 1, True)
        load_q(jnp.minimum(i + 1, nq - 1), 1 - slot)
        step(nchunk - 1, 1 - slot, 0, True)

        acc = acc_scr[...]
        inv = 1.0 / acc[HEAD_DIM:HEAD_DIM + 1, :]
        o = acc[0:HEAD_DIM, :] * inv
        cols = pl.ds(pl.multiple_of(i * tq, tq), tq)
        for g in range(Q_PER_KV):
            o_ref[g * HEAD_DIM:(g + 1) * HEAD_DIM, cols] = o[:, g * tq:(g + 1) * tq].astype(BF16)
        return carry

    lax.fori_loop(0, nq, qblock, 0)


def _attention(qhT, qlT, k, vT, *, tq, tk, qb):
    B, _, seq = qhT.shape
    gw = Q_PER_KV * HEAD_DIM
    assert (seq // tk) % 2 == 0
    return pl.pallas_call(
        functools.partial(_attn_kernel, tq=tq, tk=tk, qb=qb),
        grid=(B, N_KV_HEADS),
        in_specs=[
            pl.BlockSpec((None, gw, seq), lambda b, kh: (b, kh, 0)),
            pl.BlockSpec((None, gw, seq), lambda b, kh: (b, kh, 0)),
            pl.BlockSpec((None, seq, MXU_DEPTH), lambda b, kh: (b, 0, kh)),
            pl.BlockSpec((None, HEAD_DIM, seq), lambda b, kh: (b, kh, 0)),
        ],
        out_specs=pl.BlockSpec((None, gw, seq), lambda b, kh: (b, kh, 0)),
        out_shape=jax.ShapeDtypeStruct((B, ATT_WIDTH, seq), BF16),
        scratch_shapes=[
            pltpu.VMEM((PV_ROWS, seq), BF16),
            pltpu.VMEM((2, MXU_DEPTH, Q_PER_KV * tq), F8),
            pltpu.VMEM((1, Q_PER_KV * tq), F32),
            pltpu.VMEM((PV_ROWS, Q_PER_KV * tq), F32),
            pltpu.VMEM((2, tk, Q_PER_KV * tq), F32),
            pltpu.VMEM((2, 1, Q_PER_KV * tq), F32),
        ],
        compiler_params=_cparams(("parallel", "parallel")),
        name="attention",
    )(qhT, qlT, k, vT)


def _mem_kv_kernel(mem_ref, g_ref, w_ref, kT_ref, v_ref):
    D = mem_ref.shape[1]
    mn = _rms_rows(mem_ref[...], g_ref[...]).astype(BF16)
    kv = jnp.dot(mn, w_ref[...], preferred_element_type=F32)
    kT_ref[...] = kv[:, :D].T.astype(BF16)
    v_ref[...] = kv[:, D:].astype(BF16)


def _mem_kv(mem, g, w_kv):
    B, M, D = mem.shape
    L = w_kv.shape[0]
    return pl.pallas_call(
        _mem_kv_kernel,
        grid=(L, B),
        in_specs=[
            pl.BlockSpec((None, M, D), lambda l, b: (b, 0, 0)),
            pl.BlockSpec((1, D), lambda l, b: (0, 0)),
            pl.BlockSpec((None, D, 2 * D), lambda l, b: (l, 0, 0)),
        ],
        out_specs=[
            pl.BlockSpec((None, None, D, M), lambda l, b: (l, b, 0, 0)),
            pl.BlockSpec((None, None, M, D), lambda l, b: (l, b, 0, 0)),
        ],
        out_shape=[
            jax.ShapeDtypeStruct((L, B, D, M), BF16),
            jax.ShapeDtypeStruct((L, B, M, D), BF16),
        ],
        compiler_params=_cparams(("parallel", "parallel")),
        name="mem_kv",
    )(mem, g, w_kv)


def _post_kernel(x_ref, gm_ref, atT_ref, bg1_ref, wout_ref, xg_ref, wq_ref, mkT_ref, mv_ref, wo_ref,
                 fg_ref, wr_ref, x2_ref, h3_ref, affT_ref, *, rows_per_group):
    tm, D = x_ref.shape
    xhd = D // X_HEADS
    groups = [slice(r, r + rows_per_group) for r in range(0, tm, rows_per_group)]

    x1 = []
    for rs in groups:
        at = atT_ref[:, rs].astype(F32)
        ms = jnp.mean(at * at, axis=0, keepdims=True)
        atn = (at * lax.rsqrt(ms + EPS) * bg1_ref[:, rs]).astype(BF16)
        y = jnp.dot(gm_ref[rs, :], wout_ref[0:GM_WIDTH, :], preferred_element_type=F32)
        y = y + lax.dot_general(atn, wout_ref[GM_WIDTH:, :], (((0,), (0,)), ((), ())),
                                preferred_element_type=F32)
        x1.append(x_ref[rs, :] + y)

    q2 = []
    for x1g in x1:
        h2 = _rms_rows(x1g, xg_ref[...]).astype(BF16)
        q2.append((jnp.dot(h2, wq_ref[...], preferred_element_type=F32) * (xhd ** -0.5)).astype(BF16))

    scores = {}
    for hh in range(X_HEADS):
        cs = slice(hh * xhd, (hh + 1) * xhd)
        for gi in range(len(groups)):
            scores[hh, gi] = jnp.dot(q2[gi][:, cs], mkT_ref[cs, :], preferred_element_type=F32)
    outs = [[] for _ in groups]
    for hh in range(X_HEADS):
        cs = slice(hh * xhd, (hh + 1) * xhd)
        for gi in range(len(groups)):
            s = scores[hh, gi]
            s = s - jnp.max(s, axis=-1, keepdims=True)
            p = jnp.exp(s)
            p = (p / jnp.sum(p, axis=-1, keepdims=True)).astype(BF16)
            outs[gi].append(jnp.dot(p, mv_ref[:, cs], preferred_element_type=F32).astype(BF16))

    x2 = []
    for gi, rs in enumerate(groups):
        o2 = jnp.concatenate(outs[gi], axis=1)
        x2g = x1[gi] + jnp.dot(o2, wo_ref[...], preferred_element_type=F32)
        x2_ref[rs, :] = x2g
        x2.append(x2g)

    ne = affT_ref.shape[0]
    for gi, rs in enumerate(groups):
        h3 = _rms_rows(x2[gi], fg_ref[...]).astype(BF16)
        h3_ref[rs, :] = h3
        lg = jnp.dot(h3, wr_ref[...], preferred_element_type=F32).T[0:ne, :]
        lg = lg - jnp.max(lg, axis=0, keepdims=True)
        ex = jnp.exp(lg)
        affT_ref[:, rs] = ex / jnp.sum(ex, axis=0, keepdims=True)


def _post(x2d, gm, atT, bg1, w_out, xg, w_q, mkT, mv, w_o, fg, wr_pad, *, n_experts, layer, seq, tm):
    T, D = x2d.shape
    npb = seq // tm
    M = mv.shape[2]
    E = n_experts
    full = lambda shape: pl.BlockSpec(shape, lambda i: (0,) * len(shape))
    return pl.pallas_call(
        functools.partial(_post_kernel, rows_per_group=min(256, tm)),
        grid=(T // tm,),
        in_specs=[
            pl.BlockSpec((tm, D), lambda i: (i, 0)),
            pl.BlockSpec((tm, GM_WIDTH), lambda i: (i, 0)),
            pl.BlockSpec((None, ATT_WIDTH, tm), lambda i: (i // npb, 0, i % npb)),
            full((ATT_WIDTH, tm)),
            full((GM_WIDTH + ATT_WIDTH, D)),
            full((1, D)),
            full((D, D)),
            pl.BlockSpec((None, None, D, M), lambda i: (layer, i // npb, 0, 0)),
            pl.BlockSpec((None, None, M, D), lambda i: (layer, i // npb, 0, 0)),
            full((D, D)),
            full((1, D)),
            full((D, LANES)),
        ],
        out_specs=[
            pl.BlockSpec((tm, D), lambda i: (i, 0)),
            pl.BlockSpec((tm, D), lambda i: (i, 0)),
            pl.BlockSpec((E, tm), lambda i: (0, i)),
        ],
        out_shape=[
            jax.ShapeDtypeStruct((T, D), F32),
            jax.ShapeDtypeStruct((T, D), BF16),
            jax.ShapeDtypeStruct((E, T), F32),
        ],
        compiler_params=_cparams(("parallel",)),
        name="post",
    )(x2d, gm, atT, bg1, w_out, xg, w_q, mkT, mv, w_o, fg, wr_pad)


def _topk_kernel(aff_ref, slot_ref, off_ref, *, cap):
    E, R, L = aff_ref.shape
    a = aff_ref[...]
    bits = lax.bitcast_convert_type(a, jnp.int32)

    def count_ge(t):
        c = jnp.where(bits >= t, 1.0, 0.0)
        return jnp.sum(jnp.sum(c, axis=2, keepdims=True), axis=1, keepdims=True)

    def bis(_, carry):
        lo, hi = carry
        mid = lo + ((hi - lo) >> 1)
        ok = count_ge(mid) >= cap
        return jnp.where(ok, mid, lo), jnp.where(ok, hi, mid)

    lo0 = jnp.zeros((E, 1, 1), jnp.int32)
    hi0 = jnp.full((E, 1, 1), 0x3F800001, jnp.int32)
    thr, _ = lax.fori_loop(0, 31, bis, (lo0, hi0))

    kk = lax.broadcasted_iota(jnp.int32, (L, L), 0)
    nn = lax.broadcasted_iota(jnp.int32, (L, L), 1)
    upper = jnp.where(kk <= nn, 1.0, 0.0).astype(BF16)
    ones = jnp.ones((L, L), BF16)
    rr = lax.broadcasted_iota(jnp.int32, (E * R, E * R), 0)
    cc = lax.broadcasted_iota(jnp.int32, (E * R, E * R), 1)
    lower = jnp.where((rr // R == cc // R) & (cc < rr), 1.0, 0.0).astype(BF16)

    def prefix(xf):
        x2 = xf.reshape(E * R, L)
        xb = x2.astype(BF16)
        incl = jnp.dot(xb, upper, preferred_element_type=F32)
        tot = jnp.dot(xb, ones, preferred_element_type=F32).astype(BF16)
        rowoff = jnp.dot(lower, tot, preferred_element_type=F32)
        return (incl - x2 + rowoff).reshape(E, R, L), rowoff.reshape(E, R, L)

    gt = jnp.where(bits > thr, 1.0, 0.0)
    eq = jnp.where(bits == thr, 1.0, 0.0)
    n_gt = jnp.sum(jnp.sum(gt, axis=2, keepdims=True), axis=1, keepdims=True)
    need = cap - n_gt
    eq_rank, _ = prefix(eq)
    sel = gt + eq * jnp.where(eq_rank < need, 1.0, 0.0)
    pos, rowoff = prefix(sel)
    slot_ref[...] = jnp.where(sel > 0.0, pos, -1.0).astype(jnp.int32)
    off_ref[...] = rowoff.astype(jnp.int32)


def _topk(aff4, *, cap):
    E, B, R, L = aff4.shape
    return pl.pallas_call(
        functools.partial(_topk_kernel, cap=cap),
        grid=(B,),
        in_specs=[pl.BlockSpec((E, None, R, L), lambda b: (0, b, 0, 0))],
        out_specs=[
            pl.BlockSpec((None, E, R, L), lambda b: (b, 0, 0, 0)),
            pl.BlockSpec((None, E, R, L), lambda b: (b, 0, 0, 0)),
        ],
        out_shape=[
            jax.ShapeDtypeStruct((B, E, R, L), jnp.int32),
            jax.ShapeDtypeStruct((B, E, R, L), jnp.int32),
        ],
        compiler_params=_cparams(("parallel",)),
        name="topk",
    )(aff4)


def _last_block_below(cnt_ref, base, nblk, bound, strict):
    pos = jnp.int32(0)
    step = nblk // 2
    while step >= 1:
        cand = pos + step
        v = cnt_ref[base + cand]
        ok = (v < bound) if strict else (v <= bound)
        pos = jnp.where(ok, cand, pos)
        step //= 2
    return pos


def _gather_kernel(cnt_ref, slot_ref, gate_ref, h_ref, xs_ref, gc_ref, *, win, unroll):
    b = pl.program_id(0)
    e = pl.program_id(1)
    ne = pl.num_programs(1)
    cap, D = xs_ref.shape
    nblk = slot_ref.shape[0]
    base = (b * ne + e) * (nblk + 1)
    r_iota = lax.broadcasted_iota(jnp.int32, (win, TOK_BLOCK), 0)

    def contrib(j, lo):
        jc = jnp.minimum(j, nblk - 1)
        srow = slot_ref[jc]
        rel = jnp.where(j < nblk, srow - lo, -1)
        hit = rel == r_iota
        oh = jnp.where(hit, 1.0, 0.0).astype(BF16)
        hb = h_ref[pl.ds(pl.multiple_of(jc * TOK_BLOCK, TOK_BLOCK), TOK_BLOCK), :]
        gpart = jnp.sum(jnp.where(hit, gate_ref[jc], 0.0), axis=1, keepdims=True)
        return jnp.dot(oh, hb, preferred_element_type=F32), gpart

    spans = []
    left = jnp.int32(0)
    for r in range(cap // win):
        lo = r * win
        j_lo = _last_block_below(cnt_ref, base, nblk, lo, False)
        j_hi = _last_block_below(cnt_ref, base, nblk, lo + win, True)
        acc, gacc = contrib(j_lo, lo)
        for u in range(1, unroll):
            part, gpart = contrib(j_lo + u, lo)
            acc = acc + part
            gacc = gacc + gpart
        xs_ref[lo:lo + win, :] = acc.astype(BF16)
        gc_ref[lo:lo + win, :] = gacc
        spans.append((j_lo + unroll, j_hi + 1))
        left = jnp.maximum(left, j_hi + 1 - (j_lo + unroll))

    @pl.when(left > 0)
    def _():
        for r, (j0, j1) in enumerate(spans):
            lo = r * win

            def more(j, carry, lo=lo):
                part, gpart = contrib(j, lo)
                xs_ref[lo:lo + win, :] = (xs_ref[lo:lo + win, :].astype(F32) + part).astype(BF16)
                gc_ref[lo:lo + win, :] += gpart
                return carry

            lax.fori_loop(j0, j1, more, 0)


def _gather(cnt, slots5, gates5, h3d, *, cap, win):
    B, seq, D = h3d.shape
    E = slots5.shape[1]
    nblk = seq // TOK_BLOCK
    assert nblk & (nblk - 1) == 0 and cap % win == 0
    gs = pltpu.PrefetchScalarGridSpec(
        num_scalar_prefetch=1,
        grid=(B, E),
        in_specs=[
            pl.BlockSpec((None, None, nblk, 1, TOK_BLOCK), lambda b, e, c: (b, e, 0, 0, 0)),
            pl.BlockSpec((None, None, nblk, 1, TOK_BLOCK), lambda b, e, c: (e, b, 0, 0, 0)),
            pl.BlockSpec((None, seq, D), lambda b, e, c: (b, 0, 0), pipeline_mode=pl.Buffered(1)),
        ],
        out_specs=[
            pl.BlockSpec((None, None, cap, D), lambda b, e, c: (b, e, 0, 0)),
            pl.BlockSpec((None, None, cap, 1), lambda b, e, c: (b, e, 0, 0)),
        ],
    )
    return pl.pallas_call(
        functools.partial(_gather_kernel, win=win, unroll=min(6, nblk)),
        grid_spec=gs,
        out_shape=[
            jax.ShapeDtypeStruct((B, E, cap, D), BF16),
            jax.ShapeDtypeStruct((B, E, cap, 1), F32),
        ],
        compiler_params=_cparams(("parallel", "arbitrary")),
        name="gather",
    )(cnt, slots5, gates5, h3d)


def _ffn_kernel(xs_ref, gc_ref, wg_ref, wu_ref, wd_ref, y_ref, wg_scr, wu_scr, wd_scr, *, f_chunk):
    @pl.when(pl.program_id(1) == 0)
    def _():
        wg_scr[...] = wg_ref[...].astype(BF16)
        wu_scr[...] = wu_ref[...].astype(BF16)
        wd_scr[...] = wd_ref[...].astype(BF16)

    xs = xs_ref[...]
    y = None
    for f0 in range(0, wg_scr.shape[1], f_chunk):
        fs = slice(f0, f0 + f_chunk)
        a = jnp.dot(xs, wg_scr[:, fs], preferred_element_type=F32)
        u = jnp.dot(xs, wu_scr[:, fs], preferred_element_type=F32)
        hmid = (a * jax.nn.sigmoid(a) * u).astype(BF16)
        part = jnp.dot(hmid, wd_scr[fs, :], preferred_element_type=F32)
        y = part if y is None else y + part
    y_ref[...] = (y * gc_ref[...]).astype(BF16)


def _ffn(xs, gc, w_gate, w_up, w_down, *, layer):
    B, E, cap, D = xs.shape
    Fd = w_gate.shape[-1]
    return pl.pallas_call(
        functools.partial(_ffn_kernel, f_chunk=min(512, Fd)),
        grid=(E, B),
        in_specs=[
            pl.BlockSpec((None, None, cap, D), lambda e, b: (b, e, 0, 0)),
            pl.BlockSpec((None, None, cap, 1), lambda e, b: (b, e, 0, 0)),
            pl.BlockSpec((None, None, D, Fd), lambda e, b: (layer, e, 0, 0)),
            pl.BlockSpec((None, None, D, Fd), lambda e, b: (layer, e, 0, 0)),
            pl.BlockSpec((None, None, Fd, D), lambda e, b: (layer, e, 0, 0)),
        ],
        out_specs=pl.BlockSpec((None, None, cap, D), lambda e, b: (b, e, 0, 0)),
        out_shape=jax.ShapeDtypeStruct((B, E, cap, D), BF16),
        scratch_shapes=[
            pltpu.VMEM((D, Fd), BF16),
            pltpu.VMEM((D, Fd), BF16),
            pltpu.VMEM((Fd, D), BF16),
        ],
        compiler_params=_cparams(("parallel", "arbitrary")),
        name="ffn",
    )(xs, gc, w_gate, w_up, w_down)


def _scatter_kernel(cnt_ref, slot_ref, y_ref, x_ref, fg_ref, o_ref, *, win, final_norm):
    b = pl.program_id(0)
    sb = pl.program_id(1)
    ne, cap, D = y_ref.shape
    nloc = slot_ref.shape[1]
    nblk = nloc * pl.num_programs(1)
    r_iota = lax.broadcasted_iota(jnp.int32, (win, TOK_BLOCK), 0)
    contract0 = (((0,), (0,)), ((), ()))

    def onehot_t(srow, lo, sa):
        rel = jnp.where(srow >= lo, srow - sa, -1)
        return jnp.where(rel == r_iota, 1.0, 0.0).astype(BF16)

    def window(e, j):
        idx = (b * ne + e) * (nblk + 1) + j
        s0 = cnt_ref[idx]
        s1 = cnt_ref[idx + 1]
        a0 = (s0 // 16) * 16
        sa = pl.multiple_of(jnp.minimum(a0, cap - win), 16)
        return a0, sa, s1 - (a0 + win)

    extra = jnp.int32(0)
    per_dot = MXU_DEPTH // win
    for t in range(nloc):
        j = sb * nloc + t
        rows = slice(t * TOK_BLOCK, (t + 1) * TOK_BLOCK)
        acc = x_ref[rows, :]
        for e0 in range(0, ne, per_dot):
            gs, ys = [], []
            for e in range(e0, e0 + per_dot):
                a0, sa, over = window(e, j)
                extra = jnp.maximum(extra, over)
                gs.append(onehot_t(slot_ref[e, t], a0, sa))
                ys.append(y_ref[e, pl.ds(sa, win), :])
            acc = acc + lax.dot_general(jnp.concatenate(gs, axis=0), jnp.concatenate(ys, axis=0),
                                        contract0, preferred_element_type=F32)
        o_ref[rows, :] = acc

    @pl.when(extra > 0)
    def _():
        for t in range(nloc):
            j = sb * nloc + t
            rows = slice(t * TOK_BLOCK, (t + 1) * TOK_BLOCK)
            for e in range(ne):
                idx = (b * ne + e) * (nblk + 1) + j
                s0 = cnt_ref[idx]
                s1 = cnt_ref[idx + 1]
                a0 = (s0 // 16) * 16
                nw = (s1 - a0 + win - 1) // win

                def wbody(w, carry):
                    lo = a0 + w * win
                    sa = pl.multiple_of(jnp.minimum(lo, cap - win), 16)
                    g = onehot_t(slot_ref[e, t], lo, sa)
                    o_ref[rows, :] += lax.dot_general(g, y_ref[e, pl.ds(sa, win), :], contract0,
                                                      preferred_element_type=F32)
                    return carry

                lax.fori_loop(1, nw, wbody, 0)

    if final_norm:
        o_ref[...] = _rms_rows(o_ref[...], fg_ref[...])


def _scatter(cnt, slots5, y, x3d, fg, *, win, sb_rows, final_norm):
    B, seq, D = x3d.shape
    E, cap = y.shape[1], y.shape[2]
    nloc = sb_rows // TOK_BLOCK
    assert MXU_DEPTH % win == 0 and E % (MXU_DEPTH // win) == 0
    gs = pltpu.PrefetchScalarGridSpec(
        num_scalar_prefetch=1,
        grid=(B, seq // sb_rows),
        in_specs=[
            pl.BlockSpec((None, E, nloc, 1, TOK_BLOCK), lambda b, s, c: (b, 0, s, 0, 0)),
            pl.BlockSpec((None, E, cap, D), lambda b, s, c: (b, 0, 0, 0), pipeline_mode=pl.Buffered(1)),
            pl.BlockSpec((None, sb_rows, D), lambda b, s, c: (b, s, 0)),
            pl.BlockSpec((1, D), lambda b, s, c: (0, 0)),
        ],
        out_specs=pl.BlockSpec((None, sb_rows, D), lambda b, s, c: (b, s, 0)),
    )
    return pl.pallas_call(
        functools.partial(_scatter_kernel, win=win, final_norm=final_norm),
        grid_spec=gs,
        out_shape=jax.ShapeDtypeStruct((B, seq, D), F32),
        compiler_params=_cparams(("parallel", "arbitrary")),
        name="scatter",
    )(cnt, slots5, y, x3d, fg)


def _rope_tables_T(seq):
    rows = seq // GRID_W
    row_id = jnp.repeat(jnp.arange(rows, dtype=F32), GRID_W)
    col_id = jnp.tile(jnp.arange(GRID_W, dtype=F32), rows)
    n_pairs = HEAD_DIM // 4
    freqs = jnp.exp(-math.log(ROPE_THETA) * jnp.arange(n_pairs, dtype=F32) / n_pairs)
    ang = jnp.concatenate([freqs[:, None] * row_id[None, :], freqs[:, None] * col_id[None, :]], axis=0)
    return jnp.cos(ang), jnp.sin(ang)


def _head_perm(n_heads):
    base = jnp.concatenate([jnp.arange(0, HEAD_DIM, 2), jnp.arange(1, HEAD_DIM, 2)])
    return (jnp.arange(n_heads)[:, None] * HEAD_DIM + base[None, :]).reshape(-1)


def kernel(x, mem, mix_norm_g, w_in, gm_v_norm_g, gm_w_s, gm_b_s, q_norm_g, k_norm_g, branch_norm_g, w_out,
           xattn_norm_g, mem_norm_g, xattn_w_q, xattn_w_kv, xattn_w_o, ffn_norm_g, w_router, w_gate, w_up,
           w_down, final_norm_g):
    B, seq, D = x.shape
    L = w_in.shape[0]
    E = w_router.shape[-1]
    T = B * seq
    cap = EC_FACTOR * seq // E
    tm = min(1024, seq)
    tm_in = min(1024, seq)
    tq = min(256, seq)
    tk = min(512, seq)
    win = min(128, cap)
    sb_rows = min(512, seq)
    nblk = seq // TOK_BLOCK

    cosT, sinT = _rope_tables_T(seq)
    o_q = 2 * GM_WIDTH
    o_k = o_q + ATT_WIDTH
    o_v = o_k + KV_WIDTH
    cols = jnp.concatenate([jnp.arange(o_q), o_q + _head_perm(N_Q_HEADS), o_k + _head_perm(N_KV_HEADS),
                            jnp.arange(o_v, o_v + KV_WIDTH)])
    hp = _head_perm(1)

    mkT_all, mv_all = _mem_kv(mem, mem_norm_g.reshape(1, D), xattn_w_kv.astype(BF16))

    x2d = x.reshape(T, D)
    out = None
    for l in range(L):
        w_in_l = w_in[l][:, cols].astype(BF16)
        bs = jnp.broadcast_to(gm_b_s[l][:, :, None], (GM_GROUPS, CHUNK, LANES))
        qg = jnp.broadcast_to(q_norm_g[l][hp][:, None], (HEAD_DIM, tm_in))
        kg = jnp.broadcast_to(k_norm_g[l][hp][:, None], (HEAD_DIM, tm_in))
        gm, qhT, qlT, k8, vT = _mixer_in(
            x2d, mix_norm_g[l].reshape(1, D), w_in_l, gm_v_norm_g[l].reshape(1, GM_WIDTH),
            gm_w_s[l].astype(BF16), bs, qg, kg, branch_norm_g[l, 0].reshape(1, GM_WIDTH), cosT, sinT,
            seq=seq, tm=tm_in)
        atT = _attention(qhT, qlT, k8.reshape(B, seq, N_KV_HEADS * MXU_DEPTH), vT,
                         tq=tq, tk=tk, qb=min(256, tq))

        bg1 = jnp.broadcast_to(branch_norm_g[l, 1][:, None], (ATT_WIDTH, tm))
        wr_pad = jnp.pad(w_router[l].astype(BF16), ((0, 0), (0, LANES - E)))
        x2, h3, affT = _post(
            x2d, gm, atT, bg1, w_out[l].astype(BF16), xattn_norm_g[l].reshape(1, D),
            xattn_w_q[l].astype(BF16), mkT_all, mv_all, xattn_w_o[l].astype(BF16),
            ffn_norm_g[l].reshape(1, D), wr_pad, n_experts=E, layer=l, seq=seq, tm=tm)

        slots, offs = _topk(affT.reshape(E, B, seq // LANES, LANES), cap=cap)
        cnt = offs[:, :, ::TOK_BLOCK // LANES, 0]
        cnt = jnp.concatenate([cnt, jnp.full((B, E, 1), cap, jnp.int32)], axis=-1).reshape(-1)
        slots5 = slots.reshape(B, E, nblk, 1, TOK_BLOCK)
        gates5 = affT.reshape(E, B, nblk, 1, TOK_BLOCK)

        xs, gc = _gather(cnt, slots5, gates5, h3.reshape(B, seq, D), cap=cap, win=win)
        y = _ffn(xs, gc, w_gate, w_up, w_down, layer=l)
        out = _scatter(cnt, slots5, y, x2.reshape(B, seq, D), final_norm_g.reshape(1, D),
                       win=min(64, cap), sb_rows=sb_rows, final_norm=(l == L - 1))
        x2d = out.reshape(T, D)
    return out
```

```python
import functools
import math

import jax
import jax.numpy as jnp
from jax import lax
from jax.experimental import pallas as pl
from jax.experimental.pallas import tpu as pltpu

F32 = jnp.float32
BF16 = jnp.bfloat16
F8 = jnp.float8_e4m3fn
F8_MAX = 448.0

EPS = 1e-6
CHUNK = 128
GM_GROUPS = 4
GM_WIDTH = 512
HEAD_DIM = 64
N_Q_HEADS = 8
N_KV_HEADS = 2
Q_PER_KV = N_Q_HEADS // N_KV_HEADS
ATT_WIDTH = N_Q_HEADS * HEAD_DIM
KV_WIDTH = N_KV_HEADS * HEAD_DIM
ROPE_THETA = 10000.0
GRID_W = 64
X_HEADS = 4
N_EXPERTS = 16
EC_FACTOR = 2

LANES = 128
MXU_DEPTH = 256
PV_ROWS = 2 * HEAD_DIM
VMEM_LIMIT = 56 * 1024 * 1024

TOK_BLOCK = 256
LOG2E = 1.4426950408889634


def _cparams(sem, vmem=VMEM_LIMIT, **kw):
    return pltpu.CompilerParams(dimension_semantics=sem, vmem_limit_bytes=vmem, **kw)


def _rms_rows(x, g):
    ms = jnp.mean(x * x, axis=-1, keepdims=True)
    return x * lax.rsqrt(ms + EPS) * g


def _gelu_tanh(x):
    c = math.sqrt(2.0 / math.pi)
    return 0.5 * x * (1.0 + jnp.tanh(c * (x + 0.044715 * (x * x * x))))


def _norm_rope_T(xT, gcol, cosT, sinT, n_heads, scale):
    half = HEAD_DIM // 2
    outs = []
    for h in range(n_heads):
        blk = xT[h * HEAD_DIM:(h + 1) * HEAD_DIM, :]
        ms = jnp.mean(blk * blk, axis=0, keepdims=True)
        n = blk * lax.rsqrt(ms + EPS) * gcol
        e = n[:half, :]
        o = n[half:, :]
        re = (e * cosT - o * sinT) * scale
        ro = (e * sinT + o * cosT) * scale
        outs.append(jnp.concatenate([re, ro], axis=0))
    return outs


def _split_f8(x):
    x = jnp.clip(x, -F8_MAX, F8_MAX)
    hi = x.astype(F8).astype(F32)
    lo = (x - hi).astype(F8).astype(F32)
    return hi, lo


def _mixer_in_kernel(x_ref, g_ref, w_ref, vg_ref, ws_ref, bs_ref, qg_ref, kg_ref, bg0_ref,
                     cos_ref, sin_ref, gm_ref, qhT_ref, qlT_ref, k_ref, vT_ref, gm_scr, *, rows_per_group):
    tm = x_ref.shape[0]
    o_q = 2 * GM_WIDTH
    o_k = o_q + ATT_WIDTH
    o_v = o_k + KV_WIDTH
    groups = [slice(r, r + rows_per_group) for r in range(0, tm, rows_per_group)]

    projs = []
    for rs in groups:
        h = _rms_rows(x_ref[rs, :], g_ref[...]).astype(BF16)
        projs.append(jnp.dot(h, w_ref[...], preferred_element_type=F32))

    for rs, proj in zip(groups, projs):
        u = _gelu_tanh(proj[:, :GM_WIDTH])
        v = _gelu_tanh(proj[:, GM_WIDTH:2 * GM_WIDTH])
        vn = _rms_rows(v, vg_ref[...]).astype(BF16)
        for c in range(rows_per_group // CHUNK):
            cr = slice(c * CHUNK, (c + 1) * CHUNK)
            sr = slice(rs.start + c * CHUNK, rs.start + (c + 1) * CHUNK)
            for g in range(GM_GROUPS):
                cs = slice(g * LANES, (g + 1) * LANES)
                mixed = jnp.dot(ws_ref[g], vn[cr, cs], preferred_element_type=F32) + bs_ref[g]
                gm_scr[sr, cs] = u[cr, cs] * mixed
        gm_ref[rs, :] = _rms_rows(gm_scr[rs, :], bg0_ref[...]).astype(BF16)

        cosT = cos_ref[:, rs]
        sinT = sin_ref[:, rs]
        qT = proj[:, o_q:o_k].T
        q_heads = _norm_rope_T(qT, qg_ref[:, rs], cosT, sinT, N_Q_HEADS, (HEAD_DIM ** -0.5) * LOG2E)
        for hh in range(N_Q_HEADS):
            hi, lo = _split_f8(q_heads[hh])
            qhT_ref[hh * HEAD_DIM:(hh + 1) * HEAD_DIM, rs] = hi.astype(F8)
            qlT_ref[hh * HEAD_DIM:(hh + 1) * HEAD_DIM, rs] = lo.astype(F8)

        kT = proj[:, o_k:o_v].T
        k_heads = _norm_rope_T(kT, kg_ref[:, rs], cosT, sinT, N_KV_HEADS, 1.0)
        pieces = []
        for kh in k_heads:
            hi, lo = _split_f8(kh)
            pieces += [hi, lo, hi, jnp.zeros_like(hi)]
        k_ref[rs, :] = jnp.concatenate(pieces, axis=0).T.astype(F8)

        vT_ref[:, rs] = proj[:, o_v:].T.astype(BF16)


def _mixer_in(x2d, g, w_in, vg, ws, bs, qg, kg, bg0, cosT, sinT, *, seq, tm):
    T, D = x2d.shape
    B = T // seq
    npb = seq // tm
    in_w = w_in.shape[1]
    full = lambda shape: pl.BlockSpec(shape, lambda i: (0,) * len(shape))
    return pl.pallas_call(
        functools.partial(_mixer_in_kernel, rows_per_group=min(256, tm)),
        grid=(T // tm,),
        in_specs=[
            pl.BlockSpec((tm, D), lambda i: (i, 0)),
            full((1, D)),
            full((D, in_w)),
            full((1, GM_WIDTH)),
            full((GM_GROUPS, CHUNK, CHUNK)),
            full((GM_GROUPS, CHUNK, LANES)),
            full((HEAD_DIM, tm)),
            full((HEAD_DIM, tm)),
            full((1, GM_WIDTH)),
            pl.BlockSpec((HEAD_DIM // 2, tm), lambda i: (0, i % npb)),
            pl.BlockSpec((HEAD_DIM // 2, tm), lambda i: (0, i % npb)),
        ],
        out_specs=[
            pl.BlockSpec((tm, GM_WIDTH), lambda i: (i, 0)),
            pl.BlockSpec((None, ATT_WIDTH, tm), lambda i: (i // npb, 0, i % npb)),
            pl.BlockSpec((None, ATT_WIDTH, tm), lambda i: (i // npb, 0, i % npb)),
            pl.BlockSpec((tm, N_KV_HEADS * MXU_DEPTH), lambda i: (i, 0)),
            pl.BlockSpec((None, KV_WIDTH, tm), lambda i: (i // npb, 0, i % npb)),
        ],
        out_shape=[
            jax.ShapeDtypeStruct((T, GM_WIDTH), BF16),
            jax.ShapeDtypeStruct((B, ATT_WIDTH, seq), F8),
            jax.ShapeDtypeStruct((B, ATT_WIDTH, seq), F8),
            jax.ShapeDtypeStruct((T, N_KV_HEADS * MXU_DEPTH), F8),
            jax.ShapeDtypeStruct((B, KV_WIDTH, seq), BF16),
        ],
        scratch_shapes=[pltpu.VMEM((tm, GM_WIDTH), F32)],
        compiler_params=_cparams(("parallel",)),
        name="mixer_in",
    )(x2d, g, w_in, vg, ws, bs, qg, kg, bg0, cosT, sinT)


def _attn_kernel(qhT_ref, qlT_ref, k_ref, vT_ref, o_ref, vext_scr, qx_scr, m_scr, acc_scr, s_scr, cm_scr,
                 *, tq, tk, qb):
    seq = k_ref.shape[0]
    M = Q_PER_KV * tq
    nq = seq // tq
    nchunk = seq // tk

    vext_scr[0:HEAD_DIM, :] = vT_ref[...]
    row = lax.broadcasted_iota(jnp.int32, (PV_ROWS - HEAD_DIM, seq), 0)
    vext_scr[HEAD_DIM:, :] = jnp.where(row == 0, 1.0, 0.0).astype(BF16)

    def load_q(i, slot):
        cols = pl.ds(pl.multiple_of(i * tq, tq), tq)
        qh = jnp.concatenate([qhT_ref[g * HEAD_DIM:(g + 1) * HEAD_DIM, cols] for g in range(Q_PER_KV)], axis=1)
        ql = jnp.concatenate([qlT_ref[g * HEAD_DIM:(g + 1) * HEAD_DIM, cols] for g in range(Q_PER_KV)], axis=1)
        qx_scr[slot] = jnp.concatenate([qh, qh, ql, jnp.zeros_like(qh)], axis=0)

    def step(c, qslot, qk_chunk, do_pv):
        cur = c % 2
        nxt = 1 - cur
        if qk_chunk is not None:
            kn = k_ref[qk_chunk * tk:(qk_chunk + 1) * tk, :]
        if do_pv:
            ve = vext_scr[:, c * tk:(c + 1) * tk]
        for j in range(M // qb):
            cs = slice(j * qb, (j + 1) * qb)
            if qk_chunk is not None:
                s = jnp.dot(kn, qx_scr[qslot, :, cs], preferred_element_type=F32)
                s_scr[nxt, :, cs] = s
                cm_scr[nxt, :, cs] = jnp.max(s, axis=0, keepdims=True)
            if do_pv:
                m_old = m_scr[:, cs]
                m_new = jnp.maximum(m_old, cm_scr[cur, :, cs])
                p = jnp.exp2(s_scr[cur, :, cs] - m_new).astype(BF16)
                alpha = jnp.exp2(m_old - m_new)
                pv = jnp.dot(ve, p, preferred_element_type=F32)
                acc_scr[:, cs] = acc_scr[:, cs] * alpha + pv
                m_scr[:, cs] = m_new

    load_q(0, 0)
    step(-1, 0, 0, False)

    def qblock(i, carry):
        slot = i % 2
        m_scr[...] = jnp.full(m_scr.shape, -jnp.inf, F32)
        acc_scr[...] = jnp.zeros(acc_scr.shape, F32)
        for c in range(nchunk - 1):
            step(c, slot, c + 1, True)
        load_q(jnp.minimum(i + 1, nq - 1), 1 - slot)
        step(nchunk - 1, 1 - slot, 0, True)

        acc = acc_scr[...]
        inv = 1.0 / acc[HEAD_DIM:HEAD_DIM + 1, :]
        o = acc[0:HEAD_DIM, :] * inv
        cols = pl.ds(pl.multiple_of(i * tq, tq), tq)
        for g in range(Q_PER_KV):
            o_ref[g * HEAD_DIM:(g + 1) * HEAD_DIM, cols] = o[:, g * tq:(g + 1) * tq].astype(BF16)
        return carry

    lax.fori_loop(0, nq, qblock, 0)


def _attention(qhT, qlT, k, vT, *, tq, tk, qb):
    B, _, seq = qhT.shape
    gw = Q_PER_KV * HEAD_DIM
    assert (seq // tk) % 2 == 0
    return pl.pallas_call(
        functools.partial(_attn_kernel, tq=tq, tk=tk, qb=qb),
        grid=(B, N_KV_HEADS),
        in_specs=[
            pl.BlockSpec((None, gw, seq), lambda b, kh: (b, kh, 0)),
            pl.BlockSpec((None, gw, seq), lambda b, kh: (b, kh, 0)),
            pl.BlockSpec((None, seq, MXU_DEPTH), lambda b, kh: (b, 0, kh)),
            pl.BlockSpec((None, HEAD_DIM, seq), lambda b, kh: (b, kh, 0)),
        ],
        out_specs=pl.BlockSpec((None, gw, seq), lambda b, kh: (b, kh, 0)),
        out_shape=jax.ShapeDtypeStruct((B, ATT_WIDTH, seq), BF16),
        scratch_shapes=[
            pltpu.VMEM((PV_ROWS, seq), BF16),
            pltpu.VMEM((2, MXU_DEPTH, Q_PER_KV * tq), F8),
            pltpu.VMEM((1, Q_PER_KV * tq), F32),
            pltpu.VMEM((PV_ROWS, Q_PER_KV * tq), F32),
            pltpu.VMEM((2, tk, Q_PER_KV * tq), F32),
            pltpu.VMEM((2, 1, Q_PER_KV * tq), F32),
        ],
        compiler_params=_cparams(("parallel", "parallel")),
        name="attention",
    )(qhT, qlT, k, vT)


def _mem_kv_kernel(mem_ref, g_ref, w_ref, kT_ref, v_ref):
    D = mem_ref.shape[1]
    mn = _rms_rows(mem_ref[...], g_ref[...]).astype(BF16)
    kv = jnp.dot(mn, w_ref[...], preferred_element_type=F32)
    kT_ref[...] = kv[:, :D].T.astype(BF16)
    v_ref[...] = kv[:, D:].astype(BF16)


def _mem_kv(mem, g, w_kv):
    B, M, D = mem.shape
    L = w_kv.shape[0]
    return pl.pallas_call(
        _mem_kv_kernel,
        grid=(L, B),
        in_specs=[
            pl.BlockSpec((None, M, D), lambda l, b: (b, 0, 0)),
            pl.BlockSpec((1, D), lambda l, b: (0, 0)),
            pl.BlockSpec((None, D, 2 * D), lambda l, b: (l, 0, 0)),
        ],
        out_specs=[
            pl.BlockSpec((None, None, D, M), lambda l, b: (l, b, 0, 0)),
            pl.BlockSpec((None, None, M, D), lambda l, b: (l, b, 0, 0)),
        ],
        out_shape=[
            jax.ShapeDtypeStruct((L, B, D, M), BF16),
            jax.ShapeDtypeStruct((L, B, M, D), BF16),
        ],
        compiler_params=_cparams(("parallel", "parallel")),
        name="mem_kv",
    )(mem, g, w_kv)


def _post_kernel(x_ref, gm_ref, atT_ref, bg1_ref, wout_ref, xg_ref, wq_ref, mkT_ref, mv_ref, wo_ref,
                 fg_ref, wr_ref, x2_ref, h3_ref, affT_ref, *, rows_per_group):
    tm, D = x_ref.shape
    xhd = D // X_HEADS
    groups = [slice(r, r + rows_per_group) for r in range(0, tm, rows_per_group)]

    x1 = []
    for rs in groups:
        at = atT_ref[:, rs].astype(F32)
        ms = jnp.mean(at * at, axis=0, keepdims=True)
        atn = (at * lax.rsqrt(ms + EPS) * bg1_ref[:, rs]).astype(BF16)
        y = jnp.dot(gm_ref[rs, :], wout_ref[0:GM_WIDTH, :], preferred_element_type=F32)
        y = y + lax.dot_general(atn, wout_ref[GM_WIDTH:, :], (((0,), (0,)), ((), ())),
                                preferred_element_type=F32)
        x1.append(x_ref[rs, :] + y)

    q2 = []
    for x1g in x1:
        h2 = _rms_rows(x1g, xg_ref[...]).astype(BF16)
        q2.append((jnp.dot(h2, wq_ref[...], preferred_element_type=F32) * (xhd ** -0.5)).astype(BF16))

    scores = {}
    for hh in range(X_HEADS):
        cs = slice(hh * xhd, (hh + 1) * xhd)
        for gi in range(len(groups)):
            scores[hh, gi] = jnp.dot(q2[gi][:, cs], mkT_ref[cs, :], preferred_element_type=F32)
    outs = [[] for _ in groups]
    for hh in range(X_HEADS):
        cs = slice(hh * xhd, (hh + 1) * xhd)
        for gi in range(len(groups)):
            s = scores[hh, gi]
            s = s - jnp.max(s, axis=-1, keepdims=True)
            p = jnp.exp(s)
            p = (p / jnp.sum(p, axis=-1, keepdims=True)).astype(BF16)
            outs[gi].append(jnp.dot(p, mv_ref[:, cs], preferred_element_type=F32).astype(BF16))

    x2 = []
    for gi, rs in enumerate(groups):
        o2 = jnp.concatenate(outs[gi], axis=1)
        x2g = x1[gi] + jnp.dot(o2, wo_ref[...], preferred_element_type=F32)
        x2_ref[rs, :] = x2g
        x2.append(x2g)

    ne = affT_ref.shape[0]
    for gi, rs in enumerate(groups):
        h3 = _rms_rows(x2[gi], fg_ref[...]).astype(BF16)
        h3_ref[rs, :] = h3
        lg = jnp.dot(h3, wr_ref[...], preferred_element_type=F32).T[0:ne, :]
        lg = lg - jnp.max(lg, axis=0, keepdims=True)
        ex = jnp.exp(lg)
        affT_ref[:, rs] = ex / jnp.sum(ex, axis=0, keepdims=True)


def _post(x2d, gm, atT, bg1, w_out, xg, w_q, mkT, mv, w_o, fg, wr_pad, *, n_experts, layer, seq, tm):
    T, D = x2d.shape
    npb = seq // tm
    M = mv.shape[2]
    E = n_experts
    full = lambda shape: pl.BlockSpec(shape, lambda i: (0,) * len(shape))
    return pl.pallas_call(
        functools.partial(_post_kernel, rows_per_group=min(256, tm)),
        grid=(T // tm,),
        in_specs=[
            pl.BlockSpec((tm, D), lambda i: (i, 0)),
            pl.BlockSpec((tm, GM_WIDTH), lambda i: (i, 0)),
            pl.BlockSpec((None, ATT_WIDTH, tm), lambda i: (i // npb, 0, i % npb)),
            full((ATT_WIDTH, tm)),
            full((GM_WIDTH + ATT_WIDTH, D)),
            full((1, D)),
            full((D, D)),
            pl.BlockSpec((None, None, D, M), lambda i: (layer, i // npb, 0, 0)),
            pl.BlockSpec((None, None, M, D), lambda i: (layer, i // npb, 0, 0)),
            full((D, D)),
            full((1, D)),
            full((D, LANES)),
        ],
        out_specs=[
            pl.BlockSpec((tm, D), lambda i: (i, 0)),
            pl.BlockSpec((tm, D), lambda i: (i, 0)),
            pl.BlockSpec((E, tm), lambda i: (0, i)),
        ],
        out_shape=[
            jax.ShapeDtypeStruct((T, D), F32),
            jax.ShapeDtypeStruct((T, D), BF16),
            jax.ShapeDtypeStruct((E, T), F32),
        ],
        compiler_params=_cparams(("parallel",)),
        name="post",
    )(x2d, gm, atT, bg1, w_out, xg, w_q, mkT, mv, w_o, fg, wr_pad)


def _topk_kernel(aff_ref, slot_ref, off_ref, *, cap):
    E, R, L = aff_ref.shape
    a = aff_ref[...]
    bits = lax.bitcast_convert_type(a, jnp.int32)

    def count_ge(t):
        c = jnp.where(bits >= t, 1.0, 0.0)
        return jnp.sum(jnp.sum(c, axis=2, keepdims=True), axis=1, keepdims=True)

    def bis(_, carry):
        lo, hi = carry
        mid = lo + ((hi - lo) >> 1)
        ok = count_ge(mid) >= cap
        return jnp.where(ok, mid, lo), jnp.where(ok, hi, mid)

    lo0 = jnp.zeros((E, 1, 1), jnp.int32)
    hi0 = jnp.full((E, 1, 1), 0x3F800001, jnp.int32)
    thr, _ = lax.fori_loop(0, 31, bis, (lo0, hi0))

    kk = lax.broadcasted_iota(jnp.int32, (L, L), 0)
    nn = lax.broadcasted_iota(jnp.int32, (L, L), 1)
    upper = jnp.where(kk <= nn, 1.0, 0.0).astype(BF16)
    ones = jnp.ones((L, L), BF16)
    rr = lax.broadcasted_iota(jnp.int32, (E * R, E * R), 0)
    cc = lax.broadcasted_iota(jnp.int32, (E * R, E * R), 1)
    lower = jnp.where((rr // R == cc // R) & (cc < rr), 1.0, 0.0).astype(BF16)

    def prefix(xf):
        x2 = xf.reshape(E * R, L)
        xb = x2.astype(BF16)
        incl = jnp.dot(xb, upper, preferred_element_type=F32)
        tot = jnp.dot(xb, ones, preferred_element_type=F32).astype(BF16)
        rowoff = jnp.dot(lower, tot, preferred_element_type=F32)
        return (incl - x2 + rowoff).reshape(E, R, L), rowoff.reshape(E, R, L)

    gt = jnp.where(bits > thr, 1.0, 0.0)
    eq = jnp.where(bits == thr, 1.0, 0.0)
    n_gt = jnp.sum(jnp.sum(gt, axis=2, keepdims=True), axis=1, keepdims=True)
    need = cap - n_gt
    eq_rank, _ = prefix(eq)
    sel = gt + eq * jnp.where(eq_rank < need, 1.0, 0.0)
    pos, rowoff = prefix(sel)
    slot_ref[...] = jnp.where(sel > 0.0, pos, -1.0).astype(jnp.int32)
    off_ref[...] = rowoff.astype(jnp.int32)


def _topk(aff4, *, cap):
    E, B, R, L = aff4.shape
    return pl.pallas_call(
        functools.partial(_topk_kernel, cap=cap),
        grid=(B,),
        in_specs=[pl.BlockSpec((E, None, R, L), lambda b: (0, b, 0, 0))],
        out_specs=[
            pl.BlockSpec((None, E, R, L), lambda b: (b, 0, 0, 0)),
            pl.BlockSpec((None, E, R, L), lambda b: (b, 0, 0, 0)),
        ],
        out_shape=[
            jax.ShapeDtypeStruct((B, E, R, L), jnp.int32),
            jax.ShapeDtypeStruct((B, E, R, L), jnp.int32),
        ],
        compiler_params=_cparams(("parallel",)),
        name="topk",
    )(aff4)


def _gather_kernel(cnt_ref, slot_ref, gate_ref, h_ref, xs_ref, gc_ref, *, win):
    b = pl.program_id(0)
    eg = pl.program_id(1)
    group, cap, D = xs_ref.shape
    ne = pl.num_programs(1) * group
    nblk = slot_ref.shape[1]
    r_iota = lax.broadcasted_iota(jnp.int32, (win, TOK_BLOCK), 0)

    xs_ref[...] = jnp.zeros(xs_ref.shape, BF16)
    gc_ref[...] = jnp.zeros(gc_ref.shape, F32)

    def window(g, j):
        idx = (b * ne + eg * group + g) * (nblk + 1) + j
        s0 = cnt_ref[idx]
        s1 = cnt_ref[idx + 1]
        a0 = (s0 // 16) * 16
        return a0, s1

    def hits(g, j, lo, sa):
        srow = slot_ref[g, j]
        rel = jnp.where(srow >= lo, srow - sa, -1)
        return rel == r_iota

    def add_rows(g, j, sa, hit, part):
        rows = pl.ds(sa, win)
        xs_ref[g, rows, :] = (xs_ref[g, rows, :].astype(F32) + part).astype(BF16)
        gc_ref[g, rows, :] += jnp.sum(jnp.where(hit, gate_ref[g, j], 0.0), axis=1, keepdims=True)

    extra = jnp.int32(0)
    for j in range(nblk):
        hb = h_ref[j * TOK_BLOCK:(j + 1) * TOK_BLOCK, :]
        masks, starts = [], []
        for g in range(group):
            a0, s1 = window(g, j)
            sa = pl.multiple_of(jnp.minimum(a0, cap - win), 16)
            extra = jnp.maximum(extra, s1 - (a0 + win))
            masks.append(hits(g, j, a0, sa))
            starts.append(sa)
        lhs = jnp.concatenate([jnp.where(m, 1.0, 0.0).astype(BF16) for m in masks], axis=0)
        res = jnp.dot(lhs, hb, preferred_element_type=F32)
        for g in range(group):
            add_rows(g, j, starts[g], masks[g], res[g * win:(g + 1) * win, :])

    @pl.when(extra > 0)
    def _():
        def blk(j, carry):
            hb = h_ref[pl.ds(pl.multiple_of(j * TOK_BLOCK, TOK_BLOCK), TOK_BLOCK), :]
            for g in range(group):
                a0, s1 = window(g, j)
                nw = (s1 - a0 + win - 1) // win

                def wbody(w, c2, g=g, a0=a0):
                    lo = a0 + w * win
                    sa = pl.multiple_of(jnp.minimum(lo, cap - win), 16)
                    hit = hits(g, j, lo, sa)
                    part = jnp.dot(jnp.where(hit, 1.0, 0.0).astype(BF16), hb, preferred_element_type=F32)
                    add_rows(g, j, sa, hit, part)
                    return c2

                lax.fori_loop(1, nw, wbody, 0)
            return carry

        lax.fori_loop(0, nblk, blk, 0)


def _gather(cnt, slots5, gates5, h3d, *, cap, win):
    B, seq, D = h3d.shape
    E = slots5.shape[1]
    nblk = seq // TOK_BLOCK
    group = MXU_DEPTH // win
    assert E % group == 0 and cap % win == 0
    gs = pltpu.PrefetchScalarGridSpec(
        num_scalar_prefetch=1,
        grid=(B, E // group),
        in_specs=[
            pl.BlockSpec((None, group, nblk, 1, TOK_BLOCK), lambda b, e, c: (b, e, 0, 0, 0)),
            pl.BlockSpec((group, None, nblk, 1, TOK_BLOCK), lambda b, e, c: (e, b, 0, 0, 0)),
            pl.BlockSpec((None, seq, D), lambda b, e, c: (b, 0, 0), pipeline_mode=pl.Buffered(1)),
        ],
        out_specs=[
            pl.BlockSpec((None, group, cap, D), lambda b, e, c: (b, e, 0, 0)),
            pl.BlockSpec((None, group, cap, 1), lambda b, e, c: (b, e, 0, 0)),
        ],
    )
    return pl.pallas_call(
        functools.partial(_gather_kernel, win=win),
        grid_spec=gs,
        out_shape=[
            jax.ShapeDtypeStruct((B, E, cap, D), BF16),
            jax.ShapeDtypeStruct((B, E, cap, 1), F32),
        ],
        compiler_params=_cparams(("parallel", "arbitrary")),
        name="gather",
    )(cnt, slots5, gates5, h3d)


def _ffn_kernel(xs_ref, gc_ref, wg_ref, wu_ref, wd_ref, y_ref, wg_scr, wu_scr, wd_scr, *, f_chunk):
    @pl.when(pl.program_id(1) == 0)
    def _():
        wg_scr[...] = wg_ref[...].astype(BF16)
        wu_scr[...] = wu_ref[...].astype(BF16)
        wd_scr[...] = wd_ref[...].astype(BF16)

    xs = xs_ref[...]
    y = None
    for f0 in range(0, wg_scr.shape[1], f_chunk):
        fs = slice(f0, f0 + f_chunk)
        a = jnp.dot(xs, wg_scr[:, fs], preferred_element_type=F32)
        u = jnp.dot(xs, wu_scr[:, fs], preferred_element_type=F32)
        hmid = (a * jax.nn.sigmoid(a) * u).astype(BF16)
        part = jnp.dot(hmid, wd_scr[fs, :], preferred_element_type=F32)
        y = part if y is None else y + part
    y_ref[...] = (y * gc_ref[...]).astype(BF16)


def _ffn(xs, gc, w_gate, w_up, w_down, *, layer):
    B, E, cap, D = xs.shape
    Fd = w_gate.shape[-1]
    return pl.pallas_call(
        functools.partial(_ffn_kernel, f_chunk=min(512, Fd)),
        grid=(E, B),
        in_specs=[
            pl.BlockSpec((None, None, cap, D), lambda e, b: (b, e, 0, 0)),
            pl.BlockSpec((None, None, cap, 1), lambda e, b: (b, e, 0, 0)),
            pl.BlockSpec((None, None, D, Fd), lambda e, b: (layer, e, 0, 0)),
            pl.BlockSpec((None, None, D, Fd), lambda e, b: (layer, e, 0, 0)),
            pl.BlockSpec((None, None, Fd, D), lambda e, b: (layer, e, 0, 0)),
        ],
        out_specs=pl.BlockSpec((None, None, cap, D), lambda e, b: (b, e, 0, 0)),
        out_shape=jax.ShapeDtypeStruct((B, E, cap, D), BF16),
        scratch_shapes=[
            pltpu.VMEM((D, Fd), BF16),
            pltpu.VMEM((D, Fd), BF16),
            pltpu.VMEM((Fd, D), BF16),
        ],
        compiler_params=_cparams(("parallel", "arbitrary")),
        name="ffn",
    )(xs, gc, w_gate, w_up, w_down)


def _scatter_kernel(cnt_ref, slot_ref, y_ref, x_ref, fg_ref, o_ref, *, win, final_norm):
    b = pl.program_id(0)
    sb = pl.program_id(1)
    ne, cap, D = y_ref.shape
    nloc = slot_ref.shape[1]
    nblk = nloc * pl.num_programs(1)
    r_iota = lax.broadcasted_iota(jnp.int32, (win, TOK_BLOCK), 0)
    contract0 = (((0,), (0,)), ((), ()))

    def onehot_t(srow, lo, sa):
        rel = jnp.where(srow >= lo, srow - sa, -1)
        return jnp.where(rel == r_iota, 1.0, 0.0).astype(BF16)

    def window(e, j):
        idx = (b * ne + e) * (nblk + 1) + j
        s0 = cnt_ref[idx]
        s1 = cnt_ref[idx + 1]
        a0 = (s0 // 16) * 16
        sa = pl.multiple_of(jnp.minimum(a0, cap - win), 16)
        return a0, sa, s1 - (a0 + win)

    extra = jnp.int32(0)
    per_dot = MXU_DEPTH // win
    for t in range(nloc):
        j = sb * nloc + t
        rows = slice(t * TOK_BLOCK, (t + 1) * TOK_BLOCK)
        acc = x_ref[rows, :]
        for e0 in range(0, ne, per_dot):
            gs, ys = [], []
            for e in range(e0, e0 + per_dot):
                a0, sa, over = window(e, j)
                extra = jnp.maximum(extra, over)
                gs.append(onehot_t(slot_ref[e, t], a0, sa))
                ys.append(y_ref[e, pl.ds(sa, win), :])
            acc = acc + lax.dot_general(jnp.concatenate(gs, axis=0), jnp.concatenate(ys, axis=0),
                                        contract0, preferred_element_type=F32)
        o_ref[rows, :] = acc

    @pl.when(extra > 0)
    def _():
        for t in range(nloc):
            j = sb * nloc + t
            rows = slice(t * TOK_BLOCK, (t + 1) * TOK_BLOCK)
            for e in range(ne):
                idx = (b * ne + e) * (nblk + 1) + j
                s0 = cnt_ref[idx]
                s1 = cnt_ref[idx + 1]
                a0 = (s0 // 16) * 16
                nw = (s1 - a0 + win - 1) // win

                def wbody(w, carry):
                    lo = a0 + w * win
                    sa = pl.multiple_of(jnp.minimum(lo, cap - win), 16)
                    g = onehot_t(slot_ref[e, t], lo, sa)
                    o_ref[rows, :] += lax.dot_general(g, y_ref[e, pl.ds(sa, win), :], contract0,
                                                      preferred_element_type=F32)
                    return carry

                lax.fori_loop(1, nw, wbody, 0)

    if final_norm:
        o_ref[...] = _rms_rows(o_ref[...], fg_ref[...])


def _scatter(cnt, slots5, y, x3d, fg, *, win, sb_rows, final_norm):
    B, seq, D = x3d.shape
    E, cap = y.shape[1], y.shape[2]
    nloc = sb_rows // TOK_BLOCK
    assert MXU_DEPTH % win == 0 and E % (MXU_DEPTH // win) == 0
    gs = pltpu.PrefetchScalarGridSpec(
        num_scalar_prefetch=1,
        grid=(B, seq // sb_rows),
        in_specs=[
            pl.BlockSpec((None, E, nloc, 1, TOK_BLOCK), lambda b, s, c: (b, 0, s, 0, 0)),
            pl.BlockSpec((None, E, cap, D), lambda b, s, c: (b, 0, 0, 0), pipeline_mode=pl.Buffered(1)),
            pl.BlockSpec((None, sb_rows, D), lambda b, s, c: (b, s, 0)),
            pl.BlockSpec((1, D), lambda b, s, c: (0, 0)),
        ],
        out_specs=pl.BlockSpec((None, sb_rows, D), lambda b, s, c: (b, s, 0)),
    )
    return pl.pallas_call(
        functools.partial(_scatter_kernel, win=win, final_norm=final_norm),
        grid_spec=gs,
        out_shape=jax.ShapeDtypeStruct((B, seq, D), F32),
        compiler_params=_cparams(("parallel", "arbitrary")),
        name="scatter",
    )(cnt, slots5, y, x3d, fg)


def _rope_tables_T(seq):
    rows = seq // GRID_W
    row_id = jnp.repeat(jnp.arange(rows, dtype=F32), GRID_W)
    col_id = jnp.tile(jnp.arange(GRID_W, dtype=F32), rows)
    n_pairs = HEAD_DIM // 4
    freqs = jnp.exp(-math.log(ROPE_THETA) * jnp.arange(n_pairs, dtype=F32) / n_pairs)
    ang = jnp.concatenate([freqs[:, None] * row_id[None, :], freqs[:, None] * col_id[None, :]], axis=0)
    return jnp.cos(ang), jnp.sin(ang)


def _head_perm(n_heads):
    base = jnp.concatenate([jnp.arange(0, HEAD_DIM, 2), jnp.arange(1, HEAD_DIM, 2)])
    return (jnp.arange(n_heads)[:, None] * HEAD_DIM + base[None, :]).reshape(-1)


def kernel(x, mem, mix_norm_g, w_in, gm_v_norm_g, gm_w_s, gm_b_s, q_norm_g, k_norm_g, branch_norm_g, w_out,
           xattn_norm_g, mem_norm_g, xattn_w_q, xattn_w_kv, xattn_w_o, ffn_norm_g, w_router, w_gate, w_up,
           w_down, final_norm_g):
    B, seq, D = x.shape
    L = w_in.shape[0]
    E = w_router.shape[-1]
    T = B * seq
    cap = EC_FACTOR * seq // E
    tm = min(1024, seq)
    tm_in = min(1024, seq)
    tq = min(256, seq)
    tk = min(512, seq)
    win = min(64, cap)
    sb_rows = min(512, seq)
    nblk = seq // TOK_BLOCK

    cosT, sinT = _rope_tables_T(seq)
    o_q = 2 * GM_WIDTH
    o_k = o_q + ATT_WIDTH
    o_v = o_k + KV_WIDTH
    cols = jnp.concatenate([jnp.arange(o_q), o_q + _head_perm(N_Q_HEADS), o_k + _head_perm(N_KV_HEADS),
                            jnp.arange(o_v, o_v + KV_WIDTH)])
    hp = _head_perm(1)

    mkT_all, mv_all = _mem_kv(mem, mem_norm_g.reshape(1, D), xattn_w_kv.astype(BF16))

    x2d = x.reshape(T, D)
    out = None
    for l in range(L):
        w_in_l = w_in[l][:, cols].astype(BF16)
        bs = jnp.broadcast_to(gm_b_s[l][:, :, None], (GM_GROUPS, CHUNK, LANES))
        qg = jnp.broadcast_to(q_norm_g[l][hp][:, None], (HEAD_DIM, tm_in))
        kg = jnp.broadcast_to(k_norm_g[l][hp][:, None], (HEAD_DIM, tm_in))
        gm, qhT, qlT, k8, vT = _mixer_in(
            x2d, mix_norm_g[l].reshape(1, D), w_in_l, gm_v_norm_g[l].reshape(1, GM_WIDTH),
            gm_w_s[l].astype(BF16), bs, qg, kg, branch_norm_g[l, 0].reshape(1, GM_WIDTH), cosT, sinT,
            seq=seq, tm=tm_in)
        atT = _attention(qhT, qlT, k8.reshape(B, seq, N_KV_HEADS * MXU_DEPTH), vT,
                         tq=tq, tk=tk, qb=min(256, tq))

        bg1 = jnp.broadcast_to(branch_norm_g[l, 1][:, None], (ATT_WIDTH, tm))
        wr_pad = jnp.pad(w_router[l].astype(BF16), ((0, 0), (0, LANES - E)))
        x2, h3, affT = _post(
            x2d, gm, atT, bg1, w_out[l].astype(BF16), xattn_norm_g[l].reshape(1, D),
            xattn_w_q[l].astype(BF16), mkT_all, mv_all, xattn_w_o[l].astype(BF16),
            ffn_norm_g[l].reshape(1, D), wr_pad, n_experts=E, layer=l, seq=seq, tm=tm)

        slots, offs = _topk(affT.reshape(E, B, seq // LANES, LANES), cap=cap)
        cnt = offs[:, :, ::TOK_BLOCK // LANES, 0]
        cnt = jnp.concatenate([cnt, jnp.full((B, E, 1), cap, jnp.int32)], axis=-1).reshape(-1)
        slots5 = slots.reshape(B, E, nblk, 1, TOK_BLOCK)
        gates5 = affT.reshape(E, B, nblk, 1, TOK_BLOCK)

        xs, gc = _gather(cnt, slots5, gates5, h3.reshape(B, seq, D), cap=cap, win=win)
        y = _ffn(xs, gc, w_gate, w_up, w_down, layer=l)
        out = _scatter(cnt, slots5, y, x2.reshape(B, seq, D), final_norm_g.reshape(1, D),
                       win=win, sb_rows=sb_rows, final_norm=(l == L - 1))
        x2d = out.reshape(T, D)
    return out
```

```python
import functools
import math

import jax
import jax.numpy as jnp
from jax import lax
from jax.experimental import pallas as pl
from jax.experimental.pallas import tpu as pltpu

F32 = jnp.float32
BF16 = jnp.bfloat16
F8 = jnp.float8_e4m3fn
F8_MAX = 448.0

EPS = 1e-6
CHUNK = 128
GM_GROUPS = 4
GM_WIDTH = 512
HEAD_DIM = 64
N_Q_HEADS = 8
N_KV_HEADS = 2
Q_PER_KV = N_Q_HEADS // N_KV_HEADS
ATT_WIDTH = N_Q_HEADS * HEAD_DIM
KV_WIDTH = N_KV_HEADS * HEAD_DIM
ROPE_THETA = 10000.0
GRID_W = 64
X_HEADS = 4
N_EXPERTS = 16
EC_FACTOR = 2

LANES = 128
MXU_DEPTH = 256
PV_ROWS = 2 * HEAD_DIM
VMEM_LIMIT = 56 * 1024 * 1024

TOK_BLOCK = 256
LOG2E = 1.4426950408889634


def _cparams(sem, vmem=VMEM_LIMIT, **kw):
    return pltpu.CompilerParams(dimension_semantics=sem, vmem_limit_bytes=vmem, **kw)


def _rms_rows(x, g):
    ms = jnp.mean(x * x, axis=-1, keepdims=True)
    return x * lax.rsqrt(ms + EPS) * g


def _gelu_tanh(x):
    c = math.sqrt(2.0 / math.pi)
    return 0.5 * x * (1.0 + jnp.tanh(c * (x + 0.044715 * (x * x * x))))


def _norm_rope_T(xT, gcol, cosT, sinT, n_heads, scale):
    half = HEAD_DIM // 2
    outs = []
    for h in range(n_heads):
        blk = xT[h * HEAD_DIM:(h + 1) * HEAD_DIM, :]
        ms = jnp.mean(blk * blk, axis=0, keepdims=True)
        n = blk * lax.rsqrt(ms + EPS) * gcol
        e = n[:half, :]
        o = n[half:, :]
        re = (e * cosT - o * sinT) * scale
        ro = (e * sinT + o * cosT) * scale
        outs.append(jnp.concatenate([re, ro], axis=0))
    return outs


def _split_f8(x):
    x = jnp.clip(x, -F8_MAX, F8_MAX)
    hi = x.astype(F8).astype(F32)
    lo = (x - hi).astype(F8).astype(F32)
    return hi, lo


def _mixer_in_kernel(x_ref, g_ref, w_ref, vg_ref, ws_ref, bs_ref, qg_ref, kg_ref, bg0_ref,
                     cos_ref, sin_ref, gm_ref, qhT_ref, qlT_ref, k_ref, vT_ref, gm_scr, *, rows_per_group):
    tm = x_ref.shape[0]
    o_q = 2 * GM_WIDTH
    o_k = o_q + ATT_WIDTH
    o_v = o_k + KV_WIDTH
    groups = [slice(r, r + rows_per_group) for r in range(0, tm, rows_per_group)]

    projs = []
    for rs in groups:
        h = _rms_rows(x_ref[rs, :], g_ref[...]).astype(BF16)
        projs.append(jnp.dot(h, w_ref[...], preferred_element_type=F32))

    for rs, proj in zip(groups, projs):
        u = _gelu_tanh(proj[:, :GM_WIDTH])
        v = _gelu_tanh(proj[:, GM_WIDTH:2 * GM_WIDTH])
        vn = _rms_rows(v, vg_ref[...]).astype(BF16)
        for c in range(rows_per_group // CHUNK):
            cr = slice(c * CHUNK, (c + 1) * CHUNK)
            sr = slice(rs.start + c * CHUNK, rs.start + (c + 1) * CHUNK)
            for g in range(GM_GROUPS):
                cs = slice(g * LANES, (g + 1) * LANES)
                mixed = jnp.dot(ws_ref[g], vn[cr, cs], preferred_element_type=F32) + bs_ref[g]
                gm_scr[sr, cs] = u[cr, cs] * mixed
        gm_ref[rs, :] = _rms_rows(gm_scr[rs, :], bg0_ref[...]).astype(BF16)

        cosT = cos_ref[:, rs]
        sinT = sin_ref[:, rs]
        qT = proj[:, o_q:o_k].T
        q_heads = _norm_rope_T(qT, qg_ref[:, rs], cosT, sinT, N_Q_HEADS, (HEAD_DIM ** -0.5) * LOG2E)
        for hh in range(N_Q_HEADS):
            hi, lo = _split_f8(q_heads[hh])
            qhT_ref[hh * HEAD_DIM:(hh + 1) * HEAD_DIM, rs] = hi.astype(F8)
            qlT_ref[hh * HEAD_DIM:(hh + 1) * HEAD_DIM, rs] = lo.astype(F8)

        kT = proj[:, o_k:o_v].T
        k_heads = _norm_rope_T(kT, kg_ref[:, rs], cosT, sinT, N_KV_HEADS, 1.0)
        pieces = []
        for kh in k_heads:
            hi, lo = _split_f8(kh)
            pieces += [hi, lo, hi, jnp.zeros_like(hi)]
        k_ref[rs, :] = jnp.concatenate(pieces, axis=0).T.astype(F8)

        vT_ref[:, rs] = proj[:, o_v:].T.astype(BF16)


def _mixer_in(x2d, g, w_in, vg, ws, bs, qg, kg, bg0, cosT, sinT, *, seq, tm):
    T, D = x2d.shape
    B = T // seq
    npb = seq // tm
    in_w = w_in.shape[1]
    full = lambda shape: pl.BlockSpec(shape, lambda i: (0,) * len(shape))
    return pl.pallas_call(
        functools.partial(_mixer_in_kernel, rows_per_group=min(256, tm)),
        grid=(T // tm,),
        in_specs=[
            pl.BlockSpec((tm, D), lambda i: (i, 0)),
            full((1, D)),
            full((D, in_w)),
            full((1, GM_WIDTH)),
            full((GM_GROUPS, CHUNK, CHUNK)),
            full((GM_GROUPS, CHUNK, LANES)),
            full((HEAD_DIM, tm)),
            full((HEAD_DIM, tm)),
            full((1, GM_WIDTH)),
            pl.BlockSpec((HEAD_DIM // 2, tm), lambda i: (0, i % npb)),
            pl.BlockSpec((HEAD_DIM // 2, tm), lambda i: (0, i % npb)),
        ],
        out_specs=[
            pl.BlockSpec((tm, GM_WIDTH), lambda i: (i, 0)),
            pl.BlockSpec((None, ATT_WIDTH, tm), lambda i: (i // npb, 0, i % npb)),
            pl.BlockSpec((None, ATT_WIDTH, tm), lambda i: (i // npb, 0, i % npb)),
            pl.BlockSpec((tm, N_KV_HEADS * MXU_DEPTH), lambda i: (i, 0)),
            pl.BlockSpec((None, KV_WIDTH, tm), lambda i: (i // npb, 0, i % npb)),
        ],
        out_shape=[
            jax.ShapeDtypeStruct((T, GM_WIDTH), BF16),
            jax.ShapeDtypeStruct((B, ATT_WIDTH, seq), F8),
            jax.ShapeDtypeStruct((B, ATT_WIDTH, seq), F8),
            jax.ShapeDtypeStruct((T, N_KV_HEADS * MXU_DEPTH), F8),
            jax.ShapeDtypeStruct((B, KV_WIDTH, seq), BF16),
        ],
        scratch_shapes=[pltpu.VMEM((tm, GM_WIDTH), F32)],
        compiler_params=_cparams(("parallel",)),
        name="mixer_in",
    )(x2d, g, w_in, vg, ws, bs, qg, kg, bg0, cosT, sinT)


def _attn_kernel(qhT_ref, qlT_ref, k_ref, vT_ref, o_ref, vext_scr, qx_scr, m_scr, acc_scr, s_scr, cm_scr,
                 *, tq, tk, qb):
    seq = k_ref.shape[0]
    M = Q_PER_KV * tq
    nq = seq // tq
    nchunk = seq // tk

    vext_scr[0:HEAD_DIM, :] = vT_ref[...]
    row = lax.broadcasted_iota(jnp.int32, (PV_ROWS - HEAD_DIM, seq), 0)
    vext_scr[HEAD_DIM:, :] = jnp.where(row == 0, 1.0, 0.0).astype(BF16)

    def load_q(i, slot):
        cols = pl.ds(pl.multiple_of(i * tq, tq), tq)
        qh = jnp.concatenate([qhT_ref[g * HEAD_DIM:(g + 1) * HEAD_DIM, cols] for g in range(Q_PER_KV)], axis=1)
        ql = jnp.concatenate([qlT_ref[g * HEAD_DIM:(g + 1) * HEAD_DIM, cols] for g in range(Q_PER_KV)], axis=1)
        qx_scr[slot] = jnp.concatenate([qh, qh, ql, jnp.zeros_like(qh)], axis=0)

    def step(c, qslot, qk_chunk, do_pv):
        cur = c % 2
        nxt = 1 - cur
        if qk_chunk is not None:
            kn = k_ref[qk_chunk * tk:(qk_chunk + 1) * tk, :]
        if do_pv:
            ve = vext_scr[:, c * tk:(c + 1) * tk]
        for j in range(M // qb):
            cs = slice(j * qb, (j + 1) * qb)
            if qk_chunk is not None:
                s = jnp.dot(kn, qx_scr[qslot, :, cs], preferred_element_type=F32)
                s_scr[nxt, :, cs] = s
                cm_scr[nxt, :, cs] = jnp.max(s, axis=0, keepdims=True)
            if do_pv:
                m_old = m_scr[:, cs]
                m_new = jnp.maximum(m_old, cm_scr[cur, :, cs])
                p = jnp.exp2(s_scr[cur, :, cs] - m_new).astype(BF16)
                alpha = jnp.exp2(m_old - m_new)
                pv = jnp.dot(ve, p, preferred_element_type=F32)
                acc_scr[:, cs] = acc_scr[:, cs] * alpha + pv
                m_scr[:, cs] = m_new

    load_q(0, 0)
    step(-1, 0, 0, False)

    def qblock(i, carry):
        slot = i % 2
        m_scr[...] = jnp.full(m_scr.shape, -jnp.inf, F32)
        acc_scr[...] = jnp.zeros(acc_scr.shape, F32)
        for c in range(nchunk - 1):
            step(c, slot, c + 1, True)
        load_q(jnp.minimum(i + 1, nq - 1), 1 - slot)
        step(nchunk - 1, 1 - slot, 0, True)

        acc = acc_scr[...]
        inv = 1.0 / acc[HEAD_DIM:HEAD_DIM + 1, :]
        o = acc[0:HEAD_DIM, :] * inv
        cols = pl.ds(pl.multiple_of(i * tq, tq), tq)
        for g in range(Q_PER_KV):
            o_ref[g * HEAD_DIM:(g + 1) * HEAD_DIM, cols] = o[:, g * tq:(g + 1) * tq].astype(BF16)
        return carry

    lax.fori_loop(0, nq, qblock, 0)


def _attention(qhT, qlT, k, vT, *, tq, tk, qb):
    B, _, seq = qhT.shape
    gw = Q_PER_KV * HEAD_DIM
    assert (seq // tk) % 2 == 0
    return pl.pallas_call(
        functools.partial(_attn_kernel, tq=tq, tk=tk, qb=qb),
        grid=(B, N_KV_HEADS),
        in_specs=[
            pl.BlockSpec((None, gw, seq), lambda b, kh: (b, kh, 0)),
            pl.BlockSpec((None, gw, seq), lambda b, kh: (b, kh, 0)),
            pl.BlockSpec((None, seq, MXU_DEPTH), lambda b, kh: (b, 0, kh)),
            pl.BlockSpec((None, HEAD_DIM, seq), lambda b, kh: (b, kh, 0)),
        ],
        out_specs=pl.BlockSpec((None, gw, seq), lambda b, kh: (b, kh, 0)),
        out_shape=jax.ShapeDtypeStruct((B, ATT_WIDTH, seq), BF16),
        scratch_shapes=[
            pltpu.VMEM((PV_ROWS, seq), BF16),
            pltpu.VMEM((2, MXU_DEPTH, Q_PER_KV * tq), F8),
            pltpu.VMEM((1, Q_PER_KV * tq), F32),
            pltpu.VMEM((PV_ROWS, Q_PER_KV * tq), F32),
            pltpu.VMEM((2, tk, Q_PER_KV * tq), F32),
            pltpu.VMEM((2, 1, Q_PER_KV * tq), F32),
        ],
        compiler_params=_cparams(("parallel", "parallel")),
        name="attention",
    )(qhT, qlT, k, vT)


def _mem_kv_kernel(mem_ref, g_ref, w_ref, kT_ref, v_ref):
    D = mem_ref.shape[1]
    mn = _rms_rows(mem_ref[...], g_ref[...]).astype(BF16)
    kv = jnp.dot(mn, w_ref[...], preferred_element_type=F32)
    kT_ref[...] = kv[:, :D].T.astype(BF16)
    v_ref[...] = kv[:, D:].astype(BF16)


def _mem_kv(mem, g, w_kv):
    B, M, D = mem.shape
    L = w_kv.shape[0]
    return pl.pallas_call(
        _mem_kv_kernel,
        grid=(L, B),
        in_specs=[
            pl.BlockSpec((None, M, D), lambda l, b: (b, 0, 0)),
            pl.BlockSpec((1, D), lambda l, b: (0, 0)),
            pl.BlockSpec((None, D, 2 * D), lambda l, b: (l, 0, 0)),
        ],
        out_specs=[
            pl.BlockSpec((None, None, D, M), lambda l, b: (l, b, 0, 0)),
            pl.BlockSpec((None, None, M, D), lambda l, b: (l, b, 0, 0)),
        ],
        out_shape=[
            jax.ShapeDtypeStruct((L, B, D, M), BF16),
            jax.ShapeDtypeStruct((L, B, M, D), BF16),
        ],
        compiler_params=_cparams(("parallel", "parallel")),
        name="mem_kv",
    )(mem, g, w_kv)


def _post_kernel(x_ref, gm_ref, atT_ref, bg1_ref, wout_ref, xg_ref, wq_ref, mkT_ref, mv_ref, wo_ref,
                 fg_ref, wr_ref, x2_ref, h3_ref, affT_ref, *, rows_per_group):
    tm, D = x_ref.shape
    xhd = D // X_HEADS
    groups = [slice(r, r + rows_per_group) for r in range(0, tm, rows_per_group)]

    x1 = []
    for rs in groups:
        at = atT_ref[:, rs].astype(F32)
        ms = jnp.mean(at * at, axis=0, keepdims=True)
        atn = (at * lax.rsqrt(ms + EPS) * bg1_ref[:, rs]).astype(BF16)
        y = jnp.dot(gm_ref[rs, :], wout_ref[0:GM_WIDTH, :], preferred_element_type=F32)
        y = y + lax.dot_general(atn, wout_ref[GM_WIDTH:, :], (((0,), (0,)), ((), ())),
                                preferred_element_type=F32)
        x1.append(x_ref[rs, :] + y)

    q2 = []
    for x1g in x1:
        h2 = _rms_rows(x1g, xg_ref[...]).astype(BF16)
        q2.append((jnp.dot(h2, wq_ref[...], preferred_element_type=F32) * (xhd ** -0.5)).astype(BF16))

    scores = {}
    for hh in range(X_HEADS):
        cs = slice(hh * xhd, (hh + 1) * xhd)
        for gi in range(len(groups)):
            scores[hh, gi] = jnp.dot(q2[gi][:, cs], mkT_ref[cs, :], preferred_element_type=F32)
    outs = [[] for _ in groups]
    for hh in range(X_HEADS):
        cs = slice(hh * xhd, (hh + 1) * xhd)
        for gi in range(len(groups)):
            s = scores[hh, gi]
            s = s - jnp.max(s, axis=-1, keepdims=True)
            p = jnp.exp(s)
            p = (p / jnp.sum(p, axis=-1, keepdims=True)).astype(BF16)
            outs[gi].append(jnp.dot(p, mv_ref[:, cs], preferred_element_type=F32).astype(BF16))

    x2 = []
    for gi, rs in enumerate(groups):
        o2 = jnp.concatenate(outs[gi], axis=1)
        x2g = x1[gi] + jnp.dot(o2, wo_ref[...], preferred_element_type=F32)
        x2_ref[rs, :] = x2g
        x2.append(x2g)

    ne = affT_ref.shape[0]
    for gi, rs in enumerate(groups):
        h3 = _rms_rows(x2[gi], fg_ref[...]).astype(BF16)
        h3_ref[rs, :] = h3
        lg = jnp.dot(h3, wr_ref[...], preferred_element_type=F32).T[0:ne, :]
        lg = lg - jnp.max(lg, axis=0, keepdims=True)
        ex = jnp.exp(lg)
        affT_ref[:, rs] = ex / jnp.sum(ex, axis=0, keepdims=True)


def _post(x2d, gm, atT, bg1, w_out, xg, w_q, mkT, mv, w_o, fg, wr_pad, *, n_experts, layer, seq, tm):
    T, D = x2d.shape
    npb = seq // tm
    M = mv.shape[2]
    E = n_experts
    full = lambda shape: pl.BlockSpec(shape, lambda i: (0,) * len(shape))
    return pl.pallas_call(
        functools.partial(_post_kernel, rows_per_group=min(256, tm)),
        grid=(T // tm,),
        in_specs=[
            pl.BlockSpec((tm, D), lambda i: (i, 0)),
            pl.BlockSpec((tm, GM_WIDTH), lambda i: (i, 0)),
            pl.BlockSpec((None, ATT_WIDTH, tm), lambda i: (i // npb, 0, i % npb)),
            full((ATT_WIDTH, tm)),
            full((GM_WIDTH + ATT_WIDTH, D)),
            full((1, D)),
            full((D, D)),
            pl.BlockSpec((None, None, D, M), lambda i: (layer, i // npb, 0, 0)),
            pl.BlockSpec((None, None, M, D), lambda i: (layer, i // npb, 0, 0)),
            full((D, D)),
            full((1, D)),
            full((D, LANES)),
        ],
        out_specs=[
            pl.BlockSpec((tm, D), lambda i: (i, 0)),
            pl.BlockSpec((tm, D), lambda i: (i, 0)),
            pl.BlockSpec((E, tm), lambda i: (0, i)),
        ],
        out_shape=[
            jax.ShapeDtypeStruct((T, D), F32),
            jax.ShapeDtypeStruct((T, D), BF16),
            jax.ShapeDtypeStruct((E, T), F32),
        ],
        compiler_params=_cparams(("parallel",)),
        name="post",
    )(x2d, gm, atT, bg1, w_out, xg, w_q, mkT, mv, w_o, fg, wr_pad)


def _topk_kernel(aff_ref, slot_ref, off_ref, *, cap):
    E, R, L = aff_ref.shape
    a = aff_ref[...]
    bits = lax.bitcast_convert_type(a, jnp.int32)

    def count_ge(t):
        c = jnp.where(bits >= t, 1.0, 0.0)
        return jnp.sum(jnp.sum(c, axis=2, keepdims=True), axis=1, keepdims=True)

    def bis(_, carry):
        lo, hi = carry
        mid = lo + ((hi - lo) >> 1)
        ok = count_ge(mid) >= cap
        return jnp.where(ok, mid, lo), jnp.where(ok, hi, mid)

    lo0 = jnp.zeros((E, 1, 1), jnp.int32)
    hi0 = jnp.full((E, 1, 1), 0x3F800001, jnp.int32)
    thr, _ = lax.fori_loop(0, 31, bis, (lo0, hi0))

    kk = lax.broadcasted_iota(jnp.int32, (L, L), 0)
    nn = lax.broadcasted_iota(jnp.int32, (L, L), 1)
    upper = jnp.where(kk <= nn, 1.0, 0.0).astype(BF16)
    ones = jnp.ones((L, L), BF16)
    rr = lax.broadcasted_iota(jnp.int32, (E * R, E * R), 0)
    cc = lax.broadcasted_iota(jnp.int32, (E * R, E * R), 1)
    lower = jnp.where((rr // R == cc // R) & (cc < rr), 1.0, 0.0).astype(BF16)

    def prefix(xf):
        x2 = xf.reshape(E * R, L)
        xb = x2.astype(BF16)
        incl = jnp.dot(xb, upper, preferred_element_type=F32)
        tot = jnp.dot(xb, ones, preferred_element_type=F32).astype(BF16)
        rowoff = jnp.dot(lower, tot, preferred_element_type=F32)
        return (incl - x2 + rowoff).reshape(E, R, L), rowoff.reshape(E, R, L)

    gt = jnp.where(bits > thr, 1.0, 0.0)
    eq = jnp.where(bits == thr, 1.0, 0.0)
    n_gt = jnp.sum(jnp.sum(gt, axis=2, keepdims=True), axis=1, keepdims=True)
    need = cap - n_gt
    eq_rank, _ = prefix(eq)
    sel = gt + eq * jnp.where(eq_rank < need, 1.0, 0.0)
    pos, rowoff = prefix(sel)
    slot_ref[...] = jnp.where(sel > 0.0, pos, -1.0).astype(jnp.int32)
    off_ref[...] = rowoff.astype(jnp.int32)


def _topk(aff4, *, cap):
    E, B, R, L = aff4.shape
    return pl.pallas_call(
        functools.partial(_topk_kernel, cap=cap),
        grid=(B,),
        in_specs=[pl.BlockSpec((E, None, R, L), lambda b: (0, b, 0, 0))],
        out_specs=[
            pl.BlockSpec((None, E, R, L), lambda b: (b, 0, 0, 0)),
            pl.BlockSpec((None, E, R, L), lambda b: (b, 0, 0, 0)),
        ],
        out_shape=[
            jax.ShapeDtypeStruct((B, E, R, L), jnp.int32),
            jax.ShapeDtypeStruct((B, E, R, L), jnp.int32),
        ],
        compiler_params=_cparams(("parallel",)),
        name="topk",
    )(aff4)


def _gather_kernel(cnt_ref, slot_ref, gate_ref, h_ref, xs_ref, gc_ref, *, win):
    b = pl.program_id(0)
    eg = pl.program_id(1)
    group, cap, D = xs_ref.shape
    ne = pl.num_programs(1) * group
    nblk = slot_ref.shape[1]
    r_iota = lax.broadcasted_iota(jnp.int32, (win, TOK_BLOCK), 0)

    xs_ref[...] = jnp.zeros(xs_ref.shape, BF16)
    gc_ref[...] = jnp.zeros(gc_ref.shape, F32)

    def window(g, j):
        idx = (b * ne + eg * group + g) * (nblk + 1) + j
        s0 = cnt_ref[idx]
        s1 = cnt_ref[idx + 1]
        a0 = (s0 // 16) * 16
        return a0, s1

    def hits(g, j, lo, sa):
        srow = slot_ref[g, j]
        rel = jnp.where(srow >= lo, srow - sa, -1)
        return rel == r_iota

    def add_rows(g, j, sa, hit, part):
        rows = pl.ds(sa, win)
        xs_ref[g, rows, :] = xs_ref[g, rows, :] + part.astype(BF16)
        gc_ref[g, rows, :] += jnp.sum(jnp.where(hit, gate_ref[g, j], 0.0), axis=1, keepdims=True)

    extra = jnp.int32(0)
    for j in range(nblk):
        hb = h_ref[j * TOK_BLOCK:(j + 1) * TOK_BLOCK, :]
        masks, starts = [], []
        for g in range(group):
            a0, s1 = window(g, j)
            sa = pl.multiple_of(jnp.minimum(a0, cap - win), 16)
            extra = jnp.maximum(extra, s1 - (a0 + win))
            masks.append(hits(g, j, a0, sa))
            starts.append(sa)
        lhs = jnp.concatenate([jnp.where(m, 1.0, 0.0).astype(BF16) for m in masks], axis=0)
        res = jnp.dot(lhs, hb, preferred_element_type=F32)
        for g in range(group):
            add_rows(g, j, starts[g], masks[g], res[g * win:(g + 1) * win, :])

    @pl.when(extra > 0)
    def _():
        def blk(j, carry):
            hb = h_ref[pl.ds(pl.multiple_of(j * TOK_BLOCK, TOK_BLOCK), TOK_BLOCK), :]
            for g in range(group):
                a0, s1 = window(g, j)
                nw = (s1 - a0 + win - 1) // win

                def wbody(w, c2, g=g, a0=a0):
                    lo = a0 + w * win
                    sa = pl.multiple_of(jnp.minimum(lo, cap - win), 16)
                    hit = hits(g, j, lo, sa)
                    part = jnp.dot(jnp.where(hit, 1.0, 0.0).astype(BF16), hb, preferred_element_type=F32)
                    add_rows(g, j, sa, hit, part)
                    return c2

                lax.fori_loop(1, nw, wbody, 0)
            return carry

        lax.fori_loop(0, nblk, blk, 0)


def _gather(cnt, slots5, gates5, h3d, *, cap, win):
    B, seq, D = h3d.shape
    E = slots5.shape[1]
    nblk = seq // TOK_BLOCK
    group = MXU_DEPTH // win
    assert E % group == 0 and cap % win == 0
    gs = pltpu.PrefetchScalarGridSpec(
        num_scalar_prefetch=1,
        grid=(B, E // group),
        in_specs=[
            pl.BlockSpec((None, group, nblk, 1, TOK_BLOCK), lambda b, e, c: (b, e, 0, 0, 0)),
            pl.BlockSpec((group, None, nblk, 1, TOK_BLOCK), lambda b, e, c: (e, b, 0, 0, 0)),
            pl.BlockSpec((None, seq, D), lambda b, e, c: (b, 0, 0), pipeline_mode=pl.Buffered(1)),
        ],
        out_specs=[
            pl.BlockSpec((None, group, cap, D), lambda b, e, c: (b, e, 0, 0)),
            pl.BlockSpec((None, group, cap, 1), lambda b, e, c: (b, e, 0, 0)),
        ],
    )
    return pl.pallas_call(
        functools.partial(_gather_kernel, win=win),
        grid_spec=gs,
        out_shape=[
            jax.ShapeDtypeStruct((B, E, cap, D), BF16),
            jax.ShapeDtypeStruct((B, E, cap, 1), F32),
        ],
        compiler_params=_cparams(("parallel", "arbitrary")),
        name="gather",
    )(cnt, slots5, gates5, h3d)


def _ffn_kernel(xs_ref, gc_ref, wg_ref, wu_ref, wd_ref, y_ref, wg_scr, wu_scr, wd_scr, *, f_chunk):
    @pl.when(pl.program_id(1) == 0)
    def _():
        wg_scr[...] = wg_ref[...].astype(BF16)
        wu_scr[...] = wu_ref[...].astype(BF16)
        wd_scr[...] = wd_ref[...].astype(BF16)

    xs = xs_ref[...]
    y = None
    for f0 in range(0, wg_scr.shape[1], f_chunk):
        fs = slice(f0, f0 + f_chunk)
        a = jnp.dot(xs, wg_scr[:, fs], preferred_element_type=F32)
        u = jnp.dot(xs, wu_scr[:, fs], preferred_element_type=F32)
        hmid = (a * jax.nn.sigmoid(a) * u).astype(BF16)
        part = jnp.dot(hmid, wd_scr[fs, :], preferred_element_type=F32)
        y = part if y is None else y + part
    y_ref[...] = (y * gc_ref[...]).astype(BF16)


def _ffn(xs, gc, w_gate, w_up, w_down, *, layer):
    B, E, cap, D = xs.shape
    Fd = w_gate.shape[-1]
    return pl.pallas_call(
        functools.partial(_ffn_kernel, f_chunk=min(512, Fd)),
        grid=(E, B),
        in_specs=[
            pl.BlockSpec((None, None, cap, D), lambda e, b: (b, e, 0, 0)),
            pl.BlockSpec((None, None, cap, 1), lambda e, b: (b, e, 0, 0)),
            pl.BlockSpec((None, None, D, Fd), lambda e, b: (layer, e, 0, 0)),
            pl.BlockSpec((None, None, D, Fd), lambda e, b: (layer, e, 0, 0)),
            pl.BlockSpec((None, None, Fd, D), lambda e, b: (layer, e, 0, 0)),
        ],
        out_specs=pl.BlockSpec((None, None, cap, D), lambda e, b: (b, e, 0, 0)),
        out_shape=jax.ShapeDtypeStruct((B, E, cap, D), BF16),
        scratch_shapes=[
            pltpu.VMEM((D, Fd), BF16),
            pltpu.VMEM((D, Fd), BF16),
            pltpu.VMEM((Fd, D), BF16),
        ],
        compiler_params=_cparams(("parallel", "arbitrary")),
        name="ffn",
    )(xs, gc, w_gate, w_up, w_down)


def _scatter_kernel(cnt_ref, slot_ref, y_ref, x_ref, fg_ref, o_ref, *, win, final_norm):
    b = pl.program_id(0)
    sb = pl.program_id(1)
    ne, cap, D = y_ref.shape
    nloc = slot_ref.shape[1]
    nblk = nloc * pl.num_programs(1)
    r_iota = lax.broadcasted_iota(jnp.int32, (win, TOK_BLOCK), 0)
    contract0 = (((0,), (0,)), ((), ()))

    def onehot_t(srow, lo, sa):
        rel = jnp.where(srow >= lo, srow - sa, -1)
        return jnp.where(rel == r_iota, 1.0, 0.0).astype(BF16)

    def window(e, j):
        idx = (b * ne + e) * (nblk + 1) + j
        s0 = cnt_ref[idx]
        s1 = cnt_ref[idx + 1]
        a0 = (s0 // 16) * 16
        sa = pl.multiple_of(jnp.minimum(a0, cap - win), 16)
        return a0, sa, s1 - (a0 + win)

    extra = jnp.int32(0)
    per_dot = MXU_DEPTH // win
    for t in range(nloc):
        j = sb * nloc + t
        rows = slice(t * TOK_BLOCK, (t + 1) * TOK_BLOCK)
        acc = x_ref[rows, :]
        for e0 in range(0, ne, per_dot):
            gs, ys = [], []
            for e in range(e0, e0 + per_dot):
                a0, sa, over = window(e, j)
                extra = jnp.maximum(extra, over)
                gs.append(onehot_t(slot_ref[e, t], a0, sa))
                ys.append(y_ref[e, pl.ds(sa, win), :])
            acc = acc + lax.dot_general(jnp.concatenate(gs, axis=0), jnp.concatenate(ys, axis=0),
                                        contract0, preferred_element_type=F32)
        o_ref[rows, :] = acc

    @pl.when(extra > 0)
    def _():
        for t in range(nloc):
            j = sb * nloc + t
            rows = slice(t * TOK_BLOCK, (t + 1) * TOK_BLOCK)
            for e in range(ne):
                idx = (b * ne + e) * (nblk + 1) + j
                s0 = cnt_ref[idx]
                s1 = cnt_ref[idx + 1]
                a0 = (s0 // 16) * 16
                nw = (s1 - a0 + win - 1) // win

                def wbody(w, carry):
                    lo = a0 + w * win
                    sa = pl.multiple_of(jnp.minimum(lo, cap - win), 16)
                    g = onehot_t(slot_ref[e, t], lo, sa)
                    o_ref[rows, :] += lax.dot_general(g, y_ref[e, pl.ds(sa, win), :], contract0,
                                                      preferred_element_type=F32)
                    return carry

                lax.fori_loop(1, nw, wbody, 0)

    if final_norm:
        o_ref[...] = _rms_rows(o_ref[...], fg_ref[...])


def _scatter(cnt, slots5, y, x3d, fg, *, win, sb_rows, final_norm):
    B, seq, D = x3d.shape
    E, cap = y.shape[1], y.shape[2]
    nloc = sb_rows // TOK_BLOCK
    assert MXU_DEPTH % win == 0 and E % (MXU_DEPTH // win) == 0
    gs = pltpu.PrefetchScalarGridSpec(
        num_scalar_prefetch=1,
        grid=(B, seq // sb_rows),
        in_specs=[
            pl.BlockSpec((None, E, nloc, 1, TOK_BLOCK), lambda b, s, c: (b, 0, s, 0, 0)),
            pl.BlockSpec((None, E, cap, D), lambda b, s, c: (b, 0, 0, 0), pipeline_mode=pl.Buffered(1)),
            pl.BlockSpec((None, sb_rows, D), lambda b, s, c: (b, s, 0)),
            pl.BlockSpec((1, D), lambda b, s, c: (0, 0)),
        ],
        out_specs=pl.BlockSpec((None, sb_rows, D), lambda b, s, c: (b, s, 0)),
    )
    return pl.pallas_call(
        functools.partial(_scatter_kernel, win=win, final_norm=final_norm),
        grid_spec=gs,
        out_shape=jax.ShapeDtypeStruct((B, seq, D), F32),
        compiler_params=_cparams(("parallel", "arbitrary")),
        name="scatter",
    )(cnt, slots5, y, x3d, fg)


def _rope_tables_T(seq):
    rows = seq // GRID_W
    row_id = jnp.repeat(jnp.arange(rows, dtype=F32), GRID_W)
    col_id = jnp.tile(jnp.arange(GRID_W, dtype=F32), rows)
    n_pairs = HEAD_DIM // 4
    freqs = jnp.exp(-math.log(ROPE_THETA) * jnp.arange(n_pairs, dtype=F32) / n_pairs)
    ang = jnp.concatenate([freqs[:, None] * row_id[None, :], freqs[:, None] * col_id[None, :]], axis=0)
    return jnp.cos(ang), jnp.sin(ang)


def _head_perm(n_heads):
    base = jnp.concatenate([jnp.arange(0, HEAD_DIM, 2), jnp.arange(1, HEAD_DIM, 2)])
    return (jnp.arange(n_heads)[:, None] * HEAD_DIM + base[None, :]).reshape(-1)


def kernel(x, mem, mix_norm_g, w_in, gm_v_norm_g, gm_w_s, gm_b_s, q_norm_g, k_norm_g, branch_norm_g, w_out,
           xattn_norm_g, mem_norm_g, xattn_w_q, xattn_w_kv, xattn_w_o, ffn_norm_g, w_router, w_gate, w_up,
           w_down, final_norm_g):
    B, seq, D = x.shape
    L = w_in.shape[0]
    E = w_router.shape[-1]
    T = B * seq
    cap = EC_FACTOR * seq // E
    tm = min(1024, seq)
    tm_in = min(1024, seq)
    tq = min(256, seq)
    tk = min(512, seq)
    win = min(64, cap)
    sb_rows = min(512, seq)
    nblk = seq // TOK_BLOCK

    cosT, sinT = _rope_tables_T(seq)
    o_q = 2 * GM_WIDTH
    o_k = o_q + ATT_WIDTH
    o_v = o_k + KV_WIDTH
    cols = jnp.concatenate([jnp.arange(o_q), o_q + _head_perm(N_Q_HEADS), o_k + _head_perm(N_KV_HEADS),
                            jnp.arange(o_v, o_v + KV_WIDTH)])
    hp = _head_perm(1)

    mkT_all, mv_all = _mem_kv(mem, mem_norm_g.reshape(1, D), xattn_w_kv.astype(BF16))

    x2d = x.reshape(T, D)
    out = None
    for l in range(L):
        w_in_l = w_in[l][:, cols].astype(BF16)
        bs = jnp.broadcast_to(gm_b_s[l][:, :, None], (GM_GROUPS, CHUNK, LANES))
        qg = jnp.broadcast_to(q_norm_g[l][hp][:, None], (HEAD_DIM, tm_in))
        kg = jnp.broadcast_to(k_norm_g[l][hp][:, None], (HEAD_DIM, tm_in))
        gm, qhT, qlT, k8, vT = _mixer_in(
            x2d, mix_norm_g[l].reshape(1, D), w_in_l, gm_v_norm_g[l].reshape(1, GM_WIDTH),
            gm_w_s[l].astype(BF16), bs, qg, kg, branch_norm_g[l, 0].reshape(1, GM_WIDTH), cosT, sinT,
            seq=seq, tm=tm_in)
        atT = _attention(qhT, qlT, k8.reshape(B, seq, N_KV_HEADS * MXU_DEPTH), vT,
                         tq=tq, tk=tk, qb=min(256, tq))

        bg1 = jnp.broadcast_to(branch_norm_g[l, 1][:, None], (ATT_WIDTH, tm))
        wr_pad = jnp.pad(w_router[l].astype(BF16), ((0, 0), (0, LANES - E)))
        x2, h3, affT = _post(
            x2d, gm, atT, bg1, w_out[l].astype(BF16), xattn_norm_g[l].reshape(1, D),
            xattn_w_q[l].astype(BF16), mkT_all, mv_all, xattn_w_o[l].astype(BF16),
            ffn_norm_g[l].reshape(1, D), wr_pad, n_experts=E, layer=l, seq=seq, tm=tm)

        slots, offs = _topk(affT.reshape(E, B, seq // LANES, LANES), cap=cap)
        cnt = offs[:, :, ::TOK_BLOCK // LANES, 0]
        cnt = jnp.concatenate([cnt, jnp.full((B, E, 1), cap, jnp.int32)], axis=-1).reshape(-1)
        slots5 = slots.reshape(B, E, nblk, 1, TOK_BLOCK)
        gates5 = affT.reshape(E, B, nblk, 1, TOK_BLOCK)

        xs, gc = _gather(cnt, slots5, gates5, h3.reshape(B, seq, D), cap=cap, win=win)
        y = _ffn(xs, gc, w_gate, w_up, w_down, layer=l)
        out = _scatter(cnt, slots5, y, x2.reshape(B, seq, D), final_norm_g.reshape(1, D),
                       win=win, sb_rows=sb_rows, final_norm=(l == L - 1))
        x2d = out.reshape(T, D)
    return out
```

```python
import functools
import math

import jax
import jax.numpy as jnp
from jax import lax
from jax.experimental import pallas as pl
from jax.experimental.pallas import tpu as pltpu

F32 = jnp.float32
BF16 = jnp.bfloat16
F8 = jnp.float8_e4m3fn
F8_MAX = 448.0

EPS = 1e-6
CHUNK = 128
GM_GROUPS = 4
GM_WIDTH = 512
HEAD_DIM = 64
N_Q_HEADS = 8
N_KV_HEADS = 2
Q_PER_KV = N_Q_HEADS // N_KV_HEADS
ATT_WIDTH = N_Q_HEADS * HEAD_DIM
KV_WIDTH = N_KV_HEADS * HEAD_DIM
ROPE_THETA = 10000.0
GRID_W = 64
X_HEADS = 4
N_EXPERTS = 16
EC_FACTOR = 2

LANES = 128
MXU_DEPTH = 256
PV_ROWS = 2 * HEAD_DIM
VMEM_LIMIT = 56 * 1024 * 1024

TOK_BLOCK = 256
LOG2E = 1.4426950408889634


def _cparams(sem, vmem=VMEM_LIMIT, **kw):
    return pltpu.CompilerParams(dimension_semantics=sem, vmem_limit_bytes=vmem, **kw)


def _rms_rows(x, g):
    ms = jnp.mean(x * x, axis=-1, keepdims=True)
    return x * lax.rsqrt(ms + EPS) * g


def _gelu_tanh(x):
    c = math.sqrt(2.0 / math.pi)
    return 0.5 * x * (1.0 + jnp.tanh(c * (x + 0.044715 * (x * x * x))))


def _norm_rope_T(xT, gcol, cosT, sinT, n_heads, scale):
    half = HEAD_DIM // 2
    outs = []
    for h in range(n_heads):
        blk = xT[h * HEAD_DIM:(h + 1) * HEAD_DIM, :]
        ms = jnp.mean(blk * blk, axis=0, keepdims=True)
        n = blk * lax.rsqrt(ms + EPS) * gcol
        e = n[:half, :]
        o = n[half:, :]
        re = (e * cosT - o * sinT) * scale
        ro = (e * sinT + o * cosT) * scale
        outs.append(jnp.concatenate([re, ro], axis=0))
    return outs


def _split_f8(x):
    x = jnp.clip(x, -F8_MAX, F8_MAX)
    hi = x.astype(F8).astype(F32)
    lo = (x - hi).astype(F8).astype(F32)
    return hi, lo


def _mixer_in_kernel(x_ref, g_ref, w_ref, vg_ref, ws_ref, bs_ref, qg_ref, kg_ref, bg0_ref,
                     cos_ref, sin_ref, gm_ref, qhT_ref, qlT_ref, k_ref, vT_ref, gm_scr, *, rows_per_group):
    tm = x_ref.shape[0]
    o_q = 2 * GM_WIDTH
    o_k = o_q + ATT_WIDTH
    o_v = o_k + KV_WIDTH
    groups = [slice(r, r + rows_per_group) for r in range(0, tm, rows_per_group)]

    projs = []
    for rs in groups:
        h = _rms_rows(x_ref[rs, :], g_ref[...]).astype(BF16)
        projs.append(jnp.dot(h, w_ref[...], preferred_element_type=F32))

    for rs, proj in zip(groups, projs):
        u = _gelu_tanh(proj[:, :GM_WIDTH])
        v = _gelu_tanh(proj[:, GM_WIDTH:2 * GM_WIDTH])
        vn = _rms_rows(v, vg_ref[...]).astype(BF16)
        for c in range(rows_per_group // CHUNK):
            cr = slice(c * CHUNK, (c + 1) * CHUNK)
            sr = slice(rs.start + c * CHUNK, rs.start + (c + 1) * CHUNK)
            for g in range(GM_GROUPS):
                cs = slice(g * LANES, (g + 1) * LANES)
                mixed = jnp.dot(ws_ref[g], vn[cr, cs], preferred_element_type=F32) + bs_ref[g]
                gm_scr[sr, cs] = u[cr, cs] * mixed
        gm_ref[rs, :] = _rms_rows(gm_scr[rs, :], bg0_ref[...]).astype(BF16)

        cosT = cos_ref[:, rs]
        sinT = sin_ref[:, rs]
        qT = proj[:, o_q:o_k].T
        q_heads = _norm_rope_T(qT, qg_ref[:, rs], cosT, sinT, N_Q_HEADS, (HEAD_DIM ** -0.5) * LOG2E)
        for hh in range(N_Q_HEADS):
            hi, lo = _split_f8(q_heads[hh])
            qhT_ref[hh * HEAD_DIM:(hh + 1) * HEAD_DIM, rs] = hi.astype(F8)
            qlT_ref[hh * HEAD_DIM:(hh + 1) * HEAD_DIM, rs] = lo.astype(F8)

        kT = proj[:, o_k:o_v].T
        k_heads = _norm_rope_T(kT, kg_ref[:, rs], cosT, sinT, N_KV_HEADS, 1.0)
        pieces = []
        for kh in k_heads:
            hi, lo = _split_f8(kh)
            pieces += [hi, lo, hi, jnp.zeros_like(hi)]
        k_ref[rs, :] = jnp.concatenate(pieces, axis=0).T.astype(F8)

        vT_ref[:, rs] = proj[:, o_v:].T.astype(BF16)


def _mixer_in(x2d, g, w_in, vg, ws, bs, qg, kg, bg0, cosT, sinT, *, seq, tm):
    T, D = x2d.shape
    B = T // seq
    npb = seq // tm
    in_w = w_in.shape[1]
    full = lambda shape: pl.BlockSpec(shape, lambda i: (0,) * len(shape))
    return pl.pallas_call(
        functools.partial(_mixer_in_kernel, rows_per_group=min(256, tm)),
        grid=(T // tm,),
        in_specs=[
            pl.BlockSpec((tm, D), lambda i: (i, 0)),
            full((1, D)),
            full((D, in_w)),
            full((1, GM_WIDTH)),
            full((GM_GROUPS, CHUNK, CHUNK)),
            full((GM_GROUPS, CHUNK, LANES)),
            full((HEAD_DIM, tm)),
            full((HEAD_DIM, tm)),
            full((1, GM_WIDTH)),
            pl.BlockSpec((HEAD_DIM // 2, tm), lambda i: (0, i % npb)),
            pl.BlockSpec((HEAD_DIM // 2, tm), lambda i: (0, i % npb)),
        ],
        out_specs=[
            pl.BlockSpec((tm, GM_WIDTH), lambda i: (i, 0)),
            pl.BlockSpec((None, ATT_WIDTH, tm), lambda i: (i // npb, 0, i % npb)),
            pl.BlockSpec((None, ATT_WIDTH, tm), lambda i: (i // npb, 0, i % npb)),
            pl.BlockSpec((tm, N_KV_HEADS * MXU_DEPTH), lambda i: (i, 0)),
            pl.BlockSpec((None, KV_WIDTH, tm), lambda i: (i // npb, 0, i % npb)),
        ],
        out_shape=[
            jax.ShapeDtypeStruct((T, GM_WIDTH), BF16),
            jax.ShapeDtypeStruct((B, ATT_WIDTH, seq), F8),
            jax.ShapeDtypeStruct((B, ATT_WIDTH, seq), F8),
            jax.ShapeDtypeStruct((T, N_KV_HEADS * MXU_DEPTH), F8),
            jax.ShapeDtypeStruct((B, KV_WIDTH, seq), BF16),
        ],
        scratch_shapes=[pltpu.VMEM((tm, GM_WIDTH), F32)],
        compiler_params=_cparams(("parallel",)),
        name="mixer_in",
    )(x2d, g, w_in, vg, ws, bs, qg, kg, bg0, cosT, sinT)


def _attn_kernel(qhT_ref, qlT_ref, k_ref, vT_ref, o_ref, vext_scr, qx_scr, m_scr, acc_scr, s_scr, cm_scr,
                 *, tq, tk, qb):
    seq = k_ref.shape[0]
    M = Q_PER_KV * tq
    nq = seq // tq
    nchunk = seq // tk

    vext_scr[0:HEAD_DIM, :] = vT_ref[...]
    row = lax.broadcasted_iota(jnp.int32, (PV_ROWS - HEAD_DIM, seq), 0)
    vext_scr[HEAD_DIM:, :] = jnp.where(row == 0, 1.0, 0.0).astype(BF16)

    def load_q(i, slot):
        cols = pl.ds(pl.multiple_of(i * tq, tq), tq)
        qh = jnp.concatenate([qhT_ref[g * HEAD_DIM:(g + 1) * HEAD_DIM, cols] for g in range(Q_PER_KV)], axis=1)
        ql = jnp.concatenate([qlT_ref[g * HEAD_DIM:(g + 1) * HEAD_DIM, cols] for g in range(Q_PER_KV)], axis=1)
        qx_scr[slot] = jnp.concatenate([qh, qh, ql, jnp.zeros_like(qh)], axis=0)

    def step(c, qslot, qk_chunk, do_pv):
        cur = c % 2
        nxt = 1 - cur
        if qk_chunk is not None:
            kn = k_ref[qk_chunk * tk:(qk_chunk + 1) * tk, :]
        if do_pv:
            ve = vext_scr[:, c * tk:(c + 1) * tk]
        for j in range(M // qb):
            cs = slice(j * qb, (j + 1) * qb)
            if qk_chunk is not None:
                s = jnp.dot(kn, qx_scr[qslot, :, cs], preferred_element_type=F32)
                s_scr[nxt, :, cs] = s
                cm_scr[nxt, :, cs] = jnp.max(s, axis=0, keepdims=True)
            if do_pv:
                m_old = m_scr[:, cs]
                m_new = jnp.maximum(m_old, cm_scr[cur, :, cs])
                p = jnp.exp2(s_scr[cur, :, cs] - m_new).astype(BF16)
                alpha = jnp.exp2(m_old - m_new)
                pv = jnp.dot(ve, p, preferred_element_type=F32)
                acc_scr[:, cs] = acc_scr[:, cs] * alpha + pv
                m_scr[:, cs] = m_new

    load_q(0, 0)
    step(-1, 0, 0, False)

    def qblock(i, carry):
        slot = i % 2
        m_scr[...] = jnp.full(m_scr.shape, -jnp.inf, F32)
        acc_scr[...] = jnp.zeros(acc_scr.shape, F32)
        for c in range(nchunk - 1):
            step(c, slot, c + 1, True)
        load_q(jnp.minimum(i + 1, nq - 1), 1 - slot)
        step(nchunk - 1, 1 - slot, 0, True)

        acc = acc_scr[...]
        inv = 1.0 / acc[HEAD_DIM:HEAD_DIM + 1, :]
        o = acc[0:HEAD_DIM, :] * inv
        cols = pl.ds(pl.multiple_of(i * tq, tq), tq)
        for g in range(Q_PER_KV):
            o_ref[g * HEAD_DIM:(g + 1) * HEAD_DIM, cols] = o[:, g * tq:(g + 1) * tq].astype(BF16)
        return carry

    lax.fori_loop(0, nq, qblock, 0)


def _attention(qhT, qlT, k, vT, *, tq, tk, qb):
    B, _, seq = qhT.shape
    gw = Q_PER_KV * HEAD_DIM
    assert (seq // tk) % 2 == 0
    return pl.pallas_call(
        functools.partial(_attn_kernel, tq=tq, tk=tk, qb=qb),
        grid=(B, N_KV_HEADS),
        in_specs=[
            pl.BlockSpec((None, gw, seq), lambda b, kh: (b, kh, 0)),
            pl.BlockSpec((None, gw, seq), lambda b, kh: (b, kh, 0)),
            pl.BlockSpec((None, seq, MXU_DEPTH), lambda b, kh: (b, 0, kh)),
            pl.BlockSpec((None, HEAD_DIM, seq), lambda b, kh: (b, kh, 0)),
        ],
        out_specs=pl.BlockSpec((None, gw, seq), lambda b, kh: (b, kh, 0)),
        out_shape=jax.ShapeDtypeStruct((B, ATT_WIDTH, seq), BF16),
        scratch_shapes=[
            pltpu.VMEM((PV_ROWS, seq), BF16),
            pltpu.VMEM((2, MXU_DEPTH, Q_PER_KV * tq), F8),
            pltpu.VMEM((1, Q_PER_KV * tq), F32),
            pltpu.VMEM((PV_ROWS, Q_PER_KV * tq), F32),
            pltpu.VMEM((2, tk, Q_PER_KV * tq), F32),
            pltpu.VMEM((2, 1, Q_PER_KV * tq), F32),
        ],
        compiler_params=_cparams(("parallel", "parallel")),
        name="attention",
    )(qhT, qlT, k, vT)


def _mem_kv_kernel(mem_ref, g_ref, w_ref, kT_ref, v_ref):
    D = mem_ref.shape[1]
    mn = _rms_rows(mem_ref[...], g_ref[...]).astype(BF16)
    kv = jnp.dot(mn, w_ref[...], preferred_element_type=F32)
    kT_ref[...] = kv[:, :D].T.astype(BF16)
    v_ref[...] = kv[:, D:].astype(BF16)


def _mem_kv(mem, g, w_kv):
    B, M, D = mem.shape
    L = w_kv.shape[0]
    return pl.pallas_call(
        _mem_kv_kernel,
        grid=(L, B),
        in_specs=[
            pl.BlockSpec((None, M, D), lambda l, b: (b, 0, 0)),
            pl.BlockSpec((1, D), lambda l, b: (0, 0)),
            pl.BlockSpec((None, D, 2 * D), lambda l, b: (l, 0, 0)),
        ],
        out_specs=[
            pl.BlockSpec((None, None, D, M), lambda l, b: (l, b, 0, 0)),
            pl.BlockSpec((None, None, M, D), lambda l, b: (l, b, 0, 0)),
        ],
        out_shape=[
            jax.ShapeDtypeStruct((L, B, D, M), BF16),
            jax.ShapeDtypeStruct((L, B, M, D), BF16),
        ],
        compiler_params=_cparams(("parallel", "parallel")),
        name="mem_kv",
    )(mem, g, w_kv)


def _post_kernel(x_ref, gm_ref, atT_ref, bg1_ref, wout_ref, xg_ref, wq_ref, mkT_ref, mv_ref, wo_ref,
                 fg_ref, wr_ref, x2_ref, h3_ref, affT_ref, *, rows_per_group):
    tm, D = x_ref.shape
    xhd = D // X_HEADS
    groups = [slice(r, r + rows_per_group) for r in range(0, tm, rows_per_group)]

    x1 = []
    for rs in groups:
        at = atT_ref[:, rs].astype(F32)
        ms = jnp.mean(at * at, axis=0, keepdims=True)
        atn = (at * lax.rsqrt(ms + EPS) * bg1_ref[:, rs]).astype(BF16)
        y = jnp.dot(gm_ref[rs, :], wout_ref[0:GM_WIDTH, :], preferred_element_type=F32)
        y = y + lax.dot_general(atn, wout_ref[GM_WIDTH:, :], (((0,), (0,)), ((), ())),
                                preferred_element_type=F32)
        x1.append(x_ref[rs, :] + y)

    q2 = []
    for x1g in x1:
        h2 = _rms_rows(x1g, xg_ref[...]).astype(BF16)
        q2.append((jnp.dot(h2, wq_ref[...], preferred_element_type=F32) * (xhd ** -0.5)).astype(BF16))

    scores = {}
    for hh in range(X_HEADS):
        cs = slice(hh * xhd, (hh + 1) * xhd)
        for gi in range(len(groups)):
            scores[hh, gi] = jnp.dot(q2[gi][:, cs], mkT_ref[cs, :], preferred_element_type=F32)
    outs = [[] for _ in groups]
    for hh in range(X_HEADS):
        cs = slice(hh * xhd, (hh + 1) * xhd)
        for gi in range(len(groups)):
            s = scores[hh, gi]
            s = s - jnp.max(s, axis=-1, keepdims=True)
            p = jnp.exp(s)
            p = (p / jnp.sum(p, axis=-1, keepdims=True)).astype(BF16)
            outs[gi].append(jnp.dot(p, mv_ref[:, cs], preferred_element_type=F32).astype(BF16))

    x2 = []
    for gi, rs in enumerate(groups):
        o2 = jnp.concatenate(outs[gi], axis=1)
        x2g = x1[gi] + jnp.dot(o2, wo_ref[...], preferred_element_type=F32)
        x2_ref[rs, :] = x2g
        x2.append(x2g)

    ne = affT_ref.shape[0]
    for gi, rs in enumerate(groups):
        h3 = _rms_rows(x2[gi], fg_ref[...]).astype(BF16)
        h3_ref[rs, :] = h3
        lg = jnp.dot(h3, wr_ref[...], preferred_element_type=F32).T[0:ne, :]
        lg = lg - jnp.max(lg, axis=0, keepdims=True)
        ex = jnp.exp(lg)
        affT_ref[:, rs] = ex / jnp.sum(ex, axis=0, keepdims=True)


def _post(x2d, gm, atT, bg1, w_out, xg, w_q, mkT, mv, w_o, fg, wr_pad, *, n_experts, layer, seq, tm):
    T, D = x2d.shape
    npb = seq // tm
    M = mv.shape[2]
    E = n_experts
    full = lambda shape: pl.BlockSpec(shape, lambda i: (0,) * len(shape))
    return pl.pallas_call(
        functools.partial(_post_kernel, rows_per_group=min(256, tm)),
        grid=(T // tm,),
        in_specs=[
            pl.BlockSpec((tm, D), lambda i: (i, 0)),
            pl.BlockSpec((tm, GM_WIDTH), lambda i: (i, 0)),
            pl.BlockSpec((None, ATT_WIDTH, tm), lambda i: (i // npb, 0, i % npb)),
            full((ATT_WIDTH, tm)),
            full((GM_WIDTH + ATT_WIDTH, D)),
            full((1, D)),
            full((D, D)),
            pl.BlockSpec((None, None, D, M), lambda i: (layer, i // npb, 0, 0)),
            pl.BlockSpec((None, None, M, D), lambda i: (layer, i // npb, 0, 0)),
            full((D, D)),
            full((1, D)),
            full((D, LANES)),
        ],
        out_specs=[
            pl.BlockSpec((tm, D), lambda i: (i, 0)),
            pl.BlockSpec((tm, D), lambda i: (i, 0)),
            pl.BlockSpec((E, tm), lambda i: (0, i)),
        ],
        out_shape=[
            jax.ShapeDtypeStruct((T, D), F32),
            jax.ShapeDtypeStruct((T, D), BF16),
            jax.ShapeDtypeStruct((E, T), F32),
        ],
        compiler_params=_cparams(("parallel",)),
        name="post",
    )(x2d, gm, atT, bg1, w_out, xg, w_q, mkT, mv, w_o, fg, wr_pad)


def _topk_kernel(aff_ref, slot_ref, off_ref, *, cap):
    E, R, L = aff_ref.shape
    a = aff_ref[...]
    bits = lax.bitcast_convert_type(a, jnp.int32)

    def count_ge(t):
        c = jnp.where(bits >= t, 1.0, 0.0)
        return jnp.sum(jnp.sum(c, axis=2, keepdims=True), axis=1, keepdims=True)

    def bis(_, carry):
        lo, hi = carry
        mid = lo + ((hi - lo) >> 1)
        ok = count_ge(mid) >= cap
        return jnp.where(ok, mid, lo), jnp.where(ok, hi, mid)

    lo0 = jnp.zeros((E, 1, 1), jnp.int32)
    hi0 = jnp.full((E, 1, 1), 0x3F800001, jnp.int32)
    thr, _ = lax.fori_loop(0, 31, bis, (lo0, hi0))

    kk = lax.broadcasted_iota(jnp.int32, (L, L), 0)
    nn = lax.broadcasted_iota(jnp.int32, (L, L), 1)
    upper = jnp.where(kk <= nn, 1.0, 0.0).astype(BF16)
    ones = jnp.ones((L, L), BF16)
    rr = lax.broadcasted_iota(jnp.int32, (E * R, E * R), 0)
    cc = lax.broadcasted_iota(jnp.int32, (E * R, E * R), 1)
    lower = jnp.where((rr // R == cc // R) & (cc < rr), 1.0, 0.0).astype(BF16)

    def prefix(xf):
        x2 = xf.reshape(E * R, L)
        xb = x2.astype(BF16)
        incl = jnp.dot(xb, upper, preferred_element_type=F32)
        tot = jnp.dot(xb, ones, preferred_element_type=F32).astype(BF16)
        rowoff = jnp.dot(lower, tot, preferred_element_type=F32)
        return (incl - x2 + rowoff).reshape(E, R, L), rowoff.reshape(E, R, L)

    gt = jnp.where(bits > thr, 1.0, 0.0)
    eq = jnp.where(bits == thr, 1.0, 0.0)
    n_gt = jnp.sum(jnp.sum(gt, axis=2, keepdims=True), axis=1, keepdims=True)
    need = cap - n_gt
    eq_rank, _ = prefix(eq)
    sel = gt + eq * jnp.where(eq_rank < need, 1.0, 0.0)
    pos, rowoff = prefix(sel)
    slot_ref[...] = jnp.where(sel > 0.0, pos, -1.0).astype(jnp.int32)
    off_ref[...] = rowoff.astype(jnp.int32)


def _topk(aff4, *, cap):
    E, B, R, L = aff4.shape
    return pl.pallas_call(
        functools.partial(_topk_kernel, cap=cap),
        grid=(B,),
        in_specs=[pl.BlockSpec((E, None, R, L), lambda b: (0, b, 0, 0))],
        out_specs=[
            pl.BlockSpec((None, E, R, L), lambda b: (b, 0, 0, 0)),
            pl.BlockSpec((None, E, R, L), lambda b: (b, 0, 0, 0)),
        ],
        out_shape=[
            jax.ShapeDtypeStruct((B, E, R, L), jnp.int32),
            jax.ShapeDtypeStruct((B, E, R, L), jnp.int32),
        ],
        compiler_params=_cparams(("parallel",)),
        name="topk",
    )(aff4)


def _gather_kernel(cnt_ref, slot_ref, gate_ref, h_ref, xs_ref, gc_ref, *, win):
    b = pl.program_id(0)
    eg = pl.program_id(1)
    group, cap, D = xs_ref.shape
    ne = pl.num_programs(1) * group
    nblk = slot_ref.shape[1]
    r_iota = lax.broadcasted_iota(jnp.int32, (win, TOK_BLOCK), 0)

    xs_ref[...] = jnp.zeros(xs_ref.shape, BF16)
    gc_ref[...] = jnp.zeros(gc_ref.shape, F32)

    def window(g, j):
        idx = (b * ne + eg * group + g) * (nblk + 1) + j
        s0 = cnt_ref[idx]
        s1 = cnt_ref[idx + 1]
        a0 = (s0 // 16) * 16
        return a0, s1

    def hits(g, j, lo, sa):
        srow = slot_ref[g, j]
        rel = jnp.where(srow >= lo, srow - sa, -1)
        return rel == r_iota

    def add_rows(g, j, sa, hit, part):
        rows = pl.ds(sa, win)
        xs_ref[g, rows, :] = xs_ref[g, rows, :] + part.astype(BF16)
        gc_ref[g, rows, :] += jnp.sum(jnp.where(hit, gate_ref[g, j], 0.0), axis=1, keepdims=True)

    extra = jnp.int32(0)
    for j in range(nblk):
        hb = h_ref[j * TOK_BLOCK:(j + 1) * TOK_BLOCK, :]
        masks, starts = [], []
        for g in range(group):
            a0, s1 = window(g, j)
            sa = pl.multiple_of(jnp.minimum(a0, cap - win), 16)
            extra = jnp.maximum(extra, s1 - (a0 + win))
            masks.append(hits(g, j, a0, sa))
            starts.append(sa)
        lhs = jnp.concatenate([jnp.where(m, 1.0, 0.0).astype(BF16) for m in masks], axis=0)
        res = jnp.dot(lhs, hb, preferred_element_type=F32)
        for g in range(group):
            add_rows(g, j, starts[g], masks[g], res[g * win:(g + 1) * win, :])

    @pl.when(extra > 0)
    def _():
        def blk(j, carry):
            hb = h_ref[pl.ds(pl.multiple_of(j * TOK_BLOCK, TOK_BLOCK), TOK_BLOCK), :]
            for g in range(group):
                a0, s1 = window(g, j)
                nw = (s1 - a0 + win - 1) // win

                def wbody(w, c2, g=g, a0=a0):
                    lo = a0 + w * win
                    sa = pl.multiple_of(jnp.minimum(lo, cap - win), 16)
                    hit = hits(g, j, lo, sa)
                    part = jnp.dot(jnp.where(hit, 1.0, 0.0).astype(BF16), hb, preferred_element_type=F32)
                    add_rows(g, j, sa, hit, part)
                    return c2

                lax.fori_loop(1, nw, wbody, 0)
            return carry

        lax.fori_loop(0, nblk, blk, 0)


def _gather(cnt, slots5, gates5, h3d, *, cap, win):
    B, seq, D = h3d.shape
    E = slots5.shape[1]
    nblk = seq // TOK_BLOCK
    group = MXU_DEPTH // win
    assert E % group == 0 and cap % win == 0
    gs = pltpu.PrefetchScalarGridSpec(
        num_scalar_prefetch=1,
        grid=(B, E // group),
        in_specs=[
            pl.BlockSpec((None, group, nblk, 1, TOK_BLOCK), lambda b, e, c: (b, e, 0, 0, 0)),
            pl.BlockSpec((group, None, nblk, 1, TOK_BLOCK), lambda b, e, c: (e, b, 0, 0, 0)),
            pl.BlockSpec((None, seq, D), lambda b, e, c: (b, 0, 0), pipeline_mode=pl.Buffered(1)),
        ],
        out_specs=[
            pl.BlockSpec((None, group, cap, D), lambda b, e, c: (b, e, 0, 0)),
            pl.BlockSpec((None, group, cap, 1), lambda b, e, c: (b, e, 0, 0)),
        ],
    )
    return pl.pallas_call(
        functools.partial(_gather_kernel, win=win),
        grid_spec=gs,
        out_shape=[
            jax.ShapeDtypeStruct((B, E, cap, D), BF16),
            jax.ShapeDtypeStruct((B, E, cap, 1), F32),
        ],
        compiler_params=_cparams(("parallel", "arbitrary")),
        name="gather",
    )(cnt, slots5, gates5, h3d)


def _ffn_kernel(xs_ref, gc_ref, wg_ref, wu_ref, wd_ref, y_ref, wg_scr, wu_scr, wd_scr, *, f_chunk):
    @pl.when(pl.program_id(1) == 0)
    def _():
        wg_scr[...] = wg_ref[...].astype(BF16)
        wu_scr[...] = wu_ref[...].astype(BF16)
        wd_scr[...] = wd_ref[...].astype(BF16)

    xs = xs_ref[...]
    y = None
    for f0 in range(0, wg_scr.shape[1], f_chunk):
        fs = slice(f0, f0 + f_chunk)
        a = jnp.dot(xs, wg_scr[:, fs], preferred_element_type=F32)
        u = jnp.dot(xs, wu_scr[:, fs], preferred_element_type=F32)
        hmid = (a * jax.nn.sigmoid(a) * u).astype(BF16)
        part = jnp.dot(hmid, wd_scr[fs, :], preferred_element_type=F32)
        y = part if y is None else y + part
    y_ref[...] = (y * gc_ref[...]).astype(BF16)


def _ffn(xs, gc, w_gate, w_up, w_down, *, layer):
    B, E, cap, D = xs.shape
    Fd = w_gate.shape[-1]
    return pl.pallas_call(
        functools.partial(_ffn_kernel, f_chunk=min(512, Fd)),
        grid=(E, B),
        in_specs=[
            pl.BlockSpec((None, None, cap, D), lambda e, b: (b, e, 0, 0)),
            pl.BlockSpec((None, None, cap, 1), lambda e, b: (b, e, 0, 0)),
            pl.BlockSpec((None, None, D, Fd), lambda e, b: (layer, e, 0, 0)),
            pl.BlockSpec((None, None, D, Fd), lambda e, b: (layer, e, 0, 0)),
            pl.BlockSpec((None, None, Fd, D), lambda e, b: (layer, e, 0, 0)),
        ],
        out_specs=pl.BlockSpec((None, None, cap, D), lambda e, b: (b, e, 0, 0)),
        out_shape=jax.ShapeDtypeStruct((B, E, cap, D), BF16),
        scratch_shapes=[
            pltpu.VMEM((D, Fd), BF16),
            pltpu.VMEM((D, Fd), BF16),
            pltpu.VMEM((Fd, D), BF16),
        ],
        compiler_params=_cparams(("parallel", "arbitrary")),
        name="ffn",
    )(xs, gc, w_gate, w_up, w_down)


def _scatter_kernel(cnt_ref, slot_ref, y_ref, x_ref, fg_ref, o_ref, *, win, final_norm):
    b = pl.program_id(0)
    sb = pl.program_id(1)
    ne, cap, D = y_ref.shape
    nloc = slot_ref.shape[1]
    nblk = nloc * pl.num_programs(1)
    r_iota = lax.broadcasted_iota(jnp.int32, (win, TOK_BLOCK), 0)
    contract0 = (((0,), (0,)), ((), ()))

    def onehot_t(srow, lo, sa):
        rel = jnp.where(srow >= lo, srow - sa, -1)
        return jnp.where(rel == r_iota, 1.0, 0.0).astype(BF16)

    def window(e, j):
        idx = (b * ne + e) * (nblk + 1) + j
        s0 = cnt_ref[idx]
        s1 = cnt_ref[idx + 1]
        a0 = (s0 // 16) * 16
        sa = pl.multiple_of(jnp.minimum(a0, cap - win), 16)
        return a0, sa, s1 - (a0 + win)

    extra = jnp.int32(0)
    per_dot = MXU_DEPTH // win
    for t in range(nloc):
        j = sb * nloc + t
        rows = slice(t * TOK_BLOCK, (t + 1) * TOK_BLOCK)
        acc = x_ref[rows, :]
        for e0 in range(0, ne, per_dot):
            gs, ys = [], []
            for e in range(e0, e0 + per_dot):
                a0, sa, over = window(e, j)
                extra = jnp.maximum(extra, over)
                gs.append(onehot_t(slot_ref[e, t], a0, sa))
                ys.append(y_ref[e, pl.ds(sa, win), :])
            acc = acc + lax.dot_general(jnp.concatenate(gs, axis=0), jnp.concatenate(ys, axis=0),
                                        contract0, preferred_element_type=F32)
        o_ref[rows, :] = acc

    @pl.when(extra > 0)
    def _():
        for t in range(nloc):
            j = sb * nloc + t
            rows = slice(t * TOK_BLOCK, (t + 1) * TOK_BLOCK)
            for e in range(ne):
                idx = (b * ne + e) * (nblk + 1) + j
                s0 = cnt_ref[idx]
                s1 = cnt_ref[idx + 1]
                a0 = (s0 // 16) * 16
                nw = (s1 - a0 + win - 1) // win

                def wbody(w, carry):
                    lo = a0 + w * win
                    sa = pl.multiple_of(jnp.minimum(lo, cap - win), 16)
                    g = onehot_t(slot_ref[e, t], lo, sa)
                    o_ref[rows, :] += lax.dot_general(g, y_ref[e, pl.ds(sa, win), :], contract0,
                                                      preferred_element_type=F32)
                    return carry

                lax.fori_loop(1, nw, wbody, 0)

    if final_norm:
        o_ref[...] = _rms_rows(o_ref[...], fg_ref[...])


def _scatter(cnt, slots5, y, x3d, fg, *, win, sb_rows, final_norm):
    B, seq, D = x3d.shape
    E, cap = y.shape[1], y.shape[2]
    nloc = sb_rows // TOK_BLOCK
    assert MXU_DEPTH % win == 0 and E % (MXU_DEPTH // win) == 0
    gs = pltpu.PrefetchScalarGridSpec(
        num_scalar_prefetch=1,
        grid=(B, seq // sb_rows),
        in_specs=[
            pl.BlockSpec((None, E, nloc, 1, TOK_BLOCK), lambda b, s, c: (b, 0, s, 0, 0)),
            pl.BlockSpec((None, E, cap, D), lambda b, s, c: (b, 0, 0, 0), pipeline_mode=pl.Buffered(1)),
            pl.BlockSpec((None, sb_rows, D), lambda b, s, c: (b, s, 0)),
            pl.BlockSpec((1, D), lambda b, s, c: (0, 0)),
        ],
        out_specs=pl.BlockSpec((None, sb_rows, D), lambda b, s, c: (b, s, 0)),
    )
    return pl.pallas_call(
        functools.partial(_scatter_kernel, win=win, final_norm=final_norm),
        grid_spec=gs,
        out_shape=jax.ShapeDtypeStruct((B, seq, D), F32),
        compiler_params=_cparams(("parallel", "arbitrary")),
        name="scatter",
    )(cnt, slots5, y, x3d, fg)


def _rope_tables_T(seq):
    rows = seq // GRID_W
    row_id = jnp.repeat(jnp.arange(rows, dtype=F32), GRID_W)
    col_id = jnp.tile(jnp.arange(GRID_W, dtype=F32), rows)
    n_pairs = HEAD_DIM // 4
    freqs = jnp.exp(-math.log(ROPE_THETA) * jnp.arange(n_pairs, dtype=F32) / n_pairs)
    ang = jnp.concatenate([freqs[:, None] * row_id[None, :], freqs[:, None] * col_id[None, :]], axis=0)
    return jnp.cos(ang), jnp.sin(ang)


def _head_perm(n_heads):
    base = jnp.concatenate([jnp.arange(0, HEAD_DIM, 2), jnp.arange(1, HEAD_DIM, 2)])
    return (jnp.arange(n_heads)[:, None] * HEAD_DIM + base[None, :]).reshape(-1)


def kernel(x, mem, mix_norm_g, w_in, gm_v_norm_g, gm_w_s, gm_b_s, q_norm_g, k_norm_g, branch_norm_g, w_out,
           xattn_norm_g, mem_norm_g, xattn_w_q, xattn_w_kv, xattn_w_o, ffn_norm_g, w_router, w_gate, w_up,
           w_down, final_norm_g):
    B, seq, D = x.shape
    L = w_in.shape[0]
    E = w_router.shape[-1]
    T = B * seq
    cap = EC_FACTOR * seq // E
    tm = min(1024, seq)
    tm_in = min(2048, seq)
    tq = min(256, seq)
    tk = min(2048, seq // 2)
    win = min(64, cap)
    sb_rows = min(512, seq)
    nblk = seq // TOK_BLOCK

    cosT, sinT = _rope_tables_T(seq)
    o_q = 2 * GM_WIDTH
    o_k = o_q + ATT_WIDTH
    o_v = o_k + KV_WIDTH
    cols = jnp.concatenate([jnp.arange(o_q), o_q + _head_perm(N_Q_HEADS), o_k + _head_perm(N_KV_HEADS),
                            jnp.arange(o_v, o_v + KV_WIDTH)])
    hp = _head_perm(1)

    mkT_all, mv_all = _mem_kv(mem, mem_norm_g.reshape(1, D), xattn_w_kv.astype(BF16))

    x2d = x.reshape(T, D)
    out = None
    for l in range(L):
        w_in_l = w_in[l][:, cols].astype(BF16)
        bs = jnp.broadcast_to(gm_b_s[l][:, :, None], (GM_GROUPS, CHUNK, LANES))
        qg = jnp.broadcast_to(q_norm_g[l][hp][:, None], (HEAD_DIM, tm_in))
        kg = jnp.broadcast_to(k_norm_g[l][hp][:, None], (HEAD_DIM, tm_in))
        gm, qhT, qlT, k8, vT = _mixer_in(
            x2d, mix_norm_g[l].reshape(1, D), w_in_l, gm_v_norm_g[l].reshape(1, GM_WIDTH),
            gm_w_s[l].astype(BF16), bs, qg, kg, branch_norm_g[l, 0].reshape(1, GM_WIDTH), cosT, sinT,
            seq=seq, tm=tm_in)
        atT = _attention(qhT, qlT, k8.reshape(B, seq, N_KV_HEADS * MXU_DEPTH), vT,
                         tq=tq, tk=tk, qb=min(256, tq))

        bg1 = jnp.broadcast_to(branch_norm_g[l, 1][:, None], (ATT_WIDTH, tm))
        wr_pad = jnp.pad(w_router[l].astype(BF16), ((0, 0), (0, LANES - E)))
        x2, h3, affT = _post(
            x2d, gm, atT, bg1, w_out[l].astype(BF16), xattn_norm_g[l].reshape(1, D),
            xattn_w_q[l].astype(BF16), mkT_all, mv_all, xattn_w_o[l].astype(BF16),
            ffn_norm_g[l].reshape(1, D), wr_pad, n_experts=E, layer=l, seq=seq, tm=tm)

        slots, offs = _topk(affT.reshape(E, B, seq // LANES, LANES), cap=cap)
        cnt = offs[:, :, ::TOK_BLOCK // LANES, 0]
        cnt = jnp.concatenate([cnt, jnp.full((B, E, 1), cap, jnp.int32)], axis=-1).reshape(-1)
        slots5 = slots.reshape(B, E, nblk, 1, TOK_BLOCK)
        gates5 = affT.reshape(E, B, nblk, 1, TOK_BLOCK)

        xs, gc = _gather(cnt, slots5, gates5, h3.reshape(B, seq, D), cap=cap, win=win)
        y = _ffn(xs, gc, w_gate, w_up, w_down, layer=l)
        out = _scatter(cnt, slots5, y, x2.reshape(B, seq, D), final_norm_g.reshape(1, D),
                       win=win, sb_rows=sb_rows, final_norm=(l == L - 1))
        x2d = out.reshape(T, D)
    return out
```

```python
import functools
import math

import jax
import jax.numpy as jnp
from jax import lax
from jax.experimental import pallas as pl
from jax.experimental.pallas import tpu as pltpu

F32 = jnp.float32
BF16 = jnp.bfloat16
F8 = jnp.float8_e4m3fn
F8_MAX = 448.0

EPS = 1e-6
CHUNK = 128
GM_GROUPS = 4
GM_WIDTH = 512
HEAD_DIM = 64
N_Q_HEADS = 8
N_KV_HEADS = 2
Q_PER_KV = N_Q_HEADS // N_KV_HEADS
ATT_WIDTH = N_Q_HEADS * HEAD_DIM
KV_WIDTH = N_KV_HEADS * HEAD_DIM
ROPE_THETA = 10000.0
GRID_W = 64
X_HEADS = 4
N_EXPERTS = 16
EC_FACTOR = 2

LANES = 128
MXU_DEPTH = 256
PV_ROWS = HEAD_DIM + 32
VMEM_LIMIT = 56 * 1024 * 1024

TOK_BLOCK = 256
LOG2E = 1.4426950408889634


def _cparams(sem, vmem=VMEM_LIMIT, **kw):
    return pltpu.CompilerParams(dimension_semantics=sem, vmem_limit_bytes=vmem, **kw)


def _rms_rows(x, g):
    ms = jnp.mean(x * x, axis=-1, keepdims=True)
    return x * lax.rsqrt(ms + EPS) * g


def _gelu_tanh(x):
    c = math.sqrt(2.0 / math.pi)
    return 0.5 * x * (1.0 + jnp.tanh(c * (x + 0.044715 * (x * x * x))))


def _norm_rope_T(xT, gcol, cosT, sinT, n_heads, scale):
    half = HEAD_DIM // 2
    outs = []
    for h in range(n_heads):
        blk = xT[h * HEAD_DIM:(h + 1) * HEAD_DIM, :]
        ms = jnp.mean(blk * blk, axis=0, keepdims=True)
        n = blk * lax.rsqrt(ms + EPS) * gcol
        e = n[:half, :]
        o = n[half:, :]
        re = (e * cosT - o * sinT) * scale
        ro = (e * sinT + o * cosT) * scale
        outs.append(jnp.concatenate([re, ro], axis=0))
    return outs


def _split_f8(x):
    x = jnp.clip(x, -F8_MAX, F8_MAX)
    hi = x.astype(F8).astype(F32)
    lo = (x - hi).astype(F8).astype(F32)
    return hi, lo


def _mixer_in_kernel(x_ref, g_ref, w_ref, vg_ref, ws_ref, bs_ref, qg_ref, kg_ref, bg0_ref,
                     cos_ref, sin_ref, gm_ref, qhT_ref, qlT_ref, k_ref, vT_ref, gm_scr, *, rows_per_group):
    tm = x_ref.shape[0]
    o_q = 2 * GM_WIDTH
    o_k = o_q + ATT_WIDTH
    o_v = o_k + KV_WIDTH
    groups = [slice(r, r + rows_per_group) for r in range(0, tm, rows_per_group)]

    projs = []
    for rs in groups:
        h = _rms_rows(x_ref[rs, :], g_ref[...]).astype(BF16)
        projs.append(jnp.dot(h, w_ref[...], preferred_element_type=F32))

    for rs, proj in zip(groups, projs):
        u = _gelu_tanh(proj[:, :GM_WIDTH])
        v = _gelu_tanh(proj[:, GM_WIDTH:2 * GM_WIDTH])
        vn = _rms_rows(v, vg_ref[...]).astype(BF16)
        for c in range(rows_per_group // CHUNK):
            cr = slice(c * CHUNK, (c + 1) * CHUNK)
            sr = slice(rs.start + c * CHUNK, rs.start + (c + 1) * CHUNK)
            for g in range(GM_GROUPS):
                cs = slice(g * LANES, (g + 1) * LANES)
                mixed = jnp.dot(ws_ref[g], vn[cr, cs], preferred_element_type=F32) + bs_ref[g]
                gm_scr[sr, cs] = u[cr, cs] * mixed
        gm_ref[rs, :] = _rms_rows(gm_scr[rs, :], bg0_ref[...]).astype(BF16)

        cosT = cos_ref[:, rs]
        sinT = sin_ref[:, rs]
        qT = proj[:, o_q:o_k].T
        q_heads = _norm_rope_T(qT, qg_ref[:, rs], cosT, sinT, N_Q_HEADS, (HEAD_DIM ** -0.5) * LOG2E)
        for hh in range(N_Q_HEADS):
            hi, lo = _split_f8(q_heads[hh])
            qhT_ref[hh * HEAD_DIM:(hh + 1) * HEAD_DIM, rs] = hi.astype(F8)
            qlT_ref[hh * HEAD_DIM:(hh + 1) * HEAD_DIM, rs] = lo.astype(F8)

        kT = proj[:, o_k:o_v].T
        k_heads = _norm_rope_T(kT, kg_ref[:, rs], cosT, sinT, N_KV_HEADS, 1.0)
        pieces = []
        for kh in k_heads:
            hi, lo = _split_f8(kh)
            pieces += [hi, lo, hi, jnp.zeros_like(hi)]
        k_ref[rs, :] = jnp.concatenate(pieces, axis=0).T.astype(F8)

        vT_ref[:, rs] = proj[:, o_v:].T.astype(BF16)


def _mixer_in(x2d, g, w_in, vg, ws, bs, qg, kg, bg0, cosT, sinT, *, seq, tm):
    T, D = x2d.shape
    B = T // seq
    npb = seq // tm
    in_w = w_in.shape[1]
    full = lambda shape: pl.BlockSpec(shape, lambda i: (0,) * len(shape))
    return pl.pallas_call(
        functools.partial(_mixer_in_kernel, rows_per_group=min(256, tm)),
        grid=(T // tm,),
        in_specs=[
            pl.BlockSpec((tm, D), lambda i: (i, 0)),
            full((1, D)),
            full((D, in_w)),
            full((1, GM_WIDTH)),
            full((GM_GROUPS, CHUNK, CHUNK)),
            full((GM_GROUPS, CHUNK, LANES)),
            full((HEAD_DIM, tm)),
            full((HEAD_DIM, tm)),
            full((1, GM_WIDTH)),
            pl.BlockSpec((HEAD_DIM // 2, tm), lambda i: (0, i % npb)),
            pl.BlockSpec((HEAD_DIM // 2, tm), lambda i: (0, i % npb)),
        ],
        out_specs=[
            pl.BlockSpec((tm, GM_WIDTH), lambda i: (i, 0)),
            pl.BlockSpec((None, ATT_WIDTH, tm), lambda i: (i // npb, 0, i % npb)),
            pl.BlockSpec((None, ATT_WIDTH, tm), lambda i: (i // npb, 0, i % npb)),
            pl.BlockSpec((tm, N_KV_HEADS * MXU_DEPTH), lambda i: (i, 0)),
            pl.BlockSpec((None, KV_WIDTH, tm), lambda i: (i // npb, 0, i % npb)),
        ],
        out_shape=[
            jax.ShapeDtypeStruct((T, GM_WIDTH), BF16),
            jax.ShapeDtypeStruct((B, ATT_WIDTH, seq), F8),
            jax.ShapeDtypeStruct((B, ATT_WIDTH, seq), F8),
            jax.ShapeDtypeStruct((T, N_KV_HEADS * MXU_DEPTH), F8),
            jax.ShapeDtypeStruct((B, KV_WIDTH, seq), BF16),
        ],
        scratch_shapes=[pltpu.VMEM((tm, GM_WIDTH), F32)],
        compiler_params=_cparams(("parallel",)),
        name="mixer_in",
    )(x2d, g, w_in, vg, ws, bs, qg, kg, bg0, cosT, sinT)


def _attn_kernel(qhT_ref, qlT_ref, k_ref, vT_ref, o_ref, vext_scr, qx_scr, m_scr, acc_scr, s_scr, cm_scr,
                 *, tq, tk, qb):
    seq = k_ref.shape[0]
    M = Q_PER_KV * tq
    nq = seq // tq
    nchunk = seq // tk

    vext_scr[0:HEAD_DIM, :] = vT_ref[...]
    row = lax.broadcasted_iota(jnp.int32, (PV_ROWS - HEAD_DIM, seq), 0)
    vext_scr[HEAD_DIM:, :] = jnp.where(row == 0, 1.0, 0.0).astype(BF16)

    def load_q(i, slot):
        cols = pl.ds(pl.multiple_of(i * tq, tq), tq)
        qh = jnp.concatenate([qhT_ref[g * HEAD_DIM:(g + 1) * HEAD_DIM, cols] for g in range(Q_PER_KV)], axis=1)
        ql = jnp.concatenate([qlT_ref[g * HEAD_DIM:(g + 1) * HEAD_DIM, cols] for g in range(Q_PER_KV)], axis=1)
        qx_scr[slot] = jnp.concatenate([qh, qh, ql, jnp.zeros_like(qh)], axis=0)

    def step(c, qslot, qk_chunk, do_pv):
        cur = c % 2
        nxt = 1 - cur
        if qk_chunk is not None:
            kn = k_ref[qk_chunk * tk:(qk_chunk + 1) * tk, :]
        if do_pv:
            ve = vext_scr[:, c * tk:(c + 1) * tk]
        for j in range(M // qb):
            cs = slice(j * qb, (j + 1) * qb)
            if qk_chunk is not None:
                s = jnp.dot(kn, qx_scr[qslot, :, cs], preferred_element_type=F32)
                s_scr[nxt, :, cs] = s
                cm_scr[nxt, :, cs] = jnp.max(s, axis=0, keepdims=True)
            if do_pv:
                m_old = m_scr[:, cs]
                m_new = jnp.maximum(m_old, cm_scr[cur, :, cs])
                p = jnp.exp2(s_scr[cur, :, cs] - m_new).astype(BF16)
                alpha = jnp.exp2(m_old - m_new)
                pv = jnp.dot(ve, p, preferred_element_type=F32)
                acc_scr[:, cs] = acc_scr[:, cs] * alpha + pv
                m_scr[:, cs] = m_new

    load_q(0, 0)
    step(-1, 0, 0, False)

    def qblock(i, carry):
        slot = i % 2
        m_scr[...] = jnp.full(m_scr.shape, -jnp.inf, F32)
        acc_scr[...] = jnp.zeros(acc_scr.shape, F32)
        for c in range(nchunk - 1):
            step(c, slot, c + 1, True)
        load_q(jnp.minimum(i + 1, nq - 1), 1 - slot)
        step(nchunk - 1, 1 - slot, 0, True)

        acc = acc_scr[...]
        inv = 1.0 / acc[HEAD_DIM:HEAD_DIM + 1, :]
        o = acc[0:HEAD_DIM, :] * inv
        cols = pl.ds(pl.multiple_of(i * tq, tq), tq)
        for g in range(Q_PER_KV):
            o_ref[g * HEAD_DIM:(g + 1) * HEAD_DIM, cols] = o[:, g * tq:(g + 1) * tq].astype(BF16)
        return carry

    lax.fori_loop(0, nq, qblock, 0)


def _attention(qhT, qlT, k, vT, *, tq, tk, qb):
    B, _, seq = qhT.shape
    gw = Q_PER_KV * HEAD_DIM
    assert (seq // tk) % 2 == 0
    return pl.pallas_call(
        functools.partial(_attn_kernel, tq=tq, tk=tk, qb=qb),
        grid=(B, N_KV_HEADS),
        in_specs=[
            pl.BlockSpec((None, gw, seq), lambda b, kh: (b, kh, 0)),
            pl.BlockSpec((None, gw, seq), lambda b, kh: (b, kh, 0)),
            pl.BlockSpec((None, seq, MXU_DEPTH), lambda b, kh: (b, 0, kh)),
            pl.BlockSpec((None, HEAD_DIM, seq), lambda b, kh: (b, kh, 0)),
        ],
        out_specs=pl.BlockSpec((None, gw, seq), lambda b, kh: (b, kh, 0)),
        out_shape=jax.ShapeDtypeStruct((B, ATT_WIDTH, seq), BF16),
        scratch_shapes=[
            pltpu.VMEM((PV_ROWS, seq), BF16),
            pltpu.VMEM((2, MXU_DEPTH, Q_PER_KV * tq), F8),
            pltpu.VMEM((1, Q_PER_KV * tq), F32),
            pltpu.VMEM((PV_ROWS, Q_PER_KV * tq), F32),
            pltpu.VMEM((2, tk, Q_PER_KV * tq), F32),
            pltpu.VMEM((2, 1, Q_PER_KV * tq), F32),
        ],
        compiler_params=_cparams(("parallel", "parallel")),
        name="attention",
    )(qhT, qlT, k, vT)


def _mem_kv_kernel(mem_ref, g_ref, w_ref, kT_ref, v_ref):
    D = mem_ref.shape[1]
    mn = _rms_rows(mem_ref[...], g_ref[...]).astype(BF16)
    kv = jnp.dot(mn, w_ref[...], preferred_element_type=F32)
    kT_ref[...] = kv[:, :D].T.astype(BF16)
    v_ref[...] = kv[:, D:].astype(BF16)


def _mem_kv(mem, g, w_kv):
    B, M, D = mem.shape
    L = w_kv.shape[0]
    return pl.pallas_call(
        _mem_kv_kernel,
        grid=(L, B),
        in_specs=[
            pl.BlockSpec((None, M, D), lambda l, b: (b, 0, 0)),
            pl.BlockSpec((1, D), lambda l, b: (0, 0)),
            pl.BlockSpec((None, D, 2 * D), lambda l, b: (l, 0, 0)),
        ],
        out_specs=[
            pl.BlockSpec((None, None, D, M), lambda l, b: (l, b, 0, 0)),
            pl.BlockSpec((None, None, M, D), lambda l, b: (l, b, 0, 0)),
        ],
        out_shape=[
            jax.ShapeDtypeStruct((L, B, D, M), BF16),
            jax.ShapeDtypeStruct((L, B, M, D), BF16),
        ],
        compiler_params=_cparams(("parallel", "parallel")),
        name="mem_kv",
    )(mem, g, w_kv)


def _post_kernel(x_ref, gm_ref, atT_ref, bg1_ref, wout_ref, xg_ref, wq_ref, mkT_ref, mv_ref, wo_ref,
                 fg_ref, wr_ref, x2_ref, h3_ref, affT_ref, *, rows_per_group):
    tm, D = x_ref.shape
    xhd = D // X_HEADS
    groups = [slice(r, r + rows_per_group) for r in range(0, tm, rows_per_group)]

    x1 = []
    for rs in groups:
        at = atT_ref[:, rs].astype(F32)
        ms = jnp.mean(at * at, axis=0, keepdims=True)
        atn = (at * lax.rsqrt(ms + EPS) * bg1_ref[:, rs]).astype(BF16)
        y = jnp.dot(gm_ref[rs, :], wout_ref[0:GM_WIDTH, :], preferred_element_type=F32)
        y = y + lax.dot_general(atn, wout_ref[GM_WIDTH:, :], (((0,), (0,)), ((), ())),
                                preferred_element_type=F32)
        x1.append(x_ref[rs, :] + y)

    q2 = []
    for x1g in x1:
        h2 = _rms_rows(x1g, xg_ref[...]).astype(BF16)
        q2.append((jnp.dot(h2, wq_ref[...], preferred_element_type=F32) * (xhd ** -0.5)).astype(BF16))

    scores = {}
    for hh in range(X_HEADS):
        cs = slice(hh * xhd, (hh + 1) * xhd)
        for gi in range(len(groups)):
            scores[hh, gi] = jnp.dot(q2[gi][:, cs], mkT_ref[cs, :], preferred_element_type=F32)
    outs = [[] for _ in groups]
    for hh in range(X_HEADS):
        cs = slice(hh * xhd, (hh + 1) * xhd)
        for gi in range(len(groups)):
            s = scores[hh, gi]
            s = s - jnp.max(s, axis=-1, keepdims=True)
            p = jnp.exp(s)
            p = (p / jnp.sum(p, axis=-1, keepdims=True)).astype(BF16)
            outs[gi].append(jnp.dot(p, mv_ref[:, cs], preferred_element_type=F32).astype(BF16))

    x2 = []
    for gi, rs in enumerate(groups):
        o2 = jnp.concatenate(outs[gi], axis=1)
        x2g = x1[gi] + jnp.dot(o2, wo_ref[...], preferred_element_type=F32)
        x2_ref[rs, :] = x2g
        x2.append(x2g)

    ne = affT_ref.shape[0]
    for gi, rs in enumerate(groups):
        h3 = _rms_rows(x2[gi], fg_ref[...]).astype(BF16)
        h3_ref[rs, :] = h3
        lg = jnp.dot(h3, wr_ref[...], preferred_element_type=F32).T[0:ne, :]
        lg = lg - jnp.max(lg, axis=0, keepdims=True)
        ex = jnp.exp(lg)
        affT_ref[:, rs] = ex / jnp.sum(ex, axis=0, keepdims=True)


def _post(x2d, gm, atT, bg1, w_out, xg, w_q, mkT, mv, w_o, fg, wr_pad, *, n_experts, layer, seq, tm):
    T, D = x2d.shape
    npb = seq // tm
    M = mv.shape[2]
    E = n_experts
    full = lambda shape: pl.BlockSpec(shape, lambda i: (0,) * len(shape))
    return pl.pallas_call(
        functools.partial(_post_kernel, rows_per_group=min(256, tm)),
        grid=(T // tm,),
        in_specs=[
            pl.BlockSpec((tm, D), lambda i: (i, 0)),
            pl.BlockSpec((tm, GM_WIDTH), lambda i: (i, 0)),
            pl.BlockSpec((None, ATT_WIDTH, tm), lambda i: (i // npb, 0, i % npb)),
            full((ATT_WIDTH, tm)),
            full((GM_WIDTH + ATT_WIDTH, D)),
            full((1, D)),
            full((D, D)),
            pl.BlockSpec((None, None, D, M), lambda i: (layer, i // npb, 0, 0)),
            pl.BlockSpec((None, None, M, D), lambda i: (layer, i // npb, 0, 0)),
            full((D, D)),
            full((1, D)),
            full((D, LANES)),
        ],
        out_specs=[
            pl.BlockSpec((tm, D), lambda i: (i, 0)),
            pl.BlockSpec((tm, D), lambda i: (i, 0)),
            pl.BlockSpec((E, tm), lambda i: (0, i)),
        ],
        out_shape=[
            jax.ShapeDtypeStruct((T, D), F32),
            jax.ShapeDtypeStruct((T, D), BF16),
            jax.ShapeDtypeStruct((E, T), F32),
        ],
        compiler_params=_cparams(("parallel",)),
        name="post",
    )(x2d, gm, atT, bg1, w_out, xg, w_q, mkT, mv, w_o, fg, wr_pad)


def _topk_kernel(aff_ref, slot_ref, off_ref, *, cap):
    E, R, L = aff_ref.shape
    a = aff_ref[...]
    bits = lax.bitcast_convert_type(a, jnp.int32)

    def count_ge(t):
        c = jnp.where(bits >= t, 1.0, 0.0)
        return jnp.sum(jnp.sum(c, axis=2, keepdims=True), axis=1, keepdims=True)

    def bis(_, carry):
        lo, hi = carry
        mid = lo + ((hi - lo) >> 1)
        ok = count_ge(mid) >= cap
        return jnp.where(ok, mid, lo), jnp.where(ok, hi, mid)

    lo0 = jnp.zeros((E, 1, 1), jnp.int32)
    hi0 = jnp.full((E, 1, 1), 0x3F800001, jnp.int32)
    thr, _ = lax.fori_loop(0, 31, bis, (lo0, hi0))

    kk = lax.broadcasted_iota(jnp.int32, (L, L), 0)
    nn = lax.broadcasted_iota(jnp.int32, (L, L), 1)
    upper = jnp.where(kk <= nn, 1.0, 0.0).astype(BF16)
    ones = jnp.ones((L, L), BF16)
    rr = lax.broadcasted_iota(jnp.int32, (E * R, E * R), 0)
    cc = lax.broadcasted_iota(jnp.int32, (E * R, E * R), 1)
    lower = jnp.where((rr // R == cc // R) & (cc < rr), 1.0, 0.0).astype(BF16)

    def prefix(xf):
        x2 = xf.reshape(E * R, L)
        xb = x2.astype(BF16)
        incl = jnp.dot(xb, upper, preferred_element_type=F32)
        tot = jnp.dot(xb, ones, preferred_element_type=F32).astype(BF16)
        rowoff = jnp.dot(lower, tot, preferred_element_type=F32)
        return (incl - x2 + rowoff).reshape(E, R, L), rowoff.reshape(E, R, L)

    gt = jnp.where(bits > thr, 1.0, 0.0)
    eq = jnp.where(bits == thr, 1.0, 0.0)
    n_gt = jnp.sum(jnp.sum(gt, axis=2, keepdims=True), axis=1, keepdims=True)
    need = cap - n_gt
    eq_rank, _ = prefix(eq)
    sel = gt + eq * jnp.where(eq_rank < need, 1.0, 0.0)
    pos, rowoff = prefix(sel)
    slot_ref[...] = jnp.where(sel > 0.0, pos, -1.0).astype(jnp.int32)
    off_ref[...] = rowoff.astype(jnp.int32)


def _topk(aff4, *, cap):
    E, B, R, L = aff4.shape
    return pl.pallas_call(
        functools.partial(_topk_kernel, cap=cap),
        grid=(B,),
        in_specs=[pl.BlockSpec((E, None, R, L), lambda b: (0, b, 0, 0))],
        out_specs=[
            pl.BlockSpec((None, E, R, L), lambda b: (b, 0, 0, 0)),
            pl.BlockSpec((None, E, R, L), lambda b: (b, 0, 0, 0)),
        ],
        out_shape=[
            jax.ShapeDtypeStruct((B, E, R, L), jnp.int32),
            jax.ShapeDtypeStruct((B, E, R, L), jnp.int32),
        ],
        compiler_params=_cparams(("parallel",)),
        name="topk",
    )(aff4)


def _gather_kernel(cnt_ref, slot_ref, gate_ref, h_ref, xs_ref, gc_ref, *, win):
    b = pl.program_id(0)
    eg = pl.program_id(1)
    group, cap, D = xs_ref.shape
    ne = pl.num_programs(1) * group
    nblk = slot_ref.shape[1]
    r_iota = lax.broadcasted_iota(jnp.int32, (win, TOK_BLOCK), 0)

    xs_ref[...] = jnp.zeros(xs_ref.shape, BF16)
    gc_ref[...] = jnp.zeros(gc_ref.shape, F32)

    def window(g, j):
        idx = (b * ne + eg * group + g) * (nblk + 1) + j
        s0 = cnt_ref[idx]
        s1 = cnt_ref[idx + 1]
        a0 = (s0 // 16) * 16
        return a0, s1

    def hits(g, j, lo, sa):
        srow = slot_ref[g, j]
        rel = jnp.where(srow >= lo, srow - sa, -1)
        return rel == r_iota

    def add_rows(g, j, sa, hit, part):
        rows = pl.ds(sa, win)
        xs_ref[g, rows, :] = xs_ref[g, rows, :] + part.astype(BF16)
        gc_ref[g, rows, :] += jnp.sum(jnp.where(hit, gate_ref[g, j], 0.0), axis=1, keepdims=True)

    extra = jnp.int32(0)
    for j in range(nblk):
        hb = h_ref[j * TOK_BLOCK:(j + 1) * TOK_BLOCK, :]
        masks, starts = [], []
        for g in range(group):
            a0, s1 = window(g, j)
            sa = pl.multiple_of(jnp.minimum(a0, cap - win), 16)
            extra = jnp.maximum(extra, s1 - (a0 + win))
            masks.append(hits(g, j, a0, sa))
            starts.append(sa)
        lhs = jnp.concatenate([jnp.where(m, 1.0, 0.0).astype(BF16) for m in masks], axis=0)
        res = jnp.dot(lhs, hb, preferred_element_type=F32)
        for g in range(group):
            add_rows(g, j, starts[g], masks[g], res[g * win:(g + 1) * win, :])

    @pl.when(extra > 0)
    def _():
        def blk(j, carry):
            hb = h_ref[pl.ds(pl.multiple_of(j * TOK_BLOCK, TOK_BLOCK), TOK_BLOCK), :]
            for g in range(group):
                a0, s1 = window(g, j)
                nw = (s1 - a0 + win - 1) // win

                def wbody(w, c2, g=g, a0=a0):
                    lo = a0 + w * win
                    sa = pl.multiple_of(jnp.minimum(lo, cap - win), 16)
                    hit = hits(g, j, lo, sa)
                    part = jnp.dot(jnp.where(hit, 1.0, 0.0).astype(BF16), hb, preferred_element_type=F32)
                    add_rows(g, j, sa, hit, part)
                    return c2

                lax.fori_loop(1, nw, wbody, 0)
            return carry

        lax.fori_loop(0, nblk, blk, 0)


def _gather(cnt, slots5, gates5, h3d, *, cap, win):
    B, seq, D = h3d.shape
    E = slots5.shape[1]
    nblk = seq // TOK_BLOCK
    group = MXU_DEPTH // win
    assert E % group == 0 and cap % win == 0
    gs = pltpu.PrefetchScalarGridSpec(
        num_scalar_prefetch=1,
        grid=(B, E // group),
        in_specs=[
            pl.BlockSpec((None, group, nblk, 1, TOK_BLOCK), lambda b, e, c: (b, e, 0, 0, 0)),
            pl.BlockSpec((group, None, nblk, 1, TOK_BLOCK), lambda b, e, c: (e, b, 0, 0, 0)),
            pl.BlockSpec((None, seq, D), lambda b, e, c: (b, 0, 0), pipeline_mode=pl.Buffered(1)),
        ],
        out_specs=[
            pl.BlockSpec((None, group, cap, D), lambda b, e, c: (b, e, 0, 0)),
            pl.BlockSpec((None, group, cap, 1), lambda b, e, c: (b, e, 0, 0)),
        ],
    )
    return pl.pallas_call(
        functools.partial(_gather_kernel, win=win),
        grid_spec=gs,
        out_shape=[
            jax.ShapeDtypeStruct((B, E, cap, D), BF16),
            jax.ShapeDtypeStruct((B, E, cap, 1), F32),
        ],
        compiler_params=_cparams(("parallel", "arbitrary")),
        name="gather",
    )(cnt, slots5, gates5, h3d)


def _ffn_kernel(xs_ref, gc_ref, wg_ref, wu_ref, wd_ref, y_ref, *, f_chunk):
    xs = xs_ref[...]
    y = None
    for f0 in range(0, wg_ref.shape[1], f_chunk):
        fs = slice(f0, f0 + f_chunk)
        a = jnp.dot(xs, wg_ref[:, fs].astype(BF16), preferred_element_type=F32)
        u = jnp.dot(xs, wu_ref[:, fs].astype(BF16), preferred_element_type=F32)
        hmid = (a * jax.nn.sigmoid(a) * u).astype(BF16)
        part = jnp.dot(hmid, wd_ref[fs, :].astype(BF16), preferred_element_type=F32)
        y = part if y is None else y + part
    y_ref[...] = (y * gc_ref[...]).astype(BF16)


def _ffn(xs, gc, w_gate, w_up, w_down, *, layer):
    B, E, cap, D = xs.shape
    Fd = w_gate.shape[-1]
    return pl.pallas_call(
        functools.partial(_ffn_kernel, f_chunk=min(512, Fd)),
        grid=(E, B),
        in_specs=[
            pl.BlockSpec((None, None, cap, D), lambda e, b: (b, e, 0, 0)),
            pl.BlockSpec((None, None, cap, 1), lambda e, b: (b, e, 0, 0)),
            pl.BlockSpec((None, None, D, Fd), lambda e, b: (layer, e, 0, 0)),
            pl.BlockSpec((None, None, D, Fd), lambda e, b: (layer, e, 0, 0)),
            pl.BlockSpec((None, None, Fd, D), lambda e, b: (layer, e, 0, 0)),
        ],
        out_specs=pl.BlockSpec((None, None, cap, D), lambda e, b: (b, e, 0, 0)),
        out_shape=jax.ShapeDtypeStruct((B, E, cap, D), BF16),
        compiler_params=_cparams(("parallel", "parallel")),
        name="ffn",
    )(xs, gc, w_gate, w_up, w_down)


def _scatter_kernel(cnt_ref, slot_ref, y_ref, x_ref, fg_ref, o_ref, *, win, final_norm):
    b = pl.program_id(0)
    sb = pl.program_id(1)
    ne, cap, D = y_ref.shape
    nloc = slot_ref.shape[1]
    nblk = nloc * pl.num_programs(1)
    r_iota = lax.broadcasted_iota(jnp.int32, (win, TOK_BLOCK), 0)
    contract0 = (((0,), (0,)), ((), ()))

    def onehot_t(srow, lo, sa):
        rel = jnp.where(srow >= lo, srow - sa, -1)
        return jnp.where(rel == r_iota, 1.0, 0.0).astype(BF16)

    def window(e, j):
        idx = (b * ne + e) * (nblk + 1) + j
        s0 = cnt_ref[idx]
        s1 = cnt_ref[idx + 1]
        a0 = (s0 // 16) * 16
        sa = pl.multiple_of(jnp.minimum(a0, cap - win), 16)
        return a0, sa, s1 - (a0 + win)

    extra = jnp.int32(0)
    per_dot = MXU_DEPTH // win
    for t in range(nloc):
        j = sb * nloc + t
        rows = slice(t * TOK_BLOCK, (t + 1) * TOK_BLOCK)
        acc = x_ref[rows, :]
        for e0 in range(0, ne, per_dot):
            gs, ys = [], []
            for e in range(e0, e0 + per_dot):
                a0, sa, over = window(e, j)
                extra = jnp.maximum(extra, over)
                gs.append(onehot_t(slot_ref[e, t], a0, sa))
                ys.append(y_ref[e, pl.ds(sa, win), :])
            acc = acc + lax.dot_general(jnp.concatenate(gs, axis=0), jnp.concatenate(ys, axis=0),
                                        contract0, preferred_element_type=F32)
        o_ref[rows, :] = acc

    @pl.when(extra > 0)
    def _():
        for t in range(nloc):
            j = sb * nloc + t
            rows = slice(t * TOK_BLOCK, (t + 1) * TOK_BLOCK)
            for e in range(ne):
                idx = (b * ne + e) * (nblk + 1) + j
                s0 = cnt_ref[idx]
                s1 = cnt_ref[idx + 1]
                a0 = (s0 // 16) * 16
                nw = (s1 - a0 + win - 1) // win

                def wbody(w, carry):
                    lo = a0 + w * win
                    sa = pl.multiple_of(jnp.minimum(lo, cap - win), 16)
                    g = onehot_t(slot_ref[e, t], lo, sa)
                    o_ref[rows, :] += lax.dot_general(g, y_ref[e, pl.ds(sa, win), :], contract0,
                                                      preferred_element_type=F32)
                    return carry

                lax.fori_loop(1, nw, wbody, 0)

    if final_norm:
        o_ref[...] = _rms_rows(o_ref[...], fg_ref[...])


def _scatter(cnt, slots5, y, x3d, fg, *, win, sb_rows, final_norm):
    B, seq, D = x3d.shape
    E, cap = y.shape[1], y.shape[2]
    nloc = sb_rows // TOK_BLOCK
    assert MXU_DEPTH % win == 0 and E % (MXU_DEPTH // win) == 0
    gs = pltpu.PrefetchScalarGridSpec(
        num_scalar_prefetch=1,
        grid=(B, seq // sb_rows),
        in_specs=[
            pl.BlockSpec((None, E, nloc, 1, TOK_BLOCK), lambda b, s, c: (b, 0, s, 0, 0)),
            pl.BlockSpec((None, E, cap, D), lambda b, s, c: (b, 0, 0, 0), pipeline_mode=pl.Buffered(1)),
            pl.BlockSpec((None, sb_rows, D), lambda b, s, c: (b, s, 0)),
            pl.BlockSpec((1, D), lambda b, s, c: (0, 0)),
        ],
        out_specs=pl.BlockSpec((None, sb_rows, D), lambda b, s, c: (b, s, 0)),
    )
    return pl.pallas_call(
        functools.partial(_scatter_kernel, win=win, final_norm=final_norm),
        grid_spec=gs,
        out_shape=jax.ShapeDtypeStruct((B, seq, D), F32),
        compiler_params=_cparams(("parallel", "arbitrary")),
        name="scatter",
    )(cnt, slots5, y, x3d, fg)


def _rope_tables_T(seq):
    rows = seq // GRID_W
    row_id = jnp.repeat(jnp.arange(rows, dtype=F32), GRID_W)
    col_id = jnp.tile(jnp.arange(GRID_W, dtype=F32), rows)
    n_pairs = HEAD_DIM // 4
    freqs = jnp.exp(-math.log(ROPE_THETA) * jnp.arange(n_pairs, dtype=F32) / n_pairs)
    ang = jnp.concatenate([freqs[:, None] * row_id[None, :], freqs[:, None] * col_id[None, :]], axis=0)
    return jnp.cos(ang), jnp.sin(ang)


def _head_perm(n_heads):
    base = jnp.concatenate([jnp.arange(0, HEAD_DIM, 2), jnp.arange(1, HEAD_DIM, 2)])
    return (jnp.arange(n_heads)[:, None] * HEAD_DIM + base[None, :]).reshape(-1)


def kernel(x, mem, mix_norm_g, w_in, gm_v_norm_g, gm_w_s, gm_b_s, q_norm_g, k_norm_g, branch_norm_g, w_out,
           xattn_norm_g, mem_norm_g, xattn_w_q, xattn_w_kv, xattn_w_o, ffn_norm_g, w_router, w_gate, w_up,
           w_down, final_norm_g):
    B, seq, D = x.shape
    L = w_in.shape[0]
    E = w_router.shape[-1]
    T = B * seq
    cap = EC_FACTOR * seq // E
    tm = min(1024, seq)
    tm_in = min(2048, seq)
    tq = min(256, seq)
    tk = min(2048, seq // 2)
    win = min(64, cap)
    sb_rows = min(1024, seq)
    nblk = seq // TOK_BLOCK

    cosT, sinT = _rope_tables_T(seq)
    o_q = 2 * GM_WIDTH
    o_k = o_q + ATT_WIDTH
    o_v = o_k + KV_WIDTH
    cols = jnp.concatenate([jnp.arange(o_q), o_q + _head_perm(N_Q_HEADS), o_k + _head_perm(N_KV_HEADS),
                            jnp.arange(o_v, o_v + KV_WIDTH)])
    hp = _head_perm(1)

    mkT_all, mv_all = _mem_kv(mem, mem_norm_g.reshape(1, D), xattn_w_kv.astype(BF16))

    x2d = x.reshape(T, D)
    out = None
    for l in range(L):
        w_in_l = w_in[l][:, cols].astype(BF16)
        bs = jnp.broadcast_to(gm_b_s[l][:, :, None], (GM_GROUPS, CHUNK, LANES))
        qg = jnp.broadcast_to(q_norm_g[l][hp][:, None], (HEAD_DIM, tm_in))
        kg = jnp.broadcast_to(k_norm_g[l][hp][:, None], (HEAD_DIM, tm_in))
        gm, qhT, qlT, k8, vT = _mixer_in(
            x2d, mix_norm_g[l].reshape(1, D), w_in_l, gm_v_norm_g[l].reshape(1, GM_WIDTH),
            gm_w_s[l].astype(BF16), bs, qg, kg, branch_norm_g[l, 0].reshape(1, GM_WIDTH), cosT, sinT,
            seq=seq, tm=tm_in)
        atT = _attention(qhT, qlT, k8.reshape(B, seq, N_KV_HEADS * MXU_DEPTH), vT,
                         tq=tq, tk=tk, qb=min(256, tq))

        bg1 = jnp.broadcast_to(branch_norm_g[l, 1][:, None], (ATT_WIDTH, tm))
        wr_pad = jnp.pad(w_router[l].astype(BF16), ((0, 0), (0, LANES - E)))
        x2, h3, affT = _post(
            x2d, gm, atT, bg1, w_out[l].astype(BF16), xattn_norm_g[l].reshape(1, D),
            xattn_w_q[l].astype(BF16), mkT_all, mv_all, xattn_w_o[l].astype(BF16),
            ffn_norm_g[l].reshape(1, D), wr_pad, n_experts=E, layer=l, seq=seq, tm=tm)

        slots, offs = _topk(affT.reshape(E, B, seq // LANES, LANES), cap=cap)
        cnt = offs[:, :, ::TOK_BLOCK // LANES, 0]
        cnt = jnp.concatenate([cnt, jnp.full((B, E, 1), cap, jnp.int32)], axis=-1).reshape(-1)
        slots5 = slots.reshape(B, E, nblk, 1, TOK_BLOCK)
        gates5 = affT.reshape(E, B, nblk, 1, TOK_BLOCK)

        xs, gc = _gather(cnt, slots5, gates5, h3.reshape(B, seq, D), cap=cap, win=win)
        y = _ffn(xs, gc, w_gate, w_up, w_down, layer=l)
        out = _scatter(cnt, slots5, y, x2.reshape(B, seq, D), final_norm_g.reshape(1, D),
                       win=win, sb_rows=sb_rows, final_norm=(l == L - 1))
        x2d = out.reshape(T, D)
    return out
```

```python
import functools
import math

import jax
import jax.numpy as jnp
from jax import lax
from jax.experimental import pallas as pl
from jax.experimental.pallas import tpu as pltpu

F32 = jnp.float32
BF16 = jnp.bfloat16
F8 = jnp.float8_e4m3fn
F8_MAX = 448.0

EPS = 1e-6
CHUNK = 128
GM_GROUPS = 4
GM_WIDTH = 512
HEAD_DIM = 64
N_Q_HEADS = 8
N_KV_HEADS = 2
Q_PER_KV = N_Q_HEADS // N_KV_HEADS
ATT_WIDTH = N_Q_HEADS * HEAD_DIM
KV_WIDTH = N_KV_HEADS * HEAD_DIM
ROPE_THETA = 10000.0
GRID_W = 64
X_HEADS = 4
N_EXPERTS = 16
EC_FACTOR = 2

LANES = 128
MXU_DEPTH = 256
PV_ROWS = HEAD_DIM + 16
VMEM_LIMIT = 56 * 1024 * 1024

TOK_BLOCK = 256
LOG2E = 1.4426950408889634


def _cparams(sem, vmem=VMEM_LIMIT, **kw):
    return pltpu.CompilerParams(dimension_semantics=sem, vmem_limit_bytes=vmem, **kw)


def _rms_rows(x, g):
    ms = jnp.mean(x * x, axis=-1, keepdims=True)
    return x * lax.rsqrt(ms + EPS) * g


def _gelu_tanh(x):
    c = math.sqrt(2.0 / math.pi)
    return 0.5 * x * (1.0 + jnp.tanh(c * (x + 0.044715 * (x * x * x))))


def _norm_rope_T(xT, gcol, cosT, sinT, n_heads, scale):
    half = HEAD_DIM // 2
    outs = []
    for h in range(n_heads):
        blk = xT[h * HEAD_DIM:(h + 1) * HEAD_DIM, :]
        ms = jnp.mean(blk * blk, axis=0, keepdims=True)
        n = blk * lax.rsqrt(ms + EPS) * gcol
        e = n[:half, :]
        o = n[half:, :]
        re = (e * cosT - o * sinT) * scale
        ro = (e * sinT + o * cosT) * scale
        outs.append(jnp.concatenate([re, ro], axis=0))
    return outs


def _split_f8(x):
    x = jnp.clip(x, -F8_MAX, F8_MAX)
    hi = x.astype(F8).astype(F32)
    lo = (x - hi).astype(F8).astype(F32)
    return hi, lo


def _mixer_in_kernel(x_ref, g_ref, w_ref, vg_ref, ws_ref, bs_ref, qg_ref, kg_ref, bg0_ref,
                     cos_ref, sin_ref, gm_ref, qhT_ref, qlT_ref, k_ref, vT_ref, gm_scr, *, rows_per_group):
    tm = x_ref.shape[0]
    o_q = 2 * GM_WIDTH
    o_k = o_q + ATT_WIDTH
    o_v = o_k + KV_WIDTH
    groups = [slice(r, r + rows_per_group) for r in range(0, tm, rows_per_group)]

    projs = []
    for rs in groups:
        h = _rms_rows(x_ref[rs, :], g_ref[...]).astype(BF16)
        projs.append(jnp.dot(h, w_ref[...], preferred_element_type=F32))

    for rs, proj in zip(groups, projs):
        u = _gelu_tanh(proj[:, :GM_WIDTH])
        v = _gelu_tanh(proj[:, GM_WIDTH:2 * GM_WIDTH])
        vn = _rms_rows(v, vg_ref[...]).astype(BF16)
        n_chunks = rows_per_group // CHUNK
        for g in range(GM_GROUPS):
            cs = slice(g * LANES, (g + 1) * LANES)
            vcat = jnp.concatenate([vn[c * CHUNK:(c + 1) * CHUNK, cs] for c in range(n_chunks)], axis=1)
            mixed = jnp.dot(ws_ref[g], vcat, preferred_element_type=F32)
            for c in range(n_chunks):
                cr = slice(c * CHUNK, (c + 1) * CHUNK)
                sr = slice(rs.start + c * CHUNK, rs.start + (c + 1) * CHUNK)
                gm_scr[sr, cs] = u[cr, cs] * (mixed[:, c * LANES:(c + 1) * LANES] + bs_ref[g])
        gm_ref[rs, :] = _rms_rows(gm_scr[rs, :], bg0_ref[...]).astype(BF16)

        cosT = cos_ref[:, rs]
        sinT = sin_ref[:, rs]
        qT = proj[:, o_q:o_k].T
        q_heads = _norm_rope_T(qT, qg_ref[:, rs], cosT, sinT, N_Q_HEADS, (HEAD_DIM ** -0.5) * LOG2E)
        for hh in range(N_Q_HEADS):
            hi, lo = _split_f8(q_heads[hh])
            qhT_ref[hh * HEAD_DIM:(hh + 1) * HEAD_DIM, rs] = hi.astype(F8)
            qlT_ref[hh * HEAD_DIM:(hh + 1) * HEAD_DIM, rs] = lo.astype(F8)

        kT = proj[:, o_k:o_v].T
        k_heads = _norm_rope_T(kT, kg_ref[:, rs], cosT, sinT, N_KV_HEADS, 1.0)
        pieces = []
        for kh in k_heads:
            hi, lo = _split_f8(kh)
            pieces += [hi, lo, hi, jnp.zeros_like(hi)]
        k_ref[rs, :] = jnp.concatenate(pieces, axis=0).T.astype(F8)

        vT_ref[:, rs] = proj[:, o_v:].T.astype(BF16)


def _mixer_in(x2d, g, w_in, vg, ws, bs, qg, kg, bg0, cosT, sinT, *, seq, tm):
    T, D = x2d.shape
    B = T // seq
    npb = seq // tm
    in_w = w_in.shape[1]
    full = lambda shape: pl.BlockSpec(shape, lambda i: (0,) * len(shape))
    return pl.pallas_call(
        functools.partial(_mixer_in_kernel, rows_per_group=min(256, tm)),
        grid=(T // tm,),
        in_specs=[
            pl.BlockSpec((tm, D), lambda i: (i, 0)),
            full((1, D)),
            full((D, in_w)),
            full((1, GM_WIDTH)),
            full((GM_GROUPS, CHUNK, CHUNK)),
            full((GM_GROUPS, CHUNK, LANES)),
            full((HEAD_DIM, tm)),
            full((HEAD_DIM, tm)),
            full((1, GM_WIDTH)),
            pl.BlockSpec((HEAD_DIM // 2, tm), lambda i: (0, i % npb)),
            pl.BlockSpec((HEAD_DIM // 2, tm), lambda i: (0, i % npb)),
        ],
        out_specs=[
            pl.BlockSpec((tm, GM_WIDTH), lambda i: (i, 0)),
            pl.BlockSpec((None, ATT_WIDTH, tm), lambda i: (i // npb, 0, i % npb)),
            pl.BlockSpec((None, ATT_WIDTH, tm), lambda i: (i // npb, 0, i % npb)),
            pl.BlockSpec((tm, N_KV_HEADS * MXU_DEPTH), lambda i: (i, 0)),
            pl.BlockSpec((None, KV_WIDTH, tm), lambda i: (i // npb, 0, i % npb)),
        ],
        out_shape=[
            jax.ShapeDtypeStruct((T, GM_WIDTH), BF16),
            jax.ShapeDtypeStruct((B, ATT_WIDTH, seq), F8),
            jax.ShapeDtypeStruct((B, ATT_WIDTH, seq), F8),
            jax.ShapeDtypeStruct((T, N_KV_HEADS * MXU_DEPTH), F8),
            jax.ShapeDtypeStruct((B, KV_WIDTH, seq), BF16),
        ],
        scratch_shapes=[pltpu.VMEM((tm, GM_WIDTH), F32)],
        compiler_params=_cparams(("parallel",)),
        name="mixer_in",
    )(x2d, g, w_in, vg, ws, bs, qg, kg, bg0, cosT, sinT)


def _attn_kernel(qhT_ref, qlT_ref, k_ref, vT_ref, o_ref, vext_scr, qx_scr, m_scr, acc_scr, s_scr, cm_scr,
                 *, tq, tk, qb):
    seq = k_ref.shape[0]
    M = Q_PER_KV * tq
    nq = seq // tq
    nchunk = seq // tk

    vext_scr[0:HEAD_DIM, :] = vT_ref[...]
    row = lax.broadcasted_iota(jnp.int32, (PV_ROWS - HEAD_DIM, seq), 0)
    vext_scr[HEAD_DIM:, :] = jnp.where(row == 0, 1.0, 0.0).astype(BF16)

    def load_q(i, slot):
        cols = pl.ds(pl.multiple_of(i * tq, tq), tq)
        qh = jnp.concatenate([qhT_ref[g * HEAD_DIM:(g + 1) * HEAD_DIM, cols] for g in range(Q_PER_KV)], axis=1)
        ql = jnp.concatenate([qlT_ref[g * HEAD_DIM:(g + 1) * HEAD_DIM, cols] for g in range(Q_PER_KV)], axis=1)
        qx_scr[slot] = jnp.concatenate([qh, qh, ql, jnp.zeros_like(qh)], axis=0)

    def step(c, qslot, qk_chunk, do_pv):
        cur = c % 2
        nxt = 1 - cur
        if qk_chunk is not None:
            kn = k_ref[qk_chunk * tk:(qk_chunk + 1) * tk, :]
        if do_pv:
            ve = vext_scr[:, c * tk:(c + 1) * tk]
        for j in range(M // qb):
            cs = slice(j * qb, (j + 1) * qb)
            if qk_chunk is not None:
                s = jnp.dot(kn, qx_scr[qslot, :, cs], preferred_element_type=F32)
                s_scr[nxt, :, cs] = s
                cm_scr[nxt, :, cs] = jnp.max(s, axis=0, keepdims=True)
            if do_pv:
                m_old = m_scr[:, cs]
                m_new = jnp.maximum(m_old, cm_scr[cur, :, cs])
                p = jnp.exp2(s_scr[cur, :, cs] - m_new).astype(BF16)
                alpha = jnp.exp2(m_old - m_new)
                pv = jnp.dot(ve, p, preferred_element_type=F32)
                acc_scr[:, cs] = acc_scr[:, cs] * alpha + pv
                m_scr[:, cs] = m_new

    load_q(0, 0)
    step(-1, 0, 0, False)

    def qblock(i, carry):
        slot = i % 2
        m_scr[...] = jnp.full(m_scr.shape, -jnp.inf, F32)
        acc_scr[...] = jnp.zeros(acc_scr.shape, F32)
        for c in range(nchunk - 1):
            step(c, slot, c + 1, True)
        load_q(jnp.minimum(i + 1, nq - 1), 1 - slot)
        step(nchunk - 1, 1 - slot, 0, True)

        acc = acc_scr[...]
        inv = 1.0 / acc[HEAD_DIM:HEAD_DIM + 1, :]
        o = acc[0:HEAD_DIM, :] * inv
        cols = pl.ds(pl.multiple_of(i * tq, tq), tq)
        for g in range(Q_PER_KV):
            o_ref[g * HEAD_DIM:(g + 1) * HEAD_DIM, cols] = o[:, g * tq:(g + 1) * tq].astype(BF16)
        return carry

    lax.fori_loop(0, nq, qblock, 0)


def _attention(qhT, qlT, k, vT, *, tq, tk, qb):
    B, _, seq = qhT.shape
    gw = Q_PER_KV * HEAD_DIM
    assert (seq // tk) % 2 == 0
    return pl.pallas_call(
        functools.partial(_attn_kernel, tq=tq, tk=tk, qb=qb),
        grid=(B, N_KV_HEADS),
        in_specs=[
            pl.BlockSpec((None, gw, seq), lambda b, kh: (b, kh, 0)),
            pl.BlockSpec((None, gw, seq), lambda b, kh: (b, kh, 0)),
            pl.BlockSpec((None, seq, MXU_DEPTH), lambda b, kh: (b, 0, kh)),
            pl.BlockSpec((None, HEAD_DIM, seq), lambda b, kh: (b, kh, 0)),
        ],
        out_specs=pl.BlockSpec((None, gw, seq), lambda b, kh: (b, kh, 0)),
        out_shape=jax.ShapeDtypeStruct((B, ATT_WIDTH, seq), BF16),
        scratch_shapes=[
            pltpu.VMEM((PV_ROWS, seq), BF16),
            pltpu.VMEM((2, MXU_DEPTH, Q_PER_KV * tq), F8),
            pltpu.VMEM((1, Q_PER_KV * tq), F32),
            pltpu.VMEM((PV_ROWS, Q_PER_KV * tq), F32),
            pltpu.VMEM((2, tk, Q_PER_KV * tq), F32),
            pltpu.VMEM((2, 1, Q_PER_KV * tq), F32),
        ],
        compiler_params=_cparams(("parallel", "parallel")),
        name="attention",
    )(qhT, qlT, k, vT)


def _mem_kv_kernel(mem_ref, g_ref, w_ref, kT_ref, v_ref):
    D = mem_ref.shape[1]
    mn = _rms_rows(mem_ref[...], g_ref[...]).astype(BF16)
    kv = jnp.dot(mn, w_ref[...], preferred_element_type=F32)
    kT_ref[...] = kv[:, :D].T.astype(BF16)
    v_ref[...] = kv[:, D:].astype(BF16)


def _mem_kv(mem, g, w_kv):
    B, M, D = mem.shape
    L = w_kv.shape[0]
    return pl.pallas_call(
        _mem_kv_kernel,
        grid=(L, B),
        in_specs=[
            pl.BlockSpec((None, M, D), lambda l, b: (b, 0, 0)),
            pl.BlockSpec((1, D), lambda l, b: (0, 0)),
            pl.BlockSpec((None, D, 2 * D), lambda l, b: (l, 0, 0)),
        ],
        out_specs=[
            pl.BlockSpec((None, None, D, M), lambda l, b: (l, b, 0, 0)),
            pl.BlockSpec((None, None, M, D), lambda l, b: (l, b, 0, 0)),
        ],
        out_shape=[
            jax.ShapeDtypeStruct((L, B, D, M), BF16),
            jax.ShapeDtypeStruct((L, B, M, D), BF16),
        ],
        compiler_params=_cparams(("parallel", "parallel")),
        name="mem_kv",
    )(mem, g, w_kv)


def _post_kernel(x_ref, gm_ref, atT_ref, bg1_ref, wout_ref, xg_ref, wq_ref, mkT_ref, mv_ref, wo_ref,
                 fg_ref, wr_ref, x2_ref, h3_ref, affT_ref, *, rows_per_group):
    tm, D = x_ref.shape
    xhd = D // X_HEADS
    groups = [slice(r, r + rows_per_group) for r in range(0, tm, rows_per_group)]

    x1 = []
    for rs in groups:
        at = atT_ref[:, rs].astype(F32)
        ms = jnp.mean(at * at, axis=0, keepdims=True)
        atn = (at * lax.rsqrt(ms + EPS) * bg1_ref[:, rs]).astype(BF16)
        y = jnp.dot(gm_ref[rs, :], wout_ref[0:GM_WIDTH, :], preferred_element_type=F32)
        y = y + lax.dot_general(atn, wout_ref[GM_WIDTH:, :], (((0,), (0,)), ((), ())),
                                preferred_element_type=F32)
        x1.append(x_ref[rs, :] + y)

    q2 = []
    for x1g in x1:
        h2 = _rms_rows(x1g, xg_ref[...]).astype(BF16)
        q2.append((jnp.dot(h2, wq_ref[...], preferred_element_type=F32) * (xhd ** -0.5)).astype(BF16))

    scores = {}
    for hh in range(X_HEADS):
        cs = slice(hh * xhd, (hh + 1) * xhd)
        for gi in range(len(groups)):
            scores[hh, gi] = jnp.dot(q2[gi][:, cs], mkT_ref[cs, :], preferred_element_type=F32)
    outs = [[] for _ in groups]
    for hh in range(X_HEADS):
        cs = slice(hh * xhd, (hh + 1) * xhd)
        for gi in range(len(groups)):
            s = scores[hh, gi]
            s = s - jnp.max(s, axis=-1, keepdims=True)
            p = jnp.exp(s)
            p = (p / jnp.sum(p, axis=-1, keepdims=True)).astype(BF16)
            outs[gi].append(jnp.dot(p, mv_ref[:, cs], preferred_element_type=F32).astype(BF16))

    x2 = []
    for gi, rs in enumerate(groups):
        o2 = jnp.concatenate(outs[gi], axis=1)
        x2g = x1[gi] + jnp.dot(o2, wo_ref[...], preferred_element_type=F32)
        x2_ref[rs, :] = x2g
        x2.append(x2g)

    ne = affT_ref.shape[0]
    for gi, rs in enumerate(groups):
        h3 = _rms_rows(x2[gi], fg_ref[...]).astype(BF16)
        h3_ref[rs, :] = h3
        lg = jnp.dot(h3, wr_ref[...], preferred_element_type=F32).T[0:ne, :]
        lg = lg - jnp.max(lg, axis=0, keepdims=True)
        ex = jnp.exp(lg)
        affT_ref[:, rs] = ex / jnp.sum(ex, axis=0, keepdims=True)


def _post(x2d, gm, atT, bg1, w_out, xg, w_q, mkT, mv, w_o, fg, wr_pad, *, n_experts, layer, seq, tm):
    T, D = x2d.shape
    npb = seq // tm
    M = mv.shape[2]
    E = n_experts
    full = lambda shape: pl.BlockSpec(shape, lambda i: (0,) * len(shape))
    return pl.pallas_call(
        functools.partial(_post_kernel, rows_per_group=min(256, tm)),
        grid=(T // tm,),
        in_specs=[
            pl.BlockSpec((tm, D), lambda i: (i, 0)),
            pl.BlockSpec((tm, GM_WIDTH), lambda i: (i, 0)),
            pl.BlockSpec((None, ATT_WIDTH, tm), lambda i: (i // npb, 0, i % npb)),
            full((ATT_WIDTH, tm)),
            full((GM_WIDTH + ATT_WIDTH, D)),
            full((1, D)),
            full((D, D)),
            pl.BlockSpec((None, None, D, M), lambda i: (layer, i // npb, 0, 0)),
            pl.BlockSpec((None, None, M, D), lambda i: (layer, i // npb, 0, 0)),
            full((D, D)),
            full((1, D)),
            full((D, LANES)),
        ],
        out_specs=[
            pl.BlockSpec((tm, D), lambda i: (i, 0)),
            pl.BlockSpec((tm, D), lambda i: (i, 0)),
            pl.BlockSpec((E, tm), lambda i: (0, i)),
        ],
        out_shape=[
            jax.ShapeDtypeStruct((T, D), F32),
            jax.ShapeDtypeStruct((T, D), BF16),
            jax.ShapeDtypeStruct((E, T), F32),
        ],
        compiler_params=_cparams(("parallel",)),
        name="post",
    )(x2d, gm, atT, bg1, w_out, xg, w_q, mkT, mv, w_o, fg, wr_pad)


def _topk_kernel(aff_ref, slot_ref, off_ref, *, cap):
    E, R, L = aff_ref.shape
    a = aff_ref[...]
    bits = lax.bitcast_convert_type(a, jnp.int32)

    def count_ge(t):
        c = jnp.where(bits >= t, 1.0, 0.0)
        return jnp.sum(jnp.sum(c, axis=2, keepdims=True), axis=1, keepdims=True)

    def bis(_, carry):
        lo, hi = carry
        mid = lo + ((hi - lo) >> 1)
        ok = count_ge(mid) >= cap
        return jnp.where(ok, mid, lo), jnp.where(ok, hi, mid)

    lo0 = jnp.zeros((E, 1, 1), jnp.int32)
    hi0 = jnp.full((E, 1, 1), 0x3F800001, jnp.int32)
    thr, _ = lax.fori_loop(0, 31, bis, (lo0, hi0))

    kk = lax.broadcasted_iota(jnp.int32, (L, L), 0)
    nn = lax.broadcasted_iota(jnp.int32, (L, L), 1)
    upper = jnp.where(kk <= nn, 1.0, 0.0).astype(BF16)
    ones = jnp.ones((L, L), BF16)
    rr = lax.broadcasted_iota(jnp.int32, (E * R, E * R), 0)
    cc = lax.broadcasted_iota(jnp.int32, (E * R, E * R), 1)
    lower = jnp.where((rr // R == cc // R) & (cc < rr), 1.0, 0.0).astype(BF16)

    def prefix(xf):
        x2 = xf.reshape(E * R, L)
        xb = x2.astype(BF16)
        incl = jnp.dot(xb, upper, preferred_element_type=F32)
        tot = jnp.dot(xb, ones, preferred_element_type=F32).astype(BF16)
        rowoff = jnp.dot(lower, tot, preferred_element_type=F32)
        return (incl - x2 + rowoff).reshape(E, R, L), rowoff.reshape(E, R, L)

    gt = jnp.where(bits > thr, 1.0, 0.0)
    eq = jnp.where(bits == thr, 1.0, 0.0)
    n_gt = jnp.sum(jnp.sum(gt, axis=2, keepdims=True), axis=1, keepdims=True)
    need = cap - n_gt
    eq_rank, _ = prefix(eq)
    sel = gt + eq * jnp.where(eq_rank < need, 1.0, 0.0)
    pos, rowoff = prefix(sel)
    slot_ref[...] = jnp.where(sel > 0.0, pos, -1.0).astype(jnp.int32)
    off_ref[...] = rowoff.astype(jnp.int32)


def _topk(aff4, *, cap):
    E, B, R, L = aff4.shape
    return pl.pallas_call(
        functools.partial(_topk_kernel, cap=cap),
        grid=(B,),
        in_specs=[pl.BlockSpec((E, None, R, L), lambda b: (0, b, 0, 0))],
        out_specs=[
            pl.BlockSpec((None, E, R, L), lambda b: (b, 0, 0, 0)),
            pl.BlockSpec((None, E, R, L), lambda b: (b, 0, 0, 0)),
        ],
        out_shape=[
            jax.ShapeDtypeStruct((B, E, R, L), jnp.int32),
            jax.ShapeDtypeStruct((B, E, R, L), jnp.int32),
        ],
        compiler_params=_cparams(("parallel",)),
        name="topk",
    )(aff4)


def _gather_kernel(cnt_ref, slot_ref, gate_ref, h_ref, xs_ref, gc_ref, *, win):
    b = pl.program_id(0)
    eg = pl.program_id(1)
    group, cap, D = xs_ref.shape
    ne = pl.num_programs(1) * group
    nblk = slot_ref.shape[1]
    r_iota = lax.broadcasted_iota(jnp.int32, (win, TOK_BLOCK), 0)

    xs_ref[...] = jnp.zeros(xs_ref.shape, BF16)
    gc_ref[...] = jnp.zeros(gc_ref.shape, F32)

    def window(g, j):
        idx = (b * ne + eg * group + g) * (nblk + 1) + j
        s0 = cnt_ref[idx]
        s1 = cnt_ref[idx + 1]
        a0 = (s0 // 16) * 16
        return a0, s1

    def hits(g, j, lo, sa):
        srow = slot_ref[g, j]
        rel = jnp.where(srow >= lo, srow - sa, -1)
        return rel == r_iota

    def add_rows(g, j, sa, hit, part):
        rows = pl.ds(sa, win)
        xs_ref[g, rows, :] = xs_ref[g, rows, :] + part.astype(BF16)
        gc_ref[g, rows, :] += jnp.sum(jnp.where(hit, gate_ref[g, j], 0.0), axis=1, keepdims=True)

    extra = jnp.int32(0)
    for j in range(nblk):
        hb = h_ref[j * TOK_BLOCK:(j + 1) * TOK_BLOCK, :]
        masks, starts = [], []
        for g in range(group):
            a0, s1 = window(g, j)
            sa = pl.multiple_of(jnp.minimum(a0, cap - win), 16)
            extra = jnp.maximum(extra, s1 - (a0 + win))
            masks.append(hits(g, j, a0, sa))
            starts.append(sa)
        lhs = jnp.concatenate([jnp.where(m, 1.0, 0.0).astype(BF16) for m in masks], axis=0)
        res = jnp.dot(lhs, hb, preferred_element_type=F32)
        for g in range(group):
            add_rows(g, j, starts[g], masks[g], res[g * win:(g + 1) * win, :])

    @pl.when(extra > 0)
    def _():
        def blk(j, carry):
            hb = h_ref[pl.ds(pl.multiple_of(j * TOK_BLOCK, TOK_BLOCK), TOK_BLOCK), :]
            for g in range(group):
                a0, s1 = window(g, j)
                nw = (s1 - a0 + win - 1) // win

                def wbody(w, c2, g=g, a0=a0):
                    lo = a0 + w * win
                    sa = pl.multiple_of(jnp.minimum(lo, cap - win), 16)
                    hit = hits(g, j, lo, sa)
                    part = jnp.dot(jnp.where(hit, 1.0, 0.0).astype(BF16), hb, preferred_element_type=F32)
                    add_rows(g, j, sa, hit, part)
                    return c2

                lax.fori_loop(1, nw, wbody, 0)
            return carry

        lax.fori_loop(0, nblk, blk, 0)


def _gather(cnt, slots5, gates5, h3d, *, cap, win):
    B, seq, D = h3d.shape
    E = slots5.shape[1]
    nblk = seq // TOK_BLOCK
    group = MXU_DEPTH // win
    assert E % group == 0 and cap % win == 0
    gs = pltpu.PrefetchScalarGridSpec(
        num_scalar_prefetch=1,
        grid=(B, E // group),
        in_specs=[
            pl.BlockSpec((None, group, nblk, 1, TOK_BLOCK), lambda b, e, c: (b, e, 0, 0, 0)),
            pl.BlockSpec((group, None, nblk, 1, TOK_BLOCK), lambda b, e, c: (e, b, 0, 0, 0)),
            pl.BlockSpec((None, seq, D), lambda b, e, c: (b, 0, 0), pipeline_mode=pl.Buffered(1)),
        ],
        out_specs=[
            pl.BlockSpec((None, group, cap, D), lambda b, e, c: (b, e, 0, 0)),
            pl.BlockSpec((None, group, cap, 1), lambda b, e, c: (b, e, 0, 0)),
        ],
    )
    return pl.pallas_call(
        functools.partial(_gather_kernel, win=win),
        grid_spec=gs,
        out_shape=[
            jax.ShapeDtypeStruct((B, E, cap, D), BF16),
            jax.ShapeDtypeStruct((B, E, cap, 1), F32),
        ],
        compiler_params=_cparams(("parallel", "arbitrary")),
        name="gather",
    )(cnt, slots5, gates5, h3d)


def _ffn_kernel(xs_ref, gc_ref, wg_ref, wu_ref, wd_ref, y_ref, *, f_chunk):
    xs = xs_ref[...]
    y = None
    for f0 in range(0, wg_ref.shape[1], f_chunk):
        fs = slice(f0, f0 + f_chunk)
        a = jnp.dot(xs, wg_ref[:, fs].astype(BF16), preferred_element_type=F32)
        u = jnp.dot(xs, wu_ref[:, fs].astype(BF16), preferred_element_type=F32)
        hmid = (a * jax.nn.sigmoid(a) * u).astype(BF16)
        part = jnp.dot(hmid, wd_ref[fs, :].astype(BF16), preferred_element_type=F32)
        y = part if y is None else y + part
    y_ref[...] = (y * gc_ref[...]).astype(BF16)


def _ffn(xs, gc, w_gate, w_up, w_down, *, layer):
    B, E, cap, D = xs.shape
    Fd = w_gate.shape[-1]
    return pl.pallas_call(
        functools.partial(_ffn_kernel, f_chunk=min(512, Fd)),
        grid=(E, B),
        in_specs=[
            pl.BlockSpec((None, None, cap, D), lambda e, b: (b, e, 0, 0)),
            pl.BlockSpec((None, None, cap, 1), lambda e, b: (b, e, 0, 0)),
            pl.BlockSpec((None, None, D, Fd), lambda e, b: (layer, e, 0, 0)),
            pl.BlockSpec((None, None, D, Fd), lambda e, b: (layer, e, 0, 0)),
            pl.BlockSpec((None, None, Fd, D), lambda e, b: (layer, e, 0, 0)),
        ],
        out_specs=pl.BlockSpec((None, None, cap, D), lambda e, b: (b, e, 0, 0)),
        out_shape=jax.ShapeDtypeStruct((B, E, cap, D), BF16),
        compiler_params=_cparams(("parallel", "parallel")),
        name="ffn",
    )(xs, gc, w_gate, w_up, w_down)


def _scatter_kernel(cnt_ref, slot_ref, y_ref, x_ref, fg_ref, o_ref, *, win, final_norm):
    b = pl.program_id(0)
    sb = pl.program_id(1)
    ne, cap, D = y_ref.shape
    nloc = slot_ref.shape[1]
    nblk = nloc * pl.num_programs(1)
    r_iota = lax.broadcasted_iota(jnp.int32, (win, TOK_BLOCK), 0)
    contract0 = (((0,), (0,)), ((), ()))

    def onehot_t(srow, lo, sa):
        rel = jnp.where(srow >= lo, srow - sa, -1)
        return jnp.where(rel == r_iota, 1.0, 0.0).astype(BF16)

    def window(e, j):
        idx = (b * ne + e) * (nblk + 1) + j
        s0 = cnt_ref[idx]
        s1 = cnt_ref[idx + 1]
        a0 = (s0 // 16) * 16
        sa = pl.multiple_of(jnp.minimum(a0, cap - win), 16)
        return a0, sa, s1 - (a0 + win)

    extra = jnp.int32(0)
    per_dot = MXU_DEPTH // win
    for t in range(nloc):
        j = sb * nloc + t
        rows = slice(t * TOK_BLOCK, (t + 1) * TOK_BLOCK)
        acc = x_ref[rows, :]
        for e0 in range(0, ne, per_dot):
            gs, ys = [], []
            for e in range(e0, e0 + per_dot):
                a0, sa, over = window(e, j)
                extra = jnp.maximum(extra, over)
                gs.append(onehot_t(slot_ref[e, t], a0, sa))
                ys.append(y_ref[e, pl.ds(sa, win), :])
            acc = acc + lax.dot_general(jnp.concatenate(gs, axis=0), jnp.concatenate(ys, axis=0),
                                        contract0, preferred_element_type=F32)
        o_ref[rows, :] = acc

    @pl.when(extra > 0)
    def _():
        for t in range(nloc):
            j = sb * nloc + t
            rows = slice(t * TOK_BLOCK, (t + 1) * TOK_BLOCK)
            for e in range(ne):
                idx = (b * ne + e) * (nblk + 1) + j
                s0 = cnt_ref[idx]
                s1 = cnt_ref[idx + 1]
                a0 = (s0 // 16) * 16
                nw = (s1 - a0 + win - 1) // win

                def wbody(w, carry):
                    lo = a0 + w * win
                    sa = pl.multiple_of(jnp.minimum(lo, cap - win), 16)
                    g = onehot_t(slot_ref[e, t], lo, sa)
                    o_ref[rows, :] += lax.dot_general(g, y_ref[e, pl.ds(sa, win), :], contract0,
                                                      preferred_element_type=F32)
                    return carry

                lax.fori_loop(1, nw, wbody, 0)

    if final_norm:
        o_ref[...] = _rms_rows(o_ref[...], fg_ref[...])


def _scatter(cnt, slots5, y, x3d, fg, *, win, sb_rows, final_norm):
    B, seq, D = x3d.shape
    E, cap = y.shape[1], y.shape[2]
    nloc = sb_rows // TOK_BLOCK
    assert MXU_DEPTH % win == 0 and E % (MXU_DEPTH // win) == 0
    gs = pltpu.PrefetchScalarGridSpec(
        num_scalar_prefetch=1,
        grid=(B, seq // sb_rows),
        in_specs=[
            pl.BlockSpec((None, E, nloc, 1, TOK_BLOCK), lambda b, s, c: (b, 0, s, 0, 0)),
            pl.BlockSpec((None, E, cap, D), lambda b, s, c: (b, 0, 0, 0), pipeline_mode=pl.Buffered(1)),
            pl.BlockSpec((None, sb_rows, D), lambda b, s, c: (b, s, 0)),
            pl.BlockSpec((1, D), lambda b, s, c: (0, 0)),
        ],
        out_specs=pl.BlockSpec((None, sb_rows, D), lambda b, s, c: (b, s, 0)),
    )
    return pl.pallas_call(
        functools.partial(_scatter_kernel, win=win, final_norm=final_norm),
        grid_spec=gs,
        out_shape=jax.ShapeDtypeStruct((B, seq, D), F32),
        compiler_params=_cparams(("parallel", "arbitrary")),
        name="scatter",
    )(cnt, slots5, y, x3d, fg)


def _rope_tables_T(seq):
    rows = seq // GRID_W
    row_id = jnp.repeat(jnp.arange(rows, dtype=F32), GRID_W)
    col_id = jnp.tile(jnp.arange(GRID_W, dtype=F32), rows)
    n_pairs = HEAD_DIM // 4
    freqs = jnp.exp(-math.log(ROPE_THETA) * jnp.arange(n_pairs, dtype=F32) / n_pairs)
    ang = jnp.concatenate([freqs[:, None] * row_id[None, :], freqs[:, None] * col_id[None, :]], axis=0)
    return jnp.cos(ang), jnp.sin(ang)


def _head_perm(n_heads):
    base = jnp.concatenate([jnp.arange(0, HEAD_DIM, 2), jnp.arange(1, HEAD_DIM, 2)])
    return (jnp.arange(n_heads)[:, None] * HEAD_DIM + base[None, :]).reshape(-1)


def kernel(x, mem, mix_norm_g, w_in, gm_v_norm_g, gm_w_s, gm_b_s, q_norm_g, k_norm_g, branch_norm_g, w_out,
           xattn_norm_g, mem_norm_g, xattn_w_q, xattn_w_kv, xattn_w_o, ffn_norm_g, w_router, w_gate, w_up,
           w_down, final_norm_g):
    B, seq, D = x.shape
    L = w_in.shape[0]
    E = w_router.shape[-1]
    T = B * seq
    cap = EC_FACTOR * seq // E
    tm = min(1024, seq)
    tm_in = min(2048, seq)
    tq = min(256, seq)
    tk = min(2048, seq // 2)
    win = min(64, cap)
    sb_rows = min(1024, seq)
    nblk = seq // TOK_BLOCK

    cosT, sinT = _rope_tables_T(seq)
    o_q = 2 * GM_WIDTH
    o_k = o_q + ATT_WIDTH
    o_v = o_k + KV_WIDTH
    cols = jnp.concatenate([jnp.arange(o_q), o_q + _head_perm(N_Q_HEADS), o_k + _head_perm(N_KV_HEADS),
                            jnp.arange(o_v, o_v + KV_WIDTH)])
    hp = _head_perm(1)

    mkT_all, mv_all = _mem_kv(mem, mem_norm_g.reshape(1, D), xattn_w_kv.astype(BF16))

    x2d = x.reshape(T, D)
    out = None
    for l in range(L):
        w_in_l = w_in[l][:, cols].astype(BF16)
        bs = jnp.broadcast_to(gm_b_s[l][:, :, None], (GM_GROUPS, CHUNK, LANES))
        qg = jnp.broadcast_to(q_norm_g[l][hp][:, None], (HEAD_DIM, tm_in))
        kg = jnp.broadcast_to(k_norm_g[l][hp][:, None], (HEAD_DIM, tm_in))
        gm, qhT, qlT, k8, vT = _mixer_in(
            x2d, mix_norm_g[l].reshape(1, D), w_in_l, gm_v_norm_g[l].reshape(1, GM_WIDTH),
            gm_w_s[l].astype(BF16), bs, qg, kg, branch_norm_g[l, 0].reshape(1, GM_WIDTH), cosT, sinT,
            seq=seq, tm=tm_in)
        atT = _attention(qhT, qlT, k8.reshape(B, seq, N_KV_HEADS * MXU_DEPTH), vT,
                         tq=tq, tk=tk, qb=min(256, tq))

        bg1 = jnp.broadcast_to(branch_norm_g[l, 1][:, None], (ATT_WIDTH, tm))
        wr_pad = jnp.pad(w_router[l].astype(BF16), ((0, 0), (0, LANES - E)))
        x2, h3, affT = _post(
            x2d, gm, atT, bg1, w_out[l].astype(BF16), xattn_norm_g[l].reshape(1, D),
            xattn_w_q[l].astype(BF16), mkT_all, mv_all, xattn_w_o[l].astype(BF16),
            ffn_norm_g[l].reshape(1, D), wr_pad, n_experts=E, layer=l, seq=seq, tm=tm)

        slots, offs = _topk(affT.reshape(E, B, seq // LANES, LANES), cap=cap)
        cnt = offs[:, :, ::TOK_BLOCK // LANES, 0]
        cnt = jnp.concatenate([cnt, jnp.full((B, E, 1), cap, jnp.int32)], axis=-1).reshape(-1)
        slots5 = slots.reshape(B, E, nblk, 1, TOK_BLOCK)
        gates5 = affT.reshape(E, B, nblk, 1, TOK_BLOCK)

        xs, gc = _gather(cnt, slots5, gates5, h3.reshape(B, seq, D), cap=cap, win=win)
        y = _ffn(xs, gc, w_gate, w_up, w_down, layer=l)
        out = _scatter(cnt, slots5, y, x2.reshape(B, seq, D), final_norm_g.reshape(1, D),
                       win=win, sb_rows=sb_rows, final_norm=(l == L - 1))
        x2d = out.reshape(T, D)
    return out
```

```python
import functools
import math

import jax
import jax.numpy as jnp
from jax import lax
from jax.experimental import pallas as pl
from jax.experimental.pallas import tpu as pltpu

F32 = jnp.float32
BF16 = jnp.bfloat16
F8 = jnp.float8_e4m3fn
F8_MAX = 448.0

EPS = 1e-6
CHUNK = 128
GM_GROUPS = 4
GM_WIDTH = 512
HEAD_DIM = 64
N_Q_HEADS = 8
N_KV_HEADS = 2
Q_PER_KV = N_Q_HEADS // N_KV_HEADS
ATT_WIDTH = N_Q_HEADS * HEAD_DIM
KV_WIDTH = N_KV_HEADS * HEAD_DIM
ROPE_THETA = 10000.0
GRID_W = 64
X_HEADS = 4
N_EXPERTS = 16
EC_FACTOR = 2

LANES = 128
MXU_DEPTH = 256
PV_ROWS = HEAD_DIM + 32
VMEM_LIMIT = 56 * 1024 * 1024

TOK_BLOCK = 256
LOG2E = 1.4426950408889634


def _cparams(sem, vmem=VMEM_LIMIT, **kw):
    return pltpu.CompilerParams(dimension_semantics=sem, vmem_limit_bytes=vmem, **kw)


def _rms_rows(x, g):
    ms = jnp.mean(x * x, axis=-1, keepdims=True)
    return x * lax.rsqrt(ms + EPS) * g


def _gelu_tanh(x):
    c = math.sqrt(2.0 / math.pi)
    return 0.5 * x * (1.0 + jnp.tanh(c * (x + 0.044715 * (x * x * x))))


def _norm_rope_T(xT, gcol, cosT, sinT, n_heads, scale):
    half = HEAD_DIM // 2
    outs = []
    for h in range(n_heads):
        blk = xT[h * HEAD_DIM:(h + 1) * HEAD_DIM, :]
        ms = jnp.mean(blk * blk, axis=0, keepdims=True)
        n = blk * lax.rsqrt(ms + EPS) * gcol
        e = n[:half, :]
        o = n[half:, :]
        re = (e * cosT - o * sinT) * scale
        ro = (e * sinT + o * cosT) * scale
        outs.append(jnp.concatenate([re, ro], axis=0))
    return outs


def _split_f8(x):
    x = jnp.clip(x, -F8_MAX, F8_MAX)
    hi = x.astype(F8).astype(F32)
    lo = (x - hi).astype(F8).astype(F32)
    return hi, lo


def _mixer_in_kernel(x_ref, g_ref, w_ref, vg_ref, ws_ref, bs_ref, qg_ref, kg_ref, bg0_ref,
                     cos_ref, sin_ref, gm_ref, qhT_ref, qlT_ref, k_ref, vT_ref, gm_scr, *, rows_per_group):
    tm = x_ref.shape[0]
    o_q = 2 * GM_WIDTH
    o_k = o_q + ATT_WIDTH
    o_v = o_k + KV_WIDTH
    groups = [slice(r, r + rows_per_group) for r in range(0, tm, rows_per_group)]

    projs = []
    for rs in groups:
        h = _rms_rows(x_ref[rs, :], g_ref[...]).astype(BF16)
        projs.append(jnp.dot(h, w_ref[...], preferred_element_type=F32))

    for rs, proj in zip(groups, projs):
        u = _gelu_tanh(proj[:, :GM_WIDTH])
        v = _gelu_tanh(proj[:, GM_WIDTH:2 * GM_WIDTH])
        vn = _rms_rows(v, vg_ref[...]).astype(BF16)
        for c in range(rows_per_group // CHUNK):
            cr = slice(c * CHUNK, (c + 1) * CHUNK)
            sr = slice(rs.start + c * CHUNK, rs.start + (c + 1) * CHUNK)
            for g in range(GM_GROUPS):
                cs = slice(g * LANES, (g + 1) * LANES)
                mixed = jnp.dot(ws_ref[g], vn[cr, cs], preferred_element_type=F32) + bs_ref[g]
                gm_scr[sr, cs] = u[cr, cs] * mixed
        gm_ref[rs, :] = _rms_rows(gm_scr[rs, :], bg0_ref[...]).astype(BF16)

        cosT = cos_ref[:, rs]
        sinT = sin_ref[:, rs]
        qT = proj[:, o_q:o_k].T
        q_heads = _norm_rope_T(qT, qg_ref[:, rs], cosT, sinT, N_Q_HEADS, (HEAD_DIM ** -0.5) * LOG2E)
        for hh in range(N_Q_HEADS):
            hi, lo = _split_f8(q_heads[hh])
            qhT_ref[hh * HEAD_DIM:(hh + 1) * HEAD_DIM, rs] = hi.astype(F8)
            qlT_ref[hh * HEAD_DIM:(hh + 1) * HEAD_DIM, rs] = lo.astype(F8)

        kT = proj[:, o_k:o_v].T
        k_heads = _norm_rope_T(kT, kg_ref[:, rs], cosT, sinT, N_KV_HEADS, 1.0)
        pieces = []
        for kh in k_heads:
            hi, lo = _split_f8(kh)
            pieces += [hi, lo, hi, jnp.zeros_like(hi)]
        k_ref[rs, :] = jnp.concatenate(pieces, axis=0).T.astype(F8)

        vT_ref[:, rs] = proj[:, o_v:].T.astype(BF16)


def _mixer_in(x2d, g, w_in, vg, ws, bs, qg, kg, bg0, cosT, sinT, *, seq, tm):
    T, D = x2d.shape
    B = T // seq
    npb = seq // tm
    in_w = w_in.shape[1]
    full = lambda shape: pl.BlockSpec(shape, lambda i: (0,) * len(shape))
    return pl.pallas_call(
        functools.partial(_mixer_in_kernel, rows_per_group=min(256, tm)),
        grid=(T // tm,),
        in_specs=[
            pl.BlockSpec((tm, D), lambda i: (i, 0)),
            full((1, D)),
            full((D, in_w)),
            full((1, GM_WIDTH)),
            full((GM_GROUPS, CHUNK, CHUNK)),
            full((GM_GROUPS, CHUNK, LANES)),
            full((HEAD_DIM, tm)),
            full((HEAD_DIM, tm)),
            full((1, GM_WIDTH)),
            pl.BlockSpec((HEAD_DIM // 2, tm), lambda i: (0, i % npb)),
            pl.BlockSpec((HEAD_DIM // 2, tm), lambda i: (0, i % npb)),
        ],
        out_specs=[
            pl.BlockSpec((tm, GM_WIDTH), lambda i: (i, 0)),
            pl.BlockSpec((None, ATT_WIDTH, tm), lambda i: (i // npb, 0, i % npb)),
            pl.BlockSpec((None, ATT_WIDTH, tm), lambda i: (i // npb, 0, i % npb)),
            pl.BlockSpec((tm, N_KV_HEADS * MXU_DEPTH), lambda i: (i, 0)),
            pl.BlockSpec((None, KV_WIDTH, tm), lambda i: (i // npb, 0, i % npb)),
        ],
        out_shape=[
            jax.ShapeDtypeStruct((T, GM_WIDTH), BF16),
            jax.ShapeDtypeStruct((B, ATT_WIDTH, seq), F8),
            jax.ShapeDtypeStruct((B, ATT_WIDTH, seq), F8),
            jax.ShapeDtypeStruct((T, N_KV_HEADS * MXU_DEPTH), F8),
            jax.ShapeDtypeStruct((B, KV_WIDTH, seq), BF16),
        ],
        scratch_shapes=[pltpu.VMEM((tm, GM_WIDTH), F32)],
        compiler_params=_cparams(("parallel",)),
        name="mixer_in",
    )(x2d, g, w_in, vg, ws, bs, qg, kg, bg0, cosT, sinT)


def _attn_kernel(qhT_ref, qlT_ref, k_ref, vT_ref, o_ref, vext_scr, qx_scr, m_scr, acc_scr, s_scr, cm_scr,
                 *, tq, tk, qb):
    seq = k_ref.shape[0]
    M = Q_PER_KV * tq
    nq = seq // tq
    nchunk = seq // tk

    vext_scr[0:HEAD_DIM, :] = vT_ref[...]
    row = lax.broadcasted_iota(jnp.int32, (PV_ROWS - HEAD_DIM, seq), 0)
    vext_scr[HEAD_DIM:, :] = jnp.where(row == 0, 1.0, 0.0).astype(BF16)

    def load_q(i, slot):
        cols = pl.ds(pl.multiple_of(i * tq, tq), tq)
        qh = jnp.concatenate([qhT_ref[g * HEAD_DIM:(g + 1) * HEAD_DIM, cols] for g in range(Q_PER_KV)], axis=1)
        ql = jnp.concatenate([qlT_ref[g * HEAD_DIM:(g + 1) * HEAD_DIM, cols] for g in range(Q_PER_KV)], axis=1)
        qx_scr[slot] = jnp.concatenate([qh, qh, ql, jnp.zeros_like(qh)], axis=0)

    def step(c, qslot, qk_chunk, do_pv):
        cur = c % 2
        nxt = 1 - cur
        if qk_chunk is not None:
            kn = k_ref[qk_chunk * tk:(qk_chunk + 1) * tk, :]
        if do_pv:
            ve = vext_scr[:, c * tk:(c + 1) * tk]
        for j in range(M // qb):
            cs = slice(j * qb, (j + 1) * qb)
            if qk_chunk is not None:
                s = jnp.dot(kn, qx_scr[qslot, :, cs], preferred_element_type=F32)
                s_scr[nxt, :, cs] = s
                cm_scr[nxt, :, cs] = jnp.max(s, axis=0, keepdims=True)
            if do_pv:
                m_old = m_scr[:, cs]
                m_new = jnp.maximum(m_old, cm_scr[cur, :, cs])
                p = jnp.exp2(s_scr[cur, :, cs] - m_new).astype(BF16)
                alpha = jnp.exp2(m_old - m_new)
                pv = jnp.dot(ve, p, preferred_element_type=F32)
                acc_scr[:, cs] = acc_scr[:, cs] * alpha + pv
                m_scr[:, cs] = m_new

    load_q(0, 0)
    step(-1, 0, 0, False)

    def qblock(i, carry):
        slot = i % 2
        m_scr[...] = jnp.full(m_scr.shape, -jnp.inf, F32)
        acc_scr[...] = jnp.zeros(acc_scr.shape, F32)
        for c in range(nchunk - 1):
            step(c, slot, c + 1, True)
        load_q(jnp.minimum(i + 1, nq - 1), 1 - slot)
        step(nchunk - 1, 1 - slot, 0, True)

        acc = acc_scr[...]
        inv = 1.0 / acc[HEAD_DIM:HEAD_DIM + 1, :]
        o = acc[0:HEAD_DIM, :] * inv
        cols = pl.ds(pl.multiple_of(i * tq, tq), tq)
        for g in range(Q_PER_KV):
            o_ref[g * HEAD_DIM:(g + 1) * HEAD_DIM, cols] = o[:, g * tq:(g + 1) * tq].astype(BF16)
        return carry

    lax.fori_loop(0, nq, qblock, 0)


def _attention(qhT, qlT, k, vT, *, tq, tk, qb):
    B, _, seq = qhT.shape
    gw = Q_PER_KV * HEAD_DIM
    assert (seq // tk) % 2 == 0
    return pl.pallas_call(
        functools.partial(_attn_kernel, tq=tq, tk=tk, qb=qb),
        grid=(B, N_KV_HEADS),
        in_specs=[
            pl.BlockSpec((None, gw, seq), lambda b, kh: (b, kh, 0)),
            pl.BlockSpec((None, gw, seq), lambda b, kh: (b, kh, 0)),
            pl.BlockSpec((None, seq, MXU_DEPTH), lambda b, kh: (b, 0, kh)),
            pl.BlockSpec((None, HEAD_DIM, seq), lambda b, kh: (b, kh, 0)),
        ],
        out_specs=pl.BlockSpec((None, gw, seq), lambda b, kh: (b, kh, 0)),
        out_shape=jax.ShapeDtypeStruct((B, ATT_WIDTH, seq), BF16),
        scratch_shapes=[
            pltpu.VMEM((PV_ROWS, seq), BF16),
            pltpu.VMEM((2, MXU_DEPTH, Q_PER_KV * tq), F8),
            pltpu.VMEM((1, Q_PER_KV * tq), F32),
            pltpu.VMEM((PV_ROWS, Q_PER_KV * tq), F32),
            pltpu.VMEM((2, tk, Q_PER_KV * tq), F32),
            pltpu.VMEM((2, 1, Q_PER_KV * tq), F32),
        ],
        compiler_params=_cparams(("parallel", "parallel")),
        name="attention",
    )(qhT, qlT, k, vT)


def _mem_kv_kernel(mem_ref, g_ref, w_ref, kT_ref, v_ref):
    D = mem_ref.shape[1]
    mn = _rms_rows(mem_ref[...], g_ref[...]).astype(BF16)
    kv = jnp.dot(mn, w_ref[...], preferred_element_type=F32)
    kT_ref[...] = kv[:, :D].T.astype(BF16)
    v_ref[...] = kv[:, D:].astype(BF16)


def _mem_kv(mem, g, w_kv):
    B, M, D = mem.shape
    L = w_kv.shape[0]
    return pl.pallas_call(
        _mem_kv_kernel,
        grid=(L, B),
        in_specs=[
            pl.BlockSpec((None, M, D), lambda l, b: (b, 0, 0)),
            pl.BlockSpec((1, D), lambda l, b: (0, 0)),
            pl.BlockSpec((None, D, 2 * D), lambda l, b: (l, 0, 0)),
        ],
        out_specs=[
            pl.BlockSpec((None, None, D, M), lambda l, b: (l, b, 0, 0)),
            pl.BlockSpec((None, None, M, D), lambda l, b: (l, b, 0, 0)),
        ],
        out_shape=[
            jax.ShapeDtypeStruct((L, B, D, M), BF16),
            jax.ShapeDtypeStruct((L, B, M, D), BF16),
        ],
        compiler_params=_cparams(("parallel", "parallel")),
        name="mem_kv",
    )(mem, g, w_kv)


def _post_kernel(x_ref, gm_ref, atT_ref, bg1_ref, wout_ref, xg_ref, wq_ref, mkT_ref, mv_ref, wo_ref,
                 fg_ref, wr_ref, x2_ref, h3_ref, affT_ref, *, rows_per_group):
    tm, D = x_ref.shape
    xhd = D // X_HEADS
    groups = [slice(r, r + rows_per_group) for r in range(0, tm, rows_per_group)]

    x1 = []
    for rs in groups:
        at = atT_ref[:, rs].astype(F32)
        ms = jnp.mean(at * at, axis=0, keepdims=True)
        atn = (at * lax.rsqrt(ms + EPS) * bg1_ref[:, rs]).astype(BF16)
        y = jnp.dot(gm_ref[rs, :], wout_ref[0:GM_WIDTH, :], preferred_element_type=F32)
        y = y + lax.dot_general(atn, wout_ref[GM_WIDTH:, :], (((0,), (0,)), ((), ())),
                                preferred_element_type=F32)
        x1.append(x_ref[rs, :] + y)

    q2 = []
    for x1g in x1:
        h2 = _rms_rows(x1g, xg_ref[...]).astype(BF16)
        q2.append((jnp.dot(h2, wq_ref[...], preferred_element_type=F32) * (xhd ** -0.5)).astype(BF16))

    scores = {}
    for hh in range(X_HEADS):
        cs = slice(hh * xhd, (hh + 1) * xhd)
        for gi in range(len(groups)):
            scores[hh, gi] = jnp.dot(q2[gi][:, cs], mkT_ref[cs, :], preferred_element_type=F32)
    outs = [[] for _ in groups]
    for hh in range(X_HEADS):
        cs = slice(hh * xhd, (hh + 1) * xhd)
        for gi in range(len(groups)):
            s = scores[hh, gi]
            s = s - jnp.max(s, axis=-1, keepdims=True)
            p = jnp.exp(s)
            p = (p / jnp.sum(p, axis=-1, keepdims=True)).astype(BF16)
            outs[gi].append(jnp.dot(p, mv_ref[:, cs], preferred_element_type=F32).astype(BF16))

    x2 = []
    for gi, rs in enumerate(groups):
        o2 = jnp.concatenate(outs[gi], axis=1)
        x2g = x1[gi] + jnp.dot(o2, wo_ref[...], preferred_element_type=F32)
        x2_ref[rs, :] = x2g
        x2.append(x2g)

    ne = affT_ref.shape[0]
    for gi, rs in enumerate(groups):
        h3 = _rms_rows(x2[gi], fg_ref[...]).astype(BF16)
        h3_ref[rs, :] = h3
        lg = jnp.dot(h3, wr_ref[...], preferred_element_type=F32).T[0:ne, :]
        lg = lg - jnp.max(lg, axis=0, keepdims=True)
        ex = jnp.exp(lg)
        affT_ref[:, rs] = ex / jnp.sum(ex, axis=0, keepdims=True)


def _post(x2d, gm, atT, bg1, w_out, xg, w_q, mkT, mv, w_o, fg, wr_pad, *, n_experts, layer, seq, tm):
    T, D = x2d.shape
    npb = seq // tm
    M = mv.shape[2]
    E = n_experts
    full = lambda shape: pl.BlockSpec(shape, lambda i: (0,) * len(shape))
    return pl.pallas_call(
        functools.partial(_post_kernel, rows_per_group=min(256, tm)),
        grid=(T // tm,),
        in_specs=[
            pl.BlockSpec((tm, D), lambda i: (i, 0)),
            pl.BlockSpec((tm, GM_WIDTH), lambda i: (i, 0)),
            pl.BlockSpec((None, ATT_WIDTH, tm), lambda i: (i // npb, 0, i % npb)),
            full((ATT_WIDTH, tm)),
            full((GM_WIDTH + ATT_WIDTH, D)),
            full((1, D)),
            full((D, D)),
            pl.BlockSpec((None, None, D, M), lambda i: (layer, i // npb, 0, 0)),
            pl.BlockSpec((None, None, M, D), lambda i: (layer, i // npb, 0, 0)),
            full((D, D)),
            full((1, D)),
            full((D, LANES)),
        ],
        out_specs=[
            pl.BlockSpec((tm, D), lambda i: (i, 0)),
            pl.BlockSpec((tm, D), lambda i: (i, 0)),
            pl.BlockSpec((E, tm), lambda i: (0, i)),
        ],
        out_shape=[
            jax.ShapeDtypeStruct((T, D), F32),
            jax.ShapeDtypeStruct((T, D), BF16),
            jax.ShapeDtypeStruct((E, T), F32),
        ],
        compiler_params=_cparams(("parallel",)),
        name="post",
    )(x2d, gm, atT, bg1, w_out, xg, w_q, mkT, mv, w_o, fg, wr_pad)


def _topk_kernel(aff_ref, slot_ref, off_ref, *, cap):
    E, R, L = aff_ref.shape
    a = aff_ref[...]
    bits = lax.bitcast_convert_type(a, jnp.int32)

    def count_ge(t):
        c = jnp.where(bits >= t, 1.0, 0.0)
        return jnp.sum(jnp.sum(c, axis=2, keepdims=True), axis=1, keepdims=True)

    def bis(_, carry):
        lo, hi = carry
        mid = lo + ((hi - lo) >> 1)
        ok = count_ge(mid) >= cap
        return jnp.where(ok, mid, lo), jnp.where(ok, hi, mid)

    lo0 = jnp.zeros((E, 1, 1), jnp.int32)
    hi0 = jnp.full((E, 1, 1), 0x3F800001, jnp.int32)
    thr, _ = lax.fori_loop(0, 31, bis, (lo0, hi0))

    kk = lax.broadcasted_iota(jnp.int32, (L, L), 0)
    nn = lax.broadcasted_iota(jnp.int32, (L, L), 1)
    upper = jnp.where(kk <= nn, 1.0, 0.0).astype(BF16)
    ones = jnp.ones((L, L), BF16)
    rr = lax.broadcasted_iota(jnp.int32, (E * R, E * R), 0)
    cc = lax.broadcasted_iota(jnp.int32, (E * R, E * R), 1)
    lower = jnp.where((rr // R == cc // R) & (cc < rr), 1.0, 0.0).astype(BF16)

    def prefix(xf):
        x2 = xf.reshape(E * R, L)
        xb = x2.astype(BF16)
        incl = jnp.dot(xb, upper, preferred_element_type=F32)
        tot = jnp.dot(xb, ones, preferred_element_type=F32).astype(BF16)
        rowoff = jnp.dot(lower, tot, preferred_element_type=F32)
        return (incl - x2 + rowoff).reshape(E, R, L), rowoff.reshape(E, R, L)

    gt = jnp.where(bits > thr, 1.0, 0.0)
    eq = jnp.where(bits == thr, 1.0, 0.0)
    n_gt = jnp.sum(jnp.sum(gt, axis=2, keepdims=True), axis=1, keepdims=True)
    need = cap - n_gt
    eq_rank, _ = prefix(eq)
    sel = gt + eq * jnp.where(eq_rank < need, 1.0, 0.0)
    pos, rowoff = prefix(sel)
    slot_ref[...] = jnp.where(sel > 0.0, pos, -1.0).astype(jnp.int32)
    off_ref[...] = rowoff.astype(jnp.int32)


def _topk(aff4, *, cap):
    E, B, R, L = aff4.shape
    return pl.pallas_call(
        functools.partial(_topk_kernel, cap=cap),
        grid=(B,),
        in_specs=[pl.BlockSpec((E, None, R, L), lambda b: (0, b, 0, 0))],
        out_specs=[
            pl.BlockSpec((None, E, R, L), lambda b: (b, 0, 0, 0)),
            pl.BlockSpec((None, E, R, L), lambda b: (b, 0, 0, 0)),
        ],
        out_shape=[
            jax.ShapeDtypeStruct((B, E, R, L), jnp.int32),
            jax.ShapeDtypeStruct((B, E, R, L), jnp.int32),
        ],
        compiler_params=_cparams(("parallel",)),
        name="topk",
    )(aff4)


def _gather_kernel(cnt_ref, slot_ref, gate_ref, h_ref, xs_ref, gc_ref, *, win):
    b = pl.program_id(0)
    eg = pl.program_id(1)
    group, cap, D = xs_ref.shape
    ne = pl.num_programs(1) * group
    nblk = slot_ref.shape[1]
    r_iota = lax.broadcasted_iota(jnp.int32, (win, TOK_BLOCK), 0)

    xs_ref[...] = jnp.zeros(xs_ref.shape, BF16)
    gc_ref[...] = jnp.zeros(gc_ref.shape, F32)

    def window(g, j):
        idx = (b * ne + eg * group + g) * (nblk + 1) + j
        s0 = cnt_ref[idx]
        s1 = cnt_ref[idx + 1]
        a0 = (s0 // 16) * 16
        return a0, s1

    def hits(g, j, lo, sa):
        srow = slot_ref[g, j]
        rel = jnp.where(srow >= lo, srow - sa, -1)
        return rel == r_iota

    def add_rows(g, j, sa, hit, part):
        rows = pl.ds(sa, win)
        xs_ref[g, rows, :] = xs_ref[g, rows, :] + part.astype(BF16)
        gc_ref[g, rows, :] += jnp.sum(jnp.where(hit, gate_ref[g, j], 0.0), axis=1, keepdims=True)

    extra = jnp.int32(0)
    for j in range(nblk):
        hb = h_ref[j * TOK_BLOCK:(j + 1) * TOK_BLOCK, :]
        masks, starts = [], []
        for g in range(group):
            a0, s1 = window(g, j)
            sa = pl.multiple_of(jnp.minimum(a0, cap - win), 16)
            extra = jnp.maximum(extra, s1 - (a0 + win))
            masks.append(hits(g, j, a0, sa))
            starts.append(sa)
        lhs = jnp.concatenate([jnp.where(m, 1.0, 0.0).astype(BF16) for m in masks], axis=0)
        res = jnp.dot(lhs, hb, preferred_element_type=F32)
        for g in range(group):
            add_rows(g, j, starts[g], masks[g], res[g * win:(g + 1) * win, :])

    @pl.when(extra > 0)
    def _():
        def blk(j, carry):
            hb = h_ref[pl.ds(pl.multiple_of(j * TOK_BLOCK, TOK_BLOCK), TOK_BLOCK), :]
            for g in range(group):
                a0, s1 = window(g, j)
                nw = (s1 - a0 + win - 1) // win

                def wbody(w, c2, g=g, a0=a0):
                    lo = a0 + w * win
                    sa = pl.multiple_of(jnp.minimum(lo, cap - win), 16)
                    hit = hits(g, j, lo, sa)
                    part = jnp.dot(jnp.where(hit, 1.0, 0.0).astype(BF16), hb, preferred_element_type=F32)
                    add_rows(g, j, sa, hit, part)
                    return c2

                lax.fori_loop(1, nw, wbody, 0)
            return carry

        lax.fori_loop(0, nblk, blk, 0)


def _gather(cnt, slots5, gates5, h3d, *, cap, win):
    B, seq, D = h3d.shape
    E = slots5.shape[1]
    nblk = seq // TOK_BLOCK
    group = MXU_DEPTH // win
    assert E % group == 0 and cap % win == 0
    gs = pltpu.PrefetchScalarGridSpec(
        num_scalar_prefetch=1,
        grid=(B, E // group),
        in_specs=[
            pl.BlockSpec((None, group, nblk, 1, TOK_BLOCK), lambda b, e, c: (b, e, 0, 0, 0)),
            pl.BlockSpec((group, None, nblk, 1, TOK_BLOCK), lambda b, e, c: (e, b, 0, 0, 0)),
            pl.BlockSpec((None, seq, D), lambda b, e, c: (b, 0, 0)),
        ],
        out_specs=[
            pl.BlockSpec((None, group, cap, D), lambda b, e, c: (b, e, 0, 0)),
            pl.BlockSpec((None, group, cap, 1), lambda b, e, c: (b, e, 0, 0)),
        ],
    )
    return pl.pallas_call(
        functools.partial(_gather_kernel, win=win),
        grid_spec=gs,
        out_shape=[
            jax.ShapeDtypeStruct((B, E, cap, D), BF16),
            jax.ShapeDtypeStruct((B, E, cap, 1), F32),
        ],
        compiler_params=_cparams(("parallel", "arbitrary")),
        name="gather",
    )(cnt, slots5, gates5, h3d)


def _ffn_kernel(xs_ref, gc_ref, wg_ref, wu_ref, wd_ref, y_ref, *, f_chunk):
    xs = xs_ref[...]
    y = None
    for f0 in range(0, wg_ref.shape[1], f_chunk):
        fs = slice(f0, f0 + f_chunk)
        a = jnp.dot(xs, wg_ref[:, fs].astype(BF16), preferred_element_type=F32)
        u = jnp.dot(xs, wu_ref[:, fs].astype(BF16), preferred_element_type=F32)
        hmid = (a * jax.nn.sigmoid(a) * u).astype(BF16)
        part = jnp.dot(hmid, wd_ref[fs, :].astype(BF16), preferred_element_type=F32)
        y = part if y is None else y + part
    y_ref[...] = (y * gc_ref[...]).astype(BF16)


def _ffn(xs, gc, w_gate, w_up, w_down, *, layer):
    B, E, cap, D = xs.shape
    Fd = w_gate.shape[-1]
    return pl.pallas_call(
        functools.partial(_ffn_kernel, f_chunk=min(512, Fd)),
        grid=(E, B),
        in_specs=[
            pl.BlockSpec((None, None, cap, D), lambda e, b: (b, e, 0, 0)),
            pl.BlockSpec((None, None, cap, 1), lambda e, b: (b, e, 0, 0)),
            pl.BlockSpec((None, None, D, Fd), lambda e, b: (layer, e, 0, 0)),
            pl.BlockSpec((None, None, D, Fd), lambda e, b: (layer, e, 0, 0)),
            pl.BlockSpec((None, None, Fd, D), lambda e, b: (layer, e, 0, 0)),
        ],
        out_specs=pl.BlockSpec((None, None, cap, D), lambda e, b: (b, e, 0, 0)),
        out_shape=jax.ShapeDtypeStruct((B, E, cap, D), BF16),
        compiler_params=_cparams(("parallel", "parallel")),
        name="ffn",
    )(xs, gc, w_gate, w_up, w_down)


def _scatter_kernel(cnt_ref, slot_ref, y_ref, x_ref, fg_ref, o_ref, *, win, final_norm):
    b = pl.program_id(0)
    sb = pl.program_id(1)
    ne, cap, D = y_ref.shape
    nloc = slot_ref.shape[1]
    nblk = nloc * pl.num_programs(1)
    r_iota = lax.broadcasted_iota(jnp.int32, (win, TOK_BLOCK), 0)
    contract0 = (((0,), (0,)), ((), ()))

    def onehot_t(srow, lo, sa):
        rel = jnp.where(srow >= lo, srow - sa, -1)
        return jnp.where(rel == r_iota, 1.0, 0.0).astype(BF16)

    def window(e, j):
        idx = (b * ne + e) * (nblk + 1) + j
        s0 = cnt_ref[idx]
        s1 = cnt_ref[idx + 1]
        a0 = (s0 // 16) * 16
        sa = pl.multiple_of(jnp.minimum(a0, cap - win), 16)
        return a0, sa, s1 - (a0 + win)

    extra = jnp.int32(0)
    per_dot = MXU_DEPTH // win
    for t in range(nloc):
        j = sb * nloc + t
        rows = slice(t * TOK_BLOCK, (t + 1) * TOK_BLOCK)
        acc = x_ref[rows, :]
        for e0 in range(0, ne, per_dot):
            gs, ys = [], []
            for e in range(e0, e0 + per_dot):
                a0, sa, over = window(e, j)
                extra = jnp.maximum(extra, over)
                gs.append(onehot_t(slot_ref[e, t], a0, sa))
                ys.append(y_ref[e, pl.ds(sa, win), :])
            acc = acc + lax.dot_general(jnp.concatenate(gs, axis=0), jnp.concatenate(ys, axis=0),
                                        contract0, preferred_element_type=F32)
        o_ref[rows, :] = acc

    @pl.when(extra > 0)
    def _():
        for t in range(nloc):
            j = sb * nloc + t
            rows = slice(t * TOK_BLOCK, (t + 1) * TOK_BLOCK)
            for e in range(ne):
                idx = (b * ne + e) * (nblk + 1) + j
                s0 = cnt_ref[idx]
                s1 = cnt_ref[idx + 1]
                a0 = (s0 // 16) * 16
                nw = (s1 - a0 + win - 1) // win

                def wbody(w, carry):
                    lo = a0 + w * win
                    sa = pl.multiple_of(jnp.minimum(lo, cap - win), 16)
                    g = onehot_t(slot_ref[e, t], lo, sa)
                    o_ref[rows, :] += lax.dot_general(g, y_ref[e, pl.ds(sa, win), :], contract0,
                                                      preferred_element_type=F32)
                    return carry

                lax.fori_loop(1, nw, wbody, 0)

    if final_norm:
        o_ref[...] = _rms_rows(o_ref[...], fg_ref[...])


def _scatter(cnt, slots5, y, x3d, fg, *, win, sb_rows, final_norm):
    B, seq, D = x3d.shape
    E, cap = y.shape[1], y.shape[2]
    nloc = sb_rows // TOK_BLOCK
    assert MXU_DEPTH % win == 0 and E % (MXU_DEPTH // win) == 0
    gs = pltpu.PrefetchScalarGridSpec(
        num_scalar_prefetch=1,
        grid=(B, seq // sb_rows),
        in_specs=[
            pl.BlockSpec((None, E, nloc, 1, TOK_BLOCK), lambda b, s, c: (b, 0, s, 0, 0)),
            pl.BlockSpec((None, E, cap, D), lambda b, s, c: (b, 0, 0, 0), pipeline_mode=pl.Buffered(1)),
            pl.BlockSpec((None, sb_rows, D), lambda b, s, c: (b, s, 0)),
            pl.BlockSpec((1, D), lambda b, s, c: (0, 0)),
        ],
        out_specs=pl.BlockSpec((None, sb_rows, D), lambda b, s, c: (b, s, 0)),
    )
    return pl.pallas_call(
        functools.partial(_scatter_kernel, win=win, final_norm=final_norm),
        grid_spec=gs,
        out_shape=jax.ShapeDtypeStruct((B, seq, D), F32),
        compiler_params=_cparams(("parallel", "arbitrary")),
        name="scatter",
    )(cnt, slots5, y, x3d, fg)


def _rope_tables_T(seq):
    rows = seq // GRID_W
    row_id = jnp.repeat(jnp.arange(rows, dtype=F32), GRID_W)
    col_id = jnp.tile(jnp.arange(GRID_W, dtype=F32), rows)
    n_pairs = HEAD_DIM // 4
    freqs = jnp.exp(-math.log(ROPE_THETA) * jnp.arange(n_pairs, dtype=F32) / n_pairs)
    ang = jnp.concatenate([freqs[:, None] * row_id[None, :], freqs[:, None] * col_id[None, :]], axis=0)
    return jnp.cos(ang), jnp.sin(ang)


def _head_perm(n_heads):
    base = jnp.concatenate([jnp.arange(0, HEAD_DIM, 2), jnp.arange(1, HEAD_DIM, 2)])
    return (jnp.arange(n_heads)[:, None] * HEAD_DIM + base[None, :]).reshape(-1)


def kernel(x, mem, mix_norm_g, w_in, gm_v_norm_g, gm_w_s, gm_b_s, q_norm_g, k_norm_g, branch_norm_g, w_out,
           xattn_norm_g, mem_norm_g, xattn_w_q, xattn_w_kv, xattn_w_o, ffn_norm_g, w_router, w_gate, w_up,
           w_down, final_norm_g):
    B, seq, D = x.shape
    L = w_in.shape[0]
    E = w_router.shape[-1]
    T = B * seq
    cap = EC_FACTOR * seq // E
    tm = min(1024, seq)
    tm_in = min(2048, seq)
    tq = min(256, seq)
    tk = min(2048, seq // 2)
    win = min(64, cap)
    sb_rows = min(1024, seq)
    nblk = seq // TOK_BLOCK

    cosT, sinT = _rope_tables_T(seq)
    o_q = 2 * GM_WIDTH
    o_k = o_q + ATT_WIDTH
    o_v = o_k + KV_WIDTH
    cols = jnp.concatenate([jnp.arange(o_q), o_q + _head_perm(N_Q_HEADS), o_k + _head_perm(N_KV_HEADS),
                            jnp.arange(o_v, o_v + KV_WIDTH)])
    hp = _head_perm(1)

    mkT_all, mv_all = _mem_kv(mem, mem_norm_g.reshape(1, D), xattn_w_kv.astype(BF16))

    x2d = x.reshape(T, D)
    out = None
    for l in range(L):
        w_in_l = w_in[l][:, cols].astype(BF16)
        bs = jnp.broadcast_to(gm_b_s[l][:, :, None], (GM_GROUPS, CHUNK, LANES))
        qg = jnp.broadcast_to(q_norm_g[l][hp][:, None], (HEAD_DIM, tm_in))
        kg = jnp.broadcast_to(k_norm_g[l][hp][:, None], (HEAD_DIM, tm_in))
        gm, qhT, qlT, k8, vT = _mixer_in(
            x2d, mix_norm_g[l].reshape(1, D), w_in_l, gm_v_norm_g[l].reshape(1, GM_WIDTH),
            gm_w_s[l].astype(BF16), bs, qg, kg, branch_norm_g[l, 0].reshape(1, GM_WIDTH), cosT, sinT,
            seq=seq, tm=tm_in)
        atT = _attention(qhT, qlT, k8.reshape(B, seq, N_KV_HEADS * MXU_DEPTH), vT,
                         tq=tq, tk=tk, qb=min(256, tq))

        bg1 = jnp.broadcast_to(branch_norm_g[l, 1][:, None], (ATT_WIDTH, tm))
        wr_pad = jnp.pad(w_router[l].astype(BF16), ((0, 0), (0, LANES - E)))
        x2, h3, affT = _post(
            x2d, gm, atT, bg1, w_out[l].astype(BF16), xattn_norm_g[l].reshape(1, D),
            xattn_w_q[l].astype(BF16), mkT_all, mv_all, xattn_w_o[l].astype(BF16),
            ffn_norm_g[l].reshape(1, D), wr_pad, n_experts=E, layer=l, seq=seq, tm=tm)

        slots, offs = _topk(affT.reshape(E, B, seq // LANES, LANES), cap=cap)
        cnt = offs[:, :, ::TOK_BLOCK // LANES, 0]
        cnt = jnp.concatenate([cnt, jnp.full((B, E, 1), cap, jnp.int32)], axis=-1).reshape(-1)
        slots5 = slots.reshape(B, E, nblk, 1, TOK_BLOCK)
        gates5 = affT.reshape(E, B, nblk, 1, TOK_BLOCK)

        xs, gc = _gather(cnt, slots5, gates5, h3.reshape(B, seq, D), cap=cap, win=win)
        y = _ffn(xs, gc, w_gate, w_up, w_down, layer=l)
        out = _scatter(cnt, slots5, y, x2.reshape(B, seq, D), final_norm_g.reshape(1, D),
                       win=win, sb_rows=sb_rows, final_norm=(l == L - 1))
        x2d = out.reshape(T, D)
    return out
```

```python
import functools
import math

import jax
import jax.numpy as jnp
from jax import lax
from jax.experimental import pallas as pl
from jax.experimental.pallas import tpu as pltpu

F32 = jnp.float32
BF16 = jnp.bfloat16
F8 = jnp.float8_e4m3fn
F8_MAX = 448.0

EPS = 1e-6
CHUNK = 128
GM_GROUPS = 4
GM_WIDTH = 512
HEAD_DIM = 64
N_Q_HEADS = 8
N_KV_HEADS = 2
Q_PER_KV = N_Q_HEADS // N_KV_HEADS
ATT_WIDTH = N_Q_HEADS * HEAD_DIM
KV_WIDTH = N_KV_HEADS * HEAD_DIM
ROPE_THETA = 10000.0
GRID_W = 64
X_HEADS = 4
N_EXPERTS = 16
EC_FACTOR = 2

LANES = 128
MXU_DEPTH = 256
PV_ROWS = HEAD_DIM + 32
VMEM_LIMIT = 56 * 1024 * 1024

TOK_BLOCK = 256
LOG2E = 1.4426950408889634


def _cparams(sem, vmem=VMEM_LIMIT, **kw):
    return pltpu.CompilerParams(dimension_semantics=sem, vmem_limit_bytes=vmem, **kw)


def _rms_rows(x, g):
    ms = jnp.mean(x * x, axis=-1, keepdims=True)
    return x * lax.rsqrt(ms + EPS) * g


def _gelu_tanh(x):
    c = math.sqrt(2.0 / math.pi)
    return 0.5 * x * (1.0 + jnp.tanh(c * (x + 0.044715 * (x * x * x))))


def _norm_rope_T(xT, gcol, cosT, sinT, n_heads, scale):
    half = HEAD_DIM // 2
    outs = []
    for h in range(n_heads):
        blk = xT[h * HEAD_DIM:(h + 1) * HEAD_DIM, :]
        ms = jnp.mean(blk * blk, axis=0, keepdims=True)
        n = blk * lax.rsqrt(ms + EPS) * gcol
        e = n[:half, :]
        o = n[half:, :]
        re = (e * cosT - o * sinT) * scale
        ro = (e * sinT + o * cosT) * scale
        outs.append(jnp.concatenate([re, ro], axis=0))
    return outs


def _split_f8(x):
    x = jnp.clip(x, -F8_MAX, F8_MAX)
    hi = x.astype(F8).astype(F32)
    lo = (x - hi).astype(F8).astype(F32)
    return hi, lo


def _mixer_in_kernel(x_ref, g_ref, w_ref, vg_ref, ws_ref, bs_ref, qg_ref, kg_ref, bg0_ref,
                     cos_ref, sin_ref, gm_ref, qhT_ref, qlT_ref, k_ref, vT_ref, gm_scr, *, rows_per_group):
    tm = x_ref.shape[0]
    o_q = 2 * GM_WIDTH
    o_k = o_q + ATT_WIDTH
    o_v = o_k + KV_WIDTH
    groups = [slice(r, r + rows_per_group) for r in range(0, tm, rows_per_group)]

    projs = []
    for rs in groups:
        h = _rms_rows(x_ref[rs, :], g_ref[...]).astype(BF16)
        projs.append(jnp.dot(h, w_ref[...], preferred_element_type=F32))

    for rs, proj in zip(groups, projs):
        u = _gelu_tanh(proj[:, :GM_WIDTH])
        v = _gelu_tanh(proj[:, GM_WIDTH:2 * GM_WIDTH])
        vn = _rms_rows(v, vg_ref[...]).astype(BF16)
        for c in range(rows_per_group // CHUNK):
            cr = slice(c * CHUNK, (c + 1) * CHUNK)
            sr = slice(rs.start + c * CHUNK, rs.start + (c + 1) * CHUNK)
            for g in range(GM_GROUPS):
                cs = slice(g * LANES, (g + 1) * LANES)
                mixed = jnp.dot(ws_ref[g], vn[cr, cs], preferred_element_type=F32) + bs_ref[g]
                gm_scr[sr, cs] = u[cr, cs] * mixed
        gm_ref[rs, :] = _rms_rows(gm_scr[rs, :], bg0_ref[...]).astype(BF16)

        cosT = cos_ref[:, rs]
        sinT = sin_ref[:, rs]
        qT = proj[:, o_q:o_k].T
        q_heads = _norm_rope_T(qT, qg_ref[:, rs], cosT, sinT, N_Q_HEADS, (HEAD_DIM ** -0.5) * LOG2E)
        for hh in range(N_Q_HEADS):
            hi, lo = _split_f8(q_heads[hh])
            qhT_ref[hh * HEAD_DIM:(hh + 1) * HEAD_DIM, rs] = hi.astype(F8)
            qlT_ref[hh * HEAD_DIM:(hh + 1) * HEAD_DIM, rs] = lo.astype(F8)

        kT = proj[:, o_k:o_v].T
        k_heads = _norm_rope_T(kT, kg_ref[:, rs], cosT, sinT, N_KV_HEADS, 1.0)
        pieces = []
        for kh in k_heads:
            hi, lo = _split_f8(kh)
            pieces += [hi, lo, hi, jnp.zeros_like(hi)]
        k_ref[rs, :] = jnp.concatenate(pieces, axis=0).T.astype(F8)

        vT_ref[:, rs] = proj[:, o_v:].T.astype(BF16)


def _mixer_in(x2d, g, w_in, vg, ws, bs, qg, kg, bg0, cosT, sinT, *, seq, tm):
    T, D = x2d.shape
    B = T // seq
    npb = seq // tm
    in_w = w_in.shape[1]
    full = lambda shape: pl.BlockSpec(shape, lambda i: (0,) * len(shape))
    return pl.pallas_call(
        functools.partial(_mixer_in_kernel, rows_per_group=min(256, tm)),
        grid=(T // tm,),
        in_specs=[
            pl.BlockSpec((tm, D), lambda i: (i, 0)),
            full((1, D)),
            full((D, in_w)),
            full((1, GM_WIDTH)),
            full((GM_GROUPS, CHUNK, CHUNK)),
            full((GM_GROUPS, CHUNK, LANES)),
            full((HEAD_DIM, tm)),
            full((HEAD_DIM, tm)),
            full((1, GM_WIDTH)),
            pl.BlockSpec((HEAD_DIM // 2, tm), lambda i: (0, i % npb)),
            pl.BlockSpec((HEAD_DIM // 2, tm), lambda i: (0, i % npb)),
        ],
        out_specs=[
            pl.BlockSpec((tm, GM_WIDTH), lambda i: (i, 0)),
            pl.BlockSpec((None, ATT_WIDTH, tm), lambda i: (i // npb, 0, i % npb)),
            pl.BlockSpec((None, ATT_WIDTH, tm), lambda i: (i // npb, 0, i % npb)),
            pl.BlockSpec((tm, N_KV_HEADS * MXU_DEPTH), lambda i: (i, 0)),
            pl.BlockSpec((None, KV_WIDTH, tm), lambda i: (i // npb, 0, i % npb)),
        ],
        out_shape=[
            jax.ShapeDtypeStruct((T, GM_WIDTH), BF16),
            jax.ShapeDtypeStruct((B, ATT_WIDTH, seq), F8),
            jax.ShapeDtypeStruct((B, ATT_WIDTH, seq), F8),
            jax.ShapeDtypeStruct((T, N_KV_HEADS * MXU_DEPTH), F8),
            jax.ShapeDtypeStruct((B, KV_WIDTH, seq), BF16),
        ],
        scratch_shapes=[pltpu.VMEM((tm, GM_WIDTH), F32)],
        compiler_params=_cparams(("parallel",)),
        name="mixer_in",
    )(x2d, g, w_in, vg, ws, bs, qg, kg, bg0, cosT, sinT)


def _attn_kernel(qhT_ref, qlT_ref, k_ref, vT_ref, o_ref, vext_scr, qx_scr, m_scr, acc_scr, s_scr, cm_scr,
                 *, tq, tk, qb):
    seq = k_ref.shape[0]
    M = Q_PER_KV * tq
    nq = seq // tq
    nchunk = seq // tk

    vext_scr[0:HEAD_DIM, :] = vT_ref[...]
    row = lax.broadcasted_iota(jnp.int32, (PV_ROWS - HEAD_DIM, seq), 0)
    vext_scr[HEAD_DIM:, :] = jnp.where(row == 0, 1.0, 0.0).astype(BF16)

    def load_q(i, slot):
        cols = pl.ds(pl.multiple_of(i * tq, tq), tq)
        qh = jnp.concatenate([qhT_ref[g * HEAD_DIM:(g + 1) * HEAD_DIM, cols] for g in range(Q_PER_KV)], axis=1)
        ql = jnp.concatenate([qlT_ref[g * HEAD_DIM:(g + 1) * HEAD_DIM, cols] for g in range(Q_PER_KV)], axis=1)
        qx_scr[slot] = jnp.concatenate([qh, qh, ql, jnp.zeros_like(qh)], axis=0)

    def step(c, qslot, qk_chunk, do_pv):
        cur = c % 2
        nxt = 1 - cur
        if qk_chunk is not None:
            kn = k_ref[qk_chunk * tk:(qk_chunk + 1) * tk, :]
        if do_pv:
            ve = vext_scr[:, c * tk:(c + 1) * tk]
        for j in range(M // qb):
            cs = slice(j * qb, (j + 1) * qb)
            if qk_chunk is not None:
                s = jnp.dot(kn, qx_scr[qslot, :, cs], preferred_element_type=F32)
                s_scr[nxt, :, cs] = s
                cm_scr[nxt, :, cs] = jnp.max(s, axis=0, keepdims=True)
            if do_pv:
                m_old = m_scr[:, cs]
                m_new = jnp.maximum(m_old, cm_scr[cur, :, cs])
                p = jnp.exp2(s_scr[cur, :, cs] - m_new).astype(BF16)
                alpha = jnp.exp2(m_old - m_new)
                pv = jnp.dot(ve, p, preferred_element_type=F32)
                acc_scr[:, cs] = acc_scr[:, cs] * alpha + pv
                m_scr[:, cs] = m_new

    load_q(0, 0)
    step(-1, 0, 0, False)

    def qblock(i, carry):
        slot = i % 2
        m_scr[...] = jnp.full(m_scr.shape, -jnp.inf, F32)
        acc_scr[...] = jnp.zeros(acc_scr.shape, F32)
        for c in range(nchunk - 1):
            step(c, slot, c + 1, True)
        load_q(jnp.minimum(i + 1, nq - 1), 1 - slot)
        step(nchunk - 1, 1 - slot, 0, True)

        acc = acc_scr[...]
        inv = 1.0 / acc[HEAD_DIM:HEAD_DIM + 1, :]
        o = acc[0:HEAD_DIM, :] * inv
        cols = pl.ds(pl.multiple_of(i * tq, tq), tq)
        for g in range(Q_PER_KV):
            o_ref[g * HEAD_DIM:(g + 1) * HEAD_DIM, cols] = o[:, g * tq:(g + 1) * tq].astype(BF16)
        return carry

    lax.fori_loop(0, nq, qblock, 0)


def _attention(qhT, qlT, k, vT, *, tq, tk, qb):
    B, _, seq = qhT.shape
    gw = Q_PER_KV * HEAD_DIM
    assert (seq // tk) % 2 == 0
    return pl.pallas_call(
        functools.partial(_attn_kernel, tq=tq, tk=tk, qb=qb),
        grid=(B, N_KV_HEADS),
        in_specs=[
            pl.BlockSpec((None, gw, seq), lambda b, kh: (b, kh, 0)),
            pl.BlockSpec((None, gw, seq), lambda b, kh: (b, kh, 0)),
            pl.BlockSpec((None, seq, MXU_DEPTH), lambda b, kh: (b, 0, kh)),
            pl.BlockSpec((None, HEAD_DIM, seq), lambda b, kh: (b, kh, 0)),
        ],
        out_specs=pl.BlockSpec((None, gw, seq), lambda b, kh: (b, kh, 0)),
        out_shape=jax.ShapeDtypeStruct((B, ATT_WIDTH, seq), BF16),
        scratch_shapes=[
            pltpu.VMEM((PV_ROWS, seq), BF16),
            pltpu.VMEM((2, MXU_DEPTH, Q_PER_KV * tq), F8),
            pltpu.VMEM((1, Q_PER_KV * tq), F32),
            pltpu.VMEM((PV_ROWS, Q_PER_KV * tq), F32),
            pltpu.VMEM((2, tk, Q_PER_KV * tq), F32),
            pltpu.VMEM((2, 1, Q_PER_KV * tq), F32),
        ],
        compiler_params=_cparams(("parallel", "parallel")),
        name="attention",
    )(qhT, qlT, k, vT)


def _mem_kv_kernel(mem_ref, g_ref, w_ref, kT_ref, v_ref):
    D = mem_ref.shape[1]
    mn = _rms_rows(mem_ref[...], g_ref[...]).astype(BF16)
    kv = jnp.dot(mn, w_ref[...], preferred_element_type=F32)
    kT_ref[...] = kv[:, :D].T.astype(BF16)
    v_ref[...] = kv[:, D:].astype(BF16)


def _mem_kv(mem, g, w_kv):
    B, M, D = mem.shape
    L = w_kv.shape[0]
    return pl.pallas_call(
        _mem_kv_kernel,
        grid=(L, B),
        in_specs=[
            pl.BlockSpec((None, M, D), lambda l, b: (b, 0, 0)),
            pl.BlockSpec((1, D), lambda l, b: (0, 0)),
            pl.BlockSpec((None, D, 2 * D), lambda l, b: (l, 0, 0)),
        ],
        out_specs=[
            pl.BlockSpec((None, None, D, M), lambda l, b: (l, b, 0, 0)),
            pl.BlockSpec((None, None, M, D), lambda l, b: (l, b, 0, 0)),
        ],
        out_shape=[
            jax.ShapeDtypeStruct((L, B, D, M), BF16),
            jax.ShapeDtypeStruct((L, B, M, D), BF16),
        ],
        compiler_params=_cparams(("parallel", "parallel")),
        name="mem_kv",
    )(mem, g, w_kv)


def _post_kernel(x_ref, gm_ref, atT_ref, bg1_ref, wout_ref, xg_ref, wq_ref, mkT_ref, mv_ref, wo_ref,
                 fg_ref, wr_ref, x2_ref, h3_ref, affT_ref, *, rows_per_group):
    tm, D = x_ref.shape
    xhd = D // X_HEADS
    groups = [slice(r, r + rows_per_group) for r in range(0, tm, rows_per_group)]

    x1 = []
    for rs in groups:
        at = atT_ref[:, rs].astype(F32)
        ms = jnp.mean(at * at, axis=0, keepdims=True)
        atn = (at * lax.rsqrt(ms + EPS) * bg1_ref[:, rs]).astype(BF16)
        y = jnp.dot(gm_ref[rs, :], wout_ref[0:GM_WIDTH, :], preferred_element_type=F32)
        y = y + lax.dot_general(atn, wout_ref[GM_WIDTH:, :], (((0,), (0,)), ((), ())),
                                preferred_element_type=F32)
        x1.append(x_ref[rs, :] + y)

    q2 = []
    for x1g in x1:
        h2 = _rms_rows(x1g, xg_ref[...]).astype(BF16)
        q2.append((jnp.dot(h2, wq_ref[...], preferred_element_type=F32) * (xhd ** -0.5)).astype(BF16))

    scores = {}
    for hh in range(X_HEADS):
        cs = slice(hh * xhd, (hh + 1) * xhd)
        for gi in range(len(groups)):
            scores[hh, gi] = jnp.dot(q2[gi][:, cs], mkT_ref[cs, :], preferred_element_type=F32)
    outs = [[] for _ in groups]
    for hh in range(X_HEADS):
        cs = slice(hh * xhd, (hh + 1) * xhd)
        for gi in range(len(groups)):
            s = scores[hh, gi]
            s = s - jnp.max(s, axis=-1, keepdims=True)
            p = jnp.exp(s)
            p = (p / jnp.sum(p, axis=-1, keepdims=True)).astype(BF16)
            outs[gi].append(jnp.dot(p, mv_ref[:, cs], preferred_element_type=F32).astype(BF16))

    x2 = []
    for gi, rs in enumerate(groups):
        o2 = jnp.concatenate(outs[gi], axis=1)
        x2g = x1[gi] + jnp.dot(o2, wo_ref[...], preferred_element_type=F32)
        x2_ref[rs, :] = x2g
        x2.append(x2g)

    ne = affT_ref.shape[0]
    for gi, rs in enumerate(groups):
        h3 = _rms_rows(x2[gi], fg_ref[...]).astype(BF16)
        h3_ref[rs, :] = h3
        lg = jnp.dot(h3, wr_ref[...], preferred_element_type=F32).T[0:ne, :]
        lg = lg - jnp.max(lg, axis=0, keepdims=True)
        ex = jnp.exp(lg)
        affT_ref[:, rs] = ex / jnp.sum(ex, axis=0, keepdims=True)


def _post(x2d, gm, atT, bg1, w_out, xg, w_q, mkT, mv, w_o, fg, wr_pad, *, n_experts, layer, seq, tm):
    T, D = x2d.shape
    npb = seq // tm
    M = mv.shape[2]
    E = n_experts
    full = lambda shape: pl.BlockSpec(shape, lambda i: (0,) * len(shape))
    return pl.pallas_call(
        functools.partial(_post_kernel, rows_per_group=min(256, tm)),
        grid=(T // tm,),
        in_specs=[
            pl.BlockSpec((tm, D), lambda i: (i, 0)),
            pl.BlockSpec((tm, GM_WIDTH), lambda i: (i, 0)),
            pl.BlockSpec((None, ATT_WIDTH, tm), lambda i: (i // npb, 0, i % npb)),
            full((ATT_WIDTH, tm)),
            full((GM_WIDTH + ATT_WIDTH, D)),
            full((1, D)),
            full((D, D)),
            pl.BlockSpec((None, None, D, M), lambda i: (layer, i // npb, 0, 0)),
            pl.BlockSpec((None, None, M, D), lambda i: (layer, i // npb, 0, 0)),
            full((D, D)),
            full((1, D)),
            full((D, LANES)),
        ],
        out_specs=[
            pl.BlockSpec((tm, D), lambda i: (i, 0)),
            pl.BlockSpec((tm, D), lambda i: (i, 0)),
            pl.BlockSpec((E, tm), lambda i: (0, i)),
        ],
        out_shape=[
            jax.ShapeDtypeStruct((T, D), F32),
            jax.ShapeDtypeStruct((T, D), BF16),
            jax.ShapeDtypeStruct((E, T), F32),
        ],
        compiler_params=_cparams(("parallel",)),
        name="post",
    )(x2d, gm, atT, bg1, w_out, xg, w_q, mkT, mv, w_o, fg, wr_pad)


def _topk_kernel(aff_ref, slot_ref, off_ref, *, cap):
    E, R, L = aff_ref.shape
    a = aff_ref[...]
    bits = lax.bitcast_convert_type(a, jnp.int32)

    def count_ge(t):
        c = jnp.where(bits >= t, 1.0, 0.0)
        return jnp.sum(jnp.sum(c, axis=1, keepdims=True), axis=2, keepdims=True)

    def bis(_, carry):
        lo, hi = carry
        mid = lo + ((hi - lo) >> 1)
        ok = count_ge(mid) >= cap
        return jnp.where(ok, mid, lo), jnp.where(ok, hi, mid)

    lo0 = jnp.zeros((E, 1, 1), jnp.int32)
    hi0 = jnp.full((E, 1, 1), 0x3F800001, jnp.int32)
    thr, _ = lax.fori_loop(0, 31, bis, (lo0, hi0))

    kk = lax.broadcasted_iota(jnp.int32, (L, L), 0)
    nn = lax.broadcasted_iota(jnp.int32, (L, L), 1)
    upper = jnp.where(kk <= nn, 1.0, 0.0).astype(BF16)
    ones = jnp.ones((L, L), BF16)
    rr = lax.broadcasted_iota(jnp.int32, (E * R, E * R), 0)
    cc = lax.broadcasted_iota(jnp.int32, (E * R, E * R), 1)
    lower = jnp.where((rr // R == cc // R) & (cc < rr), 1.0, 0.0).astype(BF16)

    def prefix(xf):
        x2 = xf.reshape(E * R, L)
        xb = x2.astype(BF16)
        incl = jnp.dot(xb, upper, preferred_element_type=F32)
        tot = jnp.dot(xb, ones, preferred_element_type=F32).astype(BF16)
        rowoff = jnp.dot(lower, tot, preferred_element_type=F32)
        return (incl - x2 + rowoff).reshape(E, R, L), rowoff.reshape(E, R, L)

    gt = jnp.where(bits > thr, 1.0, 0.0)
    eq = jnp.where(bits == thr, 1.0, 0.0)
    n_gt = jnp.sum(jnp.sum(gt, axis=1, keepdims=True), axis=2, keepdims=True)
    need = cap - n_gt
    eq_rank, _ = prefix(eq)
    sel = gt + eq * jnp.where(eq_rank < need, 1.0, 0.0)
    pos, rowoff = prefix(sel)
    slot_ref[...] = jnp.where(sel > 0.0, pos, -1.0).astype(jnp.int32)
    off_ref[...] = rowoff.astype(jnp.int32)


def _topk(aff4, *, cap):
    E, B, R, L = aff4.shape
    return pl.pallas_call(
        functools.partial(_topk_kernel, cap=cap),
        grid=(B,),
        in_specs=[pl.BlockSpec((E, None, R, L), lambda b: (0, b, 0, 0))],
        out_specs=[
            pl.BlockSpec((None, E, R, L), lambda b: (b, 0, 0, 0)),
            pl.BlockSpec((None, E, R, L), lambda b: (b, 0, 0, 0)),
        ],
        out_shape=[
            jax.ShapeDtypeStruct((B, E, R, L), jnp.int32),
            jax.ShapeDtypeStruct((B, E, R, L), jnp.int32),
        ],
        compiler_params=_cparams(("parallel",)),
        name="topk",
    )(aff4)


def _gather_kernel(cnt_ref, slot_ref, gate_ref, h_ref, xs_ref, gc_ref, *, win):
    b = pl.program_id(0)
    eg = pl.program_id(1)
    group, cap, D = xs_ref.shape
    ne = pl.num_programs(1) * group
    nblk = slot_ref.shape[1]
    r_iota = lax.broadcasted_iota(jnp.int32, (win, TOK_BLOCK), 0)

    xs_ref[...] = jnp.zeros(xs_ref.shape, BF16)
    gc_ref[...] = jnp.zeros(gc_ref.shape, F32)

    def window(g, j):
        idx = (b * ne + eg * group + g) * (nblk + 1) + j
        s0 = cnt_ref[idx]
        s1 = cnt_ref[idx + 1]
        a0 = (s0 // 16) * 16
        return a0, s1

    def hits(g, j, lo, sa):
        srow = slot_ref[g, j]
        rel = jnp.where(srow >= lo, srow - sa, -1)
        return rel == r_iota

    def add_rows(g, j, sa, hit, part):
        rows = pl.ds(sa, win)
        xs_ref[g, rows, :] = xs_ref[g, rows, :] + part.astype(BF16)
        gc_ref[g, rows, :] += jnp.sum(jnp.where(hit, gate_ref[g, j], 0.0), axis=1, keepdims=True)

    extra = jnp.int32(0)
    for j in range(nblk):
        hb = h_ref[j * TOK_BLOCK:(j + 1) * TOK_BLOCK, :]
        masks, starts = [], []
        for g in range(group):
            a0, s1 = window(g, j)
            sa = pl.multiple_of(jnp.minimum(a0, cap - win), 16)
            extra = jnp.maximum(extra, s1 - (a0 + win))
            masks.append(hits(g, j, a0, sa))
            starts.append(sa)
        lhs = jnp.concatenate([jnp.where(m, 1.0, 0.0).astype(BF16) for m in masks], axis=0)
        res = jnp.dot(lhs, hb, preferred_element_type=F32)
        for g in range(group):
            add_rows(g, j, starts[g], masks[g], res[g * win:(g + 1) * win, :])

    @pl.when(extra > 0)
    def _():
        def blk(j, carry):
            hb = h_ref[pl.ds(pl.multiple_of(j * TOK_BLOCK, TOK_BLOCK), TOK_BLOCK), :]
            for g in range(group):
                a0, s1 = window(g, j)
                nw = (s1 - a0 + win - 1) // win

                def wbody(w, c2, g=g, a0=a0):
                    lo = a0 + w * win
                    sa = pl.multiple_of(jnp.minimum(lo, cap - win), 16)
                    hit = hits(g, j, lo, sa)
                    part = jnp.dot(jnp.where(hit, 1.0, 0.0).astype(BF16), hb, preferred_element_type=F32)
                    add_rows(g, j, sa, hit, part)
                    return c2

                lax.fori_loop(1, nw, wbody, 0)
            return carry

        lax.fori_loop(0, nblk, blk, 0)


def _gather(cnt, slots5, gates5, h3d, *, cap, win):
    B, seq, D = h3d.shape
    E = slots5.shape[1]
    nblk = seq // TOK_BLOCK
    group = MXU_DEPTH // win
    assert E % group == 0 and cap % win == 0
    gs = pltpu.PrefetchScalarGridSpec(
        num_scalar_prefetch=1,
        grid=(B, E // group),
        in_specs=[
            pl.BlockSpec((None, group, nblk, 1, TOK_BLOCK), lambda b, e, c: (b, e, 0, 0, 0)),
            pl.BlockSpec((group, None, nblk, 1, TOK_BLOCK), lambda b, e, c: (e, b, 0, 0, 0)),
            pl.BlockSpec((None, seq, D), lambda b, e, c: (b, 0, 0)),
        ],
        out_specs=[
            pl.BlockSpec((None, group, cap, D), lambda b, e, c: (b, e, 0, 0)),
            pl.BlockSpec((None, group, cap, 1), lambda b, e, c: (b, e, 0, 0)),
        ],
    )
    return pl.pallas_call(
        functools.partial(_gather_kernel, win=win),
        grid_spec=gs,
        out_shape=[
            jax.ShapeDtypeStruct((B, E, cap, D), BF16),
            jax.ShapeDtypeStruct((B, E, cap, 1), F32),
        ],
        compiler_params=_cparams(("parallel", "arbitrary")),
        name="gather",
    )(cnt, slots5, gates5, h3d)


def _ffn_kernel(xs_ref, gc_ref, wg_ref, wu_ref, wd_ref, y_ref, *, f_chunk):
    xs = xs_ref[...]
    y = None
    for f0 in range(0, wg_ref.shape[1], f_chunk):
        fs = slice(f0, f0 + f_chunk)
        a = jnp.dot(xs, wg_ref[:, fs].astype(BF16), preferred_element_type=F32)
        u = jnp.dot(xs, wu_ref[:, fs].astype(BF16), preferred_element_type=F32)
        hmid = (a * jax.nn.sigmoid(a) * u).astype(BF16)
        part = jnp.dot(hmid, wd_ref[fs, :].astype(BF16), preferred_element_type=F32)
        y = part if y is None else y + part
    y_ref[...] = (y * gc_ref[...]).astype(BF16)


def _ffn(xs, gc, w_gate, w_up, w_down, *, layer):
    B, E, cap, D = xs.shape
    Fd = w_gate.shape[-1]
    return pl.pallas_call(
        functools.partial(_ffn_kernel, f_chunk=min(512, Fd)),
        grid=(E, B),
        in_specs=[
            pl.BlockSpec((None, None, cap, D), lambda e, b: (b, e, 0, 0)),
            pl.BlockSpec((None, None, cap, 1), lambda e, b: (b, e, 0, 0)),
            pl.BlockSpec((None, None, D, Fd), lambda e, b: (layer, e, 0, 0)),
            pl.BlockSpec((None, None, D, Fd), lambda e, b: (layer, e, 0, 0)),
            pl.BlockSpec((None, None, Fd, D), lambda e, b: (layer, e, 0, 0)),
        ],
        out_specs=pl.BlockSpec((None, None, cap, D), lambda e, b: (b, e, 0, 0)),
        out_shape=jax.ShapeDtypeStruct((B, E, cap, D), BF16),
        compiler_params=_cparams(("parallel", "parallel")),
        name="ffn",
    )(xs, gc, w_gate, w_up, w_down)


def _scatter_kernel(cnt_ref, slot_ref, y_ref, x_ref, fg_ref, o_ref, *, win, final_norm):
    b = pl.program_id(0)
    sb = pl.program_id(1)
    ne, cap, D = y_ref.shape
    nloc = slot_ref.shape[1]
    nblk = nloc * pl.num_programs(1)
    r_iota = lax.broadcasted_iota(jnp.int32, (win, TOK_BLOCK), 0)
    contract0 = (((0,), (0,)), ((), ()))

    def onehot_t(srow, lo, sa):
        rel = jnp.where(srow >= lo, srow - sa, -1)
        return jnp.where(rel == r_iota, 1.0, 0.0).astype(BF16)

    def window(e, j):
        idx = (b * ne + e) * (nblk + 1) + j
        s0 = cnt_ref[idx]
        s1 = cnt_ref[idx + 1]
        a0 = (s0 // 16) * 16
        sa = pl.multiple_of(jnp.minimum(a0, cap - win), 16)
        return a0, sa, s1 - (a0 + win)

    extra = jnp.int32(0)
    per_dot = MXU_DEPTH // win
    for t in range(nloc):
        j = sb * nloc + t
        rows = slice(t * TOK_BLOCK, (t + 1) * TOK_BLOCK)
        acc = x_ref[rows, :]
        for e0 in range(0, ne, per_dot):
            gs, ys = [], []
            for e in range(e0, e0 + per_dot):
                a0, sa, over = window(e, j)
                extra = jnp.maximum(extra, over)
                gs.append(onehot_t(slot_ref[e, t], a0, sa))
                ys.append(y_ref[e, pl.ds(sa, win), :])
            acc = acc + lax.dot_general(jnp.concatenate(gs, axis=0), jnp.concatenate(ys, axis=0),
                                        contract0, preferred_element_type=F32)
        o_ref[rows, :] = acc

    @pl.when(extra > 0)
    def _():
        for t in range(nloc):
            j = sb * nloc + t
            rows = slice(t * TOK_BLOCK, (t + 1) * TOK_BLOCK)
            for e in range(ne):
                idx = (b * ne + e) * (nblk + 1) + j
                s0 = cnt_ref[idx]
                s1 = cnt_ref[idx + 1]
                a0 = (s0 // 16) * 16
                nw = (s1 - a0 + win - 1) // win

                def wbody(w, carry):
                    lo = a0 + w * win
                    sa = pl.multiple_of(jnp.minimum(lo, cap - win), 16)
                    g = onehot_t(slot_ref[e, t], lo, sa)
                    o_ref[rows, :] += lax.dot_general(g, y_ref[e, pl.ds(sa, win), :], contract0,
                                                      preferred_element_type=F32)
                    return carry

                lax.fori_loop(1, nw, wbody, 0)

    if final_norm:
        o_ref[...] = _rms_rows(o_ref[...], fg_ref[...])


def _scatter(cnt, slots5, y, x3d, fg, *, win, sb_rows, final_norm):
    B, seq, D = x3d.shape
    E, cap = y.shape[1], y.shape[2]
    nloc = sb_rows // TOK_BLOCK
    assert MXU_DEPTH % win == 0 and E % (MXU_DEPTH // win) == 0
    gs = pltpu.PrefetchScalarGridSpec(
        num_scalar_prefetch=1,
        grid=(B, seq // sb_rows),
        in_specs=[
            pl.BlockSpec((None, E, nloc, 1, TOK_BLOCK), lambda b, s, c: (b, 0, s, 0, 0)),
            pl.BlockSpec((None, E, cap, D), lambda b, s, c: (b, 0, 0, 0), pipeline_mode=pl.Buffered(1)),
            pl.BlockSpec((None, sb_rows, D), lambda b, s, c: (b, s, 0)),
            pl.BlockSpec((1, D), lambda b, s, c: (0, 0)),
        ],
        out_specs=pl.BlockSpec((None, sb_rows, D), lambda b, s, c: (b, s, 0)),
    )
    return pl.pallas_call(
        functools.partial(_scatter_kernel, win=win, final_norm=final_norm),
        grid_spec=gs,
        out_shape=jax.ShapeDtypeStruct((B, seq, D), F32),
        compiler_params=_cparams(("parallel", "arbitrary")),
        name="scatter",
    )(cnt, slots5, y, x3d, fg)


def _rope_tables_T(seq):
    rows = seq // GRID_W
    row_id = jnp.repeat(jnp.arange(rows, dtype=F32), GRID_W)
    col_id = jnp.tile(jnp.arange(GRID_W, dtype=F32), rows)
    n_pairs = HEAD_DIM // 4
    freqs = jnp.exp(-math.log(ROPE_THETA) * jnp.arange(n_pairs, dtype=F32) / n_pairs)
    ang = jnp.concatenate([freqs[:, None] * row_id[None, :], freqs[:, None] * col_id[None, :]], axis=0)
    return jnp.cos(ang), jnp.sin(ang)


def _head_perm(n_heads):
    base = jnp.concatenate([jnp.arange(0, HEAD_DIM, 2), jnp.arange(1, HEAD_DIM, 2)])
    return (jnp.arange(n_heads)[:, None] * HEAD_DIM + base[None, :]).reshape(-1)


def kernel(x, mem, mix_norm_g, w_in, gm_v_norm_g, gm_w_s, gm_b_s, q_norm_g, k_norm_g, branch_norm_g, w_out,
           xattn_norm_g, mem_norm_g, xattn_w_q, xattn_w_kv, xattn_w_o, ffn_norm_g, w_router, w_gate, w_up,
           w_down, final_norm_g):
    B, seq, D = x.shape
    L = w_in.shape[0]
    E = w_router.shape[-1]
    T = B * seq
    cap = EC_FACTOR * seq // E
    tm = min(1024, seq)
    tm_in = min(2048, seq)
    tq = min(256, seq)
    tk = min(2048, seq // 2)
    win = min(64, cap)
    sb_rows = min(1024, seq)
    nblk = seq // TOK_BLOCK

    cosT, sinT = _rope_tables_T(seq)
    o_q = 2 * GM_WIDTH
    o_k = o_q + ATT_WIDTH
    o_v = o_k + KV_WIDTH
    cols = jnp.concatenate([jnp.arange(o_q), o_q + _head_perm(N_Q_HEADS), o_k + _head_perm(N_KV_HEADS),
                            jnp.arange(o_v, o_v + KV_WIDTH)])
    hp = _head_perm(1)

    mkT_all, mv_all = _mem_kv(mem, mem_norm_g.reshape(1, D), xattn_w_kv.astype(BF16))

    x2d = x.reshape(T, D)
    out = None
    for l in range(L):
        w_in_l = w_in[l][:, cols].astype(BF16)
        bs = jnp.broadcast_to(gm_b_s[l][:, :, None], (GM_GROUPS, CHUNK, LANES))
        qg = jnp.broadcast_to(q_norm_g[l][hp][:, None], (HEAD_DIM, tm_in))
        kg = jnp.broadcast_to(k_norm_g[l][hp][:, None], (HEAD_DIM, tm_in))
        gm, qhT, qlT, k8, vT = _mixer_in(
            x2d, mix_norm_g[l].reshape(1, D), w_in_l, gm_v_norm_g[l].reshape(1, GM_WIDTH),
            gm_w_s[l].astype(BF16), bs, qg, kg, branch_norm_g[l, 0].reshape(1, GM_WIDTH), cosT, sinT,
            seq=seq, tm=tm_in)
        atT = _attention(qhT, qlT, k8.reshape(B, seq, N_KV_HEADS * MXU_DEPTH), vT,
                         tq=tq, tk=tk, qb=min(256, tq))

        bg1 = jnp.broadcast_to(branch_norm_g[l, 1][:, None], (ATT_WIDTH, tm))
        wr_pad = jnp.pad(w_router[l].astype(BF16), ((0, 0), (0, LANES - E)))
        x2, h3, affT = _post(
            x2d, gm, atT, bg1, w_out[l].astype(BF16), xattn_norm_g[l].reshape(1, D),
            xattn_w_q[l].astype(BF16), mkT_all, mv_all, xattn_w_o[l].astype(BF16),
            ffn_norm_g[l].reshape(1, D), wr_pad, n_experts=E, layer=l, seq=seq, tm=tm)

        slots, offs = _topk(affT.reshape(E, B, seq // LANES, LANES), cap=cap)
        cnt = offs[:, :, ::TOK_BLOCK // LANES, 0]
        cnt = jnp.concatenate([cnt, jnp.full((B, E, 1), cap, jnp.int32)], axis=-1).reshape(-1)
        slots5 = slots.reshape(B, E, nblk, 1, TOK_BLOCK)
        gates5 = affT.reshape(E, B, nblk, 1, TOK_BLOCK)

        xs, gc = _gather(cnt, slots5, gates5, h3.reshape(B, seq, D), cap=cap, win=win)
        y = _ffn(xs, gc, w_gate, w_up, w_down, layer=l)
        out = _scatter(cnt, slots5, y, x2.reshape(B, seq, D), final_norm_g.reshape(1, D),
                       win=win, sb_rows=sb_rows, final_norm=(l == L - 1))
        x2d = out.reshape(T, D)
    return out
```

```python
import functools
import math

import jax
import jax.numpy as jnp
from jax import lax
from jax.experimental import pallas as pl
from jax.experimental.pallas import tpu as pltpu

F32 = jnp.float32
BF16 = jnp.bfloat16
F8 = jnp.float8_e4m3fn
F8_MAX = 448.0

EPS = 1e-6
CHUNK = 128
GM_GROUPS = 4
GM_WIDTH = 512
HEAD_DIM = 64
N_Q_HEADS = 8
N_KV_HEADS = 2
Q_PER_KV = N_Q_HEADS // N_KV_HEADS
ATT_WIDTH = N_Q_HEADS * HEAD_DIM
KV_WIDTH = N_KV_HEADS * HEAD_DIM
ROPE_THETA = 10000.0
GRID_W = 64
X_HEADS = 4
N_EXPERTS = 16
EC_FACTOR = 2

LANES = 128
MXU_DEPTH = 256
PV_ROWS = HEAD_DIM + 32
VMEM_LIMIT = 56 * 1024 * 1024

TOK_BLOCK = 256
LOG2E = 1.4426950408889634


def _cparams(sem, vmem=VMEM_LIMIT, **kw):
    return pltpu.CompilerParams(dimension_semantics=sem, vmem_limit_bytes=vmem, **kw)


def _rms_rows(x, g):
    ms = jnp.mean(x * x, axis=-1, keepdims=True)
    return x * lax.rsqrt(ms + EPS) * g


def _gelu_tanh(x):
    c = math.sqrt(2.0 / math.pi)
    return 0.5 * x * (1.0 + jnp.tanh(c * (x + 0.044715 * (x * x * x))))


def _norm_rope_T(xT, gcol, cosT, sinT, n_heads, scale):
    half = HEAD_DIM // 2
    outs = []
    for h in range(n_heads):
        blk = xT[h * HEAD_DIM:(h + 1) * HEAD_DIM, :]
        ms = jnp.mean(blk * blk, axis=0, keepdims=True)
        n = blk * lax.rsqrt(ms + EPS) * gcol
        e = n[:half, :]
        o = n[half:, :]
        re = (e * cosT - o * sinT) * scale
        ro = (e * sinT + o * cosT) * scale
        outs.append(jnp.concatenate([re, ro], axis=0))
    return outs


def _split_f8(x):
    x = jnp.clip(x, -F8_MAX, F8_MAX)
    hi = x.astype(F8).astype(F32)
    lo = (x - hi).astype(F8).astype(F32)
    return hi, lo


def _mixer_in_kernel(x_ref, g_ref, w_ref, vg_ref, ws_ref, bs_ref, qg_ref, kg_ref, bg0_ref,
                     cos_ref, sin_ref, gm_ref, qhT_ref, qlT_ref, k_ref, vT_ref, gm_scr, *, rows_per_group):
    tm = x_ref.shape[0]
    o_q = 2 * GM_WIDTH
    o_k = o_q + ATT_WIDTH
    o_v = o_k + KV_WIDTH
    groups = [slice(r, r + rows_per_group) for r in range(0, tm, rows_per_group)]

    projs = []
    for rs in groups:
        h = _rms_rows(x_ref[rs, :], g_ref[...]).astype(BF16)
        projs.append(jnp.dot(h, w_ref[...], preferred_element_type=F32))

    for rs, proj in zip(groups, projs):
        u = _gelu_tanh(proj[:, :GM_WIDTH])
        v = _gelu_tanh(proj[:, GM_WIDTH:2 * GM_WIDTH])
        vn = _rms_rows(v, vg_ref[...]).astype(BF16)
        n_chunks = rows_per_group // CHUNK
        for g in range(GM_GROUPS):
            cs = slice(g * LANES, (g + 1) * LANES)
            vcat = jnp.concatenate([vn[c * CHUNK:(c + 1) * CHUNK, cs] for c in range(n_chunks)], axis=1)
            mixed = jnp.dot(ws_ref[g], vcat, preferred_element_type=F32)
            for c in range(n_chunks):
                cr = slice(c * CHUNK, (c + 1) * CHUNK)
                sr = slice(rs.start + c * CHUNK, rs.start + (c + 1) * CHUNK)
                gm_scr[sr, cs] = u[cr, cs] * (mixed[:, c * LANES:(c + 1) * LANES] + bs_ref[g])
        gm_ref[rs, :] = _rms_rows(gm_scr[rs, :], bg0_ref[...]).astype(BF16)

        cosT = cos_ref[:, rs]
        sinT = sin_ref[:, rs]
        qT = proj[:, o_q:o_k].T
        q_heads = _norm_rope_T(qT, qg_ref[:, rs], cosT, sinT, N_Q_HEADS, (HEAD_DIM ** -0.5) * LOG2E)
        for hh in range(N_Q_HEADS):
            hi, lo = _split_f8(q_heads[hh])
            qhT_ref[hh * HEAD_DIM:(hh + 1) * HEAD_DIM, rs] = hi.astype(F8)
            qlT_ref[hh * HEAD_DIM:(hh + 1) * HEAD_DIM, rs] = lo.astype(F8)

        kT = proj[:, o_k:o_v].T
        k_heads = _norm_rope_T(kT, kg_ref[:, rs], cosT, sinT, N_KV_HEADS, 1.0)
        pieces = []
        for kh in k_heads:
            hi, lo = _split_f8(kh)
            pieces += [hi, lo, hi, jnp.zeros_like(hi)]
        k_ref[rs, :] = jnp.concatenate(pieces, axis=0).T.astype(F8)

        vT_ref[:, rs] = proj[:, o_v:].T.astype(BF16)


def _mixer_in(x2d, g, w_in, vg, ws, bs, qg, kg, bg0, cosT, sinT, *, seq, tm):
    T, D = x2d.shape
    B = T // seq
    npb = seq // tm
    in_w = w_in.shape[1]
    full = lambda shape: pl.BlockSpec(shape, lambda i: (0,) * len(shape))
    return pl.pallas_call(
        functools.partial(_mixer_in_kernel, rows_per_group=min(256, tm)),
        grid=(T // tm,),
        in_specs=[
            pl.BlockSpec((tm, D), lambda i: (i, 0)),
            full((1, D)),
            full((D, in_w)),
            full((1, GM_WIDTH)),
            full((GM_GROUPS, CHUNK, CHUNK)),
            full((GM_GROUPS, CHUNK, LANES)),
            full((HEAD_DIM, tm)),
            full((HEAD_DIM, tm)),
            full((1, GM_WIDTH)),
            pl.BlockSpec((HEAD_DIM // 2, tm), lambda i: (0, i % npb)),
            pl.BlockSpec((HEAD_DIM // 2, tm), lambda i: (0, i % npb)),
        ],
        out_specs=[
            pl.BlockSpec((tm, GM_WIDTH), lambda i: (i, 0)),
            pl.BlockSpec((None, ATT_WIDTH, tm), lambda i: (i // npb, 0, i % npb)),
            pl.BlockSpec((None, ATT_WIDTH, tm), lambda i: (i // npb, 0, i % npb)),
            pl.BlockSpec((tm, N_KV_HEADS * MXU_DEPTH), lambda i: (i, 0)),
            pl.BlockSpec((None, KV_WIDTH, tm), lambda i: (i // npb, 0, i % npb)),
        ],
        out_shape=[
            jax.ShapeDtypeStruct((T, GM_WIDTH), BF16),
            jax.ShapeDtypeStruct((B, ATT_WIDTH, seq), F8),
            jax.ShapeDtypeStruct((B, ATT_WIDTH, seq), F8),
            jax.ShapeDtypeStruct((T, N_KV_HEADS * MXU_DEPTH), F8),
            jax.ShapeDtypeStruct((B, KV_WIDTH, seq), BF16),
        ],
        scratch_shapes=[pltpu.VMEM((tm, GM_WIDTH), F32)],
        compiler_params=_cparams(("parallel",)),
        name="mixer_in",
    )(x2d, g, w_in, vg, ws, bs, qg, kg, bg0, cosT, sinT)


def _attn_kernel(qhT_ref, qlT_ref, k_ref, vT_ref, o_ref, vext_scr, qx_scr, m_scr, acc_scr, s_scr, cm_scr,
                 *, tq, tk, qb):
    seq = k_ref.shape[0]
    M = Q_PER_KV * tq
    nq = seq // tq
    nchunk = seq // tk

    vext_scr[0:HEAD_DIM, :] = vT_ref[...]
    row = lax.broadcasted_iota(jnp.int32, (PV_ROWS - HEAD_DIM, seq), 0)
    vext_scr[HEAD_DIM:, :] = jnp.where(row == 0, 1.0, 0.0).astype(BF16)

    def load_q(i, slot):
        cols = pl.ds(pl.multiple_of(i * tq, tq), tq)
        qh = jnp.concatenate([qhT_ref[g * HEAD_DIM:(g + 1) * HEAD_DIM, cols] for g in range(Q_PER_KV)], axis=1)
        ql = jnp.concatenate([qlT_ref[g * HEAD_DIM:(g + 1) * HEAD_DIM, cols] for g in range(Q_PER_KV)], axis=1)
        qx_scr[slot] = jnp.concatenate([qh, qh, ql, jnp.zeros_like(qh)], axis=0)

    def step(c, qslot, qk_chunk, do_pv):
        cur = c % 2
        nxt = 1 - cur
        if qk_chunk is not None:
            kn = k_ref[qk_chunk * tk:(qk_chunk + 1) * tk, :]
        if do_pv:
            ve = vext_scr[:, c * tk:(c + 1) * tk]
        for j in range(M // qb):
            cs = slice(j * qb, (j + 1) * qb)
            if qk_chunk is not None:
                s = jnp.dot(kn, qx_scr[qslot, :, cs], preferred_element_type=F32)
                s_scr[nxt, :, cs] = s
                cm_scr[nxt, :, cs] = jnp.max(s, axis=0, keepdims=True)
            if do_pv:
                m_old = m_scr[:, cs]
                m_new = jnp.maximum(m_old, cm_scr[cur, :, cs])
                p = jnp.exp2(s_scr[cur, :, cs] - m_new).astype(BF16)
                alpha = jnp.exp2(m_old - m_new)
                pv = jnp.dot(ve, p, preferred_element_type=F32)
                acc_scr[:, cs] = acc_scr[:, cs] * alpha + pv
                m_scr[:, cs] = m_new

    load_q(0, 0)
    step(-1, 0, 0, False)

    def qblock(i, carry):
        slot = i % 2
        m_scr[...] = jnp.full(m_scr.shape, -jnp.inf, F32)
        acc_scr[...] = jnp.zeros(acc_scr.shape, F32)
        for c in range(nchunk - 1):
            step(c, slot, c + 1, True)
        load_q(jnp.minimum(i + 1, nq - 1), 1 - slot)
        step(nchunk - 1, 1 - slot, 0, True)

        acc = acc_scr[...]
        inv = 1.0 / acc[HEAD_DIM:HEAD_DIM + 1, :]
        o = acc[0:HEAD_DIM, :] * inv
        cols = pl.ds(pl.multiple_of(i * tq, tq), tq)
        for g in range(Q_PER_KV):
            o_ref[g * HEAD_DIM:(g + 1) * HEAD_DIM, cols] = o[:, g * tq:(g + 1) * tq].astype(BF16)
        return carry

    lax.fori_loop(0, nq, qblock, 0)


def _attention(qhT, qlT, k, vT, *, tq, tk, qb):
    B, _, seq = qhT.shape
    gw = Q_PER_KV * HEAD_DIM
    assert (seq // tk) % 2 == 0
    return pl.pallas_call(
        functools.partial(_attn_kernel, tq=tq, tk=tk, qb=qb),
        grid=(B, N_KV_HEADS),
        in_specs=[
            pl.BlockSpec((None, gw, seq), lambda b, kh: (b, kh, 0)),
            pl.BlockSpec((None, gw, seq), lambda b, kh: (b, kh, 0)),
            pl.BlockSpec((None, seq, MXU_DEPTH), lambda b, kh: (b, 0, kh)),
            pl.BlockSpec((None, HEAD_DIM, seq), lambda b, kh: (b, kh, 0)),
        ],
        out_specs=pl.BlockSpec((None, gw, seq), lambda b, kh: (b, kh, 0)),
        out_shape=jax.ShapeDtypeStruct((B, ATT_WIDTH, seq), BF16),
        scratch_shapes=[
            pltpu.VMEM((PV_ROWS, seq), BF16),
            pltpu.VMEM((2, MXU_DEPTH, Q_PER_KV * tq), F8),
            pltpu.VMEM((1, Q_PER_KV * tq), F32),
            pltpu.VMEM((PV_ROWS, Q_PER_KV * tq), F32),
            pltpu.VMEM((2, tk, Q_PER_KV * tq), F32),
            pltpu.VMEM((2, 1, Q_PER_KV * tq), F32),
        ],
        compiler_params=_cparams(("parallel", "parallel")),
        name="attention",
    )(qhT, qlT, k, vT)


def _mem_kv_kernel(mem_ref, g_ref, w_ref, kT_ref, v_ref):
    D = mem_ref.shape[1]
    mn = _rms_rows(mem_ref[...], g_ref[...]).astype(BF16)
    kv = jnp.dot(mn, w_ref[...], preferred_element_type=F32)
    kT_ref[...] = kv[:, :D].T.astype(BF16)
    v_ref[...] = kv[:, D:].astype(BF16)


def _mem_kv(mem, g, w_kv):
    B, M, D = mem.shape
    L = w_kv.shape[0]
    return pl.pallas_call(
        _mem_kv_kernel,
        grid=(L, B),
        in_specs=[
            pl.BlockSpec((None, M, D), lambda l, b: (b, 0, 0)),
            pl.BlockSpec((1, D), lambda l, b: (0, 0)),
            pl.BlockSpec((None, D, 2 * D), lambda l, b: (l, 0, 0)),
        ],
        out_specs=[
            pl.BlockSpec((None, None, D, M), lambda l, b: (l, b, 0, 0)),
            pl.BlockSpec((None, None, M, D), lambda l, b: (l, b, 0, 0)),
        ],
        out_shape=[
            jax.ShapeDtypeStruct((L, B, D, M), BF16),
            jax.ShapeDtypeStruct((L, B, M, D), BF16),
        ],
        compiler_params=_cparams(("parallel", "parallel")),
        name="mem_kv",
    )(mem, g, w_kv)


def _post_kernel(x_ref, gm_ref, atT_ref, bg1_ref, wout_ref, xg_ref, wq_ref, mkT_ref, mv_ref, wo_ref,
                 fg_ref, wr_ref, x2_ref, h3_ref, affT_ref, *, rows_per_group):
    tm, D = x_ref.shape
    xhd = D // X_HEADS
    groups = [slice(r, r + rows_per_group) for r in range(0, tm, rows_per_group)]

    x1 = []
    for rs in groups:
        at = atT_ref[:, rs].astype(F32)
        ms = jnp.mean(at * at, axis=0, keepdims=True)
        atn = (at * lax.rsqrt(ms + EPS) * bg1_ref[:, rs]).astype(BF16)
        y = jnp.dot(gm_ref[rs, :], wout_ref[0:GM_WIDTH, :], preferred_element_type=F32)
        y = y + lax.dot_general(atn, wout_ref[GM_WIDTH:, :], (((0,), (0,)), ((), ())),
                                preferred_element_type=F32)
        x1.append(x_ref[rs, :] + y)

    q2 = []
    for x1g in x1:
        h2 = _rms_rows(x1g, xg_ref[...]).astype(BF16)
        q2.append((jnp.dot(h2, wq_ref[...], preferred_element_type=F32) * (xhd ** -0.5)).astype(BF16))

    scores = {}
    for hh in range(X_HEADS):
        cs = slice(hh * xhd, (hh + 1) * xhd)
        for gi in range(len(groups)):
            scores[hh, gi] = jnp.dot(q2[gi][:, cs], mkT_ref[cs, :], preferred_element_type=F32)
    outs = [[] for _ in groups]
    for hh in range(X_HEADS):
        cs = slice(hh * xhd, (hh + 1) * xhd)
        for gi in range(len(groups)):
            s = scores[hh, gi]
            s = s - jnp.max(s, axis=-1, keepdims=True)
            p = jnp.exp(s)
            p = (p / jnp.sum(p, axis=-1, keepdims=True)).astype(BF16)
            outs[gi].append(jnp.dot(p, mv_ref[:, cs], preferred_element_type=F32).astype(BF16))

    x2 = []
    for gi, rs in enumerate(groups):
        o2 = jnp.concatenate(outs[gi], axis=1)
        x2g = x1[gi] + jnp.dot(o2, wo_ref[...], preferred_element_type=F32)
        x2_ref[rs, :] = x2g
        x2.append(x2g)

    ne = affT_ref.shape[0]
    for gi, rs in enumerate(groups):
        h3 = _rms_rows(x2[gi], fg_ref[...]).astype(BF16)
        h3_ref[rs, :] = h3
        lg = jnp.dot(h3, wr_ref[...], preferred_element_type=F32).T[0:ne, :]
        lg = lg - jnp.max(lg, axis=0, keepdims=True)
        ex = jnp.exp(lg)
        affT_ref[:, rs] = ex / jnp.sum(ex, axis=0, keepdims=True)


def _post(x2d, gm, atT, bg1, w_out, xg, w_q, mkT, mv, w_o, fg, wr_pad, *, n_experts, layer, seq, tm):
    T, D = x2d.shape
    npb = seq // tm
    M = mv.shape[2]
    E = n_experts
    full = lambda shape: pl.BlockSpec(shape, lambda i: (0,) * len(shape))
    return pl.pallas_call(
        functools.partial(_post_kernel, rows_per_group=min(256, tm)),
        grid=(T // tm,),
        in_specs=[
            pl.BlockSpec((tm, D), lambda i: (i, 0)),
            pl.BlockSpec((tm, GM_WIDTH), lambda i: (i, 0)),
            pl.BlockSpec((None, ATT_WIDTH, tm), lambda i: (i // npb, 0, i % npb)),
            full((ATT_WIDTH, tm)),
            full((GM_WIDTH + ATT_WIDTH, D)),
            full((1, D)),
            full((D, D)),
            pl.BlockSpec((None, None, D, M), lambda i: (layer, i // npb, 0, 0)),
            pl.BlockSpec((None, None, M, D), lambda i: (layer, i // npb, 0, 0)),
            full((D, D)),
            full((1, D)),
            full((D, LANES)),
        ],
        out_specs=[
            pl.BlockSpec((tm, D), lambda i: (i, 0)),
            pl.BlockSpec((tm, D), lambda i: (i, 0)),
            pl.BlockSpec((E, tm), lambda i: (0, i)),
        ],
        out_shape=[
            jax.ShapeDtypeStruct((T, D), F32),
            jax.ShapeDtypeStruct((T, D), BF16),
            jax.ShapeDtypeStruct((E, T), F32),
        ],
        compiler_params=_cparams(("parallel",)),
        name="post",
    )(x2d, gm, atT, bg1, w_out, xg, w_q, mkT, mv, w_o, fg, wr_pad)


def _topk_kernel(aff_ref, slot_ref, off_ref, *, cap):
    E, R, L = aff_ref.shape
    a = aff_ref[...]
    bits = lax.bitcast_convert_type(a, jnp.int32)

    def count_ge(t):
        c = jnp.where(bits >= t, 1.0, 0.0)
        return jnp.sum(jnp.sum(c, axis=1, keepdims=True), axis=2, keepdims=True)

    def bis(_, carry):
        lo, hi = carry
        mid = lo + ((hi - lo) >> 1)
        ok = count_ge(mid) >= cap
        return jnp.where(ok, mid, lo), jnp.where(ok, hi, mid)

    lo0 = jnp.zeros((E, 1, 1), jnp.int32)
    hi0 = jnp.full((E, 1, 1), 0x3F800001, jnp.int32)
    thr, _ = lax.fori_loop(0, 31, bis, (lo0, hi0))

    kk = lax.broadcasted_iota(jnp.int32, (L, L), 0)
    nn = lax.broadcasted_iota(jnp.int32, (L, L), 1)
    upper = jnp.where(kk <= nn, 1.0, 0.0).astype(BF16)
    ones = jnp.ones((L, L), BF16)
    rr = lax.broadcasted_iota(jnp.int32, (E * R, E * R), 0)
    cc = lax.broadcasted_iota(jnp.int32, (E * R, E * R), 1)
    lower = jnp.where((rr // R == cc // R) & (cc < rr), 1.0, 0.0).astype(BF16)

    def prefix(xf):
        x2 = xf.reshape(E * R, L)
        xb = x2.astype(BF16)
        incl = jnp.dot(xb, upper, preferred_element_type=F32)
        tot = jnp.dot(xb, ones, preferred_element_type=F32).astype(BF16)
        rowoff = jnp.dot(lower, tot, preferred_element_type=F32)
        return (incl - x2 + rowoff).reshape(E, R, L), rowoff.reshape(E, R, L)

    gt = jnp.where(bits > thr, 1.0, 0.0)
    eq = jnp.where(bits == thr, 1.0, 0.0)
    n_gt = jnp.sum(jnp.sum(gt, axis=1, keepdims=True), axis=2, keepdims=True)
    need = cap - n_gt
    eq_rank, _ = prefix(eq)
    sel = gt + eq * jnp.where(eq_rank < need, 1.0, 0.0)
    pos, rowoff = prefix(sel)
    slot_ref[...] = jnp.where(sel > 0.0, pos, -1.0).astype(jnp.int32)
    off_ref[...] = rowoff.astype(jnp.int32)


def _topk(aff4, *, cap):
    E, B, R, L = aff4.shape
    return pl.pallas_call(
        functools.partial(_topk_kernel, cap=cap),
        grid=(B,),
        in_specs=[pl.BlockSpec((E, None, R, L), lambda b: (0, b, 0, 0))],
        out_specs=[
            pl.BlockSpec((None, E, R, L), lambda b: (b, 0, 0, 0)),
            pl.BlockSpec((None, E, R, L), lambda b: (b, 0, 0, 0)),
        ],
        out_shape=[
            jax.ShapeDtypeStruct((B, E, R, L), jnp.int32),
            jax.ShapeDtypeStruct((B, E, R, L), jnp.int32),
        ],
        compiler_params=_cparams(("parallel",)),
        name="topk",
    )(aff4)


def _gather_kernel(cnt_ref, slot_ref, gate_ref, h_ref, xs_ref, gc_ref, *, win):
    b = pl.program_id(0)
    eg = pl.program_id(1)
    group, cap, D = xs_ref.shape
    ne = pl.num_programs(1) * group
    nblk = slot_ref.shape[1]
    r_iota = lax.broadcasted_iota(jnp.int32, (win, TOK_BLOCK), 0)

    xs_ref[...] = jnp.zeros(xs_ref.shape, BF16)
    gc_ref[...] = jnp.zeros(gc_ref.shape, F32)

    def window(g, j):
        idx = (b * ne + eg * group + g) * (nblk + 1) + j
        s0 = cnt_ref[idx]
        s1 = cnt_ref[idx + 1]
        a0 = (s0 // 16) * 16
        return a0, s1

    def hits(g, j, lo, sa):
        srow = slot_ref[g, j]
        rel = jnp.where(srow >= lo, srow - sa, -1)
        return rel == r_iota

    def add_rows(g, j, sa, hit, part):
        rows = pl.ds(sa, win)
        xs_ref[g, rows, :] = xs_ref[g, rows, :] + part.astype(BF16)
        gc_ref[g, rows, :] += jnp.sum(jnp.where(hit, gate_ref[g, j], 0.0), axis=1, keepdims=True)

    extra = jnp.int32(0)
    for j in range(nblk):
        hb = h_ref[j * TOK_BLOCK:(j + 1) * TOK_BLOCK, :]
        masks, starts = [], []
        for g in range(group):
            a0, s1 = window(g, j)
            sa = pl.multiple_of(jnp.minimum(a0, cap - win), 16)
            extra = jnp.maximum(extra, s1 - (a0 + win))
            masks.append(hits(g, j, a0, sa))
            starts.append(sa)
        lhs = jnp.concatenate([jnp.where(m, 1.0, 0.0).astype(BF16) for m in masks], axis=0)
        res = jnp.dot(lhs, hb, preferred_element_type=F32)
        for g in range(group):
            add_rows(g, j, starts[g], masks[g], res[g * win:(g + 1) * win, :])

    @pl.when(extra > 0)
    def _():
        def blk(j, carry):
            hb = h_ref[pl.ds(pl.multiple_of(j * TOK_BLOCK, TOK_BLOCK), TOK_BLOCK), :]
            for g in range(group):
                a0, s1 = window(g, j)
                nw = (s1 - a0 + win - 1) // win

                def wbody(w, c2, g=g, a0=a0):
                    lo = a0 + w * win
                    sa = pl.multiple_of(jnp.minimum(lo, cap - win), 16)
                    hit = hits(g, j, lo, sa)
                    part = jnp.dot(jnp.where(hit, 1.0, 0.0).astype(BF16), hb, preferred_element_type=F32)
                    add_rows(g, j, sa, hit, part)
                    return c2

                lax.fori_loop(1, nw, wbody, 0)
            return carry

        lax.fori_loop(0, nblk, blk, 0)


def _gather(cnt, slots5, gates5, h3d, *, cap, win):
    B, seq, D = h3d.shape
    E = slots5.shape[1]
    nblk = seq // TOK_BLOCK
    group = MXU_DEPTH // win
    assert E % group == 0 and cap % win == 0
    gs = pltpu.PrefetchScalarGridSpec(
        num_scalar_prefetch=1,
        grid=(B, E // group),
        in_specs=[
            pl.BlockSpec((None, group, nblk, 1, TOK_BLOCK), lambda b, e, c: (b, e, 0, 0, 0)),
            pl.BlockSpec((group, None, nblk, 1, TOK_BLOCK), lambda b, e, c: (e, b, 0, 0, 0)),
            pl.BlockSpec((None, seq, D), lambda b, e, c: (b, 0, 0)),
        ],
        out_specs=[
            pl.BlockSpec((None, group, cap, D), lambda b, e, c: (b, e, 0, 0)),
            pl.BlockSpec((None, group, cap, 1), lambda b, e, c: (b, e, 0, 0)),
        ],
    )
    return pl.pallas_call(
        functools.partial(_gather_kernel, win=win),
        grid_spec=gs,
        out_shape=[
            jax.ShapeDtypeStruct((B, E, cap, D), BF16),
            jax.ShapeDtypeStruct((B, E, cap, 1), F32),
        ],
        compiler_params=_cparams(("parallel", "arbitrary")),
        name="gather",
    )(cnt, slots5, gates5, h3d)


def _ffn_kernel(xs_ref, gc_ref, wg_ref, wu_ref, wd_ref, y_ref, *, f_chunk):
    xs = xs_ref[...]
    y = None
    for f0 in range(0, wg_ref.shape[1], f_chunk):
        fs = slice(f0, f0 + f_chunk)
        a = jnp.dot(xs, wg_ref[:, fs].astype(BF16), preferred_element_type=F32)
        u = jnp.dot(xs, wu_ref[:, fs].astype(BF16), preferred_element_type=F32)
        hmid = (a * jax.nn.sigmoid(a) * u).astype(BF16)
        part = jnp.dot(hmid, wd_ref[fs, :].astype(BF16), preferred_element_type=F32)
        y = part if y is None else y + part
    y_ref[...] = (y * gc_ref[...]).astype(BF16)


def _ffn(xs, gc, w_gate, w_up, w_down, *, layer):
    B, E, cap, D = xs.shape
    Fd = w_gate.shape[-1]
    return pl.pallas_call(
        functools.partial(_ffn_kernel, f_chunk=min(512, Fd)),
        grid=(E, B),
        in_specs=[
            pl.BlockSpec((None, None, cap, D), lambda e, b: (b, e, 0, 0)),
            pl.BlockSpec((None, None, cap, 1), lambda e, b: (b, e, 0, 0)),
            pl.BlockSpec((None, None, D, Fd), lambda e, b: (layer, e, 0, 0)),
            pl.BlockSpec((None, None, D, Fd), lambda e, b: (layer, e, 0, 0)),
            pl.BlockSpec((None, None, Fd, D), lambda e, b: (layer, e, 0, 0)),
        ],
        out_specs=pl.BlockSpec((None, None, cap, D), lambda e, b: (b, e, 0, 0)),
        out_shape=jax.ShapeDtypeStruct((B, E, cap, D), BF16),
        compiler_params=_cparams(("parallel", "parallel")),
        name="ffn",
    )(xs, gc, w_gate, w_up, w_down)


def _scatter_kernel(cnt_ref, slot_ref, y_ref, x_ref, fg_ref, o_ref, *, win, final_norm):
    b = pl.program_id(0)
    sb = pl.program_id(1)
    ne, cap, D = y_ref.shape
    nloc = slot_ref.shape[1]
    nblk = nloc * pl.num_programs(1)
    r_iota = lax.broadcasted_iota(jnp.int32, (win, TOK_BLOCK), 0)
    contract0 = (((0,), (0,)), ((), ()))

    def onehot_t(srow, lo, sa):
        rel = jnp.where(srow >= lo, srow - sa, -1)
        return jnp.where(rel == r_iota, 1.0, 0.0).astype(BF16)

    def window(e, j):
        idx = (b * ne + e) * (nblk + 1) + j
        s0 = cnt_ref[idx]
        s1 = cnt_ref[idx + 1]
        a0 = (s0 // 16) * 16
        sa = pl.multiple_of(jnp.minimum(a0, cap - win), 16)
        return a0, sa, s1 - (a0 + win)

    extra = jnp.int32(0)
    per_dot = MXU_DEPTH // win
    for t in range(nloc):
        j = sb * nloc + t
        rows = slice(t * TOK_BLOCK, (t + 1) * TOK_BLOCK)
        acc = x_ref[rows, :]
        for e0 in range(0, ne, per_dot):
            gs, ys = [], []
            for e in range(e0, e0 + per_dot):
                a0, sa, over = window(e, j)
                extra = jnp.maximum(extra, over)
                gs.append(onehot_t(slot_ref[e, t], a0, sa))
                ys.append(y_ref[e, pl.ds(sa, win), :])
            acc = acc + lax.dot_general(jnp.concatenate(gs, axis=0), jnp.concatenate(ys, axis=0),
                                        contract0, preferred_element_type=F32)
        o_ref[rows, :] = acc

    @pl.when(extra > 0)
    def _():
        for t in range(nloc):
            j = sb * nloc + t
            rows = slice(t * TOK_BLOCK, (t + 1) * TOK_BLOCK)
            for e in range(ne):
                idx = (b * ne + e) * (nblk + 1) + j
                s0 = cnt_ref[idx]
                s1 = cnt_ref[idx + 1]
                a0 = (s0 // 16) * 16
                nw = (s1 - a0 + win - 1) // win

                def wbody(w, carry):
                    lo = a0 + w * win
                    sa = pl.multiple_of(jnp.minimum(lo, cap - win), 16)
                    g = onehot_t(slot_ref[e, t], lo, sa)
                    o_ref[rows, :] += lax.dot_general(g, y_ref[e, pl.ds(sa, win), :], contract0,
                                                      preferred_element_type=F32)
                    return carry

                lax.fori_loop(1, nw, wbody, 0)

    if final_norm:
        o_ref[...] = _rms_rows(o_ref[...], fg_ref[...])


def _scatter(cnt, slots5, y, x3d, fg, *, win, sb_rows, final_norm):
    B, seq, D = x3d.shape
    E, cap = y.shape[1], y.shape[2]
    nloc = sb_rows // TOK_BLOCK
    assert MXU_DEPTH % win == 0 and E % (MXU_DEPTH // win) == 0
    gs = pltpu.PrefetchScalarGridSpec(
        num_scalar_prefetch=1,
        grid=(B, seq // sb_rows),
        in_specs=[
            pl.BlockSpec((None, E, nloc, 1, TOK_BLOCK), lambda b, s, c: (b, 0, s, 0, 0)),
            pl.BlockSpec((None, E, cap, D), lambda b, s, c: (b, 0, 0, 0), pipeline_mode=pl.Buffered(1)),
            pl.BlockSpec((None, sb_rows, D), lambda b, s, c: (b, s, 0)),
            pl.BlockSpec((1, D), lambda b, s, c: (0, 0)),
        ],
        out_specs=pl.BlockSpec((None, sb_rows, D), lambda b, s, c: (b, s, 0)),
    )
    return pl.pallas_call(
        functools.partial(_scatter_kernel, win=win, final_norm=final_norm),
        grid_spec=gs,
        out_shape=jax.ShapeDtypeStruct((B, seq, D), F32),
        compiler_params=_cparams(("parallel", "arbitrary")),
        name="scatter",
    )(cnt, slots5, y, x3d, fg)


def _rope_tables_T(seq):
    rows = seq // GRID_W
    row_id = jnp.repeat(jnp.arange(rows, dtype=F32), GRID_W)
    col_id = jnp.tile(jnp.arange(GRID_W, dtype=F32), rows)
    n_pairs = HEAD_DIM // 4
    freqs = jnp.exp(-math.log(ROPE_THETA) * jnp.arange(n_pairs, dtype=F32) / n_pairs)
    ang = jnp.concatenate([freqs[:, None] * row_id[None, :], freqs[:, None] * col_id[None, :]], axis=0)
    return jnp.cos(ang), jnp.sin(ang)


def _head_perm(n_heads):
    base = jnp.concatenate([jnp.arange(0, HEAD_DIM, 2), jnp.arange(1, HEAD_DIM, 2)])
    return (jnp.arange(n_heads)[:, None] * HEAD_DIM + base[None, :]).reshape(-1)


def kernel(x, mem, mix_norm_g, w_in, gm_v_norm_g, gm_w_s, gm_b_s, q_norm_g, k_norm_g, branch_norm_g, w_out,
           xattn_norm_g, mem_norm_g, xattn_w_q, xattn_w_kv, xattn_w_o, ffn_norm_g, w_router, w_gate, w_up,
           w_down, final_norm_g):
    B, seq, D = x.shape
    L = w_in.shape[0]
    E = w_router.shape[-1]
    T = B * seq
    cap = EC_FACTOR * seq // E
    tm = min(1024, seq)
    tm_in = min(2048, seq)
    tq = min(256, seq)
    tk = min(2048, seq // 2)
    win = min(64, cap)
    sb_rows = min(1024, seq)
    nblk = seq // TOK_BLOCK

    cosT, sinT = _rope_tables_T(seq)
    o_q = 2 * GM_WIDTH
    o_k = o_q + ATT_WIDTH
    o_v = o_k + KV_WIDTH
    cols = jnp.concatenate([jnp.arange(o_q), o_q + _head_perm(N_Q_HEADS), o_k + _head_perm(N_KV_HEADS),
                            jnp.arange(o_v, o_v + KV_WIDTH)])
    hp = _head_perm(1)

    mkT_all, mv_all = _mem_kv(mem, mem_norm_g.reshape(1, D), xattn_w_kv.astype(BF16))

    x2d = x.reshape(T, D)
    out = None
    for l in range(L):
        w_in_l = w_in[l][:, cols].astype(BF16)
        bs = jnp.broadcast_to(gm_b_s[l][:, :, None], (GM_GROUPS, CHUNK, LANES))
        qg = jnp.broadcast_to(q_norm_g[l][hp][:, None], (HEAD_DIM, tm_in))
        kg = jnp.broadcast_to(k_norm_g[l][hp][:, None], (HEAD_DIM, tm_in))
        gm, qhT, qlT, k8, vT = _mixer_in(
            x2d, mix_norm_g[l].reshape(1, D), w_in_l, gm_v_norm_g[l].reshape(1, GM_WIDTH),
            gm_w_s[l].astype(BF16), bs, qg, kg, branch_norm_g[l, 0].reshape(1, GM_WIDTH), cosT, sinT,
            seq=seq, tm=tm_in)
        atT = _attention(qhT, qlT, k8.reshape(B, seq, N_KV_HEADS * MXU_DEPTH), vT,
                         tq=tq, tk=tk, qb=min(256, tq))

        bg1 = jnp.broadcast_to(branch_norm_g[l, 1][:, None], (ATT_WIDTH, tm))
        wr_pad = jnp.pad(w_router[l].astype(BF16), ((0, 0), (0, LANES - E)))
        x2, h3, affT = _post(
            x2d, gm, atT, bg1, w_out[l].astype(BF16), xattn_norm_g[l].reshape(1, D),
            xattn_w_q[l].astype(BF16), mkT_all, mv_all, xattn_w_o[l].astype(BF16),
            ffn_norm_g[l].reshape(1, D), wr_pad, n_experts=E, layer=l, seq=seq, tm=tm)

        slots, offs = _topk(affT.reshape(E, B, seq // LANES, LANES), cap=cap)
        cnt = offs[:, :, ::TOK_BLOCK // LANES, 0]
        cnt = jnp.concatenate([cnt, jnp.full((B, E, 1), cap, jnp.int32)], axis=-1).reshape(-1)
        slots5 = slots.reshape(B, E, nblk, 1, TOK_BLOCK)
        gates5 = affT.reshape(E, B, nblk, 1, TOK_BLOCK)

        xs, gc = _gather(cnt, slots5, gates5, h3.reshape(B, seq, D), cap=cap, win=win)
        y = _ffn(xs, gc, w_gate, w_up, w_down, layer=l)
        out = _scatter(cnt, slots5, y, x2.reshape(B, seq, D), final_norm_g.reshape(1, D),
                       win=win, sb_rows=sb_rows, final_norm=(l == L - 1))
        x2d = out.reshape(T, D)
    return out
```
